```python
import math
import jax, jax.numpy as jnp
from jax import lax
import numpy as np

D_MODEL = 1024
BATCH = 2
SEQ = 8192
DEPTH = 1

HEAD_DIM = 64
HEADS_PER_GROUP = 4
DILATED_GROUPS = ((128, 1), (512, 4), (2048, 16))
N_ATTN_HEADS = HEADS_PER_GROUP * len(DILATED_GROUPS)
ATTN_WIDTH = N_ATTN_HEADS * HEAD_DIM
ATTN_OUT_WIDTH = HEADS_PER_GROUP * HEAD_DIM
ATTN_BLOCK = 128
N_REL_BUCKETS = 32
REL_MAX_DISTANCE = 2048
NEG_INF = -1e30

LRU_WIDTH = D_MODEL
LRU_HEADS = 16
LRU_HEAD_DIM = LRU_WIDTH // LRU_HEADS
CONV_WIDTH = 4
LRU_C = 8.0

N_EXPERT_GROUPS = 4
EXPERTS_PER_GROUP = 8
N_EXPERTS = N_EXPERT_GROUPS * EXPERTS_PER_GROUP
TOP_K = 2
D_EXPERT = 512
MOE_BLOCK = 128

EPS = 1e-6
IN_WIDTH = 3 * ATTN_WIDTH + 2 * LRU_WIDTH + 2 * D_MODEL

kernel_name = "hybrid_dilated_attn_rglru_hmoe_block"


def _rmsnorm(x, g):
    xf = x.astype(jnp.float32)
    y = xf * lax.rsqrt(jnp.mean(xf * xf, axis=-1, keepdims=True) + EPS) * g.astype(jnp.float32)
    return y.astype(x.dtype)


def _t5_causal_bucket(dist):
    max_exact = N_REL_BUCKETS // 2
    d_f = jnp.maximum(dist, max_exact).astype(jnp.float32)
    large = max_exact + (jnp.log(d_f / max_exact) / math.log(REL_MAX_DISTANCE / max_exact)
                         * (N_REL_BUCKETS - max_exact)).astype(jnp.int32)
    large = jnp.minimum(large, N_REL_BUCKETS - 1)
    return jnp.where(dist < max_exact, dist, large)


def _dilated_window_attention(q, k, v, rel_bias, window, dilation):
    B, S, H, Dh = q.shape
    nw = window // dilation
    span = dilation * ATTN_BLOCK
    Sp = -(-S // span) * span
    L = Sp // dilation
    nb = L // ATTN_BLOCK

    def to_blocks(t):
        t = jnp.pad(t, ((0, 0), (0, Sp - S), (0, 0), (0, 0)))
        t = t.reshape(B, L, dilation, H, Dh).transpose(0, 2, 3, 1, 4)
        return t.reshape(B, dilation, H, nb, ATTN_BLOCK, Dh)

    def with_prev(t):
        prev = jnp.pad(t[:, :, :, :-1], ((0, 0), (0, 0), (0, 0), (1, 0), (0, 0), (0, 0)))
        return jnp.concatenate([prev, t], axis=4)

    qb = to_blocks(q)
    kb = with_prev(to_blocks(k))
    vb = with_prev(to_blocks(v))

    qi = jnp.arange(ATTN_BLOCK)[:, None]
    ki = jnp.arange(2 * ATTN_BLOCK)[None, :]
    dist = ATTN_BLOCK + qi - ki
    band = (dist >= 0) & (dist <= nw)
    valid = band[None] & ((jnp.arange(nb)[:, None, None] > 0) | (ki[None] >= ATTN_BLOCK))
    bucket = _t5_causal_bucket(jnp.maximum(dist, 0) * dilation)
    bias = rel_bias.astype(jnp.float32)[bucket].transpose(2, 0, 1)

    s = jnp.einsum('brhnqd,brhnkd->brhnqk', qb, kb) * (HEAD_DIM ** -0.5) + bias[None, None, :, None]
    s = jnp.where(valid[None, None, None], s, NEG_INF)
    m = jnp.max(s, axis=-1, keepdims=True)
    p = jnp.exp(s - m)
    l = jnp.sum(p, axis=-1)
    o = jnp.einsum('brhnqk,brhnkd->brhnqd', p, vb) / l[..., None]
    lse = m[..., 0] + jnp.log(l)

    o = o.reshape(B, dilation, H, L, Dh).transpose(0, 3, 1, 2, 4).reshape(B, Sp, H, Dh)[:, :S]
    lse = lse.reshape(B, dilation, H, L).transpose(0, 3, 1, 2).reshape(B, Sp, H)[:, :S]
    return o, lse


def _dilated_attention_mixer(qkv, rel_bias):
    B, S, _ = qkv.shape
    qkv = qkv.astype(jnp.float32).reshape(B, S, 3, N_ATTN_HEADS, HEAD_DIM)
    q, k, v = qkv[:, :, 0], qkv[:, :, 1], qkv[:, :, 2]
    outs, lses = [], []
    for gi, (window, dilation) in enumerate(DILATED_GROUPS):
        hs = slice(gi * HEADS_PER_GROUP, (gi + 1) * HEADS_PER_GROUP)
        o, lse = _dilated_window_attention(q[:, :, hs], k[:, :, hs], v[:, :, hs], rel_bias[:, hs], window, dilation)
        outs.append(o)
        lses.append(lse)
    o = jnp.stack(outs)
    wts = jax.nn.softmax(jnp.stack(lses), axis=0)
    return jnp.sum(wts[..., None] * o, axis=0).reshape(B, S, ATTN_OUT_WIDTH)


def _rg_lru_mixer(xr, gate_in, conv_w, conv_b, w_rg, b_rg, w_ig, b_ig, lam):
    B, S, W = xr.shape
    xp = jnp.pad(xr, ((0, 0), (CONV_WIDTH - 1, 0), (0, 0)))
    xc = sum(xp[:, j:j + S] * conv_w[j] for j in range(CONV_WIDTH)) + conv_b
    xh = xc.reshape(B, S, LRU_HEADS, LRU_HEAD_DIM)
    r = jax.nn.sigmoid(jnp.einsum('bshi,hij->bshj', xh, w_rg).reshape(B, S, W) + b_rg)
    i = jax.nn.sigmoid(jnp.einsum('bshi,hij->bshj', xh, w_ig).reshape(B, S, W) + b_ig)
    log_a = -LRU_C * r.astype(jnp.float32) * jax.nn.softplus(-lam.astype(jnp.float32))
    a = jnp.exp(log_a)
    b = jnp.sqrt(-jnp.expm1(2.0 * log_a)) * (i * xc).astype(jnp.float32)

    def combine(c1, c2):
        a1, b1 = c1
        a2, b2 = c2
        return a1 * a2, a2 * b1 + b2

    _, h = lax.associative_scan(combine, (a, b), axis=1)
    return h.astype(xr.dtype) * jax.nn.gelu(gate_in)


def _hierarchical_moe(h, w_router_group, w_router_expert, w_gate_up, w_down):
    B, S, D = h.shape
    T = B * S
    ht = h.reshape(T, D)
    hf = ht.astype(jnp.float32)
    p_group = jax.nn.softmax(hf @ w_router_group.astype(jnp.float32), axis=-1)
    g_sel = jnp.argmax(p_group, axis=-1)
    p_sel = jnp.take_along_axis(p_group, g_sel[:, None], axis=1)[:, 0]
    logits_e = (hf @ w_router_expert.astype(jnp.float32)).reshape(T, N_EXPERT_GROUPS, EXPERTS_PER_GROUP)
    logits_sel = jnp.take_along_axis(logits_e, g_sel[:, None, None], axis=1)[:, 0]
    top_vals, top_idx = lax.top_k(logits_sel, TOP_K)
    gate = jax.nn.softmax(top_vals, axis=-1) * p_sel[:, None]
    expert = g_sel[:, None] * EXPERTS_PER_GROUP + top_idx

    TK = T * TOP_K
    e_flat = expert.reshape(TK).astype(jnp.int32)
    tok_flat = jnp.repeat(jnp.arange(T, dtype=jnp.int32), TOP_K)
    w_flat = gate.reshape(TK)
    order = jnp.argsort(e_flat)
    e_s, tok_s, w_s = e_flat[order], tok_flat[order], w_flat[order]
    counts = jnp.bincount(e_flat, length=N_EXPERTS)
    start = jnp.cumsum(counts) - counts
    padded = ((counts + MOE_BLOCK - 1) // MOE_BLOCK) * MOE_BLOCK
    pend = jnp.cumsum(padded)
    pstart = pend - padded
    dest = pstart[e_s] + (jnp.arange(TK, dtype=jnp.int32) - start[e_s])
    P = TK + N_EXPERTS * MOE_BLOCK
    nblk = P // MOE_BLOCK
    buf_tok = jnp.full((P,), T, dtype=jnp.int32).at[dest].set(tok_s)
    buf_w = jnp.zeros((P,), jnp.float32).at[dest].set(w_s)
    block_expert = jnp.minimum(jnp.searchsorted(pend, jnp.arange(nblk) * MOE_BLOCK, side='right'), N_EXPERTS - 1)
    ht_pad = jnp.concatenate([ht, jnp.zeros((1, D), ht.dtype)], axis=0)

    def block_fn(args):
        idx, e = args
        xb = ht_pad[idx]
        gu = xb @ w_gate_up[e]
        g, u = jnp.split(gu, 2, axis=-1)
        return (jax.nn.silu(g) * u) @ w_down[e]

    y_buf = lax.map(block_fn, (buf_tok.reshape(nblk, MOE_BLOCK), block_expert)).reshape(P, D)
    y = jax.ops.segment_sum(y_buf * buf_w[:, None].astype(y_buf.dtype), buf_tok, num_segments=T + 1)[:T]
    return y.reshape(B, S, D).astype(h.dtype)


def setup_inputs(seed: int = 0) -> dict:
    key = jax.random.key(seed)
    ks = jax.random.split(key, 24)
    f32 = jnp.float32
    nrm = lambda k, shape, scale: jax.random.normal(k, shape, f32) * scale
    a0 = jax.random.uniform(ks[10], (DEPTH, LRU_WIDTH), f32, minval=0.9, maxval=0.999)
    s0 = a0 ** (1.0 / LRU_C)
    return {
        "x": nrm(ks[0], (BATCH, SEQ, D_MODEL), 1.0),
        "rel_bias": nrm(ks[1], (N_REL_BUCKETS, N_ATTN_HEADS), 0.5),
        "norm1": 1.0 + nrm(ks[2], (DEPTH, D_MODEL), 0.02),
        "w_in": nrm(ks[3], (DEPTH, D_MODEL, IN_WIDTH), D_MODEL ** -0.5),
        "conv_w": nrm(ks[4], (DEPTH, CONV_WIDTH, LRU_WIDTH), CONV_WIDTH ** -0.5),
        "conv_b": nrm(ks[5], (DEPTH, LRU_WIDTH), 0.02),
        "w_rg": nrm(ks[6], (DEPTH, LRU_HEADS, LRU_HEAD_DIM, LRU_HEAD_DIM), LRU_HEAD_DIM ** -0.5),
        "b_rg": nrm(ks[7], (DEPTH, LRU_WIDTH), 0.1),
        "w_ig": nrm(ks[8], (DEPTH, LRU_HEADS, LRU_HEAD_DIM, LRU_HEAD_DIM), LRU_HEAD_DIM ** -0.5),
        "b_ig": nrm(ks[9], (DEPTH, LRU_WIDTH), 0.1),
        "lru_lambda": jnp.log(s0) - jnp.log1p(-s0),
        "w_proj_attn": nrm(ks[11], (DEPTH, ATTN_OUT_WIDTH, D_MODEL), ATTN_OUT_WIDTH ** -0.5),
        "w_proj_lru": nrm(ks[12], (DEPTH, LRU_WIDTH, D_MODEL), LRU_WIDTH ** -0.5),
        "w_out": nrm(ks[13], (DEPTH, D_MODEL, D_MODEL), D_MODEL ** -0.5),
        "norm2": 1.0 + nrm(ks[14], (DEPTH, D_MODEL), 0.02),
        "w_router_group": nrm(ks[15], (DEPTH, D_MODEL, N_EXPERT_GROUPS), D_MODEL ** -0.5),
        "w_router_expert": nrm(ks[16], (DEPTH, D_MODEL, N_EXPERTS), D_MODEL ** -0.5),
        "w_gate_up": nrm(ks[17], (DEPTH, N_EXPERTS, D_MODEL, 2 * D_EXPERT), D_MODEL ** -0.5),
        "w_down": nrm(ks[18], (DEPTH, N_EXPERTS, D_EXPERT, D_MODEL), D_EXPERT ** -0.5),
        "norm_f": 1.0 + nrm(ks[19], (D_MODEL,), 0.02),
    }


def reference(x, rel_bias, norm1, w_in, conv_w, conv_b, w_rg, b_rg, w_ig, b_ig, lru_lambda,
              w_proj_attn, w_proj_lru, w_out, norm2, w_router_group, w_router_expert,
              w_gate_up, w_down, norm_f):
    for layer in range(DEPTH):
        h = _rmsnorm(x, norm1[layer])
        proj = h @ w_in[layer]
        c1 = 3 * ATTN_WIDTH
        c2 = c1 + LRU_WIDTH
        c3 = c2 + LRU_WIDTH
        qkv, xr, g_lru, g_merge = proj[..., :c1], proj[..., c1:c2], proj[..., c2:c3], proj[..., c3:]
        attn = _dilated_attention_mixer(qkv, rel_bias).astype(x.dtype)
        lru = _rg_lru_mixer(xr, g_lru, conv_w[layer], conv_b[layer], w_rg[layer], b_rg[layer],
                            w_ig[layer], b_ig[layer], lru_lambda[layer])
        gate_a, gate_b = jnp.split(jax.nn.sigmoid(g_merge), 2, axis=-1)
        merged = gate_a * (attn @ w_proj_attn[layer]) + gate_b * (lru @ w_proj_lru[layer])
        x = x + merged @ w_out[layer]
        h2 = _rmsnorm(x, norm2[layer])
        x = x + _hierarchical_moe(h2, w_router_group[layer], w_router_expert[layer],
                                  w_gate_up[layer], w_down[layer])
    return _rmsnorm(x, norm_f)
```

```python
import functools
import math

import jax
import jax.numpy as jnp
import numpy as np
from jax import lax
from jax.experimental import pallas as pl
from jax.experimental.pallas import tpu as pltpu

D_MODEL = 1024
HEAD_DIM = 64
HEADS_PER_GROUP = 4
DILATED_GROUPS = ((128, 1), (512, 4), (2048, 16))
N_GROUPS = len(DILATED_GROUPS)
N_ATTN_HEADS = HEADS_PER_GROUP * N_GROUPS
ATTN_WIDTH = N_ATTN_HEADS * HEAD_DIM
GROUP_WIDTH = HEADS_PER_GROUP * HEAD_DIM
QKV_WIDTH = 3 * ATTN_WIDTH
ATTN_BLOCK = 128
N_REL_BUCKETS = 32
REL_MAX_DISTANCE = 2048
NEG_INF = -1e30

LRU_WIDTH = D_MODEL
LRU_HEADS = 16
LRU_HEAD_DIM = LRU_WIDTH // LRU_HEADS
CONV_WIDTH = 4
LRU_C = 8.0
LRU_CHUNK_HEADS = 4
LRU_CHUNK = LRU_CHUNK_HEADS * LRU_HEAD_DIM
N_LRU_CHUNKS = LRU_WIDTH // LRU_CHUNK

N_EXPERT_GROUPS = 4
EXPERTS_PER_GROUP = 8
N_EXPERTS = N_EXPERT_GROUPS * EXPERTS_PER_GROUP
TOP_K = 2
D_EXPERT = 512
EPS = 1e-6

REST_WIDTH = 2 * LRU_WIDTH + 2 * D_MODEL
LANES = 128
SUBLANES = 8

QKV_TILE = 512
MIX_TILE = 256
MOE_TILE = 256
DISPATCH_TILE = 512
COMBINE_TILE = 256
VMEM_LIMIT = 56 * 1024 * 1024

ROUTE_E0, ROUTE_E1, ROUTE_R0, ROUTE_R1, ROUTE_G0, ROUTE_G1 = range(6)
GROUP_LOGIT_LANE0 = N_EXPERTS


def _rmsnorm(x, g):
    return x * lax.rsqrt(jnp.mean(x * x, axis=-1, keepdims=True) + EPS) * g


def _qkv_kernel(x_ref, g_ref, w_ref, o_ref):
    h = _rmsnorm(x_ref[...], g_ref[...]).astype(jnp.bfloat16)
    o_ref[...] = jnp.dot(h, w_ref[...], preferred_element_type=jnp.float32).astype(o_ref.dtype)


def _qkv_projection(x2, norm1, w_qkv):
    T = x2.shape[0]
    return pl.pallas_call(
        _qkv_kernel,
        grid=(T // QKV_TILE,),
        in_specs=[
            pl.BlockSpec((QKV_TILE, D_MODEL), lambda i: (i, 0)),
            pl.BlockSpec((1, D_MODEL), lambda i: (0, 0)),
            pl.BlockSpec((D_MODEL, QKV_WIDTH), lambda i: (0, 0)),
        ],
        out_specs=pl.BlockSpec((QKV_TILE, QKV_WIDTH), lambda i: (i, 0)),
        out_shape=jax.ShapeDtypeStruct((T, QKV_WIDTH), jnp.bfloat16),
        compiler_params=pltpu.CompilerParams(
            dimension_semantics=("parallel",), vmem_limit_bytes=VMEM_LIMIT),
        name="qkv_projection",
    )(x2, norm1, w_qkv)


def _attn_kernel(q_ref, kp_ref, kc_ref, vp_ref, vc_ref, bias_ref, o_ref, lse_ref):
    q = q_ref[...] * (HEAD_DIM ** -0.5)
    k = jnp.concatenate([kp_ref[...], kc_ref[...]], axis=0)
    v = jnp.concatenate([vp_ref[...], vc_ref[...]], axis=0)
    outs, lses = [], []
    for h in range(HEADS_PER_GROUP):
        hs = slice(h * HEAD_DIM, (h + 1) * HEAD_DIM)
        s = lax.dot_general(q[:, hs], k[:, hs], (((1,), (1,)), ((), ())),
                            preferred_element_type=jnp.float32) + bias_ref[h]
        m = jnp.max(s, axis=-1, keepdims=True)
        p = jnp.exp(s - m)
        l = jnp.sum(p, axis=-1, keepdims=True)
        o = jnp.dot(p.astype(jnp.bfloat16), v[:, hs], preferred_element_type=jnp.float32) / l
        outs.append(o)
        lses.append(jnp.broadcast_to(m + jnp.log(l), (ATTN_BLOCK, HEAD_DIM)))
    o_ref[...] = jnp.concatenate(outs, axis=1).astype(o_ref.dtype)
    lse_ref[...] = jnp.concatenate(lses, axis=1)


def _attention_group(qkv, bias, gi, dilation):
    B, S, _ = qkv.shape
    L = S // dilation
    nb = L // ATTN_BLOCK
    qkv_r = qkv.reshape(B, L, dilation * QKV_WIDTH)
    cols_per_pos = QKV_WIDTH // GROUP_WIDTH
    kv_blocks = ATTN_WIDTH // GROUP_WIDTH

    def col(which):
        return lambda r: r * cols_per_pos + which * kv_blocks + gi

    blk = (None, ATTN_BLOCK, GROUP_WIDTH)
    cur = lambda which: pl.BlockSpec(blk, lambda b, r, n: (b, n, col(which)(r)))
    prev = lambda which: pl.BlockSpec(blk, lambda b, r, n: (b, jnp.maximum(n - 1, 0), col(which)(r)))
    o, lse = pl.pallas_call(
        _attn_kernel,
        grid=(B, dilation, nb),
        in_specs=[
            cur(0), prev(1), cur(1), prev(2), cur(2),
            pl.BlockSpec((None, HEADS_PER_GROUP, ATTN_BLOCK, 2 * ATTN_BLOCK),
                         lambda b, r, n: (jnp.minimum(n, 1), 0, 0, 0)),
        ],
        out_specs=[
            pl.BlockSpec(blk, lambda b, r, n: (b, n, r)),
            pl.BlockSpec(blk, lambda b, r, n: (b, n, r)),
        ],
        out_shape=[
            jax.ShapeDtypeStruct((B, L, dilation * GROUP_WIDTH), jnp.bfloat16),
            jax.ShapeDtypeStruct((B, L, dilation * GROUP_WIDTH), jnp.float32),
        ],
        compiler_params=pltpu.CompilerParams(
            dimension_semantics=("parallel", "parallel", "arbitrary"), vmem_limit_bytes=VMEM_LIMIT),
        name=f"dilated_attention_g{gi}",
    )(qkv_r, qkv_r, qkv_r, qkv_r, qkv_r, bias)
    return o.reshape(B * S, GROUP_WIDTH), lse.reshape(B * S, GROUP_WIDTH)


def _t5_causal_bucket(dist):
    max_exact = N_REL_BUCKETS // 2
    d_f = jnp.maximum(dist, max_exact).astype(jnp.float32)
    large = max_exact + (jnp.log(d_f / max_exact) / math.log(REL_MAX_DISTANCE / max_exact)
                         * (N_REL_BUCKETS - max_exact)).astype(jnp.int32)
    large = jnp.minimum(large, N_REL_BUCKETS - 1)
    return jnp.where(dist < max_exact, dist, large)


def _attention_bias_table(rel_bias_g, window, dilation):
    nw = window // dilation
    qi = jnp.arange(ATTN_BLOCK)[:, None]
    ki = jnp.arange(2 * ATTN_BLOCK)[None, :]
    dist = ATTN_BLOCK + qi - ki
    band = (dist >= 0) & (dist <= nw)
    bucket = _t5_causal_bucket(jnp.maximum(dist, 0) * dilation)
    bias = rel_bias_g.astype(jnp.float32)[bucket].transpose(2, 0, 1)
    later = jnp.where(band[None], bias, NEG_INF)
    first = jnp.where((band & (ki >= ATTN_BLOCK))[None], bias, NEG_INF)
    return jnp.stack([first, later])


def _gelu_tanh(x):
    return 0.5 * x * (1.0 + jnp.tanh(math.sqrt(2.0 / math.pi) * (x + 0.044715 * (x * x * x))))


def _softplus(x):
    return jnp.maximum(x, 0.0) + jnp.log(1.0 + jnp.exp(-jnp.abs(x)))


def _mixer_kernel(x_ref, o1_ref, o2_ref, o3_ref, l1_ref, l2_ref, l3_ref,
                  n1_ref, win_ref, cw_ref, cb_ref, wg_ref, brg_ref, big_ref, lam_ref,
                  wpa_ref, wpl_ref, wout_ref, n2_ref, wr_ref,
                  x1_ref, h2_ref, route_ref, cnt_ref,
                  xbuf, a_sc, b_sc, h_sc, cnt_sc):
    tm = MIX_TILE
    first_tile = pl.program_id(1) == 0

    @pl.when(first_tile)
    def _():
        xbuf[0:SUBLANES, :] = jnp.zeros((SUBLANES, LRU_WIDTH), jnp.float32)
        h_sc[...] = jnp.zeros_like(h_sc)

    @pl.when(first_tile & (pl.program_id(0) == 0))
    def _():
        cnt_sc[...] = jnp.zeros_like(cnt_sc)

    x = x_ref[...]
    h = _rmsnorm(x, n1_ref[...]).astype(jnp.bfloat16)
    proj = jnp.dot(h, win_ref[...], preferred_element_type=jnp.float32)
    xr = proj[:, 0:LRU_WIDTH]
    g_lru = proj[:, LRU_WIDTH:2 * LRU_WIDTH]
    g_a = proj[:, 2 * LRU_WIDTH:2 * LRU_WIDTH + D_MODEL]
    g_b = proj[:, 2 * LRU_WIDTH + D_MODEL:]

    xbuf[SUBLANES:SUBLANES + tm, :] = xr
    xc = xr * cw_ref[CONV_WIDTH - 1:CONV_WIDTH, :] + cb_ref[...]
    for j in range(CONV_WIDTH - 1):
        back = CONV_WIDTH - 1 - j
        xc = xc + xbuf[SUBLANES - back:SUBLANES - back + tm, :] * cw_ref[j:j + 1, :]
    xbuf[0:SUBLANES, :] = xbuf[tm:tm + SUBLANES, :]

    xcb = xc.astype(jnp.bfloat16)
    r_parts, i_parts = [], []
    for c in range(N_LRU_CHUNKS):
        ri = jnp.dot(xcb[:, c * LRU_CHUNK:(c + 1) * LRU_CHUNK], wg_ref[c],
                     preferred_element_type=jnp.float32)
        r_parts.append(ri[:, :LRU_CHUNK])
        i_parts.append(ri[:, LRU_CHUNK:])
    r = jax.nn.sigmoid(jnp.concatenate(r_parts, axis=1) + brg_ref[...])
    ig = jax.nn.sigmoid(jnp.concatenate(i_parts, axis=1) + big_ref[...])
    log_a = (-LRU_C * _softplus(-lam_ref[...])) * r
    a = jnp.exp(log_a)
    bb = jnp.sqrt(1.0 - a * a) * (ig * xc)

    row8 = lax.broadcasted_iota(jnp.int32, (tm, 1), 0) % SUBLANES
    for k in (1, 2, 4):
        a_s = pltpu.roll(a, k, axis=0)
        b_s = pltpu.roll(bb, k, axis=0)
        take = row8 >= k
        bb = jnp.where(take, a * b_s + bb, bb)
        a = jnp.where(take, a * a_s, a)
    a_sc[...] = a
    b_sc[...] = bb

    def sweep(g, carry):
        rows = pl.ds(pl.multiple_of(g * SUBLANES, SUBLANES), SUBLANES)
        hh = a_sc[rows, :] * carry + b_sc[rows, :]
        b_sc[rows, :] = hh
        return hh[SUBLANES - 1:SUBLANES, :]

    h_sc[...] = lax.fori_loop(0, tm // SUBLANES, sweep, h_sc[...], unroll=4)
    lru = (b_sc[...] * _gelu_tanh(g_lru)).astype(jnp.bfloat16)
    p_lru = jnp.dot(lru, wpl_ref[...], preferred_element_type=jnp.float32)

    l1, l2, l3 = l1_ref[...], l2_ref[...], l3_ref[...]
    lm = jnp.maximum(jnp.maximum(l1, l2), l3)
    e1, e2, e3 = jnp.exp(l1 - lm), jnp.exp(l2 - lm), jnp.exp(l3 - lm)
    attn = (e1 * o1_ref[...].astype(jnp.float32) + e2 * o2_ref[...].astype(jnp.float32)
            + e3 * o3_ref[...].astype(jnp.float32)) / (e1 + e2 + e3)
    p_attn = jnp.dot(attn.astype(jnp.bfloat16), wpa_ref[...], preferred_element_type=jnp.float32)

    merged = jax.nn.sigmoid(g_a) * p_attn + jax.nn.sigmoid(g_b) * p_lru
    x1 = x + jnp.dot(merged.astype(jnp.bfloat16), wout_ref[...], preferred_element_type=jnp.float32)
    x1_ref[...] = x1
    h2 = _rmsnorm(x1, n2_ref[...])
    h2_ref[...] = h2

    logits = jnp.dot(h2, wr_ref[...], preferred_element_type=jnp.float32,
                     precision=lax.Precision.HIGHEST)
    lane = lax.broadcasted_iota(jnp.int32, (tm, LANES), 1)
    big = jnp.int32(LANES)
    lowest = jnp.float32(-3.0e38)
    is_g = (lane >= GROUP_LOGIT_LANE0) & (lane < GROUP_LOGIT_LANE0 + N_EXPERT_GROUPS)
    gl = jnp.where(is_g, logits, lowest)
    gmax = jnp.max(gl, axis=-1, keepdims=True)
    gsel = jnp.min(jnp.where(gl == gmax, lane, big), axis=-1, keepdims=True) - GROUP_LOGIT_LANE0
    p_sel = 1.0 / jnp.sum(jnp.where(is_g, jnp.exp(gl - gmax), 0.0), axis=-1, keepdims=True)
    is_e = (lane >= gsel * EXPERTS_PER_GROUP) & (lane < (gsel + 1) * EXPERTS_PER_GROUP)
    el = jnp.where(is_e, logits, lowest)
    v1 = jnp.max(el, axis=-1, keepdims=True)
    i1 = jnp.min(jnp.where(el == v1, lane, big), axis=-1, keepdims=True)
    el2 = jnp.where(lane == i1, lowest, el)
    v2 = jnp.max(el2, axis=-1, keepdims=True)
    i2 = jnp.min(jnp.where(el2 == v2, lane, big), axis=-1, keepdims=True)
    t = jnp.exp(v2 - v1)
    g0 = p_sel / (1.0 + t)
    g1 = p_sel * t / (1.0 + t)

    oh0 = lane == i1
    oh1 = lane == i2
    both = (oh0 | oh1)
    tri = (lax.broadcasted_iota(jnp.int32, (tm, tm), 0)
           > lax.broadcasted_iota(jnp.int32, (tm, tm), 1)).astype(jnp.bfloat16)
    before = jnp.dot(tri, both.astype(jnp.bfloat16), preferred_element_type=jnp.float32) + cnt_sc[...]
    rank0 = jnp.sum(jnp.where(oh0, before, 0.0), axis=-1, keepdims=True)
    rank1 = jnp.sum(jnp.where(oh1, before, 0.0), axis=-1, keepdims=True)
    cnt_sc[...] = cnt_sc[...] + jnp.sum(both.astype(jnp.float32), axis=0, keepdims=True)
    cnt_ref[...] = jnp.broadcast_to(cnt_sc[...], cnt_ref.shape)

    route = jnp.zeros((tm, LANES), jnp.float32)
    for lane_id, val in ((ROUTE_E0, i1.astype(jnp.float32)), (ROUTE_E1, i2.astype(jnp.float32)),
                         (ROUTE_R0, rank0), (ROUTE_R1, rank1), (ROUTE_G0, g0), (ROUTE_G1, g1)):
        route = jnp.where(lane == lane_id, val, route)
    route_ref[...] = route


def _token_mixer(x2, attn_o, attn_lse, weights, B, S):
    T = x2.shape[0]
    tm = MIX_TILE
    nt = S // tm
    row = lambda b, j: (b * nt + j, 0)
    const2 = lambda b, j: (0, 0)
    const3 = lambda b, j: (0, 0, 0)

    def resident(shape):
        idx = const2 if len(shape) == 2 else const3
        return pl.BlockSpec(shape, idx, pipeline_mode=pl.Buffered(1))

    in_specs = [pl.BlockSpec((tm, D_MODEL), row)]
    in_specs += [pl.BlockSpec((tm, GROUP_WIDTH), row)] * (2 * N_GROUPS)
    in_specs += [resident(w.shape) for w in weights]
    return pl.pallas_call(
        _mixer_kernel,
        grid=(B, nt),
        in_specs=in_specs,
        out_specs=[
            pl.BlockSpec((tm, D_MODEL), row),
            pl.BlockSpec((tm, D_MODEL), row),
            pl.BlockSpec((tm, LANES), row),
            pl.BlockSpec((SUBLANES, LANES), const2),
        ],
        out_shape=[
            jax.ShapeDtypeStruct((T, D_MODEL), jnp.float32),
            jax.ShapeDtypeStruct((T, D_MODEL), jnp.float32),
            jax.ShapeDtypeStruct((T, LANES), jnp.float32),
            jax.ShapeDtypeStruct((SUBLANES, LANES), jnp.float32),
        ],
        scratch_shapes=[
            pltpu.VMEM((tm + 2 * SUBLANES, LRU_WIDTH), jnp.float32),
            pltpu.VMEM((tm, LRU_WIDTH), jnp.float32),
            pltpu.VMEM((tm, LRU_WIDTH), jnp.float32),
            pltpu.VMEM((1, LRU_WIDTH), jnp.float32),
            pltpu.VMEM((1, LANES), jnp.float32),
        ],
        compiler_params=pltpu.CompilerParams(
            dimension_semantics=("arbitrary", "arbitrary"), vmem_limit_bytes=VMEM_LIMIT),
        name="token_mixer",
    )(x2, *attn_o, *attn_lse, *weights)


def _row_copy(src_hbm, src_row, dst, dst_row, sem):
    return pltpu.make_async_copy(src_hbm.at[pl.ds(src_row, 1), :], dst.at[pl.ds(dst_row, 1), :], sem)


def _dispatch_kernel(dest_ref, h2_hbm, xs_in_hbm, xs_hbm, sem):
    del xs_in_hbm
    base = pl.program_id(0) * DISPATCH_TILE

    def issue(t, carry):
        for k in range(TOP_K):
            _row_copy(h2_hbm, base + t, xs_hbm, dest_ref[k * dest_ref.shape[0] // TOP_K + base + t], sem).start()
        return carry

    lax.fori_loop(0, DISPATCH_TILE, issue, 0)

    def drain(t, carry):
        for k in range(TOP_K):
            _row_copy(h2_hbm, 0, xs_hbm, 0, sem).wait()
        return carry

    lax.fori_loop(0, DISPATCH_TILE, drain, 0)


def _dispatch(dest, h2, n_rows):
    T = h2.shape[0]
    xs0 = jnp.zeros((n_rows, D_MODEL), jnp.float32)
    return pl.pallas_call(
        _dispatch_kernel,
        grid_spec=pltpu.PrefetchScalarGridSpec(
            num_scalar_prefetch=1,
            grid=(T // DISPATCH_TILE,),
            in_specs=[pl.BlockSpec(memory_space=pl.ANY), pl.BlockSpec(memory_space=pl.ANY)],
            out_specs=pl.BlockSpec(memory_space=pl.ANY),
            scratch_shapes=[pltpu.SemaphoreType.DMA(())],
        ),
        out_shape=jax.ShapeDtypeStruct((n_rows, D_MODEL), jnp.float32),
        input_output_aliases={2: 0},
        compiler_params=pltpu.CompilerParams(dimension_semantics=("arbitrary",)),
        name="moe_dispatch",
    )(dest, h2, xs0)


def _expert_kernel(be_ref, nv_ref, xs_ref, wgu_ref, wd_ref, ys_ref, wgu_bf, wd_bf):
    i = pl.program_id(0)
    new_expert = (i == 0) | (be_ref[i] != be_ref[jnp.maximum(i - 1, 0)])

    @pl.when(new_expert)
    def _():
        wgu_bf[...] = wgu_ref[...].astype(jnp.bfloat16)
        wd_bf[...] = wd_ref[...].astype(jnp.bfloat16)

    @pl.when(i < nv_ref[0])
    def _():
        gu = jnp.dot(xs_ref[...].astype(jnp.bfloat16), wgu_bf[...], preferred_element_type=jnp.float32)
        g = gu[:, :D_EXPERT]
        u = gu[:, D_EXPERT:]
        act = (g * jax.nn.sigmoid(g) * u).astype(jnp.bfloat16)
        ys_ref[...] = jnp.dot(act, wd_bf[...], preferred_element_type=jnp.float32)

    @pl.when(i >= nv_ref[0])
    def _():
        ys_ref[...] = jnp.zeros_like(ys_ref)


def _experts(block_expert, n_valid, xs, w_gate_up, w_down):
    n_rows = xs.shape[0]
    nblk = n_rows // MOE_TILE
    return pl.pallas_call(
        _expert_kernel,
        grid_spec=pltpu.PrefetchScalarGridSpec(
            num_scalar_prefetch=2,
            grid=(nblk,),
            in_specs=[
                pl.BlockSpec((MOE_TILE, D_MODEL), lambda i, be, nv: (jnp.minimum(i, nv[0] - 1), 0)),
                pl.BlockSpec((None, D_MODEL, 2 * D_EXPERT), lambda i, be, nv: (be[i], 0, 0)),
                pl.BlockSpec((None, D_EXPERT, D_MODEL), lambda i, be, nv: (be[i], 0, 0)),
            ],
            out_specs=pl.BlockSpec((MOE_TILE, D_MODEL), lambda i, be, nv: (i, 0)),
            scratch_shapes=[
                pltpu.VMEM((D_MODEL, 2 * D_EXPERT), jnp.bfloat16),
                pltpu.VMEM((D_EXPERT, D_MODEL), jnp.bfloat16),
            ],
        ),
        out_shape=jax.ShapeDtypeStruct((n_rows, D_MODEL), jnp.float32),
        compiler_params=pltpu.CompilerParams(
            dimension_semantics=("arbitrary",), vmem_limit_bytes=VMEM_LIMIT),
        name="moe_experts",
    )(block_expert, n_valid, xs, w_gate_up, w_down)


def _combine_kernel(dest_ref, ys_hbm, x1_ref, route_ref, nf_ref, out_ref, y0_buf, y1_buf, sems):
    tc = COMBINE_TILE
    base = pl.program_id(0) * tc
    n_tok = dest_ref.shape[0] // TOP_K
    bufs = (y0_buf, y1_buf)

    def issue(t, carry):
        for k in range(TOP_K):
            _row_copy(ys_hbm, dest_ref[k * n_tok + base + t], bufs[k], t, sems.at[k]).start()
        return carry

    lax.fori_loop(0, tc, issue, 0)

    def drain(t, carry):
        for k in range(TOP_K):
            _row_copy(ys_hbm, 0, bufs[k], 0, sems.at[k]).wait()
        return carry

    lax.fori_loop(0, tc, drain, 0)

    route = route_ref[...]
    g0 = route[:, ROUTE_G0:ROUTE_G0 + 1]
    g1 = route[:, ROUTE_G1:ROUTE_G1 + 1]
    x2 = x1_ref[...] + g0 * y0_buf[...] + g1 * y1_buf[...]
    out_ref[...] = _rmsnorm(x2, nf_ref[...])


def _combine(dest, ys, x1, route, norm_f):
    T = x1.shape[0]
    tc = COMBINE_TILE
    return pl.pallas_call(
        _combine_kernel,
        grid_spec=pltpu.PrefetchScalarGridSpec(
            num_scalar_prefetch=1,
            grid=(T // tc,),
            in_specs=[
                pl.BlockSpec(memory_space=pl.ANY),
                pl.BlockSpec((tc, D_MODEL), lambda i, d: (i, 0)),
                pl.BlockSpec((tc, LANES), lambda i, d: (i, 0)),
                pl.BlockSpec((1, D_MODEL), lambda i, d: (0, 0)),
            ],
            out_specs=pl.BlockSpec((tc, D_MODEL), lambda i, d: (i, 0)),
            scratch_shapes=[
                pltpu.VMEM((tc, D_MODEL), jnp.float32),
                pltpu.VMEM((tc, D_MODEL), jnp.float32),
                pltpu.SemaphoreType.DMA((TOP_K,)),
            ],
        ),
        out_shape=jax.ShapeDtypeStruct((T, D_MODEL), jnp.float32),
        compiler_params=pltpu.CompilerParams(
            dimension_semantics=("arbitrary",), vmem_limit_bytes=VMEM_LIMIT),
        name="moe_combine",
    )(dest, ys, x1, route, norm_f)


def _block_diag_gates(w_rg, w_ig):
    def bd(w):
        w4 = w.reshape(N_LRU_CHUNKS, LRU_CHUNK_HEADS, LRU_HEAD_DIM, LRU_HEAD_DIM)
        eye = jnp.eye(LRU_CHUNK_HEADS, dtype=w.dtype)
        return jnp.einsum('chij,hk->chikj', w4, eye).reshape(N_LRU_CHUNKS, LRU_CHUNK, LRU_CHUNK)
    return jnp.concatenate([bd(w_rg), bd(w_ig)], axis=-1).astype(jnp.bfloat16)


def kernel(x, rel_bias, norm1, w_in, conv_w, conv_b, w_rg, b_rg, w_ig, b_ig, lru_lambda,
           w_proj_attn, w_proj_lru, w_out, norm2, w_router_group, w_router_expert,
           w_gate_up, w_down, norm_f):
    B, S, D = x.shape
    T = B * S
    depth = w_in.shape[0]
    bf16 = jnp.bfloat16
    x2 = x.reshape(T, D)
    out = None
    for layer in range(depth):
        row = lambda v: v[layer].reshape(1, -1)
        w_qkv = w_in[layer][:, :QKV_WIDTH].astype(bf16)
        w_rest = w_in[layer][:, QKV_WIDTH:].astype(bf16)

        qkv = _qkv_projection(x2, row(norm1), w_qkv).reshape(B, S, QKV_WIDTH)
        attn_o, attn_lse = [], []
        for gi, (window, dilation) in enumerate(DILATED_GROUPS):
            hs = slice(gi * HEADS_PER_GROUP, (gi + 1) * HEADS_PER_GROUP)
            bias = _attention_bias_table(rel_bias[:, hs], window, dilation)
            o, lse = _attention_group(qkv, bias, gi, dilation)
            attn_o.append(o)
            attn_lse.append(lse)

        w_router = jnp.zeros((D, LANES), jnp.float32)
        w_router = w_router.at[:, :N_EXPERTS].set(w_router_expert[layer].astype(jnp.float32))
        w_router = w_router.at[:, GROUP_LOGIT_LANE0:GROUP_LOGIT_LANE0 + N_EXPERT_GROUPS].set(
            w_router_group[layer].astype(jnp.float32))
        weights = [
            row(norm1), w_rest, conv_w[layer], row(conv_b),
            _block_diag_gates(w_rg[layer], w_ig[layer]), row(b_rg), row(b_ig), row(lru_lambda),
            w_proj_attn[layer].astype(bf16), w_proj_lru[layer].astype(bf16), w_out[layer].astype(bf16),
            row(norm2), w_router,
        ]
        x1, h2, route, counts = _token_mixer(x2, attn_o, attn_lse, weights, B, S)

        counts = counts[0, :N_EXPERTS].astype(jnp.int32)
        padded = ((counts + MOE_TILE - 1) // MOE_TILE) * MOE_TILE
        pend = jnp.cumsum(padded)
        pstart = pend - padded
        e_sel = route[:, ROUTE_E0:ROUTE_E1 + 1].astype(jnp.int32)
        rank = route[:, ROUTE_R0:ROUTE_R1 + 1].astype(jnp.int32)
        dest = (pstart[e_sel] + rank).T.reshape(TOP_K * T)
        n_rows = TOP_K * T + N_EXPERTS * MOE_TILE
        nblk = n_rows // MOE_TILE
        block_expert = jnp.minimum(
            jnp.searchsorted(pend, jnp.arange(nblk, dtype=jnp.int32) * MOE_TILE, side='right'),
            N_EXPERTS - 1).astype(jnp.int32)
        n_valid = (pend[-1:] // MOE_TILE).astype(jnp.int32)

        xs = _dispatch(dest, h2, n_rows)
        ys = _experts(block_expert, n_valid, xs, w_gate_up[layer], w_down[layer])
        last = layer == depth - 1
        assert last, "kernel supports the problem's single-layer configuration"
        out = _combine(dest, ys, x1, route, norm_f.reshape(1, -1))
    return out.reshape(B, S, D)
```

```python
import functools
import math

import jax
import jax.numpy as jnp
import numpy as np
from jax import lax
from jax.experimental import pallas as pl
from jax.experimental.pallas import tpu as pltpu

D_MODEL = 1024
HEAD_DIM = 64
HEADS_PER_GROUP = 4
DILATED_GROUPS = ((128, 1), (512, 4), (2048, 16))
N_GROUPS = len(DILATED_GROUPS)
N_ATTN_HEADS = HEADS_PER_GROUP * N_GROUPS
ATTN_WIDTH = N_ATTN_HEADS * HEAD_DIM
GROUP_WIDTH = HEADS_PER_GROUP * HEAD_DIM
QKV_WIDTH = 3 * ATTN_WIDTH
ATTN_BLOCK = 128
N_REL_BUCKETS = 32
REL_MAX_DISTANCE = 2048
NEG_INF = -1e30

LRU_WIDTH = D_MODEL
LRU_HEADS = 16
LRU_HEAD_DIM = LRU_WIDTH // LRU_HEADS
CONV_WIDTH = 4
LRU_C = 8.0
LRU_CHUNK_HEADS = 4
LRU_CHUNK = LRU_CHUNK_HEADS * LRU_HEAD_DIM
N_LRU_CHUNKS = LRU_WIDTH // LRU_CHUNK

N_EXPERT_GROUPS = 4
EXPERTS_PER_GROUP = 8
N_EXPERTS = N_EXPERT_GROUPS * EXPERTS_PER_GROUP
TOP_K = 2
D_EXPERT = 512
EPS = 1e-6

REST_WIDTH = 2 * LRU_WIDTH + 2 * D_MODEL
LANES = 128
SUBLANES = 8

QKV_TILE = 512
MIX_TILE = 256
MOE_TILE = 256
DISPATCH_TILE = 512
COMBINE_TILE = 256
VMEM_LIMIT = 56 * 1024 * 1024

ROUTE_E0, ROUTE_E1, ROUTE_R0, ROUTE_R1, ROUTE_G0, ROUTE_G1 = range(6)
GROUP_LOGIT_LANE0 = N_EXPERTS


def _rmsnorm(x, g):
    return x * lax.rsqrt(jnp.mean(x * x, axis=-1, keepdims=True) + EPS) * g


def _qkv_kernel(x_ref, g_ref, w_ref, o_ref):
    h = _rmsnorm(x_ref[...], g_ref[...]).astype(jnp.bfloat16)
    o_ref[...] = jnp.dot(h, w_ref[...], preferred_element_type=jnp.float32).astype(o_ref.dtype)


def _qkv_projection(x2, norm1, w_qkv):
    T = x2.shape[0]
    return pl.pallas_call(
        _qkv_kernel,
        grid=(T // QKV_TILE,),
        in_specs=[
            pl.BlockSpec((QKV_TILE, D_MODEL), lambda i: (i, 0)),
            pl.BlockSpec((1, D_MODEL), lambda i: (0, 0)),
            pl.BlockSpec((D_MODEL, QKV_WIDTH), lambda i: (0, 0)),
        ],
        out_specs=pl.BlockSpec((QKV_TILE, QKV_WIDTH), lambda i: (i, 0)),
        out_shape=jax.ShapeDtypeStruct((T, QKV_WIDTH), jnp.bfloat16),
        compiler_params=pltpu.CompilerParams(
            dimension_semantics=("parallel",), vmem_limit_bytes=VMEM_LIMIT),
        name="qkv_projection",
    )(x2, norm1, w_qkv)


def _attn_kernel(q_ref, kp_ref, kc_ref, vp_ref, vc_ref, bias_ref, o_ref, lse_ref):
    q = q_ref[...] * (HEAD_DIM ** -0.5)
    k = jnp.concatenate([kp_ref[...], kc_ref[...]], axis=0)
    v = jnp.concatenate([vp_ref[...], vc_ref[...]], axis=0)
    outs, lses = [], []
    for h in range(HEADS_PER_GROUP):
        hs = slice(h * HEAD_DIM, (h + 1) * HEAD_DIM)
        s = lax.dot_general(q[:, hs], k[:, hs], (((1,), (1,)), ((), ())),
                            preferred_element_type=jnp.float32) + bias_ref[h]
        m = jnp.max(s, axis=-1, keepdims=True)
        p = jnp.exp(s - m)
        l = jnp.sum(p, axis=-1, keepdims=True)
        o = jnp.dot(p.astype(jnp.bfloat16), v[:, hs], preferred_element_type=jnp.float32) / l
        outs.append(o)
        lses.append(jnp.broadcast_to(m + jnp.log(l), (ATTN_BLOCK, HEAD_DIM)))
    o_ref[...] = jnp.concatenate(outs, axis=1).astype(o_ref.dtype)
    lse_ref[...] = jnp.concatenate(lses, axis=1)


def _attention_group(qkv, bias, gi, dilation):
    B, S, _ = qkv.shape
    L = S // dilation
    nb = L // ATTN_BLOCK
    qkv_r = qkv.reshape(B, L, dilation * QKV_WIDTH)
    cols_per_pos = QKV_WIDTH // GROUP_WIDTH
    kv_blocks = ATTN_WIDTH // GROUP_WIDTH

    def col(which):
        return lambda r: r * cols_per_pos + which * kv_blocks + gi

    blk = (None, ATTN_BLOCK, GROUP_WIDTH)
    cur = lambda which: pl.BlockSpec(blk, lambda b, r, n: (b, n, col(which)(r)))
    prev = lambda which: pl.BlockSpec(blk, lambda b, r, n: (b, jnp.maximum(n - 1, 0), col(which)(r)))
    o, lse = pl.pallas_call(
        _attn_kernel,
        grid=(B, dilation, nb),
        in_specs=[
            cur(0), prev(1), cur(1), prev(2), cur(2),
            pl.BlockSpec((None, HEADS_PER_GROUP, ATTN_BLOCK, 2 * ATTN_BLOCK),
                         lambda b, r, n: (jnp.minimum(n, 1), 0, 0, 0)),
        ],
        out_specs=[
            pl.BlockSpec(blk, lambda b, r, n: (b, n, r)),
            pl.BlockSpec(blk, lambda b, r, n: (b, n, r)),
        ],
        out_shape=[
            jax.ShapeDtypeStruct((B, L, dilation * GROUP_WIDTH), jnp.bfloat16),
            jax.ShapeDtypeStruct((B, L, dilation * GROUP_WIDTH), jnp.float32),
        ],
        compiler_params=pltpu.CompilerParams(
            dimension_semantics=("parallel", "parallel", "arbitrary"), vmem_limit_bytes=VMEM_LIMIT),
        name=f"dilated_attention_g{gi}",
    )(qkv_r, qkv_r, qkv_r, qkv_r, qkv_r, bias)
    return o.reshape(B * S, GROUP_WIDTH), lse.reshape(B * S, GROUP_WIDTH)


def _t5_causal_bucket(dist):
    max_exact = N_REL_BUCKETS // 2
    d_f = jnp.maximum(dist, max_exact).astype(jnp.float32)
    large = max_exact + (jnp.log(d_f / max_exact) / math.log(REL_MAX_DISTANCE / max_exact)
                         * (N_REL_BUCKETS - max_exact)).astype(jnp.int32)
    large = jnp.minimum(large, N_REL_BUCKETS - 1)
    return jnp.where(dist < max_exact, dist, large)


def _attention_bias_table(rel_bias_g, window, dilation):
    nw = window // dilation
    qi = jnp.arange(ATTN_BLOCK)[:, None]
    ki = jnp.arange(2 * ATTN_BLOCK)[None, :]
    dist = ATTN_BLOCK + qi - ki
    band = (dist >= 0) & (dist <= nw)
    bucket = _t5_causal_bucket(jnp.maximum(dist, 0) * dilation)
    onehot = (bucket[None, :, :, None] == jnp.arange(N_REL_BUCKETS)).astype(jnp.float32)
    bias = jnp.sum(onehot * rel_bias_g.astype(jnp.float32).T[:, None, None, :], axis=-1)
    later = jnp.where(band[None], bias, NEG_INF)
    first = jnp.where((band & (ki >= ATTN_BLOCK))[None], bias, NEG_INF)
    return jnp.stack([first, later])


def _gelu_tanh(x):
    return 0.5 * x * (1.0 + jnp.tanh(math.sqrt(2.0 / math.pi) * (x + 0.044715 * (x * x * x))))


def _softplus(x):
    return jnp.maximum(x, 0.0) + jnp.log(1.0 + jnp.exp(-jnp.abs(x)))


def _mixer_kernel(x_ref, o1_ref, o2_ref, o3_ref, l1_ref, l2_ref, l3_ref,
                  n1_ref, win_ref, cw_ref, cb_ref, wg_ref, brg_ref, big_ref, lam_ref,
                  wpa_ref, wpl_ref, wout_ref, n2_ref, wr_ref,
                  x1_ref, h2_ref, route_ref, cnt_ref,
                  xbuf, a_sc, b_sc, h_sc, cnt_sc):
    tm = MIX_TILE
    first_tile = pl.program_id(1) == 0

    @pl.when(first_tile)
    def _():
        xbuf[0:SUBLANES, :] = jnp.zeros((SUBLANES, LRU_WIDTH), jnp.float32)
        h_sc[...] = jnp.zeros_like(h_sc)

    @pl.when(first_tile & (pl.program_id(0) == 0))
    def _():
        cnt_sc[...] = jnp.zeros_like(cnt_sc)

    x = x_ref[...]
    h = _rmsnorm(x, n1_ref[...]).astype(jnp.bfloat16)
    proj = jnp.dot(h, win_ref[...], preferred_element_type=jnp.float32)
    xr = proj[:, 0:LRU_WIDTH]
    g_lru = proj[:, LRU_WIDTH:2 * LRU_WIDTH]
    g_a = proj[:, 2 * LRU_WIDTH:2 * LRU_WIDTH + D_MODEL]
    g_b = proj[:, 2 * LRU_WIDTH + D_MODEL:]

    xbuf[SUBLANES:SUBLANES + tm, :] = xr
    xc = xr * cw_ref[CONV_WIDTH - 1:CONV_WIDTH, :] + cb_ref[...]
    for j in range(CONV_WIDTH - 1):
        back = CONV_WIDTH - 1 - j
        xc = xc + xbuf[SUBLANES - back:SUBLANES - back + tm, :] * cw_ref[j:j + 1, :]
    xbuf[0:SUBLANES, :] = xbuf[tm:tm + SUBLANES, :]

    xcb = xc.astype(jnp.bfloat16)
    r_parts, i_parts = [], []
    for c in range(N_LRU_CHUNKS):
        ri = jnp.dot(xcb[:, c * LRU_CHUNK:(c + 1) * LRU_CHUNK], wg_ref[c],
                     preferred_element_type=jnp.float32)
        r_parts.append(ri[:, :LRU_CHUNK])
        i_parts.append(ri[:, LRU_CHUNK:])
    r = jax.nn.sigmoid(jnp.concatenate(r_parts, axis=1) + brg_ref[...])
    ig = jax.nn.sigmoid(jnp.concatenate(i_parts, axis=1) + big_ref[...])
    log_a = (-LRU_C * _softplus(-lam_ref[...])) * r
    a = jnp.exp(log_a)
    bb = jnp.sqrt(1.0 - a * a) * (ig * xc)

    row8 = lax.broadcasted_iota(jnp.int32, (tm, 1), 0) % SUBLANES
    for k in (1, 2, 4):
        a_s = pltpu.roll(a, k, axis=0)
        b_s = pltpu.roll(bb, k, axis=0)
        take = row8 >= k
        bb = jnp.where(take, a * b_s + bb, bb)
        a = jnp.where(take, a * a_s, a)
    a_sc[...] = a
    b_sc[...] = bb

    def sweep(g, carry):
        rows = pl.ds(pl.multiple_of(g * SUBLANES, SUBLANES), SUBLANES)
        hh = a_sc[rows, :] * carry + b_sc[rows, :]
        b_sc[rows, :] = hh
        return hh[SUBLANES - 1:SUBLANES, :]

    h_sc[...] = lax.fori_loop(0, tm // SUBLANES, sweep, h_sc[...], unroll=4)
    lru = (b_sc[...] * _gelu_tanh(g_lru)).astype(jnp.bfloat16)
    p_lru = jnp.dot(lru, wpl_ref[...], preferred_element_type=jnp.float32)

    l1, l2, l3 = l1_ref[...], l2_ref[...], l3_ref[...]
    lm = jnp.maximum(jnp.maximum(l1, l2), l3)
    e1, e2, e3 = jnp.exp(l1 - lm), jnp.exp(l2 - lm), jnp.exp(l3 - lm)
    attn = (e1 * o1_ref[...].astype(jnp.float32) + e2 * o2_ref[...].astype(jnp.float32)
            + e3 * o3_ref[...].astype(jnp.float32)) / (e1 + e2 + e3)
    p_attn = jnp.dot(attn.astype(jnp.bfloat16), wpa_ref[...], preferred_element_type=jnp.float32)

    merged = jax.nn.sigmoid(g_a) * p_attn + jax.nn.sigmoid(g_b) * p_lru
    x1 = x + jnp.dot(merged.astype(jnp.bfloat16), wout_ref[...], preferred_element_type=jnp.float32)
    x1_ref[...] = x1
    h2 = _rmsnorm(x1, n2_ref[...])
    h2_ref[...] = h2

    logits = jnp.dot(h2, wr_ref[...], preferred_element_type=jnp.float32,
                     precision=lax.Precision.HIGHEST)
    lane = lax.broadcasted_iota(jnp.int32, (tm, LANES), 1)
    big = jnp.int32(LANES)
    lowest = jnp.float32(-3.0e38)
    is_g = (lane >= GROUP_LOGIT_LANE0) & (lane < GROUP_LOGIT_LANE0 + N_EXPERT_GROUPS)
    gl = jnp.where(is_g, logits, lowest)
    gmax = jnp.max(gl, axis=-1, keepdims=True)
    gsel = jnp.min(jnp.where(gl == gmax, lane, big), axis=-1, keepdims=True) - GROUP_LOGIT_LANE0
    p_sel = 1.0 / jnp.sum(jnp.where(is_g, jnp.exp(gl - gmax), 0.0), axis=-1, keepdims=True)
    is_e = (lane >= gsel * EXPERTS_PER_GROUP) & (lane < (gsel + 1) * EXPERTS_PER_GROUP)
    el = jnp.where(is_e, logits, lowest)
    v1 = jnp.max(el, axis=-1, keepdims=True)
    i1 = jnp.min(jnp.where(el == v1, lane, big), axis=-1, keepdims=True)
    el2 = jnp.where(lane == i1, lowest, el)
    v2 = jnp.max(el2, axis=-1, keepdims=True)
    i2 = jnp.min(jnp.where(el2 == v2, lane, big), axis=-1, keepdims=True)
    t = jnp.exp(v2 - v1)
    g0 = p_sel / (1.0 + t)
    g1 = p_sel * t / (1.0 + t)

    oh0 = lane == i1
    oh1 = lane == i2
    both = (oh0 | oh1)
    tri = (lax.broadcasted_iota(jnp.int32, (tm, tm), 0)
           > lax.broadcasted_iota(jnp.int32, (tm, tm), 1)).astype(jnp.bfloat16)
    before = jnp.dot(tri, both.astype(jnp.bfloat16), preferred_element_type=jnp.float32) + cnt_sc[...]
    rank0 = jnp.sum(jnp.where(oh0, before, 0.0), axis=-1, keepdims=True)
    rank1 = jnp.sum(jnp.where(oh1, before, 0.0), axis=-1, keepdims=True)
    cnt_sc[...] = cnt_sc[...] + jnp.sum(both.astype(jnp.float32), axis=0, keepdims=True)
    cnt_ref[...] = jnp.broadcast_to(cnt_sc[...], cnt_ref.shape)

    route = jnp.zeros((tm, LANES), jnp.float32)
    for lane_id, val in ((ROUTE_E0, i1.astype(jnp.float32)), (ROUTE_E1, i2.astype(jnp.float32)),
                         (ROUTE_R0, rank0), (ROUTE_R1, rank1), (ROUTE_G0, g0), (ROUTE_G1, g1)):
        route = jnp.where(lane == lane_id, val, route)
    route_ref[...] = route


def _token_mixer(x2, attn_o, attn_lse, weights, B, S):
    T = x2.shape[0]
    tm = MIX_TILE
    nt = S // tm
    row = lambda b, j: (b * nt + j, 0)
    const2 = lambda b, j: (0, 0)
    const3 = lambda b, j: (0, 0, 0)

    def resident(shape):
        idx = const2 if len(shape) == 2 else const3
        return pl.BlockSpec(shape, idx, pipeline_mode=pl.Buffered(1))

    in_specs = [pl.BlockSpec((tm, D_MODEL), row)]
    in_specs += [pl.BlockSpec((tm, GROUP_WIDTH), row)] * (2 * N_GROUPS)
    in_specs += [resident(w.shape) for w in weights]
    return pl.pallas_call(
        _mixer_kernel,
        grid=(B, nt),
        in_specs=in_specs,
        out_specs=[
            pl.BlockSpec((tm, D_MODEL), row),
            pl.BlockSpec((tm, D_MODEL), row),
            pl.BlockSpec((tm, LANES), row),
            pl.BlockSpec((SUBLANES, LANES), const2),
        ],
        out_shape=[
            jax.ShapeDtypeStruct((T, D_MODEL), jnp.float32),
            jax.ShapeDtypeStruct((T, D_MODEL), jnp.float32),
            jax.ShapeDtypeStruct((T, LANES), jnp.float32),
            jax.ShapeDtypeStruct((SUBLANES, LANES), jnp.float32),
        ],
        scratch_shapes=[
            pltpu.VMEM((tm + 2 * SUBLANES, LRU_WIDTH), jnp.float32),
            pltpu.VMEM((tm, LRU_WIDTH), jnp.float32),
            pltpu.VMEM((tm, LRU_WIDTH), jnp.float32),
            pltpu.VMEM((1, LRU_WIDTH), jnp.float32),
            pltpu.VMEM((1, LANES), jnp.float32),
        ],
        compiler_params=pltpu.CompilerParams(
            dimension_semantics=("arbitrary", "arbitrary"), vmem_limit_bytes=VMEM_LIMIT),
        name="token_mixer",
    )(x2, *attn_o, *attn_lse, *weights)


def _row_copy(src_hbm, src_row, dst, dst_row, sem):
    return pltpu.make_async_copy(src_hbm.at[pl.ds(src_row, 1), :], dst.at[pl.ds(dst_row, 1), :], sem)


def _dispatch_kernel(dest_ref, h2_ref, xs_in_hbm, xs_hbm, sem):
    del xs_in_hbm
    base = pl.program_id(0) * DISPATCH_TILE
    n_tok = dest_ref.shape[0] // TOP_K

    def issue(t, carry):
        for k in range(TOP_K):
            _row_copy(h2_ref, t, xs_hbm, dest_ref[k * n_tok + base + t], sem).start()
        return carry

    lax.fori_loop(0, DISPATCH_TILE, issue, 0)

    def drain(t, carry):
        for k in range(TOP_K):
            _row_copy(h2_ref, 0, xs_hbm, 0, sem).wait()
        return carry

    lax.fori_loop(0, DISPATCH_TILE, drain, 0)


def _dispatch(dest, h2, n_rows):
    T = h2.shape[0]
    xs0 = jnp.zeros((n_rows, D_MODEL), jnp.float32)
    return pl.pallas_call(
        _dispatch_kernel,
        grid_spec=pltpu.PrefetchScalarGridSpec(
            num_scalar_prefetch=1,
            grid=(T // DISPATCH_TILE,),
            in_specs=[pl.BlockSpec((DISPATCH_TILE, D_MODEL), lambda i, d: (i, 0)),
                      pl.BlockSpec(memory_space=pl.ANY)],
            out_specs=pl.BlockSpec(memory_space=pl.ANY),
            scratch_shapes=[pltpu.SemaphoreType.DMA(())],
        ),
        out_shape=jax.ShapeDtypeStruct((n_rows, D_MODEL), jnp.float32),
        input_output_aliases={2: 0},
        compiler_params=pltpu.CompilerParams(dimension_semantics=("arbitrary",)),
        name="moe_dispatch",
    )(dest, h2, xs0)


def _expert_kernel(be_ref, nv_ref, xs_ref, wgu_ref, wd_ref, ys_ref, wgu_bf, wd_bf):
    i = pl.program_id(0)
    new_expert = (i == 0) | (be_ref[i] != be_ref[jnp.maximum(i - 1, 0)])

    @pl.when(new_expert)
    def _():
        wgu_bf[...] = wgu_ref[...].astype(jnp.bfloat16)
        wd_bf[...] = wd_ref[...].astype(jnp.bfloat16)

    @pl.when(i < nv_ref[0])
    def _():
        gu = jnp.dot(xs_ref[...].astype(jnp.bfloat16), wgu_bf[...], preferred_element_type=jnp.float32)
        g = gu[:, :D_EXPERT]
        u = gu[:, D_EXPERT:]
        act = (g * jax.nn.sigmoid(g) * u).astype(jnp.bfloat16)
        ys_ref[...] = jnp.dot(act, wd_bf[...], preferred_element_type=jnp.float32)

    @pl.when(i >= nv_ref[0])
    def _():
        ys_ref[...] = jnp.zeros_like(ys_ref)


def _experts(block_expert, n_valid, xs, w_gate_up, w_down):
    n_rows = xs.shape[0]
    nblk = n_rows // MOE_TILE
    return pl.pallas_call(
        _expert_kernel,
        grid_spec=pltpu.PrefetchScalarGridSpec(
            num_scalar_prefetch=2,
            grid=(nblk,),
            in_specs=[
                pl.BlockSpec((MOE_TILE, D_MODEL), lambda i, be, nv: (jnp.minimum(i, nv[0] - 1), 0)),
                pl.BlockSpec((None, D_MODEL, 2 * D_EXPERT), lambda i, be, nv: (be[i], 0, 0)),
                pl.BlockSpec((None, D_EXPERT, D_MODEL), lambda i, be, nv: (be[i], 0, 0)),
            ],
            out_specs=pl.BlockSpec((MOE_TILE, D_MODEL), lambda i, be, nv: (i, 0)),
            scratch_shapes=[
                pltpu.VMEM((D_MODEL, 2 * D_EXPERT), jnp.bfloat16),
                pltpu.VMEM((D_EXPERT, D_MODEL), jnp.bfloat16),
            ],
        ),
        out_shape=jax.ShapeDtypeStruct((n_rows, D_MODEL), jnp.float32),
        compiler_params=pltpu.CompilerParams(
            dimension_semantics=("arbitrary",), vmem_limit_bytes=VMEM_LIMIT),
        name="moe_experts",
    )(block_expert, n_valid, xs, w_gate_up, w_down)


def _combine_kernel(dest_ref, ys_hbm, x1_ref, route_ref, nf_ref, out_ref, y0_buf, y1_buf, sems):
    tc = COMBINE_TILE
    base = pl.program_id(0) * tc
    n_tok = dest_ref.shape[0] // TOP_K
    bufs = (y0_buf, y1_buf)

    def issue(t, carry):
        for k in range(TOP_K):
            _row_copy(ys_hbm, dest_ref[k * n_tok + base + t], bufs[k], t, sems.at[k]).start()
        return carry

    lax.fori_loop(0, tc, issue, 0)

    def drain(t, carry):
        for k in range(TOP_K):
            _row_copy(ys_hbm, 0, bufs[k], 0, sems.at[k]).wait()
        return carry

    lax.fori_loop(0, tc, drain, 0)

    route = route_ref[...]
    g0 = route[:, ROUTE_G0:ROUTE_G0 + 1]
    g1 = route[:, ROUTE_G1:ROUTE_G1 + 1]
    x2 = x1_ref[...] + g0 * y0_buf[...] + g1 * y1_buf[...]
    out_ref[...] = _rmsnorm(x2, nf_ref[...])


def _combine(dest, ys, x1, route, norm_f):
    T = x1.shape[0]
    tc = COMBINE_TILE
    return pl.pallas_call(
        _combine_kernel,
        grid_spec=pltpu.PrefetchScalarGridSpec(
            num_scalar_prefetch=1,
            grid=(T // tc,),
            in_specs=[
                pl.BlockSpec(memory_space=pl.ANY),
                pl.BlockSpec((tc, D_MODEL), lambda i, d: (i, 0)),
                pl.BlockSpec((tc, LANES), lambda i, d: (i, 0)),
                pl.BlockSpec((1, D_MODEL), lambda i, d: (0, 0)),
            ],
            out_specs=pl.BlockSpec((tc, D_MODEL), lambda i, d: (i, 0)),
            scratch_shapes=[
                pltpu.VMEM((tc, D_MODEL), jnp.float32),
                pltpu.VMEM((tc, D_MODEL), jnp.float32),
                pltpu.SemaphoreType.DMA((TOP_K,)),
            ],
        ),
        out_shape=jax.ShapeDtypeStruct((T, D_MODEL), jnp.float32),
        compiler_params=pltpu.CompilerParams(
            dimension_semantics=("arbitrary",), vmem_limit_bytes=VMEM_LIMIT),
        name="moe_combine",
    )(dest, ys, x1, route, norm_f)


def _block_diag_gates(w_rg, w_ig):
    def bd(w):
        w4 = w.reshape(N_LRU_CHUNKS, LRU_CHUNK_HEADS, LRU_HEAD_DIM, LRU_HEAD_DIM)
        eye = jnp.eye(LRU_CHUNK_HEADS, dtype=w.dtype)
        return jnp.einsum('chij,hk->chikj', w4, eye).reshape(N_LRU_CHUNKS, LRU_CHUNK, LRU_CHUNK)
    return jnp.concatenate([bd(w_rg), bd(w_ig)], axis=-1).astype(jnp.bfloat16)


def kernel(x, rel_bias, norm1, w_in, conv_w, conv_b, w_rg, b_rg, w_ig, b_ig, lru_lambda,
           w_proj_attn, w_proj_lru, w_out, norm2, w_router_group, w_router_expert,
           w_gate_up, w_down, norm_f):
    B, S, D = x.shape
    T = B * S
    depth = w_in.shape[0]
    bf16 = jnp.bfloat16
    x2 = x.reshape(T, D)
    out = None
    for layer in range(depth):
        row = lambda v: v[layer].reshape(1, -1)
        w_qkv = w_in[layer][:, :QKV_WIDTH].astype(bf16)
        w_rest = w_in[layer][:, QKV_WIDTH:].astype(bf16)

        qkv = _qkv_projection(x2, row(norm1), w_qkv).reshape(B, S, QKV_WIDTH)
        attn_o, attn_lse = [], []
        for gi, (window, dilation) in enumerate(DILATED_GROUPS):
            hs = slice(gi * HEADS_PER_GROUP, (gi + 1) * HEADS_PER_GROUP)
            bias = _attention_bias_table(rel_bias[:, hs], window, dilation)
            o, lse = _attention_group(qkv, bias, gi, dilation)
            attn_o.append(o)
            attn_lse.append(lse)

        w_router = jnp.zeros((D, LANES), jnp.float32)
        w_router = w_router.at[:, :N_EXPERTS].set(w_router_expert[layer].astype(jnp.float32))
        w_router = w_router.at[:, GROUP_LOGIT_LANE0:GROUP_LOGIT_LANE0 + N_EXPERT_GROUPS].set(
            w_router_group[layer].astype(jnp.float32))
        weights = [
            row(norm1), w_rest, conv_w[layer], row(conv_b),
            _block_diag_gates(w_rg[layer], w_ig[layer]), row(b_rg), row(b_ig), row(lru_lambda),
            w_proj_attn[layer].astype(bf16), w_proj_lru[layer].astype(bf16), w_out[layer].astype(bf16),
            row(norm2), w_router,
        ]
        x1, h2, route, counts = _token_mixer(x2, attn_o, attn_lse, weights, B, S)

        counts = counts[0, :N_EXPERTS].astype(jnp.int32)
        padded = ((counts + MOE_TILE - 1) // MOE_TILE) * MOE_TILE
        pend = jnp.cumsum(padded)
        pstart = pend - padded
        e_sel = route[:, ROUTE_E0:ROUTE_E1 + 1].astype(jnp.int32)
        rank = route[:, ROUTE_R0:ROUTE_R1 + 1].astype(jnp.int32)
        dest = (pstart[e_sel] + rank).T.reshape(TOP_K * T)
        n_rows = TOP_K * T + N_EXPERTS * MOE_TILE
        nblk = n_rows // MOE_TILE
        block_row0 = jnp.arange(nblk, dtype=jnp.int32) * MOE_TILE
        block_expert = jnp.minimum(
            jnp.sum((pend[None, :] <= block_row0[:, None]).astype(jnp.int32), axis=1), N_EXPERTS - 1)
        n_valid = (pend[-1:] // MOE_TILE).astype(jnp.int32)

        xs = _dispatch(dest, h2, n_rows)
        ys = _experts(block_expert, n_valid, xs, w_gate_up[layer], w_down[layer])
        last = layer == depth - 1
        assert last, "kernel supports the problem's single-layer configuration"
        out = _combine(dest, ys, x1, route, norm_f.reshape(1, -1))
    return out.reshape(B, S, D)
```

```python
import math

import jax
import jax.numpy as jnp
from jax import lax
from jax.experimental import pallas as pl
from jax.experimental.pallas import tpu as pltpu

D_MODEL = 1024
HEAD_DIM = 64
HEADS_PER_GROUP = 4
DILATED_GROUPS = ((128, 1), (512, 4), (2048, 16))
N_GROUPS = len(DILATED_GROUPS)
N_ATTN_HEADS = HEADS_PER_GROUP * N_GROUPS
ATTN_WIDTH = N_ATTN_HEADS * HEAD_DIM
GROUP_WIDTH = HEADS_PER_GROUP * HEAD_DIM
QKV_WIDTH = 3 * ATTN_WIDTH
ATTN_BLOCK = 128
N_REL_BUCKETS = 32
REL_MAX_DISTANCE = 2048
NEG_INF = -1e30

LRU_WIDTH = D_MODEL
LRU_HEADS = 16
LRU_HEAD_DIM = LRU_WIDTH // LRU_HEADS
CONV_WIDTH = 4
LRU_C = 8.0
LRU_CHUNK_HEADS = 4
LRU_CHUNK = LRU_CHUNK_HEADS * LRU_HEAD_DIM
N_LRU_CHUNKS = LRU_WIDTH // LRU_CHUNK

N_EXPERT_GROUPS = 4
EXPERTS_PER_GROUP = 8
N_EXPERTS = N_EXPERT_GROUPS * EXPERTS_PER_GROUP
TOP_K = 2
D_EXPERT = 512
EPS = 1e-6

REST_WIDTH = 2 * LRU_WIDTH + 2 * D_MODEL
LANES = 128
SUBLANES = 8

ATTN_TILE = 256
ATTN_QB = 4
MIX_TILE = 256
MOE_TILE = 256
DISPATCH_TILE = 512
COMBINE_TILE = 256
VMEM_LIMIT = 56 * 1024 * 1024

ROUTE_E0, ROUTE_E1, ROUTE_R0, ROUTE_R1, ROUTE_G0, ROUTE_G1 = range(6)
GROUP_LOGIT_LANE0 = N_EXPERTS


def _rmsnorm(x, g):
    return x * lax.rsqrt(jnp.mean(x * x, axis=-1, keepdims=True) + EPS) * g


def _residue_major_perm(n, d):
    per = n // d
    out = lax.broadcasted_iota(jnp.int32, (n, n), 0)
    src = lax.broadcasted_iota(jnp.int32, (n, n), 1)
    m = jnp.bitwise_and(out, per - 1)
    r = jnp.right_shift(out, per.bit_length() - 1)
    return (src == m * d + r).astype(jnp.bfloat16)


def _qkv_kernel(x_ref, g_ref, w_ref, *o_refs):
    tm = ATTN_TILE
    h = _rmsnorm(x_ref[...], g_ref[...]).astype(jnp.bfloat16)
    qkv = jnp.dot(h, w_ref[...], preferred_element_type=jnp.float32).astype(jnp.bfloat16)
    for gi, (o_ref, (_, d)) in enumerate(zip(o_refs, DILATED_GROUPS)):
        part = qkv[:, gi * ATTN_WIDTH:(gi + 1) * ATTN_WIDTH]
        if d > 1:
            part = jnp.dot(_residue_major_perm(tm, d), part,
                           preferred_element_type=jnp.float32).astype(jnp.bfloat16)
        o_ref[...] = part.reshape(d, tm // d, ATTN_WIDTH)


def _qkv_projection(x2, norm1, w_qkv, B, S):
    tm = ATTN_TILE
    nt = S // tm
    return pl.pallas_call(
        _qkv_kernel,
        grid=(B, nt),
        in_specs=[
            pl.BlockSpec((tm, D_MODEL), lambda b, j: (b * nt + j, 0)),
            pl.BlockSpec((1, D_MODEL), lambda b, j: (0, 0)),
            pl.BlockSpec((D_MODEL, QKV_WIDTH), lambda b, j: (0, 0)),
        ],
        out_specs=[pl.BlockSpec((None, d, tm // d, ATTN_WIDTH), lambda b, j: (b, 0, j, 0))
                   for _, d in DILATED_GROUPS],
        out_shape=[jax.ShapeDtypeStruct((B, d, S // d, ATTN_WIDTH), jnp.bfloat16)
                   for _, d in DILATED_GROUPS],
        compiler_params=pltpu.CompilerParams(
            dimension_semantics=("parallel", "parallel"), vmem_limit_bytes=VMEM_LIMIT),
        name="qkv_projection",
    )(x2, norm1, w_qkv)


def _attn_kernel(q_ref, kp_ref, kc_ref, vp_ref, vc_ref, bias0_ref, bias_ref, o_ref, lse_ref):
    q = q_ref[...] * (HEAD_DIM ** -0.5)
    k = jnp.concatenate([kp_ref[...], kc_ref[...]], axis=0)
    v = jnp.concatenate([vp_ref[...], vc_ref[...]], axis=0)
    for b in range(ATTN_QB):
        rows = slice(b * ATTN_BLOCK, (b + 1) * ATTN_BLOCK)
        keys = slice(b * ATTN_BLOCK, (b + 2) * ATTN_BLOCK)
        outs, lses = [], []
        for h in range(HEADS_PER_GROUP):
            hs = slice(h * HEAD_DIM, (h + 1) * HEAD_DIM)
            bias = bias0_ref[h] if b == 0 else bias_ref[h]
            s = lax.dot_general(q[rows, hs], k[keys, hs], (((1,), (1,)), ((), ())),
                                preferred_element_type=jnp.float32) + bias
            m = jnp.max(s, axis=-1, keepdims=True)
            p = jnp.exp(s - m)
            l = jnp.sum(p, axis=-1, keepdims=True)
            o = jnp.dot(p.astype(jnp.bfloat16), v[keys, hs], preferred_element_type=jnp.float32) / l
            outs.append(o)
            lses.append(jnp.broadcast_to(m + jnp.log(l), (ATTN_BLOCK, HEAD_DIM)))
        o_ref[rows, :] = jnp.concatenate(outs, axis=1).astype(o_ref.dtype)
        lse_ref[rows, :] = jnp.concatenate(lses, axis=1)


def _attention_group(qkv_g, bias, gi, dilation):
    B, _, L, _ = qkv_g.shape
    rows = ATTN_QB * ATTN_BLOCK
    cur = lambda which: pl.BlockSpec((None, None, rows, GROUP_WIDTH), lambda b, r, n: (b, r, n, which))
    prev = lambda which: pl.BlockSpec((None, None, ATTN_BLOCK, GROUP_WIDTH),
                                      lambda b, r, n: (b, r, jnp.maximum(n * ATTN_QB - 1, 0), which))
    bias_blk = (None, HEADS_PER_GROUP, ATTN_BLOCK, 2 * ATTN_BLOCK)
    return pl.pallas_call(
        _attn_kernel,
        grid=(B, dilation, L // rows),
        in_specs=[
            cur(0), prev(1), cur(1), prev(2), cur(2),
            pl.BlockSpec(bias_blk, lambda b, r, n: (jnp.minimum(n, 1), 0, 0, 0)),
            pl.BlockSpec(bias_blk, lambda b, r, n: (1, 0, 0, 0)),
        ],
        out_specs=[cur(0), cur(0)],
        out_shape=[
            jax.ShapeDtypeStruct((B, dilation, L, GROUP_WIDTH), jnp.bfloat16),
            jax.ShapeDtypeStruct((B, dilation, L, GROUP_WIDTH), jnp.float32),
        ],
        compiler_params=pltpu.CompilerParams(
            dimension_semantics=("parallel", "parallel", "arbitrary"), vmem_limit_bytes=VMEM_LIMIT),
        name=f"dilated_attention_g{gi}",
    )(qkv_g, qkv_g, qkv_g, qkv_g, qkv_g, bias, bias)


def _t5_causal_bucket(dist):
    max_exact = N_REL_BUCKETS // 2
    d_f = jnp.maximum(dist, max_exact).astype(jnp.float32)
    large = max_exact + (jnp.log(d_f / max_exact) / math.log(REL_MAX_DISTANCE / max_exact)
                         * (N_REL_BUCKETS - max_exact)).astype(jnp.int32)
    large = jnp.minimum(large, N_REL_BUCKETS - 1)
    return jnp.where(dist < max_exact, dist, large)


def _attention_bias_table(rel_bias_g, window, dilation):
    nw = window // dilation
    qi = jnp.arange(ATTN_BLOCK)[:, None]
    ki = jnp.arange(2 * ATTN_BLOCK)[None, :]
    dist = ATTN_BLOCK + qi - ki
    band = (dist >= 0) & (dist <= nw)
    bucket = _t5_causal_bucket(jnp.maximum(dist, 0) * dilation)
    onehot = (bucket[None, :, :, None] == jnp.arange(N_REL_BUCKETS)).astype(jnp.float32)
    bias = jnp.sum(onehot * rel_bias_g.astype(jnp.float32).T[:, None, None, :], axis=-1)
    later = jnp.where(band[None], bias, NEG_INF)
    first = jnp.where((band & (ki >= ATTN_BLOCK))[None], bias, NEG_INF)
    return jnp.stack([first, later])


def _gelu_tanh(x):
    return 0.5 * x * (1.0 + jnp.tanh(math.sqrt(2.0 / math.pi) * (x + 0.044715 * (x * x * x))))


def _softplus(x):
    return jnp.maximum(x, 0.0) + jnp.log(1.0 + jnp.exp(-jnp.abs(x)))


def _natural_order(blk_ref, d, slabs):
    if d == 1:
        return blk_ref[0].astype(jnp.float32)
    per = MIX_TILE // d
    for r in range(d):
        rows = blk_ref[r].astype(jnp.float32)
        for c in range(GROUP_WIDTH // LANES):
            slabs[c, pl.ds(r, per, stride=d), :] = rows[:, c * LANES:(c + 1) * LANES]
    return jnp.concatenate([slabs[c] for c in range(GROUP_WIDTH // LANES)], axis=1)


def _mixer_kernel(x_ref, o1_ref, o2_ref, o3_ref, l1_ref, l2_ref, l3_ref,
                  n1_ref, win_ref, cw_ref, cb_ref, wg_ref, brg_ref, big_ref, lam_ref,
                  wpa_ref, wpl_ref, wout_ref, n2_ref, wrh_ref, wrl_ref,
                  x1_ref, h2_ref, route_ref, cnt_ref,
                  xbuf, a_sc, b_sc, h_sc, cnt_sc, *slabs):
    tm = MIX_TILE
    first_tile = pl.program_id(1) == 0

    @pl.when(first_tile)
    def _():
        xbuf[0:SUBLANES, :] = jnp.zeros((SUBLANES, LRU_WIDTH), jnp.float32)
        h_sc[...] = jnp.zeros_like(h_sc)

    @pl.when(first_tile & (pl.program_id(0) == 0))
    def _():
        cnt_sc[...] = jnp.zeros_like(cnt_sc)

    x = x_ref[...]
    h = _rmsnorm(x, n1_ref[...]).astype(jnp.bfloat16)
    proj = jnp.dot(h, win_ref[...], preferred_element_type=jnp.float32)
    xr = proj[:, 0:LRU_WIDTH]
    g_lru = proj[:, LRU_WIDTH:2 * LRU_WIDTH]
    g_a = proj[:, 2 * LRU_WIDTH:2 * LRU_WIDTH + D_MODEL]
    g_b = proj[:, 2 * LRU_WIDTH + D_MODEL:]

    xbuf[SUBLANES:SUBLANES + tm, :] = xr
    xc = xr * cw_ref[CONV_WIDTH - 1:CONV_WIDTH, :] + cb_ref[...]
    for j in range(CONV_WIDTH - 1):
        back = CONV_WIDTH - 1 - j
        xc = xc + xbuf[SUBLANES - back:SUBLANES - back + tm, :] * cw_ref[j:j + 1, :]
    xbuf[0:SUBLANES, :] = xbuf[tm:tm + SUBLANES, :]

    xcb = xc.astype(jnp.bfloat16)
    r_parts, i_parts = [], []
    for c in range(N_LRU_CHUNKS):
        ri = jnp.dot(xcb[:, c * LRU_CHUNK:(c + 1) * LRU_CHUNK], wg_ref[c],
                     preferred_element_type=jnp.float32)
        r_parts.append(ri[:, :LRU_CHUNK])
        i_parts.append(ri[:, LRU_CHUNK:])
    r = jax.nn.sigmoid(jnp.concatenate(r_parts, axis=1) + brg_ref[...])
    ig = jax.nn.sigmoid(jnp.concatenate(i_parts, axis=1) + big_ref[...])
    log_a = (-LRU_C * _softplus(-lam_ref[...])) * r
    a = jnp.exp(log_a)
    bb = jnp.sqrt(1.0 - a * a) * (ig * xc)

    row8 = lax.broadcasted_iota(jnp.int32, (tm, 1), 0) % SUBLANES
    for k in (1, 2, 4):
        a_s = pltpu.roll(a, k, axis=0)
        b_s = pltpu.roll(bb, k, axis=0)
        take = row8 >= k
        bb = jnp.where(take, a * b_s + bb, bb)
        a = jnp.where(take, a * a_s, a)
    a_sc[...] = a
    b_sc[...] = bb

    def sweep(g, carry):
        rows = pl.ds(pl.multiple_of(g * SUBLANES, SUBLANES), SUBLANES)
        hh = a_sc[rows, :] * carry + b_sc[rows, :]
        b_sc[rows, :] = hh
        return hh[SUBLANES - 1:SUBLANES, :]

    h_sc[...] = lax.fori_loop(0, tm // SUBLANES, sweep, h_sc[...], unroll=4)
    lru = (b_sc[...] * _gelu_tanh(g_lru)).astype(jnp.bfloat16)
    p_lru = jnp.dot(lru, wpl_ref[...], preferred_element_type=jnp.float32)

    dil = [d for _, d in DILATED_GROUPS]
    o1, o2, o3 = (_natural_order(ref, d, slabs[2 * g]) for g, (ref, d) in
                  enumerate(zip((o1_ref, o2_ref, o3_ref), dil)))
    l1, l2, l3 = (_natural_order(ref, d, slabs[2 * g + 1]) for g, (ref, d) in
                  enumerate(zip((l1_ref, l2_ref, l3_ref), dil)))
    lm = jnp.maximum(jnp.maximum(l1, l2), l3)
    e1, e2, e3 = jnp.exp(l1 - lm), jnp.exp(l2 - lm), jnp.exp(l3 - lm)
    attn = (e1 * o1 + e2 * o2 + e3 * o3) / (e1 + e2 + e3)
    p_attn = jnp.dot(attn.astype(jnp.bfloat16), wpa_ref[...], preferred_element_type=jnp.float32)

    merged = jax.nn.sigmoid(g_a) * p_attn + jax.nn.sigmoid(g_b) * p_lru
    x1 = x + jnp.dot(merged.astype(jnp.bfloat16), wout_ref[...], preferred_element_type=jnp.float32)
    x1_ref[...] = x1
    h2 = _rmsnorm(x1, n2_ref[...])
    h2_ref[...] = h2

    h2_hi = h2.astype(jnp.bfloat16)
    h2_lo = (h2 - h2_hi.astype(jnp.float32)).astype(jnp.bfloat16)
    logits = (jnp.dot(h2_hi, wrh_ref[...], preferred_element_type=jnp.float32)
              + (jnp.dot(h2_lo, wrh_ref[...], preferred_element_type=jnp.float32)
                 + jnp.dot(h2_hi, wrl_ref[...], preferred_element_type=jnp.float32)))
    lane = lax.broadcasted_iota(jnp.int32, (tm, LANES), 1)
    big = jnp.int32(LANES)
    lowest = jnp.float32(-3.0e38)
    is_g = (lane >= GROUP_LOGIT_LANE0) & (lane < GROUP_LOGIT_LANE0 + N_EXPERT_GROUPS)
    gl = jnp.where(is_g, logits, lowest)
    gmax = jnp.max(gl, axis=-1, keepdims=True)
    gsel = jnp.min(jnp.where(gl == gmax, lane, big), axis=-1, keepdims=True) - GROUP_LOGIT_LANE0
    p_sel = 1.0 / jnp.sum(jnp.where(is_g, jnp.exp(gl - gmax), 0.0), axis=-1, keepdims=True)
    is_e = (lane >= gsel * EXPERTS_PER_GROUP) & (lane < (gsel + 1) * EXPERTS_PER_GROUP)
    el = jnp.where(is_e, logits, lowest)
    v1 = jnp.max(el, axis=-1, keepdims=True)
    i1 = jnp.min(jnp.where(el == v1, lane, big), axis=-1, keepdims=True)
    el2 = jnp.where(lane == i1, lowest, el)
    v2 = jnp.max(el2, axis=-1, keepdims=True)
    i2 = jnp.min(jnp.where(el2 == v2, lane, big), axis=-1, keepdims=True)
    t = jnp.exp(v2 - v1)
    g0 = p_sel / (1.0 + t)
    g1 = p_sel * t / (1.0 + t)

    oh0 = lane == i1
    oh1 = lane == i2
    both = (oh0 | oh1)
    tri = (lax.broadcasted_iota(jnp.int32, (tm, tm), 0)
           > lax.broadcasted_iota(jnp.int32, (tm, tm), 1)).astype(jnp.bfloat16)
    before = jnp.dot(tri, both.astype(jnp.bfloat16), preferred_element_type=jnp.float32) + cnt_sc[...]
    rank0 = jnp.sum(jnp.where(oh0, before, 0.0), axis=-1, keepdims=True)
    rank1 = jnp.sum(jnp.where(oh1, before, 0.0), axis=-1, keepdims=True)
    cnt_sc[...] = cnt_sc[...] + jnp.sum(both.astype(jnp.float32), axis=0, keepdims=True)
    cnt_ref[...] = jnp.broadcast_to(cnt_sc[...], cnt_ref.shape)

    route = jnp.zeros((tm, LANES), jnp.float32)
    for lane_id, val in ((ROUTE_E0, i1.astype(jnp.float32)), (ROUTE_E1, i2.astype(jnp.float32)),
                         (ROUTE_R0, rank0), (ROUTE_R1, rank1), (ROUTE_G0, g0), (ROUTE_G1, g1)):
        route = jnp.where(lane == lane_id, val, route)
    route_ref[...] = route


def _token_mixer(x2, attn_o, attn_lse, weights, B, S):
    T = x2.shape[0]
    tm = MIX_TILE
    nt = S // tm
    row = lambda b, j: (b * nt + j, 0)
    const2 = lambda b, j: (0, 0)
    const3 = lambda b, j: (0, 0, 0)

    def resident(shape):
        idx = const2 if len(shape) == 2 else const3
        return pl.BlockSpec(shape, idx, pipeline_mode=pl.Buffered(1))

    in_specs = [pl.BlockSpec((tm, D_MODEL), row)]
    group_specs = [pl.BlockSpec((None, d, tm // d, GROUP_WIDTH), lambda b, j: (b, 0, j, 0))
                   for _, d in DILATED_GROUPS]
    in_specs += group_specs + group_specs
    in_specs += [resident(w.shape) for w in weights]
    return pl.pallas_call(
        _mixer_kernel,
        grid=(B, nt),
        in_specs=in_specs,
        out_specs=[
            pl.BlockSpec((tm, D_MODEL), row),
            pl.BlockSpec((tm, D_MODEL), row),
            pl.BlockSpec((tm, LANES), row),
            pl.BlockSpec((SUBLANES, LANES), const2),
        ],
        out_shape=[
            jax.ShapeDtypeStruct((T, D_MODEL), jnp.float32),
            jax.ShapeDtypeStruct((T, D_MODEL), jnp.float32),
            jax.ShapeDtypeStruct((T, LANES), jnp.float32),
            jax.ShapeDtypeStruct((SUBLANES, LANES), jnp.float32),
        ],
        scratch_shapes=[
            pltpu.VMEM((tm + 2 * SUBLANES, LRU_WIDTH), jnp.float32),
            pltpu.VMEM((tm, LRU_WIDTH), jnp.float32),
            pltpu.VMEM((tm, LRU_WIDTH), jnp.float32),
            pltpu.VMEM((1, LRU_WIDTH), jnp.float32),
            pltpu.VMEM((1, LANES), jnp.float32),
        ] + [pltpu.VMEM((GROUP_WIDTH // LANES, tm, LANES), jnp.float32)] * (2 * N_GROUPS),
        compiler_params=pltpu.CompilerParams(
            dimension_semantics=("arbitrary", "arbitrary"), vmem_limit_bytes=VMEM_LIMIT),
        name="token_mixer",
    )(x2, *attn_o, *attn_lse, *weights)


def _row_copy(src_hbm, src_row, dst, dst_row, sem):
    return pltpu.make_async_copy(src_hbm.at[pl.ds(src_row, 1), :], dst.at[pl.ds(dst_row, 1), :], sem)


def _dispatch_kernel(dest_ref, h2_ref, xs_in_hbm, xs_hbm, sem):
    del xs_in_hbm
    base = pl.program_id(0) * DISPATCH_TILE
    n_tok = dest_ref.shape[0] // TOP_K

    def issue(t, carry):
        for k in range(TOP_K):
            _row_copy(h2_ref, t, xs_hbm, dest_ref[k * n_tok + base + t], sem).start()
        return carry

    lax.fori_loop(0, DISPATCH_TILE, issue, 0)

    def drain(t, carry):
        for k in range(TOP_K):
            _row_copy(h2_ref, 0, xs_hbm, 0, sem).wait()
        return carry

    lax.fori_loop(0, DISPATCH_TILE, drain, 0)


def _dispatch(dest, h2, n_rows):
    T = h2.shape[0]
    xs0 = jnp.zeros((n_rows, D_MODEL), jnp.float32)
    return pl.pallas_call(
        _dispatch_kernel,
        grid_spec=pltpu.PrefetchScalarGridSpec(
            num_scalar_prefetch=1,
            grid=(T // DISPATCH_TILE,),
            in_specs=[pl.BlockSpec((DISPATCH_TILE, D_MODEL), lambda i, d: (i, 0)),
                      pl.BlockSpec(memory_space=pl.ANY)],
            out_specs=pl.BlockSpec(memory_space=pl.ANY),
            scratch_shapes=[pltpu.SemaphoreType.DMA(())],
        ),
        out_shape=jax.ShapeDtypeStruct((n_rows, D_MODEL), jnp.float32),
        input_output_aliases={2: 0},
        compiler_params=pltpu.CompilerParams(dimension_semantics=("arbitrary",)),
        name="moe_dispatch",
    )(dest, h2, xs0)


def _expert_kernel(be_ref, nv_ref, xs_ref, wgu_ref, wd_ref, ys_ref, wgu_bf, wd_bf):
    i = pl.program_id(0)
    new_expert = (i == 0) | (be_ref[i] != be_ref[jnp.maximum(i - 1, 0)])

    @pl.when(new_expert)
    def _():
        wgu_bf[...] = wgu_ref[...].astype(jnp.bfloat16)
        wd_bf[...] = wd_ref[...].astype(jnp.bfloat16)

    @pl.when(i < nv_ref[0])
    def _():
        gu = jnp.dot(xs_ref[...].astype(jnp.bfloat16), wgu_bf[...], preferred_element_type=jnp.float32)
        g = gu[:, :D_EXPERT]
        u = gu[:, D_EXPERT:]
        act = (g * jax.nn.sigmoid(g) * u).astype(jnp.bfloat16)
        ys_ref[...] = jnp.dot(act, wd_bf[...], preferred_element_type=jnp.float32)

    @pl.when(i >= nv_ref[0])
    def _():
        ys_ref[...] = jnp.zeros_like(ys_ref)


def _experts(block_expert, n_valid, xs, w_gate_up, w_down):
    n_rows = xs.shape[0]
    nblk = n_rows // MOE_TILE
    last_valid = lambda nv: jnp.maximum(nv[0] - 1, 0)
    return pl.pallas_call(
        _expert_kernel,
        grid_spec=pltpu.PrefetchScalarGridSpec(
            num_scalar_prefetch=2,
            grid=(nblk,),
            in_specs=[
                pl.BlockSpec((MOE_TILE, D_MODEL), lambda i, be, nv: (jnp.minimum(i, last_valid(nv)), 0)),
                pl.BlockSpec((None, D_MODEL, 2 * D_EXPERT), lambda i, be, nv: (be[i], 0, 0)),
                pl.BlockSpec((None, D_EXPERT, D_MODEL), lambda i, be, nv: (be[i], 0, 0)),
            ],
            out_specs=pl.BlockSpec((MOE_TILE, D_MODEL), lambda i, be, nv: (i, 0)),
            scratch_shapes=[
                pltpu.VMEM((D_MODEL, 2 * D_EXPERT), jnp.bfloat16),
                pltpu.VMEM((D_EXPERT, D_MODEL), jnp.bfloat16),
            ],
        ),
        out_shape=jax.ShapeDtypeStruct((n_rows, D_MODEL), jnp.float32),
        compiler_params=pltpu.CompilerParams(
            dimension_semantics=("arbitrary",), vmem_limit_bytes=VMEM_LIMIT),
        name="moe_experts",
    )(block_expert, n_valid, xs, w_gate_up, w_down)


def _combine_kernel(dest_ref, ys_hbm, x1_ref, route_ref, nf_ref, out_ref, y0_buf, y1_buf, sems):
    tc = COMBINE_TILE
    base = pl.program_id(0) * tc
    n_tok = dest_ref.shape[0] // TOP_K
    bufs = (y0_buf, y1_buf)

    def issue(t, carry):
        for k in range(TOP_K):
            _row_copy(ys_hbm, dest_ref[k * n_tok + base + t], bufs[k], t, sems.at[k]).start()
        return carry

    lax.fori_loop(0, tc, issue, 0)

    def drain(t, carry):
        for k in range(TOP_K):
            _row_copy(ys_hbm, 0, bufs[k], 0, sems.at[k]).wait()
        return carry

    lax.fori_loop(0, tc, drain, 0)

    route = route_ref[...]
    g0 = route[:, ROUTE_G0:ROUTE_G0 + 1]
    g1 = route[:, ROUTE_G1:ROUTE_G1 + 1]
    x2 = x1_ref[...] + g0 * y0_buf[...] + g1 * y1_buf[...]
    out_ref[...] = _rmsnorm(x2, nf_ref[...])


def _combine(dest, ys, x1, route, norm_f):
    T = x1.shape[0]
    tc = COMBINE_TILE
    return pl.pallas_call(
        _combine_kernel,
        grid_spec=pltpu.PrefetchScalarGridSpec(
            num_scalar_prefetch=1,
            grid=(T // tc,),
            in_specs=[
                pl.BlockSpec(memory_space=pl.ANY),
                pl.BlockSpec((tc, D_MODEL), lambda i, d: (i, 0)),
                pl.BlockSpec((tc, LANES), lambda i, d: (i, 0)),
                pl.BlockSpec((1, D_MODEL), lambda i, d: (0, 0)),
            ],
            out_specs=pl.BlockSpec((tc, D_MODEL), lambda i, d: (i, 0)),
            scratch_shapes=[
                pltpu.VMEM((tc, D_MODEL), jnp.float32),
                pltpu.VMEM((tc, D_MODEL), jnp.float32),
                pltpu.SemaphoreType.DMA((TOP_K,)),
            ],
        ),
        out_shape=jax.ShapeDtypeStruct((T, D_MODEL), jnp.float32),
        compiler_params=pltpu.CompilerParams(
            dimension_semantics=("arbitrary",), vmem_limit_bytes=VMEM_LIMIT),
        name="moe_combine",
    )(dest, ys, x1, route, norm_f)


def _block_diag_gates(w_rg, w_ig):
    def bd(w):
        w4 = w.reshape(N_LRU_CHUNKS, LRU_CHUNK_HEADS, LRU_HEAD_DIM, LRU_HEAD_DIM)
        eye = jnp.eye(LRU_CHUNK_HEADS, dtype=w.dtype)
        return jnp.einsum('chij,hk->chikj', w4, eye).reshape(N_LRU_CHUNKS, LRU_CHUNK, LRU_CHUNK)
    return jnp.concatenate([bd(w_rg), bd(w_ig)], axis=-1).astype(jnp.bfloat16)


def kernel(x, rel_bias, norm1, w_in, conv_w, conv_b, w_rg, b_rg, w_ig, b_ig, lru_lambda,
           w_proj_attn, w_proj_lru, w_out, norm2, w_router_group, w_router_expert,
           w_gate_up, w_down, norm_f):
    B, S, D = x.shape
    T = B * S
    assert w_in.shape[0] == 1, "single-layer block"
    layer = 0
    bf16 = jnp.bfloat16
    x2 = x.reshape(T, D)
    row = lambda v: v[layer].reshape(1, -1)
    w_qkv = (w_in[layer][:, :QKV_WIDTH].reshape(D, 3, N_GROUPS, GROUP_WIDTH)
             .transpose(0, 2, 1, 3).reshape(D, QKV_WIDTH).astype(bf16))
    w_rest = w_in[layer][:, QKV_WIDTH:].astype(bf16)

    qkv_groups = _qkv_projection(x2, row(norm1), w_qkv, B, S)
    attn_o, attn_lse = [], []
    for gi, (window, dilation) in enumerate(DILATED_GROUPS):
        hs = slice(gi * HEADS_PER_GROUP, (gi + 1) * HEADS_PER_GROUP)
        bias = _attention_bias_table(rel_bias[:, hs], window, dilation)
        o, lse = _attention_group(qkv_groups[gi], bias, gi, dilation)
        attn_o.append(o)
        attn_lse.append(lse)

    w_router = jnp.zeros((D, LANES), jnp.float32)
    w_router = w_router.at[:, :N_EXPERTS].set(w_router_expert[layer].astype(jnp.float32))
    w_router = w_router.at[:, GROUP_LOGIT_LANE0:GROUP_LOGIT_LANE0 + N_EXPERT_GROUPS].set(
        w_router_group[layer].astype(jnp.float32))
    w_router_hi = w_router.astype(bf16)
    weights = [
        row(norm1), w_rest, conv_w[layer], row(conv_b),
        _block_diag_gates(w_rg[layer], w_ig[layer]), row(b_rg), row(b_ig), row(lru_lambda),
        w_proj_attn[layer].astype(bf16), w_proj_lru[layer].astype(bf16), w_out[layer].astype(bf16),
        row(norm2), w_router_hi, (w_router - w_router_hi.astype(jnp.float32)).astype(bf16),
    ]
    x1, h2, route, counts = _token_mixer(x2, attn_o, attn_lse, weights, B, S)

    counts = counts[0, :N_EXPERTS].astype(jnp.int32)
    padded = ((counts + MOE_TILE - 1) // MOE_TILE) * MOE_TILE
    pend = jnp.cumsum(padded)
    pstart = pend - padded
    e_sel = route[:, ROUTE_E0:ROUTE_E1 + 1].astype(jnp.int32)
    rank = route[:, ROUTE_R0:ROUTE_R1 + 1].astype(jnp.int32)
    dest = (pstart[e_sel] + rank).T.reshape(TOP_K * T)
    n_rows = TOP_K * T + N_EXPERTS * MOE_TILE
    nblk = n_rows // MOE_TILE
    block_row0 = jnp.arange(nblk, dtype=jnp.int32) * MOE_TILE
    block_expert = jnp.minimum(
        jnp.sum((pend[None, :] <= block_row0[:, None]).astype(jnp.int32), axis=1), N_EXPERTS - 1)
    n_valid = (pend[-1:] // MOE_TILE).astype(jnp.int32)

    xs = _dispatch(dest, h2, n_rows)
    ys = _experts(block_expert, n_valid, xs, w_gate_up[layer], w_down[layer])
    out = _combine(dest, ys, x1, route, norm_f.reshape(1, -1))
    return out.reshape(B, S, D)
```

```python
import math

import jax
import jax.numpy as jnp
from jax import lax
from jax.experimental import pallas as pl
from jax.experimental.pallas import tpu as pltpu

D_MODEL = 1024
HEAD_DIM = 64
HEADS_PER_GROUP = 4
DILATED_GROUPS = ((128, 1), (512, 4), (2048, 16))
N_GROUPS = len(DILATED_GROUPS)
N_ATTN_HEADS = HEADS_PER_GROUP * N_GROUPS
ATTN_WIDTH = N_ATTN_HEADS * HEAD_DIM
GROUP_WIDTH = HEADS_PER_GROUP * HEAD_DIM
QKV_WIDTH = 3 * ATTN_WIDTH
ATTN_BLOCK = 128
N_REL_BUCKETS = 32
REL_MAX_DISTANCE = 2048
NEG_INF = -1e30

LRU_WIDTH = D_MODEL
LRU_HEADS = 16
LRU_HEAD_DIM = LRU_WIDTH // LRU_HEADS
CONV_WIDTH = 4
LRU_C = 8.0
LRU_CHUNK_HEADS = 4
LRU_CHUNK = LRU_CHUNK_HEADS * LRU_HEAD_DIM
N_LRU_CHUNKS = LRU_WIDTH // LRU_CHUNK

N_EXPERT_GROUPS = 4
EXPERTS_PER_GROUP = 8
N_EXPERTS = N_EXPERT_GROUPS * EXPERTS_PER_GROUP
TOP_K = 2
D_EXPERT = 512
EPS = 1e-6

REST_WIDTH = 2 * LRU_WIDTH + 2 * D_MODEL
LANES = 128
SUBLANES = 8

ATTN_TILE = 256
ATTN_QB = 4
MIX_TILE = 256
MOE_TILE = 256
MOE_CHUNK = SUBLANES
MOE_BLOCK_CHUNKS = MOE_TILE // MOE_CHUNK
MOE_SLOTS = 768
assert MOE_SLOTS >= TOP_K * MIX_TILE + N_EXPERTS * (MOE_CHUNK - 1) and MOE_SLOTS % MOE_CHUNK == 0
TILE_CHUNKS = MOE_SLOTS // MOE_CHUNK
VMEM_LIMIT = 56 * 1024 * 1024

ROUTE_S0, ROUTE_S1, ROUTE_G0, ROUTE_G1 = range(4)
GROUP_LOGIT_LANE0 = N_EXPERTS


def _rmsnorm(x, g):
    return x * lax.rsqrt(jnp.mean(x * x, axis=-1, keepdims=True) + EPS) * g


def _residue_major_perm(n, d):
    per = n // d
    out = lax.broadcasted_iota(jnp.int32, (n, n), 0)
    src = lax.broadcasted_iota(jnp.int32, (n, n), 1)
    m = jnp.bitwise_and(out, per - 1)
    r = jnp.right_shift(out, per.bit_length() - 1)
    return (src == m * d + r).astype(jnp.bfloat16)


def _qkv_kernel(x_ref, g_ref, w_ref, *o_refs):
    tm = ATTN_TILE
    h = _rmsnorm(x_ref[...], g_ref[...]).astype(jnp.bfloat16)
    qkv = jnp.dot(h, w_ref[...], preferred_element_type=jnp.float32).astype(jnp.bfloat16)
    for gi, (o_ref, (_, d)) in enumerate(zip(o_refs, DILATED_GROUPS)):
        part = qkv[:, gi * ATTN_WIDTH:(gi + 1) * ATTN_WIDTH]
        if d > 1:
            part = jnp.dot(_residue_major_perm(tm, d), part,
                           preferred_element_type=jnp.float32).astype(jnp.bfloat16)
        o_ref[...] = part.reshape(d, tm // d, ATTN_WIDTH)


def _qkv_projection(x2, norm1, w_qkv, B, S):
    tm = ATTN_TILE
    nt = S // tm
    return pl.pallas_call(
        _qkv_kernel,
        grid=(B, nt),
        in_specs=[
            pl.BlockSpec((tm, D_MODEL), lambda b, j: (b * nt + j, 0)),
            pl.BlockSpec((1, D_MODEL), lambda b, j: (0, 0)),
            pl.BlockSpec((D_MODEL, QKV_WIDTH), lambda b, j: (0, 0)),
        ],
        out_specs=[pl.BlockSpec((None, d, tm // d, ATTN_WIDTH), lambda b, j: (b, 0, j, 0))
                   for _, d in DILATED_GROUPS],
        out_shape=[jax.ShapeDtypeStruct((B, d, S // d, ATTN_WIDTH), jnp.bfloat16)
                   for _, d in DILATED_GROUPS],
        compiler_params=pltpu.CompilerParams(
            dimension_semantics=("parallel", "parallel"), vmem_limit_bytes=VMEM_LIMIT),
        name="qkv_projection",
    )(x2, norm1, w_qkv)


def _attn_kernel(q_ref, kp_ref, kc_ref, vp_ref, vc_ref, bias0_ref, bias_ref, o_ref, lse_ref):
    q = q_ref[...] * (HEAD_DIM ** -0.5)
    k = jnp.concatenate([kp_ref[...], kc_ref[...]], axis=0)
    v = jnp.concatenate([vp_ref[...], vc_ref[...]], axis=0)
    for b in range(ATTN_QB):
        rows = slice(b * ATTN_BLOCK, (b + 1) * ATTN_BLOCK)
        keys = slice(b * ATTN_BLOCK, (b + 2) * ATTN_BLOCK)
        outs, lses = [], []
        for h in range(HEADS_PER_GROUP):
            hs = slice(h * HEAD_DIM, (h + 1) * HEAD_DIM)
            bias = bias0_ref[h] if b == 0 else bias_ref[h]
            s = lax.dot_general(q[rows, hs], k[keys, hs], (((1,), (1,)), ((), ())),
                                preferred_element_type=jnp.float32) + bias
            m = jnp.max(s, axis=-1, keepdims=True)
            p = jnp.exp(s - m)
            l = jnp.sum(p, axis=-1, keepdims=True)
            o = jnp.dot(p.astype(jnp.bfloat16), v[keys, hs], preferred_element_type=jnp.float32) / l
            outs.append(o)
            lses.append(jnp.broadcast_to(m + jnp.log(l), (ATTN_BLOCK, HEAD_DIM)))
        o_ref[rows, :] = jnp.concatenate(outs, axis=1).astype(o_ref.dtype)
        lse_ref[rows, :] = jnp.concatenate(lses, axis=1)


def _attention_group(qkv_g, bias, gi, dilation):
    B, _, L, _ = qkv_g.shape
    rows = ATTN_QB * ATTN_BLOCK
    cur = lambda which: pl.BlockSpec((None, None, rows, GROUP_WIDTH), lambda b, r, n: (b, r, n, which))
    prev = lambda which: pl.BlockSpec((None, None, ATTN_BLOCK, GROUP_WIDTH),
                                      lambda b, r, n: (b, r, jnp.maximum(n * ATTN_QB - 1, 0), which))
    bias_blk = (None, HEADS_PER_GROUP, ATTN_BLOCK, 2 * ATTN_BLOCK)
    return pl.pallas_call(
        _attn_kernel,
        grid=(B, dilation, L // rows),
        in_specs=[
            cur(0), prev(1), cur(1), prev(2), cur(2),
            pl.BlockSpec(bias_blk, lambda b, r, n: (jnp.minimum(n, 1), 0, 0, 0)),
            pl.BlockSpec(bias_blk, lambda b, r, n: (1, 0, 0, 0)),
        ],
        out_specs=[cur(0), cur(0)],
        out_shape=[
            jax.ShapeDtypeStruct((B, dilation, L, GROUP_WIDTH), jnp.bfloat16),
            jax.ShapeDtypeStruct((B, dilation, L, GROUP_WIDTH), jnp.float32),
        ],
        compiler_params=pltpu.CompilerParams(
            dimension_semantics=("parallel", "parallel", "arbitrary"), vmem_limit_bytes=VMEM_LIMIT),
        name=f"dilated_attention_g{gi}",
    )(qkv_g, qkv_g, qkv_g, qkv_g, qkv_g, bias, bias)


def _t5_causal_bucket(dist):
    max_exact = N_REL_BUCKETS // 2
    d_f = jnp.maximum(dist, max_exact).astype(jnp.float32)
    large = max_exact + (jnp.log(d_f / max_exact) / math.log(REL_MAX_DISTANCE / max_exact)
                         * (N_REL_BUCKETS - max_exact)).astype(jnp.int32)
    large = jnp.minimum(large, N_REL_BUCKETS - 1)
    return jnp.where(dist < max_exact, dist, large)


def _attention_bias_table(rel_bias_g, window, dilation):
    nw = window // dilation
    qi = jnp.arange(ATTN_BLOCK)[:, None]
    ki = jnp.arange(2 * ATTN_BLOCK)[None, :]
    dist = ATTN_BLOCK + qi - ki
    band = (dist >= 0) & (dist <= nw)
    bucket = _t5_causal_bucket(jnp.maximum(dist, 0) * dilation)
    onehot = (bucket[None, :, :, None] == jnp.arange(N_REL_BUCKETS)).astype(jnp.float32)
    bias = jnp.sum(onehot * rel_bias_g.astype(jnp.float32).T[:, None, None, :], axis=-1)
    later = jnp.where(band[None], bias, NEG_INF)
    first = jnp.where((band & (ki >= ATTN_BLOCK))[None], bias, NEG_INF)
    return jnp.stack([first, later])


def _gelu_tanh(x):
    return 0.5 * x * (1.0 + jnp.tanh(math.sqrt(2.0 / math.pi) * (x + 0.044715 * (x * x * x))))


def _softplus(x):
    return jnp.maximum(x, 0.0) + jnp.log(1.0 + jnp.exp(-jnp.abs(x)))


def _natural_order(blk_ref, d, slabs):
    if d == 1:
        return blk_ref[0].astype(jnp.float32)
    per = MIX_TILE // d
    for r in range(d):
        rows = blk_ref[r].astype(jnp.float32)
        for c in range(GROUP_WIDTH // LANES):
            slabs[c, pl.ds(r, per, stride=d), :] = rows[:, c * LANES:(c + 1) * LANES]
    return jnp.concatenate([slabs[c] for c in range(GROUP_WIDTH // LANES)], axis=1)


def _mixer_kernel(x_ref, o1_ref, o2_ref, o3_ref, l1_ref, l2_ref, l3_ref,
                  n1_ref, win_ref, cw_ref, cb_ref, wg_ref, brg_ref, big_ref, lam_ref,
                  wpa_ref, wpl_ref, wout_ref, n2_ref, wrh_ref, wrl_ref,
                  x1_ref, xs_ref, route_ref, cnt_ref,
                  xbuf, a_sc, b_sc, h_sc, *slabs):
    tm = MIX_TILE
    first_tile = pl.program_id(1) == 0

    @pl.when(first_tile)
    def _():
        xbuf[0:SUBLANES, :] = jnp.zeros((SUBLANES, LRU_WIDTH), jnp.float32)
        h_sc[...] = jnp.zeros_like(h_sc)

    x = x_ref[...]
    h = _rmsnorm(x, n1_ref[...]).astype(jnp.bfloat16)
    proj = jnp.dot(h, win_ref[...], preferred_element_type=jnp.float32)
    xr = proj[:, 0:LRU_WIDTH]
    g_lru = proj[:, LRU_WIDTH:2 * LRU_WIDTH]
    g_a = proj[:, 2 * LRU_WIDTH:2 * LRU_WIDTH + D_MODEL]
    g_b = proj[:, 2 * LRU_WIDTH + D_MODEL:]

    xbuf[SUBLANES:SUBLANES + tm, :] = xr
    xc = xr * cw_ref[CONV_WIDTH - 1:CONV_WIDTH, :] + cb_ref[...]
    for j in range(CONV_WIDTH - 1):
        back = CONV_WIDTH - 1 - j
        xc = xc + xbuf[SUBLANES - back:SUBLANES - back + tm, :] * cw_ref[j:j + 1, :]
    xbuf[0:SUBLANES, :] = xbuf[tm:tm + SUBLANES, :]

    xcb = xc.astype(jnp.bfloat16)
    r_parts, i_parts = [], []
    for c in range(N_LRU_CHUNKS):
        ri = jnp.dot(xcb[:, c * LRU_CHUNK:(c + 1) * LRU_CHUNK], wg_ref[c],
                     preferred_element_type=jnp.float32)
        r_parts.append(ri[:, :LRU_CHUNK])
        i_parts.append(ri[:, LRU_CHUNK:])
    r = jax.nn.sigmoid(jnp.concatenate(r_parts, axis=1) + brg_ref[...])
    ig = jax.nn.sigmoid(jnp.concatenate(i_parts, axis=1) + big_ref[...])
    log_a = (-LRU_C * _softplus(-lam_ref[...])) * r
    a = jnp.exp(log_a)
    bb = jnp.sqrt(1.0 - a * a) * (ig * xc)

    row8 = lax.broadcasted_iota(jnp.int32, (tm, 1), 0) % SUBLANES
    for k in (1, 2, 4):
        a_s = pltpu.roll(a, k, axis=0)
        b_s = pltpu.roll(bb, k, axis=0)
        take = row8 >= k
        bb = jnp.where(take, a * b_s + bb, bb)
        a = jnp.where(take, a * a_s, a)
    a_sc[...] = a
    b_sc[...] = bb

    def sweep(g, carry):
        rows = pl.ds(pl.multiple_of(g * SUBLANES, SUBLANES), SUBLANES)
        hh = a_sc[rows, :] * carry + b_sc[rows, :]
        b_sc[rows, :] = hh
        return hh[SUBLANES - 1:SUBLANES, :]

    h_sc[...] = lax.fori_loop(0, tm // SUBLANES, sweep, h_sc[...], unroll=4)
    lru = (b_sc[...] * _gelu_tanh(g_lru)).astype(jnp.bfloat16)
    p_lru = jnp.dot(lru, wpl_ref[...], preferred_element_type=jnp.float32)

    dil = [d for _, d in DILATED_GROUPS]
    o1, o2, o3 = (_natural_order(ref, d, slabs[2 * g]) for g, (ref, d) in
                  enumerate(zip((o1_ref, o2_ref, o3_ref), dil)))
    l1, l2, l3 = (_natural_order(ref, d, slabs[2 * g + 1]) for g, (ref, d) in
                  enumerate(zip((l1_ref, l2_ref, l3_ref), dil)))
    lm = jnp.maximum(jnp.maximum(l1, l2), l3)
    e1, e2, e3 = jnp.exp(l1 - lm), jnp.exp(l2 - lm), jnp.exp(l3 - lm)
    attn = (e1 * o1 + e2 * o2 + e3 * o3) / (e1 + e2 + e3)
    p_attn = jnp.dot(attn.astype(jnp.bfloat16), wpa_ref[...], preferred_element_type=jnp.float32)

    merged = jax.nn.sigmoid(g_a) * p_attn + jax.nn.sigmoid(g_b) * p_lru
    x1 = x + jnp.dot(merged.astype(jnp.bfloat16), wout_ref[...], preferred_element_type=jnp.float32)
    x1_ref[...] = x1
    h2 = _rmsnorm(x1, n2_ref[...])

    h2_hi = h2.astype(jnp.bfloat16)
    h2_lo = (h2 - h2_hi.astype(jnp.float32)).astype(jnp.bfloat16)
    logits = (jnp.dot(h2_hi, wrh_ref[...], preferred_element_type=jnp.float32)
              + (jnp.dot(h2_lo, wrh_ref[...], preferred_element_type=jnp.float32)
                 + jnp.dot(h2_hi, wrl_ref[...], preferred_element_type=jnp.float32)))
    lane = lax.broadcasted_iota(jnp.int32, (tm, LANES), 1)
    big = jnp.int32(LANES)
    lowest = jnp.float32(-3.0e38)
    is_g = (lane >= GROUP_LOGIT_LANE0) & (lane < GROUP_LOGIT_LANE0 + N_EXPERT_GROUPS)
    gl = jnp.where(is_g, logits, lowest)
    gmax = jnp.max(gl, axis=-1, keepdims=True)
    gsel = jnp.min(jnp.where(gl == gmax, lane, big), axis=-1, keepdims=True) - GROUP_LOGIT_LANE0
    p_sel = 1.0 / jnp.sum(jnp.where(is_g, jnp.exp(gl - gmax), 0.0), axis=-1, keepdims=True)
    is_e = (lane >= gsel * EXPERTS_PER_GROUP) & (lane < (gsel + 1) * EXPERTS_PER_GROUP)
    el = jnp.where(is_e, logits, lowest)
    v1 = jnp.max(el, axis=-1, keepdims=True)
    i1 = jnp.min(jnp.where(el == v1, lane, big), axis=-1, keepdims=True)
    el2 = jnp.where(lane == i1, lowest, el)
    v2 = jnp.max(el2, axis=-1, keepdims=True)
    i2 = jnp.min(jnp.where(el2 == v2, lane, big), axis=-1, keepdims=True)
    t = jnp.exp(v2 - v1)
    g0 = p_sel / (1.0 + t)
    g1 = p_sel * t / (1.0 + t)

    oh0 = lane == i1
    oh1 = lane == i2
    both = (oh0 | oh1).astype(jnp.float32)
    tri = (lax.broadcasted_iota(jnp.int32, (tm, tm), 0)
           > lax.broadcasted_iota(jnp.int32, (tm, tm), 1)).astype(jnp.bfloat16)
    before = jnp.dot(tri, both.astype(jnp.bfloat16), preferred_element_type=jnp.float32)
    cnt = jnp.sum(both, axis=0, keepdims=True)
    padded = jnp.floor((cnt + (MOE_CHUNK - 1.0)) * (1.0 / MOE_CHUNK)) * MOE_CHUNK
    upper = (lax.broadcasted_iota(jnp.int32, (LANES, LANES), 0)
             < lax.broadcasted_iota(jnp.int32, (LANES, LANES), 1)).astype(jnp.bfloat16)
    loff = jnp.dot(jnp.broadcast_to(padded, (SUBLANES, LANES)).astype(jnp.bfloat16), upper,
                   preferred_element_type=jnp.float32)[0:1, :]
    base = before + loff
    slot0 = jnp.sum(jnp.where(oh0, base, 0.0), axis=-1, keepdims=True)
    slot1 = jnp.sum(jnp.where(oh1, base, 0.0), axis=-1, keepdims=True)
    cnt_ref[...] = jnp.broadcast_to(cnt, cnt_ref.shape)

    slot_id = lax.broadcasted_iota(jnp.int32, (tm, MOE_SLOTS), 1)
    place = ((slot_id == slot0.astype(jnp.int32)) | (slot_id == slot1.astype(jnp.int32)))
    xs_ref[...] = lax.dot_general(place.astype(jnp.bfloat16), h2_hi, (((0,), (0,)), ((), ())),
                                  preferred_element_type=jnp.float32)

    route = jnp.zeros((tm, LANES), jnp.float32)
    for lane_id, val in ((ROUTE_S0, slot0), (ROUTE_S1, slot1), (ROUTE_G0, g0), (ROUTE_G1, g1)):
        route = jnp.where(lane == lane_id, val, route)
    route_ref[...] = route


def _token_mixer(x2, attn_o, attn_lse, weights, B, S):
    T = x2.shape[0]
    tm = MIX_TILE
    nt = S // tm
    row = lambda b, j: (b * nt + j, 0)
    const2 = lambda b, j: (0, 0)
    const3 = lambda b, j: (0, 0, 0)

    def resident(shape):
        idx = const2 if len(shape) == 2 else const3
        return pl.BlockSpec(shape, idx, pipeline_mode=pl.Buffered(1))

    in_specs = [pl.BlockSpec((tm, D_MODEL), row)]
    group_specs = [pl.BlockSpec((None, d, tm // d, GROUP_WIDTH), lambda b, j: (b, 0, j, 0))
                   for _, d in DILATED_GROUPS]
    in_specs += group_specs + group_specs
    in_specs += [resident(w.shape) for w in weights]
    return pl.pallas_call(
        _mixer_kernel,
        grid=(B, nt),
        in_specs=in_specs,
        out_specs=[
            pl.BlockSpec((tm, D_MODEL), row),
            pl.BlockSpec((MOE_SLOTS, D_MODEL), row),
            pl.BlockSpec((tm, LANES), row),
            pl.BlockSpec((None, SUBLANES, LANES), lambda b, j: (b * nt + j, 0, 0)),
        ],
        out_shape=[
            jax.ShapeDtypeStruct((T, D_MODEL), jnp.float32),
            jax.ShapeDtypeStruct((B * nt * MOE_SLOTS, D_MODEL), jnp.float32),
            jax.ShapeDtypeStruct((T, LANES), jnp.float32),
            jax.ShapeDtypeStruct((B * nt, SUBLANES, LANES), jnp.float32),
        ],
        scratch_shapes=[
            pltpu.VMEM((tm + 2 * SUBLANES, LRU_WIDTH), jnp.float32),
            pltpu.VMEM((tm, LRU_WIDTH), jnp.float32),
            pltpu.VMEM((tm, LRU_WIDTH), jnp.float32),
            pltpu.VMEM((1, LRU_WIDTH), jnp.float32),
        ] + [pltpu.VMEM((GROUP_WIDTH // LANES, tm, LANES), jnp.float32)] * (2 * N_GROUPS),
        compiler_params=pltpu.CompilerParams(
            dimension_semantics=("arbitrary", "arbitrary"), vmem_limit_bytes=VMEM_LIMIT),
        name="token_mixer",
    )(x2, *attn_o, *attn_lse, *weights)


def _moe_tables(cnt_tile):
    ntiles = cnt_tile.shape[0]
    nch = (cnt_tile + MOE_CHUNK - 1) // MOE_CHUNK
    lo_c = jnp.cumsum(nch, axis=1) - nch
    n_lc = jnp.sum(nch, axis=1)
    per_expert = jnp.sum(nch, axis=0)
    region = ((per_expert + MOE_BLOCK_CHUNKS - 1) // MOE_BLOCK_CHUNKS) * MOE_BLOCK_CHUNKS
    pend = jnp.cumsum(region)
    glob = (pend - region)[None, :] + jnp.cumsum(nch, axis=0) - nch

    max_rows = TOP_K * ntiles * MIX_TILE + ntiles * N_EXPERTS * (MOE_CHUNK - 1)
    max_chunks = -(-max_rows // MOE_CHUNK) + N_EXPERTS * (MOE_BLOCK_CHUNKS - 1)
    max_blocks = -(-max_chunks // MOE_BLOCK_CHUNKS)
    max_chunks = max_blocks * MOE_BLOCK_CHUNKS

    seg_start = glob.T.reshape(-1)
    seg_src = (jnp.arange(ntiles, dtype=jnp.int32)[:, None] * TILE_CHUNKS + lo_c).T.reshape(-1)
    step = jnp.diff(seg_src - seg_start, prepend=0)
    g = jnp.arange(max_chunks, dtype=jnp.int32)
    src = g + jnp.sum(jnp.where(seg_start[None, :] <= g[:, None], step[None, :], 0), axis=1)
    src = jnp.clip(src, 0, ntiles * TILE_CHUNKS - 1)

    lstep = jnp.diff(glob - lo_c, axis=1, prepend=0)
    lc = jnp.arange(TILE_CHUNKS, dtype=jnp.int32)
    comb = lc[None, :] + jnp.sum(
        jnp.where(lo_c[:, None, :] <= lc[None, :, None], lstep[:, None, :], 0), axis=2)
    comb = jnp.clip(comb, 0, max_chunks - 1).reshape(-1)

    blk0 = jnp.arange(max_blocks, dtype=jnp.int32) * MOE_BLOCK_CHUNKS
    block_expert = jnp.minimum(
        jnp.sum((pend[None, :] <= blk0[:, None]).astype(jnp.int32), axis=1), N_EXPERTS - 1)
    n_valid = pend[-1:] // MOE_BLOCK_CHUNKS
    as_i32 = lambda v: v.astype(jnp.int32)
    return as_i32(src), as_i32(comb), as_i32(n_lc), as_i32(block_expert), as_i32(n_valid)


def _chunk_copy(src_hbm, src_chunk, dst, dst_chunk, sem):
    rows = lambda c: pl.ds(pl.multiple_of(c * MOE_CHUNK, MOE_CHUNK), MOE_CHUNK)
    return pltpu.make_async_copy(src_hbm.at[rows(src_chunk), :], dst.at[rows(dst_chunk), :], sem)


def _expert_kernel(be_ref, nv_ref, src_ref, xs_hbm, wgu_ref, wd_ref, ys_ref, xbuf, wgu_bf, wd_bf, sems):
    i = pl.program_id(0)
    n_valid = nv_ref[0]

    def gather(blk, slot):
        return [_chunk_copy(xs_hbm, src_ref[blk * MOE_BLOCK_CHUNKS + c], xbuf.at[slot], c, sems.at[slot])
                for c in range(MOE_BLOCK_CHUNKS)]

    @pl.when(i == 0)
    def _():
        for cp in gather(0, 0):
            cp.start()

    @pl.when(i + 1 < n_valid)
    def _():
        for cp in gather(i + 1, (i + 1) % 2):
            cp.start()

    new_expert = (i == 0) | (be_ref[i] != be_ref[jnp.maximum(i - 1, 0)])

    @pl.when(new_expert)
    def _():
        wgu_bf[...] = wgu_ref[...].astype(jnp.bfloat16)
        wd_bf[...] = wd_ref[...].astype(jnp.bfloat16)

    @pl.when(i < n_valid)
    def _():
        slot = i % 2
        for cp in gather(i, slot):
            cp.wait()
        gu = jnp.dot(xbuf[slot].astype(jnp.bfloat16), wgu_bf[...], preferred_element_type=jnp.float32)
        g = gu[:, :D_EXPERT]
        u = gu[:, D_EXPERT:]
        act = (g * jax.nn.sigmoid(g) * u).astype(jnp.bfloat16)
        ys_ref[...] = jnp.dot(act, wd_bf[...], preferred_element_type=jnp.float32)

    @pl.when(i >= n_valid)
    def _():
        ys_ref[...] = jnp.zeros_like(ys_ref)


def _experts(block_expert, n_valid, src, xs, w_gate_up, w_down):
    nblk = block_expert.shape[0]
    return pl.pallas_call(
        _expert_kernel,
        grid_spec=pltpu.PrefetchScalarGridSpec(
            num_scalar_prefetch=3,
            grid=(nblk,),
            in_specs=[
                pl.BlockSpec(memory_space=pl.ANY),
                pl.BlockSpec((None, D_MODEL, 2 * D_EXPERT), lambda i, be, nv, src: (be[i], 0, 0)),
                pl.BlockSpec((None, D_EXPERT, D_MODEL), lambda i, be, nv, src: (be[i], 0, 0)),
            ],
            out_specs=pl.BlockSpec((MOE_TILE, D_MODEL), lambda i, be, nv, src: (i, 0)),
            scratch_shapes=[
                pltpu.VMEM((2, MOE_TILE, D_MODEL), jnp.float32),
                pltpu.VMEM((D_MODEL, 2 * D_EXPERT), jnp.bfloat16),
                pltpu.VMEM((D_EXPERT, D_MODEL), jnp.bfloat16),
                pltpu.SemaphoreType.DMA((2,)),
            ],
        ),
        out_shape=jax.ShapeDtypeStruct((nblk * MOE_TILE, D_MODEL), jnp.float32),
        compiler_params=pltpu.CompilerParams(
            dimension_semantics=("arbitrary",), vmem_limit_bytes=VMEM_LIMIT),
        name="moe_experts",
    )(block_expert, n_valid, src, xs, w_gate_up, w_down)


def _combine_kernel(comb_ref, nlc_ref, ys_hbm, x1_ref, route_ref, nf_ref, out_ref, ybuf, sems):
    tc = MIX_TILE
    i = pl.program_id(0)

    def for_each_chunk(tile, slot, act):
        def body(lc, carry):
            act(_chunk_copy(ys_hbm, comb_ref[tile * TILE_CHUNKS + lc], ybuf.at[slot], lc, sems.at[slot]))
            return carry
        lax.fori_loop(0, nlc_ref[tile], body, 0)

    @pl.when(i == 0)
    def _():
        ybuf[...] = jnp.zeros_like(ybuf)
        for_each_chunk(0, 0, lambda cp: cp.start())

    @pl.when(i + 1 < pl.num_programs(0))
    def _():
        for_each_chunk(i + 1, (i + 1) % 2, lambda cp: cp.start())

    slot = i % 2
    for_each_chunk(i, slot, lambda cp: cp.wait())

    y = ybuf[slot].astype(jnp.bfloat16)
    route = route_ref[...]
    slot_id = lax.broadcasted_iota(jnp.int32, (tc, MOE_SLOTS), 1)
    pick = lambda lane: (slot_id == route[:, lane:lane + 1].astype(jnp.int32)).astype(jnp.bfloat16)
    y0 = jnp.dot(pick(ROUTE_S0), y, preferred_element_type=jnp.float32)
    y1 = jnp.dot(pick(ROUTE_S1), y, preferred_element_type=jnp.float32)
    g0 = route[:, ROUTE_G0:ROUTE_G0 + 1]
    g1 = route[:, ROUTE_G1:ROUTE_G1 + 1]
    out_ref[...] = _rmsnorm(x1_ref[...] + g0 * y0 + g1 * y1, nf_ref[...])


def _combine(comb, n_lc, ys, x1, route, norm_f):
    T = x1.shape[0]
    tc = MIX_TILE
    return pl.pallas_call(
        _combine_kernel,
        grid_spec=pltpu.PrefetchScalarGridSpec(
            num_scalar_prefetch=2,
            grid=(T // tc,),
            in_specs=[
                pl.BlockSpec(memory_space=pl.ANY),
                pl.BlockSpec((tc, D_MODEL), lambda i, c, n: (i, 0)),
                pl.BlockSpec((tc, LANES), lambda i, c, n: (i, 0)),
                pl.BlockSpec((1, D_MODEL), lambda i, c, n: (0, 0)),
            ],
            out_specs=pl.BlockSpec((tc, D_MODEL), lambda i, c, n: (i, 0)),
            scratch_shapes=[
                pltpu.VMEM((2, MOE_SLOTS, D_MODEL), jnp.float32),
                pltpu.SemaphoreType.DMA((2,)),
            ],
        ),
        out_shape=jax.ShapeDtypeStruct((T, D_MODEL), jnp.float32),
        compiler_params=pltpu.CompilerParams(
            dimension_semantics=("arbitrary",), vmem_limit_bytes=VMEM_LIMIT),
        name="moe_combine",
    )(comb, n_lc, ys, x1, route, norm_f)


def _block_diag_gates(w_rg, w_ig):
    def bd(w):
        w4 = w.reshape(N_LRU_CHUNKS, LRU_CHUNK_HEADS, LRU_HEAD_DIM, LRU_HEAD_DIM)
        eye = jnp.eye(LRU_CHUNK_HEADS, dtype=w.dtype)
        return jnp.einsum('chij,hk->chikj', w4, eye).reshape(N_LRU_CHUNKS, LRU_CHUNK, LRU_CHUNK)
    return jnp.concatenate([bd(w_rg), bd(w_ig)], axis=-1).astype(jnp.bfloat16)


def kernel(x, rel_bias, norm1, w_in, conv_w, conv_b, w_rg, b_rg, w_ig, b_ig, lru_lambda,
           w_proj_attn, w_proj_lru, w_out, norm2, w_router_group, w_router_expert,
           w_gate_up, w_down, norm_f):
    B, S, D = x.shape
    T = B * S
    assert w_in.shape[0] == 1, "single-layer block"
    layer = 0
    bf16 = jnp.bfloat16
    x2 = x.reshape(T, D)
    row = lambda v: v[layer].reshape(1, -1)
    w_qkv = (w_in[layer][:, :QKV_WIDTH].reshape(D, 3, N_GROUPS, GROUP_WIDTH)
             .transpose(0, 2, 1, 3).reshape(D, QKV_WIDTH).astype(bf16))
    w_rest = w_in[layer][:, QKV_WIDTH:].astype(bf16)

    qkv_groups = _qkv_projection(x2, row(norm1), w_qkv, B, S)
    attn_o, attn_lse = [], []
    for gi, (window, dilation) in enumerate(DILATED_GROUPS):
        hs = slice(gi * HEADS_PER_GROUP, (gi + 1) * HEADS_PER_GROUP)
        bias = _attention_bias_table(rel_bias[:, hs], window, dilation)
        o, lse = _attention_group(qkv_groups[gi], bias, gi, dilation)
        attn_o.append(o)
        attn_lse.append(lse)

    w_router = jnp.zeros((D, LANES), jnp.float32)
    w_router = w_router.at[:, :N_EXPERTS].set(w_router_expert[layer].astype(jnp.float32))
    w_router = w_router.at[:, GROUP_LOGIT_LANE0:GROUP_LOGIT_LANE0 + N_EXPERT_GROUPS].set(
        w_router_group[layer].astype(jnp.float32))
    w_router_hi = w_router.astype(bf16)
    weights = [
        row(norm1), w_rest, conv_w[layer], row(conv_b),
        _block_diag_gates(w_rg[layer], w_ig[layer]), row(b_rg), row(b_ig), row(lru_lambda),
        w_proj_attn[layer].astype(bf16), w_proj_lru[layer].astype(bf16), w_out[layer].astype(bf16),
        row(norm2), w_router_hi, (w_router - w_router_hi.astype(jnp.float32)).astype(bf16),
    ]
    x1, xs, route, counts = _token_mixer(x2, attn_o, attn_lse, weights, B, S)

    src, comb, n_lc, block_expert, n_valid = _moe_tables(counts[:, 0, :N_EXPERTS].astype(jnp.int32))
    ys = _experts(block_expert, n_valid, src, xs, w_gate_up[layer], w_down[layer])
    out = _combine(comb, n_lc, ys, x1, route, norm_f.reshape(1, -1))
    return out.reshape(B, S, D)
```

```python
import math

import jax
import jax.numpy as jnp
from jax import lax
from jax.experimental import pallas as pl
from jax.experimental.pallas import tpu as pltpu

D_MODEL = 1024
HEAD_DIM = 64
HEADS_PER_GROUP = 4
DILATED_GROUPS = ((128, 1), (512, 4), (2048, 16))
N_GROUPS = len(DILATED_GROUPS)
N_ATTN_HEADS = HEADS_PER_GROUP * N_GROUPS
ATTN_WIDTH = N_ATTN_HEADS * HEAD_DIM
GROUP_WIDTH = HEADS_PER_GROUP * HEAD_DIM
QKV_WIDTH = 3 * ATTN_WIDTH
ATTN_BLOCK = 128
N_REL_BUCKETS = 32
REL_MAX_DISTANCE = 2048
NEG_INF = -1e30

LRU_WIDTH = D_MODEL
LRU_HEADS = 16
LRU_HEAD_DIM = LRU_WIDTH // LRU_HEADS
CONV_WIDTH = 4
LRU_C = 8.0
LRU_CHUNK_HEADS = 4
LRU_CHUNK = LRU_CHUNK_HEADS * LRU_HEAD_DIM
N_LRU_CHUNKS = LRU_WIDTH // LRU_CHUNK

N_EXPERT_GROUPS = 4
EXPERTS_PER_GROUP = 8
N_EXPERTS = N_EXPERT_GROUPS * EXPERTS_PER_GROUP
TOP_K = 2
D_EXPERT = 512
EPS = 1e-6

REST_WIDTH = 2 * LRU_WIDTH + 2 * D_MODEL
LANES = 128
SUBLANES = 8

ATTN_TILE = 256
ATTN_QB = 4
MIX_TILE = 256
MOE_TILE = 256
MOE_CHUNK = SUBLANES
MOE_BLOCK_CHUNKS = MOE_TILE // MOE_CHUNK
MOE_SLOTS = 768
assert MOE_SLOTS >= TOP_K * MIX_TILE + N_EXPERTS * (MOE_CHUNK - 1) and MOE_SLOTS % MOE_CHUNK == 0
TILE_CHUNKS = MOE_SLOTS // MOE_CHUNK
VMEM_LIMIT = 56 * 1024 * 1024

ROUTE_S0, ROUTE_S1, ROUTE_G0, ROUTE_G1 = range(4)
GROUP_LOGIT_LANE0 = N_EXPERTS


def _rmsnorm(x, g):
    return x * lax.rsqrt(jnp.mean(x * x, axis=-1, keepdims=True) + EPS) * g


def _pack_bf16_pairs(v):
    w = v.shape[1] // 2
    bits = lax.bitcast_convert_type(v, jnp.uint32)
    return bits[:, w:] | (bits[:, :w] >> 16)


def _unpack_bf16_pairs(p):
    lo = lax.bitcast_convert_type(p << 16, jnp.float32)
    hi = lax.bitcast_convert_type(p & jnp.uint32(0xFFFF0000), jnp.float32)
    return lo.astype(jnp.bfloat16), hi.astype(jnp.bfloat16)


def _residue_major_perm(n, d):
    per = n // d
    out = lax.broadcasted_iota(jnp.int32, (n, n), 0)
    src = lax.broadcasted_iota(jnp.int32, (n, n), 1)
    m = jnp.bitwise_and(out, per - 1)
    r = jnp.right_shift(out, per.bit_length() - 1)
    return (src == m * d + r).astype(jnp.bfloat16)


def _qkv_kernel(x_ref, g_ref, w_ref, *o_refs):
    tm = ATTN_TILE
    h = _rmsnorm(x_ref[...], g_ref[...]).astype(jnp.bfloat16)
    qkv = jnp.dot(h, w_ref[...], preferred_element_type=jnp.float32).astype(jnp.bfloat16)
    for gi, (o_ref, (_, d)) in enumerate(zip(o_refs, DILATED_GROUPS)):
        part = qkv[:, gi * ATTN_WIDTH:(gi + 1) * ATTN_WIDTH]
        if d > 1:
            part = jnp.dot(_residue_major_perm(tm, d), part,
                           preferred_element_type=jnp.float32).astype(jnp.bfloat16)
        o_ref[...] = part.reshape(d, tm // d, ATTN_WIDTH)


def _qkv_projection(x2, norm1, w_qkv, B, S):
    tm = ATTN_TILE
    nt = S // tm
    return pl.pallas_call(
        _qkv_kernel,
        grid=(B, nt),
        in_specs=[
            pl.BlockSpec((tm, D_MODEL), lambda b, j: (b * nt + j, 0)),
            pl.BlockSpec((1, D_MODEL), lambda b, j: (0, 0)),
            pl.BlockSpec((D_MODEL, QKV_WIDTH), lambda b, j: (0, 0)),
        ],
        out_specs=[pl.BlockSpec((None, d, tm // d, ATTN_WIDTH), lambda b, j: (b, 0, j, 0))
                   for _, d in DILATED_GROUPS],
        out_shape=[jax.ShapeDtypeStruct((B, d, S // d, ATTN_WIDTH), jnp.bfloat16)
                   for _, d in DILATED_GROUPS],
        compiler_params=pltpu.CompilerParams(
            dimension_semantics=("parallel", "parallel"), vmem_limit_bytes=VMEM_LIMIT),
        name="qkv_projection",
    )(x2, norm1, w_qkv)


def _attn_kernel(q_ref, kp_ref, kc_ref, vp_ref, vc_ref, bias0_ref, bias_ref, o_ref, lse_ref):
    q = q_ref[...] * (HEAD_DIM ** -0.5)
    k = jnp.concatenate([kp_ref[...], kc_ref[...]], axis=0)
    v = jnp.concatenate([vp_ref[...], vc_ref[...]], axis=0)
    for b in range(ATTN_QB):
        rows = slice(b * ATTN_BLOCK, (b + 1) * ATTN_BLOCK)
        keys = slice(b * ATTN_BLOCK, (b + 2) * ATTN_BLOCK)
        outs, lses = [], []
        for h in range(HEADS_PER_GROUP):
            hs = slice(h * HEAD_DIM, (h + 1) * HEAD_DIM)
            bias = bias0_ref[h] if b == 0 else bias_ref[h]
            s = lax.dot_general(q[rows, hs], k[keys, hs], (((1,), (1,)), ((), ())),
                                preferred_element_type=jnp.float32) + bias
            m = jnp.max(s, axis=-1, keepdims=True)
            p = jnp.exp(s - m)
            l = jnp.sum(p, axis=-1, keepdims=True)
            o = jnp.dot(p.astype(jnp.bfloat16), v[keys, hs], preferred_element_type=jnp.float32) / l
            outs.append(o)
            lses.append(jnp.broadcast_to(m + jnp.log(l), (ATTN_BLOCK, HEAD_DIM)))
        o_ref[rows, :] = jnp.concatenate(outs, axis=1).astype(o_ref.dtype)
        lse_ref[rows, :] = jnp.concatenate(lses, axis=1)


def _attention_group(qkv_g, bias, gi, dilation):
    B, _, L, _ = qkv_g.shape
    rows = ATTN_QB * ATTN_BLOCK
    cur = lambda which: pl.BlockSpec((None, None, rows, GROUP_WIDTH), lambda b, r, n: (b, r, n, which))
    prev = lambda which: pl.BlockSpec((None, None, ATTN_BLOCK, GROUP_WIDTH),
                                      lambda b, r, n: (b, r, jnp.maximum(n * ATTN_QB - 1, 0), which))
    bias_blk = (None, HEADS_PER_GROUP, ATTN_BLOCK, 2 * ATTN_BLOCK)
    return pl.pallas_call(
        _attn_kernel,
        grid=(B, dilation, L // rows),
        in_specs=[
            cur(0), prev(1), cur(1), prev(2), cur(2),
            pl.BlockSpec(bias_blk, lambda b, r, n: (jnp.minimum(n, 1), 0, 0, 0)),
            pl.BlockSpec(bias_blk, lambda b, r, n: (1, 0, 0, 0)),
        ],
        out_specs=[cur(0), cur(0)],
        out_shape=[
            jax.ShapeDtypeStruct((B, dilation, L, GROUP_WIDTH), jnp.bfloat16),
            jax.ShapeDtypeStruct((B, dilation, L, GROUP_WIDTH), jnp.float32),
        ],
        compiler_params=pltpu.CompilerParams(
            dimension_semantics=("parallel", "parallel", "arbitrary"), vmem_limit_bytes=VMEM_LIMIT),
        name=f"dilated_attention_g{gi}",
    )(qkv_g, qkv_g, qkv_g, qkv_g, qkv_g, bias, bias)


def _t5_causal_bucket(dist):
    max_exact = N_REL_BUCKETS // 2
    d_f = jnp.maximum(dist, max_exact).astype(jnp.float32)
    large = max_exact + (jnp.log(d_f / max_exact) / math.log(REL_MAX_DISTANCE / max_exact)
                         * (N_REL_BUCKETS - max_exact)).astype(jnp.int32)
    large = jnp.minimum(large, N_REL_BUCKETS - 1)
    return jnp.where(dist < max_exact, dist, large)


def _attention_bias_table(rel_bias_g, window, dilation):
    nw = window // dilation
    qi = jnp.arange(ATTN_BLOCK)[:, None]
    ki = jnp.arange(2 * ATTN_BLOCK)[None, :]
    dist = ATTN_BLOCK + qi - ki
    band = (dist >= 0) & (dist <= nw)
    bucket = _t5_causal_bucket(jnp.maximum(dist, 0) * dilation)
    onehot = (bucket[None, :, :, None] == jnp.arange(N_REL_BUCKETS)).astype(jnp.float32)
    bias = jnp.sum(onehot * rel_bias_g.astype(jnp.float32).T[:, None, None, :], axis=-1)
    later = jnp.where(band[None], bias, NEG_INF)
    first = jnp.where((band & (ki >= ATTN_BLOCK))[None], bias, NEG_INF)
    return jnp.stack([first, later])


def _gelu_tanh(x):
    return 0.5 * x * (1.0 + jnp.tanh(math.sqrt(2.0 / math.pi) * (x + 0.044715 * (x * x * x))))


def _softplus(x):
    return jnp.maximum(x, 0.0) + jnp.log(1.0 + jnp.exp(-jnp.abs(x)))


def _natural_order(blk_ref, d, slabs):
    if d == 1:
        return blk_ref[0].astype(jnp.float32)
    per = MIX_TILE // d
    for r in range(d):
        rows = blk_ref[r].astype(jnp.float32)
        for c in range(GROUP_WIDTH // LANES):
            slabs[c, pl.ds(r, per, stride=d), :] = rows[:, c * LANES:(c + 1) * LANES]
    return jnp.concatenate([slabs[c] for c in range(GROUP_WIDTH // LANES)], axis=1)


def _mixer_kernel(x_ref, o1_ref, o2_ref, o3_ref, l1_ref, l2_ref, l3_ref,
                  n1_ref, win_ref, cw_ref, cb_ref, wg_ref, brg_ref, big_ref, lam_ref,
                  wpa_ref, wpl_ref, wout_ref, n2_ref, wrh_ref, wrl_ref,
                  x1_ref, xs_ref, route_ref, cnt_ref,
                  xbuf, a_sc, b_sc, h_sc, *slabs):
    tm = MIX_TILE
    first_tile = pl.program_id(1) == 0

    @pl.when(first_tile)
    def _():
        xbuf[0:SUBLANES, :] = jnp.zeros((SUBLANES, LRU_WIDTH), jnp.float32)
        h_sc[...] = jnp.zeros_like(h_sc)

    x = x_ref[...]
    h = _rmsnorm(x, n1_ref[...]).astype(jnp.bfloat16)
    proj = jnp.dot(h, win_ref[...], preferred_element_type=jnp.float32)
    xr = proj[:, 0:LRU_WIDTH]
    g_lru = proj[:, LRU_WIDTH:2 * LRU_WIDTH]
    g_a = proj[:, 2 * LRU_WIDTH:2 * LRU_WIDTH + D_MODEL]
    g_b = proj[:, 2 * LRU_WIDTH + D_MODEL:]

    xbuf[SUBLANES:SUBLANES + tm, :] = xr
    xc = xr * cw_ref[CONV_WIDTH - 1:CONV_WIDTH, :] + cb_ref[...]
    for j in range(CONV_WIDTH - 1):
        back = CONV_WIDTH - 1 - j
        xc = xc + xbuf[SUBLANES - back:SUBLANES - back + tm, :] * cw_ref[j:j + 1, :]
    xbuf[0:SUBLANES, :] = xbuf[tm:tm + SUBLANES, :]

    xcb = xc.astype(jnp.bfloat16)
    r_parts, i_parts = [], []
    for c in range(N_LRU_CHUNKS):
        ri = jnp.dot(xcb[:, c * LRU_CHUNK:(c + 1) * LRU_CHUNK], wg_ref[c],
                     preferred_element_type=jnp.float32)
        r_parts.append(ri[:, :LRU_CHUNK])
        i_parts.append(ri[:, LRU_CHUNK:])
    r = jax.nn.sigmoid(jnp.concatenate(r_parts, axis=1) + brg_ref[...])
    ig = jax.nn.sigmoid(jnp.concatenate(i_parts, axis=1) + big_ref[...])
    log_a = (-LRU_C * _softplus(-lam_ref[...])) * r
    a = jnp.exp(log_a)
    bb = jnp.sqrt(1.0 - a * a) * (ig * xc)

    row8 = lax.broadcasted_iota(jnp.int32, (tm, 1), 0) % SUBLANES
    for k in (1, 2, 4):
        a_s = pltpu.roll(a, k, axis=0)
        b_s = pltpu.roll(bb, k, axis=0)
        take = row8 >= k
        bb = jnp.where(take, a * b_s + bb, bb)
        a = jnp.where(take, a * a_s, a)
    a_sc[...] = a
    b_sc[...] = bb

    def sweep(g, carry):
        rows = pl.ds(pl.multiple_of(g * SUBLANES, SUBLANES), SUBLANES)
        hh = a_sc[rows, :] * carry + b_sc[rows, :]
        b_sc[rows, :] = hh
        return hh[SUBLANES - 1:SUBLANES, :]

    h_sc[...] = lax.fori_loop(0, tm // SUBLANES, sweep, h_sc[...], unroll=4)
    lru = (b_sc[...] * _gelu_tanh(g_lru)).astype(jnp.bfloat16)
    p_lru = jnp.dot(lru, wpl_ref[...], preferred_element_type=jnp.float32)

    dil = [d for _, d in DILATED_GROUPS]
    o1, o2, o3 = (_natural_order(ref, d, slabs[2 * g]) for g, (ref, d) in
                  enumerate(zip((o1_ref, o2_ref, o3_ref), dil)))
    l1, l2, l3 = (_natural_order(ref, d, slabs[2 * g + 1]) for g, (ref, d) in
                  enumerate(zip((l1_ref, l2_ref, l3_ref), dil)))
    lm = jnp.maximum(jnp.maximum(l1, l2), l3)
    e1, e2, e3 = jnp.exp(l1 - lm), jnp.exp(l2 - lm), jnp.exp(l3 - lm)
    attn = (e1 * o1 + e2 * o2 + e3 * o3) / (e1 + e2 + e3)
    p_attn = jnp.dot(attn.astype(jnp.bfloat16), wpa_ref[...], preferred_element_type=jnp.float32)

    merged = jax.nn.sigmoid(g_a) * p_attn + jax.nn.sigmoid(g_b) * p_lru
    x1 = x + jnp.dot(merged.astype(jnp.bfloat16), wout_ref[...], preferred_element_type=jnp.float32)
    x1_ref[...] = x1
    h2 = _rmsnorm(x1, n2_ref[...])

    h2_hi = h2.astype(jnp.bfloat16)
    h2_lo = (h2 - h2_hi.astype(jnp.float32)).astype(jnp.bfloat16)
    logits = (jnp.dot(h2_hi, wrh_ref[...], preferred_element_type=jnp.float32)
              + (jnp.dot(h2_lo, wrh_ref[...], preferred_element_type=jnp.float32)
                 + jnp.dot(h2_hi, wrl_ref[...], preferred_element_type=jnp.float32)))
    lane = lax.broadcasted_iota(jnp.int32, (tm, LANES), 1)
    big = jnp.int32(LANES)
    lowest = jnp.float32(-3.0e38)
    is_g = (lane >= GROUP_LOGIT_LANE0) & (lane < GROUP_LOGIT_LANE0 + N_EXPERT_GROUPS)
    gl = jnp.where(is_g, logits, lowest)
    gmax = jnp.max(gl, axis=-1, keepdims=True)
    gsel = jnp.min(jnp.where(gl == gmax, lane, big), axis=-1, keepdims=True) - GROUP_LOGIT_LANE0
    p_sel = 1.0 / jnp.sum(jnp.where(is_g, jnp.exp(gl - gmax), 0.0), axis=-1, keepdims=True)
    is_e = (lane >= gsel * EXPERTS_PER_GROUP) & (lane < (gsel + 1) * EXPERTS_PER_GROUP)
    el = jnp.where(is_e, logits, lowest)
    v1 = jnp.max(el, axis=-1, keepdims=True)
    i1 = jnp.min(jnp.where(el == v1, lane, big), axis=-1, keepdims=True)
    el2 = jnp.where(lane == i1, lowest, el)
    v2 = jnp.max(el2, axis=-1, keepdims=True)
    i2 = jnp.min(jnp.where(el2 == v2, lane, big), axis=-1, keepdims=True)
    t = jnp.exp(v2 - v1)
    g0 = p_sel / (1.0 + t)
    g1 = p_sel * t / (1.0 + t)

    oh0 = lane == i1
    oh1 = lane == i2
    both = (oh0 | oh1).astype(jnp.float32)
    tri = (lax.broadcasted_iota(jnp.int32, (tm, tm), 0)
           > lax.broadcasted_iota(jnp.int32, (tm, tm), 1)).astype(jnp.bfloat16)
    before = jnp.dot(tri, both.astype(jnp.bfloat16), preferred_element_type=jnp.float32)
    cnt = jnp.sum(both, axis=0, keepdims=True)
    padded = jnp.floor((cnt + (MOE_CHUNK - 1.0)) * (1.0 / MOE_CHUNK)) * MOE_CHUNK
    upper = (lax.broadcasted_iota(jnp.int32, (LANES, LANES), 0)
             < lax.broadcasted_iota(jnp.int32, (LANES, LANES), 1)).astype(jnp.bfloat16)
    loff = jnp.dot(jnp.broadcast_to(padded, (SUBLANES, LANES)).astype(jnp.bfloat16), upper,
                   preferred_element_type=jnp.float32)[0:1, :]
    base = before + loff
    slot0 = jnp.sum(jnp.where(oh0, base, 0.0), axis=-1, keepdims=True)
    slot1 = jnp.sum(jnp.where(oh1, base, 0.0), axis=-1, keepdims=True)
    cnt_ref[...] = jnp.broadcast_to(cnt, cnt_ref.shape)

    slot_id = lax.broadcasted_iota(jnp.int32, (tm, MOE_SLOTS), 1)
    place = ((slot_id == slot0.astype(jnp.int32)) | (slot_id == slot1.astype(jnp.int32)))
    xs_ref[...] = _pack_bf16_pairs(
        lax.dot_general(place.astype(jnp.bfloat16), h2_hi, (((0,), (0,)), ((), ())),
                        preferred_element_type=jnp.float32))

    route = jnp.zeros((tm, LANES), jnp.float32)
    for lane_id, val in ((ROUTE_S0, slot0), (ROUTE_S1, slot1), (ROUTE_G0, g0), (ROUTE_G1, g1)):
        route = jnp.where(lane == lane_id, val, route)
    route_ref[...] = route


def _token_mixer(x2, attn_o, attn_lse, weights, B, S):
    T = x2.shape[0]
    tm = MIX_TILE
    nt = S // tm
    row = lambda b, j: (b * nt + j, 0)
    const2 = lambda b, j: (0, 0)
    const3 = lambda b, j: (0, 0, 0)

    def resident(shape):
        idx = const2 if len(shape) == 2 else const3
        return pl.BlockSpec(shape, idx, pipeline_mode=pl.Buffered(1))

    in_specs = [pl.BlockSpec((tm, D_MODEL), row)]
    group_specs = [pl.BlockSpec((None, d, tm // d, GROUP_WIDTH), lambda b, j: (b, 0, j, 0))
                   for _, d in DILATED_GROUPS]
    in_specs += group_specs + group_specs
    in_specs += [resident(w.shape) for w in weights]
    return pl.pallas_call(
        _mixer_kernel,
        grid=(B, nt),
        in_specs=in_specs,
        out_specs=[
            pl.BlockSpec((tm, D_MODEL), row),
            pl.BlockSpec((MOE_SLOTS, D_MODEL // 2), row),
            pl.BlockSpec((tm, LANES), row),
            pl.BlockSpec((None, SUBLANES, LANES), lambda b, j: (b * nt + j, 0, 0)),
        ],
        out_shape=[
            jax.ShapeDtypeStruct((T, D_MODEL), jnp.float32),
            jax.ShapeDtypeStruct((B * nt * MOE_SLOTS, D_MODEL // 2), jnp.uint32),
            jax.ShapeDtypeStruct((T, LANES), jnp.float32),
            jax.ShapeDtypeStruct((B * nt, SUBLANES, LANES), jnp.float32),
        ],
        scratch_shapes=[
            pltpu.VMEM((tm + 2 * SUBLANES, LRU_WIDTH), jnp.float32),
            pltpu.VMEM((tm, LRU_WIDTH), jnp.float32),
            pltpu.VMEM((tm, LRU_WIDTH), jnp.float32),
            pltpu.VMEM((1, LRU_WIDTH), jnp.float32),
        ] + [pltpu.VMEM((GROUP_WIDTH // LANES, tm, LANES), jnp.float32)] * (2 * N_GROUPS),
        compiler_params=pltpu.CompilerParams(
            dimension_semantics=("arbitrary", "arbitrary"), vmem_limit_bytes=VMEM_LIMIT),
        name="token_mixer",
    )(x2, *attn_o, *attn_lse, *weights)


def _moe_tables(cnt_tile):
    ntiles = cnt_tile.shape[0]
    nch = (cnt_tile + MOE_CHUNK - 1) // MOE_CHUNK
    lo_c = jnp.cumsum(nch, axis=1) - nch
    n_lc = jnp.sum(nch, axis=1)
    per_expert = jnp.sum(nch, axis=0)
    region = ((per_expert + MOE_BLOCK_CHUNKS - 1) // MOE_BLOCK_CHUNKS) * MOE_BLOCK_CHUNKS
    pend = jnp.cumsum(region)
    glob = (pend - region)[None, :] + jnp.cumsum(nch, axis=0) - nch

    max_rows = TOP_K * ntiles * MIX_TILE + ntiles * N_EXPERTS * (MOE_CHUNK - 1)
    max_chunks = -(-max_rows // MOE_CHUNK) + N_EXPERTS * (MOE_BLOCK_CHUNKS - 1)
    max_blocks = -(-max_chunks // MOE_BLOCK_CHUNKS)
    max_chunks = max_blocks * MOE_BLOCK_CHUNKS

    seg_start = glob.T.reshape(-1)
    seg_src = (jnp.arange(ntiles, dtype=jnp.int32)[:, None] * TILE_CHUNKS + lo_c).T.reshape(-1)
    step = jnp.diff(seg_src - seg_start, prepend=0)
    g = jnp.arange(max_chunks, dtype=jnp.int32)
    src = g + jnp.sum(jnp.where(seg_start[None, :] <= g[:, None], step[None, :], 0), axis=1)
    src = jnp.clip(src, 0, ntiles * TILE_CHUNKS - 1)

    lstep = jnp.diff(glob - lo_c, axis=1, prepend=0)
    lc = jnp.arange(TILE_CHUNKS, dtype=jnp.int32)
    comb = lc[None, :] + jnp.sum(
        jnp.where(lo_c[:, None, :] <= lc[None, :, None], lstep[:, None, :], 0), axis=2)
    comb = jnp.clip(comb, 0, max_chunks - 1).reshape(-1)

    blk0 = jnp.arange(max_blocks, dtype=jnp.int32) * MOE_BLOCK_CHUNKS
    block_expert = jnp.minimum(
        jnp.sum((pend[None, :] <= blk0[:, None]).astype(jnp.int32), axis=1), N_EXPERTS - 1)
    n_valid = pend[-1:] // MOE_BLOCK_CHUNKS
    as_i32 = lambda v: v.astype(jnp.int32)
    return as_i32(src), as_i32(comb), as_i32(n_lc), as_i32(block_expert), as_i32(n_valid)


def _chunk_copy(src_hbm, src_chunk, dst, dst_chunk, sem):
    rows = lambda c: pl.ds(pl.multiple_of(c * MOE_CHUNK, MOE_CHUNK), MOE_CHUNK)
    return pltpu.make_async_copy(src_hbm.at[rows(src_chunk), :], dst.at[rows(dst_chunk), :], sem)


def _expert_kernel(be_ref, nv_ref, src_ref, xs_hbm, wgu_ref, wd_ref, ys_ref, xbuf, wgu_bf, wd_bf, sems):
    i = pl.program_id(0)
    n_valid = nv_ref[0]

    def gather(blk, slot):
        return [_chunk_copy(xs_hbm, src_ref[blk * MOE_BLOCK_CHUNKS + c], xbuf.at[slot], c, sems.at[slot])
                for c in range(MOE_BLOCK_CHUNKS)]

    @pl.when(i == 0)
    def _():
        for cp in gather(0, 0):
            cp.start()

    @pl.when(i + 1 < n_valid)
    def _():
        for cp in gather(i + 1, (i + 1) % 2):
            cp.start()

    new_expert = (i == 0) | (be_ref[i] != be_ref[jnp.maximum(i - 1, 0)])

    @pl.when(new_expert)
    def _():
        wgu_bf[...] = wgu_ref[...].astype(jnp.bfloat16)
        wd_bf[...] = wd_ref[...].astype(jnp.bfloat16)

    @pl.when(i < n_valid)
    def _():
        slot = i % 2
        for cp in gather(i, slot):
            cp.wait()
        x_lo, x_hi = _unpack_bf16_pairs(xbuf[slot])
        half = D_MODEL // 2
        gu = (jnp.dot(x_lo, wgu_bf[:half, :], preferred_element_type=jnp.float32)
              + jnp.dot(x_hi, wgu_bf[half:, :], preferred_element_type=jnp.float32))
        g = gu[:, :D_EXPERT]
        u = gu[:, D_EXPERT:]
        act = (g * jax.nn.sigmoid(g) * u).astype(jnp.bfloat16)
        y = jnp.dot(act, wd_bf[...], preferred_element_type=jnp.float32)
        ys_ref[...] = _pack_bf16_pairs(y.astype(jnp.bfloat16).astype(jnp.float32))

    @pl.when(i >= n_valid)
    def _():
        ys_ref[...] = jnp.zeros_like(ys_ref)


def _experts(block_expert, n_valid, src, xs, w_gate_up, w_down):
    nblk = block_expert.shape[0]
    return pl.pallas_call(
        _expert_kernel,
        grid_spec=pltpu.PrefetchScalarGridSpec(
            num_scalar_prefetch=3,
            grid=(nblk,),
            in_specs=[
                pl.BlockSpec(memory_space=pl.ANY),
                pl.BlockSpec((None, D_MODEL, 2 * D_EXPERT), lambda i, be, nv, src: (be[i], 0, 0)),
                pl.BlockSpec((None, D_EXPERT, D_MODEL), lambda i, be, nv, src: (be[i], 0, 0)),
            ],
            out_specs=pl.BlockSpec((MOE_TILE, D_MODEL // 2), lambda i, be, nv, src: (i, 0)),
            scratch_shapes=[
                pltpu.VMEM((2, MOE_TILE, D_MODEL // 2), jnp.uint32),
                pltpu.VMEM((D_MODEL, 2 * D_EXPERT), jnp.bfloat16),
                pltpu.VMEM((D_EXPERT, D_MODEL), jnp.bfloat16),
                pltpu.SemaphoreType.DMA((2,)),
            ],
        ),
        out_shape=jax.ShapeDtypeStruct((nblk * MOE_TILE, D_MODEL // 2), jnp.uint32),
        compiler_params=pltpu.CompilerParams(
            dimension_semantics=("arbitrary",), vmem_limit_bytes=VMEM_LIMIT),
        name="moe_experts",
    )(block_expert, n_valid, src, xs, w_gate_up, w_down)


def _combine_kernel(comb_ref, nlc_ref, ys_hbm, x1_ref, route_ref, nf_ref, out_ref, ybuf, sems):
    tc = MIX_TILE
    i = pl.program_id(0)

    def for_each_chunk(tile, slot, act):
        def body(lc, carry):
            act(_chunk_copy(ys_hbm, comb_ref[tile * TILE_CHUNKS + lc], ybuf.at[slot], lc, sems.at[slot]))
            return carry
        lax.fori_loop(0, nlc_ref[tile], body, 0)

    @pl.when(i == 0)
    def _():
        ybuf[...] = jnp.zeros_like(ybuf)
        for_each_chunk(0, 0, lambda cp: cp.start())

    @pl.when(i + 1 < pl.num_programs(0))
    def _():
        for_each_chunk(i + 1, (i + 1) % 2, lambda cp: cp.start())

    slot = i % 2
    for_each_chunk(i, slot, lambda cp: cp.wait())

    y_lo, y_hi = _unpack_bf16_pairs(ybuf[slot])
    route = route_ref[...]
    slot_id = lax.broadcasted_iota(jnp.int32, (tc, MOE_SLOTS), 1)

    def pick_rows(lane):
        sel = (slot_id == route[:, lane:lane + 1].astype(jnp.int32)).astype(jnp.bfloat16)
        return jnp.concatenate([jnp.dot(sel, y_lo, preferred_element_type=jnp.float32),
                                jnp.dot(sel, y_hi, preferred_element_type=jnp.float32)], axis=1)

    y0 = pick_rows(ROUTE_S0)
    y1 = pick_rows(ROUTE_S1)
    g0 = route[:, ROUTE_G0:ROUTE_G0 + 1]
    g1 = route[:, ROUTE_G1:ROUTE_G1 + 1]
    out_ref[...] = _rmsnorm(x1_ref[...] + g0 * y0 + g1 * y1, nf_ref[...])


def _combine(comb, n_lc, ys, x1, route, norm_f):
    T = x1.shape[0]
    tc = MIX_TILE
    return pl.pallas_call(
        _combine_kernel,
        grid_spec=pltpu.PrefetchScalarGridSpec(
            num_scalar_prefetch=2,
            grid=(T // tc,),
            in_specs=[
                pl.BlockSpec(memory_space=pl.ANY),
                pl.BlockSpec((tc, D_MODEL), lambda i, c, n: (i, 0)),
                pl.BlockSpec((tc, LANES), lambda i, c, n: (i, 0)),
                pl.BlockSpec((1, D_MODEL), lambda i, c, n: (0, 0)),
            ],
            out_specs=pl.BlockSpec((tc, D_MODEL), lambda i, c, n: (i, 0)),
            scratch_shapes=[
                pltpu.VMEM((2, MOE_SLOTS, D_MODEL // 2), jnp.uint32),
                pltpu.SemaphoreType.DMA((2,)),
            ],
        ),
        out_shape=jax.ShapeDtypeStruct((T, D_MODEL), jnp.float32),
        compiler_params=pltpu.CompilerParams(
            dimension_semantics=("arbitrary",), vmem_limit_bytes=VMEM_LIMIT),
        name="moe_combine",
    )(comb, n_lc, ys, x1, route, norm_f)


def _block_diag_gates(w_rg, w_ig):
    def bd(w):
        w4 = w.reshape(N_LRU_CHUNKS, LRU_CHUNK_HEADS, LRU_HEAD_DIM, LRU_HEAD_DIM)
        eye = jnp.eye(LRU_CHUNK_HEADS, dtype=w.dtype)
        return jnp.einsum('chij,hk->chikj', w4, eye).reshape(N_LRU_CHUNKS, LRU_CHUNK, LRU_CHUNK)
    return jnp.concatenate([bd(w_rg), bd(w_ig)], axis=-1).astype(jnp.bfloat16)


def kernel(x, rel_bias, norm1, w_in, conv_w, conv_b, w_rg, b_rg, w_ig, b_ig, lru_lambda,
           w_proj_attn, w_proj_lru, w_out, norm2, w_router_group, w_router_expert,
           w_gate_up, w_down, norm_f):
    B, S, D = x.shape
    T = B * S
    assert w_in.shape[0] == 1, "single-layer block"
    layer = 0
    bf16 = jnp.bfloat16
    x2 = x.reshape(T, D)
    row = lambda v: v[layer].reshape(1, -1)
    w_qkv = (w_in[layer][:, :QKV_WIDTH].reshape(D, 3, N_GROUPS, GROUP_WIDTH)
             .transpose(0, 2, 1, 3).reshape(D, QKV_WIDTH).astype(bf16))
    w_rest = w_in[layer][:, QKV_WIDTH:].astype(bf16)

    qkv_groups = _qkv_projection(x2, row(norm1), w_qkv, B, S)
    attn_o, attn_lse = [], []
    for gi, (window, dilation) in enumerate(DILATED_GROUPS):
        hs = slice(gi * HEADS_PER_GROUP, (gi + 1) * HEADS_PER_GROUP)
        bias = _attention_bias_table(rel_bias[:, hs], window, dilation)
        o, lse = _attention_group(qkv_groups[gi], bias, gi, dilation)
        attn_o.append(o)
        attn_lse.append(lse)

    w_router = jnp.zeros((D, LANES), jnp.float32)
    w_router = w_router.at[:, :N_EXPERTS].set(w_router_expert[layer].astype(jnp.float32))
    w_router = w_router.at[:, GROUP_LOGIT_LANE0:GROUP_LOGIT_LANE0 + N_EXPERT_GROUPS].set(
        w_router_group[layer].astype(jnp.float32))
    w_router_hi = w_router.astype(bf16)
    weights = [
        row(norm1), w_rest, conv_w[layer], row(conv_b),
        _block_diag_gates(w_rg[layer], w_ig[layer]), row(b_rg), row(b_ig), row(lru_lambda),
        w_proj_attn[layer].astype(bf16), w_proj_lru[layer].astype(bf16), w_out[layer].astype(bf16),
        row(norm2), w_router_hi, (w_router - w_router_hi.astype(jnp.float32)).astype(bf16),
    ]
    x1, xs, route, counts = _token_mixer(x2, attn_o, attn_lse, weights, B, S)

    src, comb, n_lc, block_expert, n_valid = _moe_tables(counts[:, 0, :N_EXPERTS].astype(jnp.int32))
    ys = _experts(block_expert, n_valid, src, xs, w_gate_up[layer], w_down[layer])
    out = _combine(comb, n_lc, ys, x1, route, norm_f.reshape(1, -1))
    return out.reshape(B, S, D)
```

```python
import math

import jax
import jax.numpy as jnp
from jax import lax
from jax.experimental import pallas as pl
from jax.experimental.pallas import tpu as pltpu

D_MODEL = 1024
HEAD_DIM = 64
HEADS_PER_GROUP = 4
DILATED_GROUPS = ((128, 1), (512, 4), (2048, 16))
N_GROUPS = len(DILATED_GROUPS)
N_ATTN_HEADS = HEADS_PER_GROUP * N_GROUPS
ATTN_WIDTH = N_ATTN_HEADS * HEAD_DIM
GROUP_WIDTH = HEADS_PER_GROUP * HEAD_DIM
QKV_WIDTH = 3 * ATTN_WIDTH
ATTN_BLOCK = 128
N_REL_BUCKETS = 32
REL_MAX_DISTANCE = 2048
NEG_INF = -1e30

LRU_WIDTH = D_MODEL
LRU_HEADS = 16
LRU_HEAD_DIM = LRU_WIDTH // LRU_HEADS
CONV_WIDTH = 4
LRU_C = 8.0
LRU_CHUNK_HEADS = 4
LRU_CHUNK = LRU_CHUNK_HEADS * LRU_HEAD_DIM
N_LRU_CHUNKS = LRU_WIDTH // LRU_CHUNK

N_EXPERT_GROUPS = 4
EXPERTS_PER_GROUP = 8
N_EXPERTS = N_EXPERT_GROUPS * EXPERTS_PER_GROUP
TOP_K = 2
D_EXPERT = 512
EPS = 1e-6

REST_WIDTH = 2 * LRU_WIDTH + 2 * D_MODEL
LANES = 128
SUBLANES = 8

ATTN_TILE = 256
ATTN_QB = 4
MIX_TILE = 256
MOE_TILE = 256
MOE_CHUNK = SUBLANES
MOE_BLOCK_CHUNKS = MOE_TILE // MOE_CHUNK
MOE_SLOTS = 768
assert MOE_SLOTS >= TOP_K * MIX_TILE + N_EXPERTS * (MOE_CHUNK - 1) and MOE_SLOTS % MOE_CHUNK == 0
TILE_CHUNKS = MOE_SLOTS // MOE_CHUNK
VMEM_LIMIT = 56 * 1024 * 1024

ROUTE_S0, ROUTE_S1, ROUTE_G0, ROUTE_G1 = range(4)
GROUP_LOGIT_LANE0 = N_EXPERTS


def _rmsnorm(x, g):
    return x * lax.rsqrt(jnp.mean(x * x, axis=-1, keepdims=True) + EPS) * g


def _pack_bf16_pairs(v):
    w = v.shape[1] // 2
    bits = lax.bitcast_convert_type(v, jnp.uint32)
    return bits[:, w:] | (bits[:, :w] >> 16)


def _unpack_bf16_pairs(p):
    lo = lax.bitcast_convert_type(p << 16, jnp.float32)
    hi = lax.bitcast_convert_type(p & jnp.uint32(0xFFFF0000), jnp.float32)
    return lo.astype(jnp.bfloat16), hi.astype(jnp.bfloat16)


def _residue_major_perm(n, d):
    per = n // d
    out = lax.broadcasted_iota(jnp.int32, (n, n), 0)
    src = lax.broadcasted_iota(jnp.int32, (n, n), 1)
    m = jnp.bitwise_and(out, per - 1)
    r = jnp.right_shift(out, per.bit_length() - 1)
    return (src == m * d + r).astype(jnp.bfloat16)


def _qkv_kernel(x_ref, g_ref, w_ref, *o_refs):
    tm = ATTN_TILE
    h = _rmsnorm(x_ref[...], g_ref[...]).astype(jnp.bfloat16)
    qkv = jnp.dot(h, w_ref[...], preferred_element_type=jnp.float32).astype(jnp.bfloat16)
    for gi, (o_ref, (_, d)) in enumerate(zip(o_refs, DILATED_GROUPS)):
        part = qkv[:, gi * ATTN_WIDTH:(gi + 1) * ATTN_WIDTH]
        if d > 1:
            part = jnp.dot(_residue_major_perm(tm, d), part,
                           preferred_element_type=jnp.float32).astype(jnp.bfloat16)
        o_ref[...] = part.reshape(d, tm // d, ATTN_WIDTH)


def _qkv_projection(x2, norm1, w_qkv, B, S):
    tm = ATTN_TILE
    nt = S // tm
    return pl.pallas_call(
        _qkv_kernel,
        grid=(B, nt),
        in_specs=[
            pl.BlockSpec((tm, D_MODEL), lambda b, j: (b * nt + j, 0)),
            pl.BlockSpec((1, D_MODEL), lambda b, j: (0, 0)),
            pl.BlockSpec((D_MODEL, QKV_WIDTH), lambda b, j: (0, 0)),
        ],
        out_specs=[pl.BlockSpec((None, d, tm // d, ATTN_WIDTH), lambda b, j: (b, 0, j, 0))
                   for _, d in DILATED_GROUPS],
        out_shape=[jax.ShapeDtypeStruct((B, d, S // d, ATTN_WIDTH), jnp.bfloat16)
                   for _, d in DILATED_GROUPS],
        compiler_params=pltpu.CompilerParams(
            dimension_semantics=("parallel", "parallel"), vmem_limit_bytes=VMEM_LIMIT),
        name="qkv_projection",
    )(x2, norm1, w_qkv)


def _attn_kernel(q_ref, kp_ref, kc_ref, vp_ref, vc_ref, bias0_ref, bias_ref, o_ref, lse_ref):
    q = q_ref[...] * (HEAD_DIM ** -0.5)
    k = jnp.concatenate([kp_ref[...], kc_ref[...]], axis=0)
    v = jnp.concatenate([vp_ref[...], vc_ref[...]], axis=0)
    for b in range(ATTN_QB):
        rows = slice(b * ATTN_BLOCK, (b + 1) * ATTN_BLOCK)
        keys = slice(b * ATTN_BLOCK, (b + 2) * ATTN_BLOCK)
        outs, lses = [], []
        for h in range(HEADS_PER_GROUP):
            hs = slice(h * HEAD_DIM, (h + 1) * HEAD_DIM)
            bias = bias0_ref[h] if b == 0 else bias_ref[h]
            s = lax.dot_general(q[rows, hs], k[keys, hs], (((1,), (1,)), ((), ())),
                                preferred_element_type=jnp.float32) + bias
            m = jnp.max(s, axis=-1, keepdims=True)
            p = jnp.exp(s - m)
            l = jnp.sum(p, axis=-1, keepdims=True)
            o = jnp.dot(p.astype(jnp.bfloat16), v[keys, hs], preferred_element_type=jnp.float32) / l
            outs.append(o)
            lses.append(jnp.broadcast_to(m + jnp.log(l), (ATTN_BLOCK, HEAD_DIM)))
        o_ref[rows, :] = jnp.concatenate(outs, axis=1).astype(o_ref.dtype)
        lse_ref[rows, :] = jnp.concatenate(lses, axis=1)


def _attention_group(qkv_g, bias, gi, dilation):
    B, _, L, _ = qkv_g.shape
    rows = ATTN_QB * ATTN_BLOCK
    cur = lambda which: pl.BlockSpec((None, None, rows, GROUP_WIDTH), lambda b, r, n: (b, r, n, which))
    prev = lambda which: pl.BlockSpec((None, None, ATTN_BLOCK, GROUP_WIDTH),
                                      lambda b, r, n: (b, r, jnp.maximum(n * ATTN_QB - 1, 0), which))
    bias_blk = (None, HEADS_PER_GROUP, ATTN_BLOCK, 2 * ATTN_BLOCK)
    return pl.pallas_call(
        _attn_kernel,
        grid=(B, dilation, L // rows),
        in_specs=[
            cur(0), prev(1), cur(1), prev(2), cur(2),
            pl.BlockSpec(bias_blk, lambda b, r, n: (jnp.minimum(n, 1), 0, 0, 0)),
            pl.BlockSpec(bias_blk, lambda b, r, n: (1, 0, 0, 0)),
        ],
        out_specs=[cur(0), cur(0)],
        out_shape=[
            jax.ShapeDtypeStruct((B, dilation, L, GROUP_WIDTH), jnp.bfloat16),
            jax.ShapeDtypeStruct((B, dilation, L, GROUP_WIDTH), jnp.float32),
        ],
        compiler_params=pltpu.CompilerParams(
            dimension_semantics=("parallel", "parallel", "arbitrary"), vmem_limit_bytes=VMEM_LIMIT),
        name=f"dilated_attention_g{gi}",
    )(qkv_g, qkv_g, qkv_g, qkv_g, qkv_g, bias, bias)


def _t5_causal_bucket(dist):
    max_exact = N_REL_BUCKETS // 2
    d_f = jnp.maximum(dist, max_exact).astype(jnp.float32)
    large = max_exact + (jnp.log(d_f / max_exact) / math.log(REL_MAX_DISTANCE / max_exact)
                         * (N_REL_BUCKETS - max_exact)).astype(jnp.int32)
    large = jnp.minimum(large, N_REL_BUCKETS - 1)
    return jnp.where(dist < max_exact, dist, large)


def _attention_bias_table(rel_bias_g, window, dilation):
    nw = window // dilation
    qi = jnp.arange(ATTN_BLOCK)[:, None]
    ki = jnp.arange(2 * ATTN_BLOCK)[None, :]
    dist = ATTN_BLOCK + qi - ki
    band = (dist >= 0) & (dist <= nw)
    bucket = _t5_causal_bucket(jnp.maximum(dist, 0) * dilation)
    onehot = (bucket[None, :, :, None] == jnp.arange(N_REL_BUCKETS)).astype(jnp.float32)
    bias = jnp.sum(onehot * rel_bias_g.astype(jnp.float32).T[:, None, None, :], axis=-1)
    later = jnp.where(band[None], bias, NEG_INF)
    first = jnp.where((band & (ki >= ATTN_BLOCK))[None], bias, NEG_INF)
    return jnp.stack([first, later])


def _gelu_tanh(x):
    return 0.5 * x * (1.0 + jnp.tanh(math.sqrt(2.0 / math.pi) * (x + 0.044715 * (x * x * x))))


def _softplus(x):
    return jnp.maximum(x, 0.0) + jnp.log(1.0 + jnp.exp(-jnp.abs(x)))


def _natural_order(blk_ref, d, slabs):
    if d == 1:
        return blk_ref[0].astype(jnp.float32)
    per = MIX_TILE // d
    for r in range(d):
        rows = blk_ref[r].astype(jnp.float32)
        for c in range(GROUP_WIDTH // LANES):
            slabs[c, pl.ds(r, per, stride=d), :] = rows[:, c * LANES:(c + 1) * LANES]
    return jnp.concatenate([slabs[c] for c in range(GROUP_WIDTH // LANES)], axis=1)


def _mixer_kernel(x_ref, o1_ref, o2_ref, o3_ref, l1_ref, l2_ref, l3_ref,
                  n1_ref, win_ref, cw_ref, cb_ref, wg_ref, brg_ref, big_ref, lam_ref,
                  wpa_ref, wpl_ref, wout_ref, n2_ref, wrh_ref, wrl_ref,
                  x1_ref, xs_ref, route_ref, cnt_ref,
                  xbuf, a_sc, b_sc, h_sc, *slabs):
    tm = MIX_TILE
    first_tile = pl.program_id(1) == 0

    @pl.when(first_tile)
    def _():
        xbuf[0:SUBLANES, :] = jnp.zeros((SUBLANES, LRU_WIDTH), jnp.float32)
        h_sc[...] = jnp.zeros_like(h_sc)

    x = x_ref[...]
    h = _rmsnorm(x, n1_ref[...]).astype(jnp.bfloat16)
    proj = jnp.dot(h, win_ref[...], preferred_element_type=jnp.float32)
    xr = proj[:, 0:LRU_WIDTH]
    g_lru = proj[:, LRU_WIDTH:2 * LRU_WIDTH]
    g_a = proj[:, 2 * LRU_WIDTH:2 * LRU_WIDTH + D_MODEL]
    g_b = proj[:, 2 * LRU_WIDTH + D_MODEL:]

    xbuf[SUBLANES:SUBLANES + tm, :] = xr
    xc = xr * cw_ref[CONV_WIDTH - 1:CONV_WIDTH, :] + cb_ref[...]
    for j in range(CONV_WIDTH - 1):
        back = CONV_WIDTH - 1 - j
        xc = xc + xbuf[SUBLANES - back:SUBLANES - back + tm, :] * cw_ref[j:j + 1, :]
    xbuf[0:SUBLANES, :] = xbuf[tm:tm + SUBLANES, :]

    xcb = xc.astype(jnp.bfloat16)
    r_parts, i_parts = [], []
    for c in range(N_LRU_CHUNKS):
        ri = jnp.dot(xcb[:, c * LRU_CHUNK:(c + 1) * LRU_CHUNK], wg_ref[c],
                     preferred_element_type=jnp.float32)
        r_parts.append(ri[:, :LRU_CHUNK])
        i_parts.append(ri[:, LRU_CHUNK:])
    r = jax.nn.sigmoid(jnp.concatenate(r_parts, axis=1) + brg_ref[...])
    ig = jax.nn.sigmoid(jnp.concatenate(i_parts, axis=1) + big_ref[...])
    log_a = (-LRU_C * _softplus(-lam_ref[...])) * r
    a = jnp.exp(log_a)
    bb = jnp.sqrt(1.0 - a * a) * (ig * xc)

    row8 = lax.broadcasted_iota(jnp.int32, (tm, 1), 0) % SUBLANES
    for k in (1, 2, 4):
        a_s = pltpu.roll(a, k, axis=0)
        b_s = pltpu.roll(bb, k, axis=0)
        take = row8 >= k
        bb = jnp.where(take, a * b_s + bb, bb)
        a = jnp.where(take, a * a_s, a)
    a_sc[...] = a
    b_sc[...] = bb

    def sweep(g, carry):
        rows = pl.ds(pl.multiple_of(g * SUBLANES, SUBLANES), SUBLANES)
        hh = a_sc[rows, :] * carry + b_sc[rows, :]
        b_sc[rows, :] = hh
        return hh[SUBLANES - 1:SUBLANES, :]

    h_sc[...] = lax.fori_loop(0, tm // SUBLANES, sweep, h_sc[...], unroll=4)
    lru = (b_sc[...] * _gelu_tanh(g_lru)).astype(jnp.bfloat16)
    p_lru = jnp.dot(lru, wpl_ref[...], preferred_element_type=jnp.float32)

    dil = [d for _, d in DILATED_GROUPS]
    o1, o2, o3 = (_natural_order(ref, d, slabs[2 * g]) for g, (ref, d) in
                  enumerate(zip((o1_ref, o2_ref, o3_ref), dil)))
    l1, l2, l3 = (_natural_order(ref, d, slabs[2 * g + 1]) for g, (ref, d) in
                  enumerate(zip((l1_ref, l2_ref, l3_ref), dil)))
    lm = jnp.maximum(jnp.maximum(l1, l2), l3)
    e1, e2, e3 = jnp.exp(l1 - lm), jnp.exp(l2 - lm), jnp.exp(l3 - lm)
    attn = (e1 * o1 + e2 * o2 + e3 * o3) / (e1 + e2 + e3)
    p_attn = jnp.dot(attn.astype(jnp.bfloat16), wpa_ref[...], preferred_element_type=jnp.float32)

    merged = jax.nn.sigmoid(g_a) * p_attn + jax.nn.sigmoid(g_b) * p_lru
    x1 = x + jnp.dot(merged.astype(jnp.bfloat16), wout_ref[...], preferred_element_type=jnp.float32)
    x1_ref[...] = x1
    h2 = _rmsnorm(x1, n2_ref[...])

    h2_hi = h2.astype(jnp.bfloat16)
    h2_lo = (h2 - h2_hi.astype(jnp.float32)).astype(jnp.bfloat16)
    logits = (jnp.dot(h2_hi, wrh_ref[...], preferred_element_type=jnp.float32)
              + (jnp.dot(h2_lo, wrh_ref[...], preferred_element_type=jnp.float32)
                 + jnp.dot(h2_hi, wrl_ref[...], preferred_element_type=jnp.float32)))
    lane = lax.broadcasted_iota(jnp.int32, (tm, LANES), 1)
    big = jnp.int32(LANES)
    lowest = jnp.float32(-3.0e38)
    is_g = (lane >= GROUP_LOGIT_LANE0) & (lane < GROUP_LOGIT_LANE0 + N_EXPERT_GROUPS)
    gl = jnp.where(is_g, logits, lowest)
    gmax = jnp.max(gl, axis=-1, keepdims=True)
    gsel = jnp.min(jnp.where(gl == gmax, lane, big), axis=-1, keepdims=True) - GROUP_LOGIT_LANE0
    p_sel = 1.0 / jnp.sum(jnp.where(is_g, jnp.exp(gl - gmax), 0.0), axis=-1, keepdims=True)
    is_e = (lane >= gsel * EXPERTS_PER_GROUP) & (lane < (gsel + 1) * EXPERTS_PER_GROUP)
    el = jnp.where(is_e, logits, lowest)
    v1 = jnp.max(el, axis=-1, keepdims=True)
    i1 = jnp.min(jnp.where(el == v1, lane, big), axis=-1, keepdims=True)
    el2 = jnp.where(lane == i1, lowest, el)
    v2 = jnp.max(el2, axis=-1, keepdims=True)
    i2 = jnp.min(jnp.where(el2 == v2, lane, big), axis=-1, keepdims=True)
    t = jnp.exp(v2 - v1)
    g0 = p_sel / (1.0 + t)
    g1 = p_sel * t / (1.0 + t)

    oh0 = lane == i1
    oh1 = lane == i2
    both = (oh0 | oh1).astype(jnp.float32)
    tri = (lax.broadcasted_iota(jnp.int32, (tm, tm), 0)
           > lax.broadcasted_iota(jnp.int32, (tm, tm), 1)).astype(jnp.bfloat16)
    before = jnp.dot(tri, both.astype(jnp.bfloat16), preferred_element_type=jnp.float32)
    cnt = jnp.sum(both, axis=0, keepdims=True)
    padded = jnp.floor((cnt + (MOE_CHUNK - 1.0)) * (1.0 / MOE_CHUNK)) * MOE_CHUNK
    upper = (lax.broadcasted_iota(jnp.int32, (LANES, LANES), 0)
             < lax.broadcasted_iota(jnp.int32, (LANES, LANES), 1)).astype(jnp.bfloat16)
    loff = jnp.dot(jnp.broadcast_to(padded, (SUBLANES, LANES)).astype(jnp.bfloat16), upper,
                   preferred_element_type=jnp.float32)[0:1, :]
    base = before + loff
    slot0 = jnp.sum(jnp.where(oh0, base, 0.0), axis=-1, keepdims=True)
    slot1 = jnp.sum(jnp.where(oh1, base, 0.0), axis=-1, keepdims=True)
    cnt_ref[...] = jnp.broadcast_to(cnt, cnt_ref.shape)

    slot_id = lax.broadcasted_iota(jnp.int32, (tm, MOE_SLOTS), 1)
    place = ((slot_id == slot0.astype(jnp.int32)) | (slot_id == slot1.astype(jnp.int32)))
    xs_ref[...] = _pack_bf16_pairs(
        lax.dot_general(place.astype(jnp.bfloat16), h2_hi, (((0,), (0,)), ((), ())),
                        preferred_element_type=jnp.float32))

    route = jnp.zeros((tm, LANES), jnp.float32)
    for lane_id, val in ((ROUTE_S0, slot0), (ROUTE_S1, slot1), (ROUTE_G0, g0), (ROUTE_G1, g1)):
        route = jnp.where(lane == lane_id, val, route)
    route_ref[...] = route


def _token_mixer(x2, attn_o, attn_lse, weights, B, S):
    T = x2.shape[0]
    tm = MIX_TILE
    nt = S // tm
    row = lambda b, j: (b * nt + j, 0)
    const2 = lambda b, j: (0, 0)
    const3 = lambda b, j: (0, 0, 0)

    def resident(shape):
        idx = const2 if len(shape) == 2 else const3
        return pl.BlockSpec(shape, idx, pipeline_mode=pl.Buffered(1))

    in_specs = [pl.BlockSpec((tm, D_MODEL), row)]
    group_specs = [pl.BlockSpec((None, d, tm // d, GROUP_WIDTH), lambda b, j: (b, 0, j, 0))
                   for _, d in DILATED_GROUPS]
    in_specs += group_specs + group_specs
    in_specs += [resident(w.shape) for w in weights]
    return pl.pallas_call(
        _mixer_kernel,
        grid=(B, nt),
        in_specs=in_specs,
        out_specs=[
            pl.BlockSpec((tm, D_MODEL), row),
            pl.BlockSpec((MOE_SLOTS, D_MODEL // 2), row),
            pl.BlockSpec((tm, LANES), row),
            pl.BlockSpec((None, SUBLANES, LANES), lambda b, j: (b * nt + j, 0, 0)),
        ],
        out_shape=[
            jax.ShapeDtypeStruct((T, D_MODEL), jnp.float32),
            jax.ShapeDtypeStruct((B * nt * MOE_SLOTS, D_MODEL // 2), jnp.uint32),
            jax.ShapeDtypeStruct((T, LANES), jnp.float32),
            jax.ShapeDtypeStruct((B * nt, SUBLANES, LANES), jnp.float32),
        ],
        scratch_shapes=[
            pltpu.VMEM((tm + 2 * SUBLANES, LRU_WIDTH), jnp.float32),
            pltpu.VMEM((tm, LRU_WIDTH), jnp.float32),
            pltpu.VMEM((tm, LRU_WIDTH), jnp.float32),
            pltpu.VMEM((1, LRU_WIDTH), jnp.float32),
        ] + [pltpu.VMEM((GROUP_WIDTH // LANES, tm, LANES), jnp.float32)] * (2 * N_GROUPS),
        compiler_params=pltpu.CompilerParams(
            dimension_semantics=("arbitrary", "arbitrary"), vmem_limit_bytes=VMEM_LIMIT),
        name="token_mixer",
    )(x2, *attn_o, *attn_lse, *weights)


def _moe_tables(cnt_tile):
    ntiles = cnt_tile.shape[0]
    nch = (cnt_tile + MOE_CHUNK - 1) // MOE_CHUNK
    lo_c = jnp.cumsum(nch, axis=1) - nch
    per_expert = jnp.sum(nch, axis=0)
    region = ((per_expert + MOE_BLOCK_CHUNKS - 1) // MOE_BLOCK_CHUNKS) * MOE_BLOCK_CHUNKS
    pend = jnp.cumsum(region)
    glob = (pend - region)[None, :] + jnp.cumsum(nch, axis=0) - nch

    max_rows = TOP_K * ntiles * MIX_TILE + ntiles * N_EXPERTS * (MOE_CHUNK - 1)
    max_chunks = -(-max_rows // MOE_CHUNK) + N_EXPERTS * (MOE_BLOCK_CHUNKS - 1)
    max_blocks = -(-max_chunks // MOE_BLOCK_CHUNKS)
    max_chunks = max_blocks * MOE_BLOCK_CHUNKS

    seg_start = glob.T.reshape(-1)
    seg_src = (jnp.arange(ntiles, dtype=jnp.int32)[:, None] * TILE_CHUNKS + lo_c).T.reshape(-1)
    step = jnp.diff(seg_src - seg_start, prepend=0)
    g = jnp.arange(max_chunks, dtype=jnp.int32)
    src = g + jnp.sum(jnp.where(seg_start[None, :] <= g[:, None], step[None, :], 0), axis=1)
    src = jnp.clip(src, 0, ntiles * TILE_CHUNKS - 1)

    lstep = jnp.diff(glob - lo_c, axis=1, prepend=0)
    lc = jnp.arange(TILE_CHUNKS, dtype=jnp.int32)
    comb = lc[None, :] + jnp.sum(
        jnp.where(lo_c[:, None, :] <= lc[None, :, None], lstep[:, None, :], 0), axis=2)
    comb = jnp.clip(comb, 0, max_chunks - 1).reshape(-1)

    eidx = jnp.arange(N_EXPERTS, dtype=jnp.int32)
    nonempty = region > 0
    later = jnp.where(nonempty[None, :] & (eidx[None, :] > eidx[:, None]), eidx[None, :], N_EXPERTS)
    next_expert = jnp.min(later, axis=1)
    next_expert = jnp.where(next_expert == N_EXPERTS, -1, next_expert)
    wslot = (jnp.cumsum(nonempty.astype(jnp.int32)) - nonempty.astype(jnp.int32)) % 2
    last_expert = jnp.max(jnp.where(nonempty, eidx, 0))
    blk0 = jnp.arange(max_blocks, dtype=jnp.int32) * MOE_BLOCK_CHUNKS
    block_expert = jnp.minimum(jnp.sum((pend[None, :] <= blk0[:, None]).astype(jnp.int32), axis=1),
                               last_expert)
    of_block = lambda per_expert: jnp.sum(
        jnp.where(block_expert[:, None] == eidx[None, :], per_expert[None, :], 0), axis=1)
    n_valid = pend[-1:] // MOE_BLOCK_CHUNKS
    as_i32 = lambda v: v.astype(jnp.int32)
    expert_tables = (as_i32(block_expert), as_i32(of_block(next_expert)), as_i32(of_block(wslot)),
                     as_i32(n_valid), as_i32(src))
    return expert_tables, as_i32(comb)


def _chunk_copy(src_hbm, src_chunk, dst, dst_chunk, sem):
    rows = lambda c: pl.ds(pl.multiple_of(c * MOE_CHUNK, MOE_CHUNK), MOE_CHUNK)
    return pltpu.make_async_copy(src_hbm.at[rows(src_chunk), :], dst.at[rows(dst_chunk), :], sem)


def _expert_kernel(be_ref, nxt_ref, ws_ref, nv_ref, src_ref, xs_hbm, wgu_hbm, wd_hbm, ys_ref,
                   xbuf, wgu_f32, wd_f32, wgu_bf, wd_bf, sems, wsems):
    i = pl.program_id(0)
    valid = i < nv_ref[0]

    def gather(blk, slot):
        return [_chunk_copy(xs_hbm, src_ref[blk * MOE_BLOCK_CHUNKS + c], xbuf.at[slot], c, sems.at[slot])
                for c in range(MOE_BLOCK_CHUNKS)]

    def fetch_weights(e, slot):
        return [pltpu.make_async_copy(wgu_hbm.at[e], wgu_f32.at[slot], wsems.at[slot]),
                pltpu.make_async_copy(wd_hbm.at[e], wd_f32.at[slot], wsems.at[slot])]

    @pl.when(i == 0)
    def _():
        for cp in gather(0, 0) + fetch_weights(be_ref[0], 0):
            cp.start()

    @pl.when(i + 1 < nv_ref[0])
    def _():
        for cp in gather(i + 1, (i + 1) % 2):
            cp.start()

    new_expert = valid & ((i == 0) | (be_ref[i] != be_ref[jnp.maximum(i - 1, 0)]))

    @pl.when(new_expert)
    def _():
        ws = ws_ref[i]
        for cp in fetch_weights(be_ref[i], ws):
            cp.wait()
        wgu_bf[...] = wgu_f32[ws].astype(jnp.bfloat16)
        wd_bf[...] = wd_f32[ws].astype(jnp.bfloat16)

        @pl.when(nxt_ref[i] >= 0)
        def _():
            for cp in fetch_weights(nxt_ref[i], 1 - ws):
                cp.start()

    @pl.when(valid)
    def _():
        slot = i % 2
        for cp in gather(i, slot):
            cp.wait()
        x_lo, x_hi = _unpack_bf16_pairs(xbuf[slot])
        half = D_MODEL // 2
        gu = (jnp.dot(x_lo, wgu_bf[:half, :], preferred_element_type=jnp.float32)
              + jnp.dot(x_hi, wgu_bf[half:, :], preferred_element_type=jnp.float32))
        g = gu[:, :D_EXPERT]
        u = gu[:, D_EXPERT:]
        act = (g * jax.nn.sigmoid(g) * u).astype(jnp.bfloat16)
        y = jnp.dot(act, wd_bf[...], preferred_element_type=jnp.float32)
        ys_ref[...] = _pack_bf16_pairs(y.astype(jnp.bfloat16).astype(jnp.float32))

    @pl.when(jnp.logical_not(valid))
    def _():
        ys_ref[...] = jnp.zeros_like(ys_ref)


def _experts(tables, xs, w_gate_up, w_down):
    block_expert, block_next, block_wslot, n_valid, src = tables
    nblk = block_expert.shape[0]
    return pl.pallas_call(
        _expert_kernel,
        grid_spec=pltpu.PrefetchScalarGridSpec(
            num_scalar_prefetch=5,
            grid=(nblk,),
            in_specs=[pl.BlockSpec(memory_space=pl.ANY)] * 3,
            out_specs=pl.BlockSpec((MOE_TILE, D_MODEL // 2), lambda i, *_: (i, 0)),
            scratch_shapes=[
                pltpu.VMEM((2, MOE_TILE, D_MODEL // 2), jnp.uint32),
                pltpu.VMEM((2, D_MODEL, 2 * D_EXPERT), jnp.float32),
                pltpu.VMEM((2, D_EXPERT, D_MODEL), jnp.float32),
                pltpu.VMEM((D_MODEL, 2 * D_EXPERT), jnp.bfloat16),
                pltpu.VMEM((D_EXPERT, D_MODEL), jnp.bfloat16),
                pltpu.SemaphoreType.DMA((2,)),
                pltpu.SemaphoreType.DMA((2,)),
            ],
        ),
        out_shape=jax.ShapeDtypeStruct((nblk * MOE_TILE, D_MODEL // 2), jnp.uint32),
        compiler_params=pltpu.CompilerParams(
            dimension_semantics=("arbitrary",), vmem_limit_bytes=VMEM_LIMIT),
        name="moe_experts",
    )(block_expert, block_next, block_wslot, n_valid, src, xs, w_gate_up, w_down)


def _combine_kernel(comb_ref, ys_hbm, x1_ref, route_ref, nf_ref, out_ref, ybuf, sems):
    tc = MIX_TILE
    i = pl.program_id(0)
    last = pl.num_programs(0) - 1

    def gather(tile, slot):
        return [_chunk_copy(ys_hbm, comb_ref[tile * TILE_CHUNKS + lc], ybuf.at[slot], lc, sems.at[slot])
                for lc in range(TILE_CHUNKS)]

    @pl.when(i == 0)
    def _():
        for cp in gather(0, 0):
            cp.start()

    nxt = jnp.minimum(i + 1, last)
    for cp in gather(nxt, (i + 1) % 2):
        cp.start()

    slot = i % 2
    for cp in gather(i, slot):
        cp.wait()

    y_lo, y_hi = _unpack_bf16_pairs(ybuf[slot])
    route = route_ref[...]
    slot_id = lax.broadcasted_iota(jnp.int32, (tc, MOE_SLOTS), 1)

    def pick_rows(lane):
        sel = (slot_id == route[:, lane:lane + 1].astype(jnp.int32)).astype(jnp.bfloat16)
        return jnp.concatenate([jnp.dot(sel, y_lo, preferred_element_type=jnp.float32),
                                jnp.dot(sel, y_hi, preferred_element_type=jnp.float32)], axis=1)

    y0 = pick_rows(ROUTE_S0)
    y1 = pick_rows(ROUTE_S1)
    g0 = route[:, ROUTE_G0:ROUTE_G0 + 1]
    g1 = route[:, ROUTE_G1:ROUTE_G1 + 1]
    out_ref[...] = _rmsnorm(x1_ref[...] + g0 * y0 + g1 * y1, nf_ref[...])

    @pl.when(i == last)
    def _():
        for cp in gather(nxt, (i + 1) % 2):
            cp.wait()


def _combine(comb, ys, x1, route, norm_f):
    T = x1.shape[0]
    tc = MIX_TILE
    return pl.pallas_call(
        _combine_kernel,
        grid_spec=pltpu.PrefetchScalarGridSpec(
            num_scalar_prefetch=1,
            grid=(T // tc,),
            in_specs=[
                pl.BlockSpec(memory_space=pl.ANY),
                pl.BlockSpec((tc, D_MODEL), lambda i, c: (i, 0)),
                pl.BlockSpec((tc, LANES), lambda i, c: (i, 0)),
                pl.BlockSpec((1, D_MODEL), lambda i, c: (0, 0)),
            ],
            out_specs=pl.BlockSpec((tc, D_MODEL), lambda i, c: (i, 0)),
            scratch_shapes=[
                pltpu.VMEM((2, MOE_SLOTS, D_MODEL // 2), jnp.uint32),
                pltpu.SemaphoreType.DMA((2,)),
            ],
        ),
        out_shape=jax.ShapeDtypeStruct((T, D_MODEL), jnp.float32),
        compiler_params=pltpu.CompilerParams(
            dimension_semantics=("arbitrary",), vmem_limit_bytes=VMEM_LIMIT),
        name="moe_combine",
    )(comb, ys, x1, route, norm_f)


def _block_diag_gates(w_rg, w_ig):
    def bd(w):
        w4 = w.reshape(N_LRU_CHUNKS, LRU_CHUNK_HEADS, LRU_HEAD_DIM, LRU_HEAD_DIM)
        eye = jnp.eye(LRU_CHUNK_HEADS, dtype=w.dtype)
        return jnp.einsum('chij,hk->chikj', w4, eye).reshape(N_LRU_CHUNKS, LRU_CHUNK, LRU_CHUNK)
    return jnp.concatenate([bd(w_rg), bd(w_ig)], axis=-1).astype(jnp.bfloat16)


def kernel(x, rel_bias, norm1, w_in, conv_w, conv_b, w_rg, b_rg, w_ig, b_ig, lru_lambda,
           w_proj_attn, w_proj_lru, w_out, norm2, w_router_group, w_router_expert,
           w_gate_up, w_down, norm_f):
    B, S, D = x.shape
    T = B * S
    assert w_in.shape[0] == 1, "single-layer block"
    layer = 0
    bf16 = jnp.bfloat16
    x2 = x.reshape(T, D)
    row = lambda v: v[layer].reshape(1, -1)
    w_qkv = (w_in[layer][:, :QKV_WIDTH].reshape(D, 3, N_GROUPS, GROUP_WIDTH)
             .transpose(0, 2, 1, 3).reshape(D, QKV_WIDTH).astype(bf16))
    w_rest = w_in[layer][:, QKV_WIDTH:].astype(bf16)

    qkv_groups = _qkv_projection(x2, row(norm1), w_qkv, B, S)
    attn_o, attn_lse = [], []
    for gi, (window, dilation) in enumerate(DILATED_GROUPS):
        hs = slice(gi * HEADS_PER_GROUP, (gi + 1) * HEADS_PER_GROUP)
        bias = _attention_bias_table(rel_bias[:, hs], window, dilation)
        o, lse = _attention_group(qkv_groups[gi], bias, gi, dilation)
        attn_o.append(o)
        attn_lse.append(lse)

    w_router = jnp.zeros((D, LANES), jnp.float32)
    w_router = w_router.at[:, :N_EXPERTS].set(w_router_expert[layer].astype(jnp.float32))
    w_router = w_router.at[:, GROUP_LOGIT_LANE0:GROUP_LOGIT_LANE0 + N_EXPERT_GROUPS].set(
        w_router_group[layer].astype(jnp.float32))
    w_router_hi = w_router.astype(bf16)
    weights = [
        row(norm1), w_rest, conv_w[layer], row(conv_b),
        _block_diag_gates(w_rg[layer], w_ig[layer]), row(b_rg), row(b_ig), row(lru_lambda),
        w_proj_attn[layer].astype(bf16), w_proj_lru[layer].astype(bf16), w_out[layer].astype(bf16),
        row(norm2), w_router_hi, (w_router - w_router_hi.astype(jnp.float32)).astype(bf16),
    ]
    x1, xs, route, counts = _token_mixer(x2, attn_o, attn_lse, weights, B, S)

    expert_tables, comb = _moe_tables(counts[:, 0, :N_EXPERTS].astype(jnp.int32))
    ys = _experts(expert_tables, xs, w_gate_up[layer], w_down[layer])
    out = _combine(comb, ys, x1, route, norm_f.reshape(1, -1))
    return out.reshape(B, S, D)
```

```python
import math

import jax
import jax.numpy as jnp
from jax import lax
from jax.experimental import pallas as pl
from jax.experimental.pallas import tpu as pltpu

D_MODEL = 1024
HEAD_DIM = 64
HEADS_PER_GROUP = 4
DILATED_GROUPS = ((128, 1), (512, 4), (2048, 16))
N_GROUPS = len(DILATED_GROUPS)
N_ATTN_HEADS = HEADS_PER_GROUP * N_GROUPS
ATTN_WIDTH = N_ATTN_HEADS * HEAD_DIM
GROUP_WIDTH = HEADS_PER_GROUP * HEAD_DIM
QKV_WIDTH = 3 * ATTN_WIDTH
ATTN_BLOCK = 128
N_REL_BUCKETS = 32
REL_MAX_DISTANCE = 2048
NEG_INF = -1e30

LRU_WIDTH = D_MODEL
LRU_HEADS = 16
LRU_HEAD_DIM = LRU_WIDTH // LRU_HEADS
CONV_WIDTH = 4
LRU_C = 8.0
LRU_CHUNK_HEADS = 4
LRU_CHUNK = LRU_CHUNK_HEADS * LRU_HEAD_DIM
N_LRU_CHUNKS = LRU_WIDTH // LRU_CHUNK

N_EXPERT_GROUPS = 4
EXPERTS_PER_GROUP = 8
N_EXPERTS = N_EXPERT_GROUPS * EXPERTS_PER_GROUP
TOP_K = 2
D_EXPERT = 512
EPS = 1e-6

REST_WIDTH = 2 * LRU_WIDTH + 2 * D_MODEL
LANES = 128
SUBLANES = 8

ATTN_TILE = 256
ATTN_QB = 4
MIX_TILE = 256
MOE_TILE = 256
MOE_CHUNK = SUBLANES
MOE_BLOCK_CHUNKS = MOE_TILE // MOE_CHUNK
MOE_SLOTS = 768
assert MOE_SLOTS >= TOP_K * MIX_TILE + N_EXPERTS * (MOE_CHUNK - 1) and MOE_SLOTS % MOE_CHUNK == 0
TILE_CHUNKS = MOE_SLOTS // MOE_CHUNK
VMEM_LIMIT = 56 * 1024 * 1024

ROUTE_S0, ROUTE_S1, ROUTE_G0, ROUTE_G1 = range(4)
GROUP_LOGIT_LANE0 = N_EXPERTS


def _rmsnorm(x, g):
    return x * lax.rsqrt(jnp.mean(x * x, axis=-1, keepdims=True) + EPS) * g


def _pack_bf16_pairs(v):
    w = v.shape[1] // 2
    bits = lax.bitcast_convert_type(v, jnp.uint32)
    return bits[:, w:] | (bits[:, :w] >> 16)


def _unpack_bf16_pairs(p):
    lo = lax.bitcast_convert_type(p << 16, jnp.float32)
    hi = lax.bitcast_convert_type(p & jnp.uint32(0xFFFF0000), jnp.float32)
    return lo.astype(jnp.bfloat16), hi.astype(jnp.bfloat16)


def _residue_major_perm(n, d):
    per = n // d
    out = lax.broadcasted_iota(jnp.int32, (n, n), 0)
    src = lax.broadcasted_iota(jnp.int32, (n, n), 1)
    m = jnp.bitwise_and(out, per - 1)
    r = jnp.right_shift(out, per.bit_length() - 1)
    return (src == m * d + r).astype(jnp.bfloat16)


def _qkv_kernel(x_ref, g_ref, w_ref, *o_refs):
    tm = ATTN_TILE
    h = _rmsnorm(x_ref[...], g_ref[...]).astype(jnp.bfloat16)
    qkv = jnp.dot(h, w_ref[...], preferred_element_type=jnp.float32).astype(jnp.bfloat16)
    for gi, (o_ref, (_, d)) in enumerate(zip(o_refs, DILATED_GROUPS)):
        part = qkv[:, gi * ATTN_WIDTH:(gi + 1) * ATTN_WIDTH]
        if d > 1:
            part = jnp.dot(_residue_major_perm(tm, d), part,
                           preferred_element_type=jnp.float32).astype(jnp.bfloat16)
        o_ref[...] = part.reshape(d, tm // d, ATTN_WIDTH)


def _qkv_projection(x2, norm1, w_qkv, B, S):
    tm = ATTN_TILE
    nt = S // tm
    return pl.pallas_call(
        _qkv_kernel,
        grid=(B, nt),
        in_specs=[
            pl.BlockSpec((tm, D_MODEL), lambda b, j: (b * nt + j, 0)),
            pl.BlockSpec((1, D_MODEL), lambda b, j: (0, 0)),
            pl.BlockSpec((D_MODEL, QKV_WIDTH), lambda b, j: (0, 0)),
        ],
        out_specs=[pl.BlockSpec((None, d, tm // d, ATTN_WIDTH), lambda b, j: (b, 0, j, 0))
                   for _, d in DILATED_GROUPS],
        out_shape=[jax.ShapeDtypeStruct((B, d, S // d, ATTN_WIDTH), jnp.bfloat16)
                   for _, d in DILATED_GROUPS],
        compiler_params=pltpu.CompilerParams(
            dimension_semantics=("parallel", "parallel"), vmem_limit_bytes=VMEM_LIMIT),
        name="qkv_projection",
    )(x2, norm1, w_qkv)


def _attn_kernel(q_ref, kp_ref, kc_ref, vp_ref, vc_ref, bias0_ref, bias_ref, o_ref, lse_ref):
    q = q_ref[...] * (HEAD_DIM ** -0.5)
    k = jnp.concatenate([kp_ref[...], kc_ref[...]], axis=0)
    v = jnp.concatenate([vp_ref[...], vc_ref[...]], axis=0)
    for b in range(ATTN_QB):
        rows = slice(b * ATTN_BLOCK, (b + 1) * ATTN_BLOCK)
        keys = slice(b * ATTN_BLOCK, (b + 2) * ATTN_BLOCK)
        outs, lses = [], []
        for h in range(HEADS_PER_GROUP):
            hs = slice(h * HEAD_DIM, (h + 1) * HEAD_DIM)
            bias = bias0_ref[h] if b == 0 else bias_ref[h]
            s = lax.dot_general(q[rows, hs], k[keys, hs], (((1,), (1,)), ((), ())),
                                preferred_element_type=jnp.float32) + bias
            m = jnp.max(s, axis=-1, keepdims=True)
            p = jnp.exp(s - m)
            l = jnp.sum(p, axis=-1, keepdims=True)
            o = jnp.dot(p.astype(jnp.bfloat16), v[keys, hs], preferred_element_type=jnp.float32) / l
            outs.append(o)
            lses.append(jnp.broadcast_to(m + jnp.log(l), (ATTN_BLOCK, HEAD_DIM)))
        o_ref[rows, :] = jnp.concatenate(outs, axis=1).astype(o_ref.dtype)
        lse_ref[rows, :] = jnp.concatenate(lses, axis=1)


def _attention_group(qkv_g, bias, gi, dilation):
    B, _, L, _ = qkv_g.shape
    rows = ATTN_QB * ATTN_BLOCK
    cur = lambda which: pl.BlockSpec((None, None, rows, GROUP_WIDTH), lambda b, r, n: (b, r, n, which))
    prev = lambda which: pl.BlockSpec((None, None, ATTN_BLOCK, GROUP_WIDTH),
                                      lambda b, r, n: (b, r, jnp.maximum(n * ATTN_QB - 1, 0), which))
    bias_blk = (None, HEADS_PER_GROUP, ATTN_BLOCK, 2 * ATTN_BLOCK)
    return pl.pallas_call(
        _attn_kernel,
        grid=(B, dilation, L // rows),
        in_specs=[
            cur(0), prev(1), cur(1), prev(2), cur(2),
            pl.BlockSpec(bias_blk, lambda b, r, n: (jnp.minimum(n, 1), 0, 0, 0)),
            pl.BlockSpec(bias_blk, lambda b, r, n: (1, 0, 0, 0)),
        ],
        out_specs=[cur(0), cur(0)],
        out_shape=[
            jax.ShapeDtypeStruct((B, dilation, L, GROUP_WIDTH), jnp.bfloat16),
            jax.ShapeDtypeStruct((B, dilation, L, GROUP_WIDTH), jnp.float32),
        ],
        compiler_params=pltpu.CompilerParams(
            dimension_semantics=("parallel", "parallel", "arbitrary"), vmem_limit_bytes=VMEM_LIMIT),
        name=f"dilated_attention_g{gi}",
    )(qkv_g, qkv_g, qkv_g, qkv_g, qkv_g, bias, bias)


def _t5_causal_bucket(dist):
    max_exact = N_REL_BUCKETS // 2
    d_f = jnp.maximum(dist, max_exact).astype(jnp.float32)
    large = max_exact + (jnp.log(d_f / max_exact) / math.log(REL_MAX_DISTANCE / max_exact)
                         * (N_REL_BUCKETS - max_exact)).astype(jnp.int32)
    large = jnp.minimum(large, N_REL_BUCKETS - 1)
    return jnp.where(dist < max_exact, dist, large)


def _attention_bias_table(rel_bias_g, window, dilation):
    nw = window // dilation
    qi = jnp.arange(ATTN_BLOCK)[:, None]
    ki = jnp.arange(2 * ATTN_BLOCK)[None, :]
    dist = ATTN_BLOCK + qi - ki
    band = (dist >= 0) & (dist <= nw)
    bucket = _t5_causal_bucket(jnp.maximum(dist, 0) * dilation)
    onehot = (bucket[None, :, :, None] == jnp.arange(N_REL_BUCKETS)).astype(jnp.float32)
    bias = jnp.sum(onehot * rel_bias_g.astype(jnp.float32).T[:, None, None, :], axis=-1)
    later = jnp.where(band[None], bias, NEG_INF)
    first = jnp.where((band & (ki >= ATTN_BLOCK))[None], bias, NEG_INF)
    return jnp.stack([first, later])


def _gelu_tanh(x):
    return 0.5 * x * (1.0 + jnp.tanh(math.sqrt(2.0 / math.pi) * (x + 0.044715 * (x * x * x))))


def _softplus(x):
    return jnp.maximum(x, 0.0) + jnp.log(1.0 + jnp.exp(-jnp.abs(x)))


def _natural_order(blk_ref, d, slabs):
    if d == 1:
        return blk_ref[0].astype(jnp.float32)
    per = MIX_TILE // d
    for r in range(d):
        rows = blk_ref[r].astype(jnp.float32)
        for c in range(GROUP_WIDTH // LANES):
            slabs[c, pl.ds(r, per, stride=d), :] = rows[:, c * LANES:(c + 1) * LANES]
    return jnp.concatenate([slabs[c] for c in range(GROUP_WIDTH // LANES)], axis=1)


def _mix_tile(x_ref, o1_ref, o2_ref, o3_ref, l1_ref, l2_ref, l3_ref,
              n1_ref, win_ref, cw_ref, cb_ref, wg_ref, brg_ref, big_ref, lam_ref,
              wpa_ref, wpl_ref, wout_ref, n2_ref, wrh_ref, wrl_ref,
              x1_ref, xs_ref, route_ref, cnt_ref,
              xbuf, a_sc, b_sc, h_sc, *slabs):
    tm = MIX_TILE
    first_tile = pl.program_id(0) == 0

    @pl.when(first_tile)
    def _():
        xbuf[0:SUBLANES, :] = jnp.zeros((SUBLANES, LRU_WIDTH), jnp.float32)
        h_sc[...] = jnp.zeros_like(h_sc)

    x = x_ref[...]
    h = _rmsnorm(x, n1_ref[...]).astype(jnp.bfloat16)
    proj = jnp.dot(h, win_ref[...], preferred_element_type=jnp.float32)
    xr = proj[:, 0:LRU_WIDTH]
    g_lru = proj[:, LRU_WIDTH:2 * LRU_WIDTH]
    g_a = proj[:, 2 * LRU_WIDTH:2 * LRU_WIDTH + D_MODEL]
    g_b = proj[:, 2 * LRU_WIDTH + D_MODEL:]

    yield
    xbuf[SUBLANES:SUBLANES + tm, :] = xr
    xc = xr * cw_ref[CONV_WIDTH - 1:CONV_WIDTH, :] + cb_ref[...]
    for j in range(CONV_WIDTH - 1):
        back = CONV_WIDTH - 1 - j
        xc = xc + xbuf[SUBLANES - back:SUBLANES - back + tm, :] * cw_ref[j:j + 1, :]
    xbuf[0:SUBLANES, :] = xbuf[tm:tm + SUBLANES, :]

    xcb = xc.astype(jnp.bfloat16)
    r_parts, i_parts = [], []
    for c in range(N_LRU_CHUNKS):
        ri = jnp.dot(xcb[:, c * LRU_CHUNK:(c + 1) * LRU_CHUNK], wg_ref[c],
                     preferred_element_type=jnp.float32)
        r_parts.append(ri[:, :LRU_CHUNK])
        i_parts.append(ri[:, LRU_CHUNK:])
    yield
    r = jax.nn.sigmoid(jnp.concatenate(r_parts, axis=1) + brg_ref[...])
    ig = jax.nn.sigmoid(jnp.concatenate(i_parts, axis=1) + big_ref[...])
    log_a = (-LRU_C * _softplus(-lam_ref[...])) * r
    a = jnp.exp(log_a)
    gap = 1.0 - a * a
    root = jnp.where(gap > 0.0, gap * lax.rsqrt(gap), 0.0)
    bb = root * (ig * xc)

    yield
    row8 = lax.broadcasted_iota(jnp.int32, (tm, 1), 0) % SUBLANES
    for k in (1, 2, 4):
        a_s = pltpu.roll(a, k, axis=0)
        b_s = pltpu.roll(bb, k, axis=0)
        take = row8 >= k
        bb = jnp.where(take, a * b_s + bb, bb)
        a = jnp.where(take, a * a_s, a)
    a_sc[...] = a
    b_sc[...] = bb

    yield
    carry = h_sc[...]
    for g in range(tm // SUBLANES):
        rows = slice(g * SUBLANES, (g + 1) * SUBLANES)
        hh = a_sc[rows, :] * carry + b_sc[rows, :]
        b_sc[rows, :] = hh
        carry = hh[SUBLANES - 1:SUBLANES, :]
    h_sc[...] = carry
    yield
    lru = (b_sc[...] * _gelu_tanh(g_lru)).astype(jnp.bfloat16)
    p_lru = jnp.dot(lru, wpl_ref[...], preferred_element_type=jnp.float32)

    yield
    dil = [d for _, d in DILATED_GROUPS]
    o1, o2, o3 = (_natural_order(ref, d, slabs[2 * g]) for g, (ref, d) in
                  enumerate(zip((o1_ref, o2_ref, o3_ref), dil)))
    l1, l2, l3 = (_natural_order(ref, d, slabs[2 * g + 1]) for g, (ref, d) in
                  enumerate(zip((l1_ref, l2_ref, l3_ref), dil)))
    lm = jnp.maximum(jnp.maximum(l1, l2), l3)
    e1, e2, e3 = jnp.exp(l1 - lm), jnp.exp(l2 - lm), jnp.exp(l3 - lm)
    attn = (e1 * o1 + e2 * o2 + e3 * o3) / (e1 + e2 + e3)
    p_attn = jnp.dot(attn.astype(jnp.bfloat16), wpa_ref[...], preferred_element_type=jnp.float32)

    yield
    merged = jax.nn.sigmoid(g_a) * p_attn + jax.nn.sigmoid(g_b) * p_lru
    x1 = x + jnp.dot(merged.astype(jnp.bfloat16), wout_ref[...], preferred_element_type=jnp.float32)
    x1_ref[...] = x1
    h2 = _rmsnorm(x1, n2_ref[...])

    yield
    h2_hi = h2.astype(jnp.bfloat16)
    h2_lo = (h2 - h2_hi.astype(jnp.float32)).astype(jnp.bfloat16)
    logits = (jnp.dot(h2_hi, wrh_ref[...], preferred_element_type=jnp.float32)
              + (jnp.dot(h2_lo, wrh_ref[...], preferred_element_type=jnp.float32)
                 + jnp.dot(h2_hi, wrl_ref[...], preferred_element_type=jnp.float32)))
    lane = lax.broadcasted_iota(jnp.int32, (tm, LANES), 1)
    big = jnp.int32(LANES)
    lowest = jnp.float32(-3.0e38)
    is_g = (lane >= GROUP_LOGIT_LANE0) & (lane < GROUP_LOGIT_LANE0 + N_EXPERT_GROUPS)
    gl = jnp.where(is_g, logits, lowest)
    gmax = jnp.max(gl, axis=-1, keepdims=True)
    gsel = jnp.min(jnp.where(gl == gmax, lane, big), axis=-1, keepdims=True) - GROUP_LOGIT_LANE0
    p_sel = 1.0 / jnp.sum(jnp.where(is_g, jnp.exp(gl - gmax), 0.0), axis=-1, keepdims=True)
    is_e = (lane >= gsel * EXPERTS_PER_GROUP) & (lane < (gsel + 1) * EXPERTS_PER_GROUP)
    el = jnp.where(is_e, logits, lowest)
    v1 = jnp.max(el, axis=-1, keepdims=True)
    i1 = jnp.min(jnp.where(el == v1, lane, big), axis=-1, keepdims=True)
    el2 = jnp.where(lane == i1, lowest, el)
    v2 = jnp.max(el2, axis=-1, keepdims=True)
    i2 = jnp.min(jnp.where(el2 == v2, lane, big), axis=-1, keepdims=True)
    t = jnp.exp(v2 - v1)
    g0 = p_sel / (1.0 + t)
    g1 = p_sel * t / (1.0 + t)

    yield
    oh0 = lane == i1
    oh1 = lane == i2
    both = (oh0 | oh1).astype(jnp.float32)
    tri = (lax.broadcasted_iota(jnp.int32, (tm, tm), 0)
           > lax.broadcasted_iota(jnp.int32, (tm, tm), 1)).astype(jnp.bfloat16)
    before = jnp.dot(tri, both.astype(jnp.bfloat16), preferred_element_type=jnp.float32)
    cnt = jnp.sum(both, axis=0, keepdims=True)
    padded = jnp.floor((cnt + (MOE_CHUNK - 1.0)) * (1.0 / MOE_CHUNK)) * MOE_CHUNK
    upper = (lax.broadcasted_iota(jnp.int32, (LANES, LANES), 0)
             < lax.broadcasted_iota(jnp.int32, (LANES, LANES), 1)).astype(jnp.bfloat16)
    loff = jnp.dot(jnp.broadcast_to(padded, (SUBLANES, LANES)).astype(jnp.bfloat16), upper,
                   preferred_element_type=jnp.float32)[0:1, :]
    base = before + loff
    slot0 = jnp.sum(jnp.where(oh0, base, 0.0), axis=-1, keepdims=True)
    slot1 = jnp.sum(jnp.where(oh1, base, 0.0), axis=-1, keepdims=True)
    cnt_ref[...] = jnp.broadcast_to(cnt, cnt_ref.shape)

    slot_id = lax.broadcasted_iota(jnp.int32, (tm, MOE_SLOTS), 1)
    place = ((slot_id == slot0.astype(jnp.int32)) | (slot_id == slot1.astype(jnp.int32)))
    xs_ref[...] = _pack_bf16_pairs(
        lax.dot_general(place.astype(jnp.bfloat16), h2_hi, (((0,), (0,)), ((), ())),
                        preferred_element_type=jnp.float32))

    route = jnp.zeros((tm, LANES), jnp.float32)
    for lane_id, val in ((ROUTE_S0, slot0), (ROUTE_S1, slot1), (ROUTE_G0, g0), (ROUTE_G1, g1)):
        route = jnp.where(lane == lane_id, val, route)
    route_ref[...] = route


_DONE = object()
N_MIXER_TILED_INPUTS = 1 + 2 * N_GROUPS
N_MIXER_OUTPUTS = 4


def _mixer_kernel(*refs):
    n_batch = refs[0].shape[0]
    n_weights = len(refs) - N_MIXER_TILED_INPUTS - N_MIXER_OUTPUTS - n_batch * _MIXER_SCRATCH_PER_SEQ
    tiled = refs[:N_MIXER_TILED_INPUTS]
    weights = refs[N_MIXER_TILED_INPUTS:N_MIXER_TILED_INPUTS + n_weights]
    outs = refs[N_MIXER_TILED_INPUTS + n_weights:N_MIXER_TILED_INPUTS + n_weights + N_MIXER_OUTPUTS]
    scratch = refs[N_MIXER_TILED_INPUTS + n_weights + N_MIXER_OUTPUTS:]
    stages = []
    for b in range(n_batch):
        own = scratch[b * _MIXER_SCRATCH_PER_SEQ:(b + 1) * _MIXER_SCRATCH_PER_SEQ]
        stages.append(_mix_tile(*[r.at[b] for r in tiled], *weights, *[r.at[b] for r in outs], *own))
    while stages:
        for gen in list(stages):
            if next(gen, _DONE) is _DONE:
                stages.remove(gen)


def _mixer_scratch(tm):
    return [
        pltpu.VMEM((tm + 2 * SUBLANES, LRU_WIDTH), jnp.float32),
        pltpu.VMEM((tm, LRU_WIDTH), jnp.float32),
        pltpu.VMEM((tm, LRU_WIDTH), jnp.float32),
        pltpu.VMEM((1, LRU_WIDTH), jnp.float32),
    ] + [pltpu.VMEM((GROUP_WIDTH // LANES, tm, LANES), jnp.float32)] * (2 * N_GROUPS)


_MIXER_SCRATCH_PER_SEQ = len(_mixer_scratch(MIX_TILE))


def _token_mixer(x, attn_o, attn_lse, weights):
    B, S, _ = x.shape
    tm = MIX_TILE
    nt = S // tm
    rows = lambda j: (0, j, 0)

    def resident(shape):
        return pl.BlockSpec(shape, lambda j: (0,) * len(shape), pipeline_mode=pl.Buffered(1))

    in_specs = [pl.BlockSpec((B, tm, D_MODEL), rows)]
    group_specs = [pl.BlockSpec((B, d, tm // d, GROUP_WIDTH), lambda j: (0, 0, j, 0))
                   for _, d in DILATED_GROUPS]
    in_specs += group_specs + group_specs
    in_specs += [resident(w.shape) for w in weights]
    x1, xs, route, counts = pl.pallas_call(
        _mixer_kernel,
        grid=(nt,),
        in_specs=in_specs,
        out_specs=[
            pl.BlockSpec((B, tm, D_MODEL), rows),
            pl.BlockSpec((B, MOE_SLOTS, D_MODEL // 2), rows),
            pl.BlockSpec((B, tm, LANES), rows),
            pl.BlockSpec((B, None, SUBLANES, LANES), lambda j: (0, j, 0, 0)),
        ],
        out_shape=[
            jax.ShapeDtypeStruct((B, S, D_MODEL), jnp.float32),
            jax.ShapeDtypeStruct((B, nt * MOE_SLOTS, D_MODEL // 2), jnp.uint32),
            jax.ShapeDtypeStruct((B, S, LANES), jnp.float32),
            jax.ShapeDtypeStruct((B, nt, SUBLANES, LANES), jnp.float32),
        ],
        scratch_shapes=_mixer_scratch(tm) * B,
        compiler_params=pltpu.CompilerParams(
            dimension_semantics=("arbitrary",), vmem_limit_bytes=VMEM_LIMIT),
        name="token_mixer",
    )(x, *attn_o, *attn_lse, *weights)
    return (x1.reshape(B * S, D_MODEL), xs.reshape(B * nt * MOE_SLOTS, D_MODEL // 2),
            route.reshape(B * S, LANES), counts.reshape(B * nt, SUBLANES, LANES))


def _moe_tables(cnt_tile):
    ntiles = cnt_tile.shape[0]
    nch = (cnt_tile + MOE_CHUNK - 1) // MOE_CHUNK
    lo_c = jnp.cumsum(nch, axis=1) - nch
    per_expert = jnp.sum(nch, axis=0)
    region = ((per_expert + MOE_BLOCK_CHUNKS - 1) // MOE_BLOCK_CHUNKS) * MOE_BLOCK_CHUNKS
    pend = jnp.cumsum(region)
    glob = (pend - region)[None, :] + jnp.cumsum(nch, axis=0) - nch

    max_rows = TOP_K * ntiles * MIX_TILE + ntiles * N_EXPERTS * (MOE_CHUNK - 1)
    max_chunks = -(-max_rows // MOE_CHUNK) + N_EXPERTS * (MOE_BLOCK_CHUNKS - 1)
    max_blocks = -(-max_chunks // MOE_BLOCK_CHUNKS)
    max_chunks = max_blocks * MOE_BLOCK_CHUNKS

    seg_start = glob.T.reshape(-1)
    seg_src = (jnp.arange(ntiles, dtype=jnp.int32)[:, None] * TILE_CHUNKS + lo_c).T.reshape(-1)
    step = jnp.diff(seg_src - seg_start, prepend=0)
    g = jnp.arange(max_chunks, dtype=jnp.int32)
    src = g + jnp.sum(jnp.where(seg_start[None, :] <= g[:, None], step[None, :], 0), axis=1)
    src = jnp.clip(src, 0, ntiles * TILE_CHUNKS - 1)

    lstep = jnp.diff(glob - lo_c, axis=1, prepend=0)
    lc = jnp.arange(TILE_CHUNKS, dtype=jnp.int32)
    comb = lc[None, :] + jnp.sum(
        jnp.where(lo_c[:, None, :] <= lc[None, :, None], lstep[:, None, :], 0), axis=2)
    comb = jnp.clip(comb, 0, max_chunks - 1).reshape(-1)

    eidx = jnp.arange(N_EXPERTS, dtype=jnp.int32)
    nonempty = region > 0
    later = jnp.where(nonempty[None, :] & (eidx[None, :] > eidx[:, None]), eidx[None, :], N_EXPERTS)
    next_expert = jnp.min(later, axis=1)
    next_expert = jnp.where(next_expert == N_EXPERTS, -1, next_expert)
    wslot = (jnp.cumsum(nonempty.astype(jnp.int32)) - nonempty.astype(jnp.int32)) % 2
    last_expert = jnp.max(jnp.where(nonempty, eidx, 0))
    blk0 = jnp.arange(max_blocks, dtype=jnp.int32) * MOE_BLOCK_CHUNKS
    block_expert = jnp.minimum(jnp.sum((pend[None, :] <= blk0[:, None]).astype(jnp.int32), axis=1),
                               last_expert)
    of_block = lambda per_expert: jnp.sum(
        jnp.where(block_expert[:, None] == eidx[None, :], per_expert[None, :], 0), axis=1)
    n_valid = pend[-1:] // MOE_BLOCK_CHUNKS
    as_i32 = lambda v: v.astype(jnp.int32)
    expert_tables = (as_i32(block_expert), as_i32(of_block(next_expert)), as_i32(of_block(wslot)),
                     as_i32(n_valid), as_i32(src))
    return expert_tables, as_i32(comb)


def _chunk_copy(src_hbm, src_chunk, dst, dst_chunk, sem):
    rows = lambda c: pl.ds(pl.multiple_of(c * MOE_CHUNK, MOE_CHUNK), MOE_CHUNK)
    return pltpu.make_async_copy(src_hbm.at[rows(src_chunk), :], dst.at[rows(dst_chunk), :], sem)


def _expert_kernel(be_ref, nxt_ref, ws_ref, nv_ref, src_ref, xs_hbm, wgu_hbm, wd_hbm, ys_ref,
                   xbuf, wgu_f32, wd_f32, wgu_bf, wd_bf, sems, wsems):
    i = pl.program_id(0)
    valid = i < nv_ref[0]

    def gather(blk, slot):
        return [_chunk_copy(xs_hbm, src_ref[blk * MOE_BLOCK_CHUNKS + c], xbuf.at[slot], c, sems.at[slot])
                for c in range(MOE_BLOCK_CHUNKS)]

    def fetch_weights(e, slot):
        return [pltpu.make_async_copy(wgu_hbm.at[e], wgu_f32.at[slot], wsems.at[slot]),
                pltpu.make_async_copy(wd_hbm.at[e], wd_f32.at[slot], wsems.at[slot])]

    @pl.when(i == 0)
    def _():
        for cp in gather(0, 0) + fetch_weights(be_ref[0], 0):
            cp.start()

    @pl.when(i + 1 < nv_ref[0])
    def _():
        for cp in gather(i + 1, (i + 1) % 2):
            cp.start()

    new_expert = valid & ((i == 0) | (be_ref[i] != be_ref[jnp.maximum(i - 1, 0)]))

    @pl.when(new_expert)
    def _():
        ws = ws_ref[i]
        for cp in fetch_weights(be_ref[i], ws):
            cp.wait()
        wgu_bf[...] = wgu_f32[ws].astype(jnp.bfloat16)
        wd_bf[...] = wd_f32[ws].astype(jnp.bfloat16)

        @pl.when(nxt_ref[i] >= 0)
        def _():
            for cp in fetch_weights(nxt_ref[i], 1 - ws):
                cp.start()

    @pl.when(valid)
    def _():
        slot = i % 2
        for cp in gather(i, slot):
            cp.wait()
        x_lo, x_hi = _unpack_bf16_pairs(xbuf[slot])
        half = D_MODEL // 2
        gu = (jnp.dot(x_lo, wgu_bf[:half, :], preferred_element_type=jnp.float32)
              + jnp.dot(x_hi, wgu_bf[half:, :], preferred_element_type=jnp.float32))
        g = gu[:, :D_EXPERT]
        u = gu[:, D_EXPERT:]
        act = (g * jax.nn.sigmoid(g) * u).astype(jnp.bfloat16)
        y = jnp.dot(act, wd_bf[...], preferred_element_type=jnp.float32)
        ys_ref[...] = _pack_bf16_pairs(y.astype(jnp.bfloat16).astype(jnp.float32))

    @pl.when(jnp.logical_not(valid))
    def _():
        ys_ref[...] = jnp.zeros_like(ys_ref)


def _experts(tables, xs, w_gate_up, w_down):
    block_expert, block_next, block_wslot, n_valid, src = tables
    nblk = block_expert.shape[0]
    return pl.pallas_call(
        _expert_kernel,
        grid_spec=pltpu.PrefetchScalarGridSpec(
            num_scalar_prefetch=5,
            grid=(nblk,),
            in_specs=[pl.BlockSpec(memory_space=pl.ANY)] * 3,
            out_specs=pl.BlockSpec((MOE_TILE, D_MODEL // 2), lambda i, *_: (i, 0)),
            scratch_shapes=[
                pltpu.VMEM((2, MOE_TILE, D_MODEL // 2), jnp.uint32),
                pltpu.VMEM((2, D_MODEL, 2 * D_EXPERT), jnp.float32),
                pltpu.VMEM((2, D_EXPERT, D_MODEL), jnp.float32),
                pltpu.VMEM((D_MODEL, 2 * D_EXPERT), jnp.bfloat16),
                pltpu.VMEM((D_EXPERT, D_MODEL), jnp.bfloat16),
                pltpu.SemaphoreType.DMA((2,)),
                pltpu.SemaphoreType.DMA((2,)),
            ],
        ),
        out_shape=jax.ShapeDtypeStruct((nblk * MOE_TILE, D_MODEL // 2), jnp.uint32),
        compiler_params=pltpu.CompilerParams(
            dimension_semantics=("arbitrary",), vmem_limit_bytes=VMEM_LIMIT),
        name="moe_experts",
    )(block_expert, block_next, block_wslot, n_valid, src, xs, w_gate_up, w_down)


def _combine_kernel(comb_ref, ys_hbm, x1_ref, route_ref, nf_ref, out_ref, ybuf, sems):
    tc = MIX_TILE
    i = pl.program_id(0)
    last = pl.num_programs(0) - 1

    def gather(tile, slot):
        return [_chunk_copy(ys_hbm, comb_ref[tile * TILE_CHUNKS + lc], ybuf.at[slot], lc, sems.at[slot])
                for lc in range(TILE_CHUNKS)]

    @pl.when(i == 0)
    def _():
        for cp in gather(0, 0):
            cp.start()

    nxt = jnp.minimum(i + 1, last)
    for cp in gather(nxt, (i + 1) % 2):
        cp.start()

    slot = i % 2
    for cp in gather(i, slot):
        cp.wait()

    y_lo, y_hi = _unpack_bf16_pairs(ybuf[slot])
    route = route_ref[...]
    slot_id = lax.broadcasted_iota(jnp.int32, (tc, MOE_SLOTS), 1)

    def pick_rows(lane):
        sel = (slot_id == route[:, lane:lane + 1].astype(jnp.int32)).astype(jnp.bfloat16)
        return jnp.concatenate([jnp.dot(sel, y_lo, preferred_element_type=jnp.float32),
                                jnp.dot(sel, y_hi, preferred_element_type=jnp.float32)], axis=1)

    y0 = pick_rows(ROUTE_S0)
    y1 = pick_rows(ROUTE_S1)
    g0 = route[:, ROUTE_G0:ROUTE_G0 + 1]
    g1 = route[:, ROUTE_G1:ROUTE_G1 + 1]
    out_ref[...] = _rmsnorm(x1_ref[...] + g0 * y0 + g1 * y1, nf_ref[...])

    @pl.when(i == last)
    def _():
        for cp in gather(nxt, (i + 1) % 2):
            cp.wait()


def _combine(comb, ys, x1, route, norm_f):
    T = x1.shape[0]
    tc = MIX_TILE
    return pl.pallas_call(
        _combine_kernel,
        grid_spec=pltpu.PrefetchScalarGridSpec(
            num_scalar_prefetch=1,
            grid=(T // tc,),
            in_specs=[
                pl.BlockSpec(memory_space=pl.ANY),
                pl.BlockSpec((tc, D_MODEL), lambda i, c: (i, 0)),
                pl.BlockSpec((tc, LANES), lambda i, c: (i, 0)),
                pl.BlockSpec((1, D_MODEL), lambda i, c: (0, 0)),
            ],
            out_specs=pl.BlockSpec((tc, D_MODEL), lambda i, c: (i, 0)),
            scratch_shapes=[
                pltpu.VMEM((2, MOE_SLOTS, D_MODEL // 2), jnp.uint32),
                pltpu.SemaphoreType.DMA((2,)),
            ],
        ),
        out_shape=jax.ShapeDtypeStruct((T, D_MODEL), jnp.float32),
        compiler_params=pltpu.CompilerParams(
            dimension_semantics=("arbitrary",), vmem_limit_bytes=VMEM_LIMIT),
        name="moe_combine",
    )(comb, ys, x1, route, norm_f)


def _block_diag_gates(w_rg, w_ig):
    def bd(w):
        w4 = w.reshape(N_LRU_CHUNKS, LRU_CHUNK_HEADS, LRU_HEAD_DIM, LRU_HEAD_DIM)
        eye = jnp.eye(LRU_CHUNK_HEADS, dtype=w.dtype)
        return jnp.einsum('chij,hk->chikj', w4, eye).reshape(N_LRU_CHUNKS, LRU_CHUNK, LRU_CHUNK)
    return jnp.concatenate([bd(w_rg), bd(w_ig)], axis=-1).astype(jnp.bfloat16)


def kernel(x, rel_bias, norm1, w_in, conv_w, conv_b, w_rg, b_rg, w_ig, b_ig, lru_lambda,
           w_proj_attn, w_proj_lru, w_out, norm2, w_router_group, w_router_expert,
           w_gate_up, w_down, norm_f):
    B, S, D = x.shape
    T = B * S
    assert w_in.shape[0] == 1, "single-layer block"
    layer = 0
    bf16 = jnp.bfloat16
    x2 = x.reshape(T, D)
    row = lambda v: v[layer].reshape(1, -1)
    w_qkv = (w_in[layer][:, :QKV_WIDTH].reshape(D, 3, N_GROUPS, GROUP_WIDTH)
             .transpose(0, 2, 1, 3).reshape(D, QKV_WIDTH).astype(bf16))
    w_rest = w_in[layer][:, QKV_WIDTH:].astype(bf16)

    qkv_groups = _qkv_projection(x2, row(norm1), w_qkv, B, S)
    attn_o, attn_lse = [], []
    for gi, (window, dilation) in enumerate(DILATED_GROUPS):
        hs = slice(gi * HEADS_PER_GROUP, (gi + 1) * HEADS_PER_GROUP)
        bias = _attention_bias_table(rel_bias[:, hs], window, dilation)
        o, lse = _attention_group(qkv_groups[gi], bias, gi, dilation)
        attn_o.append(o)
        attn_lse.append(lse)

    w_router = jnp.zeros((D, LANES), jnp.float32)
    w_router = w_router.at[:, :N_EXPERTS].set(w_router_expert[layer].astype(jnp.float32))
    w_router = w_router.at[:, GROUP_LOGIT_LANE0:GROUP_LOGIT_LANE0 + N_EXPERT_GROUPS].set(
        w_router_group[layer].astype(jnp.float32))
    w_router_hi = w_router.astype(bf16)
    weights = [
        row(norm1), w_rest, conv_w[layer], row(conv_b),
        _block_diag_gates(w_rg[layer], w_ig[layer]), row(b_rg), row(b_ig), row(lru_lambda),
        w_proj_attn[layer].astype(bf16), w_proj_lru[layer].astype(bf16), w_out[layer].astype(bf16),
        row(norm2), w_router_hi, (w_router - w_router_hi.astype(jnp.float32)).astype(bf16),
    ]
    x1, xs, route, counts = _token_mixer(x, attn_o, attn_lse, weights)

    expert_tables, comb = _moe_tables(counts[:, 0, :N_EXPERTS].astype(jnp.int32))
    ys = _experts(expert_tables, xs, w_gate_up[layer], w_down[layer])
    out = _combine(comb, ys, x1, route, norm_f.reshape(1, -1))
    return out.reshape(B, S, D)
```

```python
import math

import jax
import jax.numpy as jnp
from jax import lax
from jax.experimental import pallas as pl
from jax.experimental.pallas import tpu as pltpu

D_MODEL = 1024
HEAD_DIM = 64
HEADS_PER_GROUP = 4
DILATED_GROUPS = ((128, 1), (512, 4), (2048, 16))
N_GROUPS = len(DILATED_GROUPS)
N_ATTN_HEADS = HEADS_PER_GROUP * N_GROUPS
ATTN_WIDTH = N_ATTN_HEADS * HEAD_DIM
GROUP_WIDTH = HEADS_PER_GROUP * HEAD_DIM
QKV_WIDTH = 3 * ATTN_WIDTH
ATTN_BLOCK = 128
N_REL_BUCKETS = 32
REL_MAX_DISTANCE = 2048
NEG_INF = -1e30

LRU_WIDTH = D_MODEL
LRU_HEADS = 16
LRU_HEAD_DIM = LRU_WIDTH // LRU_HEADS
CONV_WIDTH = 4
LRU_C = 8.0
LRU_CHUNK_HEADS = 4
LRU_CHUNK = LRU_CHUNK_HEADS * LRU_HEAD_DIM
N_LRU_CHUNKS = LRU_WIDTH // LRU_CHUNK

N_EXPERT_GROUPS = 4
EXPERTS_PER_GROUP = 8
N_EXPERTS = N_EXPERT_GROUPS * EXPERTS_PER_GROUP
TOP_K = 2
D_EXPERT = 512
EPS = 1e-6

REST_WIDTH = 2 * LRU_WIDTH + 2 * D_MODEL
LANES = 128
SUBLANES = 8

ATTN_TILE = 256
ATTN_QB = 4
MIX_TILE = 256
MOE_TILE = 256
MOE_CHUNK = SUBLANES
MOE_BLOCK_CHUNKS = MOE_TILE // MOE_CHUNK
MOE_SLOTS = 768
assert MOE_SLOTS >= TOP_K * MIX_TILE + N_EXPERTS * (MOE_CHUNK - 1) and MOE_SLOTS % MOE_CHUNK == 0
TILE_CHUNKS = MOE_SLOTS // MOE_CHUNK
VMEM_LIMIT = 56 * 1024 * 1024

ROUTE_S0, ROUTE_S1, ROUTE_G0, ROUTE_G1 = range(4)
GROUP_LOGIT_LANE0 = N_EXPERTS


_DONE = object()


def _rmsnorm(x, g):
    return x * lax.rsqrt(jnp.mean(x * x, axis=-1, keepdims=True) + EPS) * g


def _pack_bf16_pairs(v):
    w = v.shape[1] // 2
    bits = lax.bitcast_convert_type(v, jnp.uint32)
    return bits[:, w:] | (bits[:, :w] >> 16)


def _unpack_bf16_pairs(p):
    lo = lax.bitcast_convert_type(p << 16, jnp.float32)
    hi = lax.bitcast_convert_type(p & jnp.uint32(0xFFFF0000), jnp.float32)
    return lo.astype(jnp.bfloat16), hi.astype(jnp.bfloat16)


def _residue_major_perm(n, d):
    per = n // d
    out = lax.broadcasted_iota(jnp.int32, (n, n), 0)
    src = lax.broadcasted_iota(jnp.int32, (n, n), 1)
    m = jnp.bitwise_and(out, per - 1)
    r = jnp.right_shift(out, per.bit_length() - 1)
    return (src == m * d + r).astype(jnp.bfloat16)


def _qkv_kernel(x_ref, g_ref, w_ref, *o_refs):
    tm = ATTN_TILE
    h = _rmsnorm(x_ref[...], g_ref[...]).astype(jnp.bfloat16)
    qkv = jnp.dot(h, w_ref[...], preferred_element_type=jnp.float32).astype(jnp.bfloat16)
    for gi, (o_ref, (_, d)) in enumerate(zip(o_refs, DILATED_GROUPS)):
        part = qkv[:, gi * ATTN_WIDTH:(gi + 1) * ATTN_WIDTH]
        if d > 1:
            part = jnp.dot(_residue_major_perm(tm, d), part,
                           preferred_element_type=jnp.float32).astype(jnp.bfloat16)
        o_ref[...] = part.reshape(d, tm // d, ATTN_WIDTH)


def _qkv_projection(x2, norm1, w_qkv, B, S):
    tm = ATTN_TILE
    nt = S // tm
    return pl.pallas_call(
        _qkv_kernel,
        grid=(B, nt),
        in_specs=[
            pl.BlockSpec((tm, D_MODEL), lambda b, j: (b * nt + j, 0)),
            pl.BlockSpec((1, D_MODEL), lambda b, j: (0, 0)),
            pl.BlockSpec((D_MODEL, QKV_WIDTH), lambda b, j: (0, 0)),
        ],
        out_specs=[pl.BlockSpec((None, d, tm // d, ATTN_WIDTH), lambda b, j: (b, 0, j, 0))
                   for _, d in DILATED_GROUPS],
        out_shape=[jax.ShapeDtypeStruct((B, d, S // d, ATTN_WIDTH), jnp.bfloat16)
                   for _, d in DILATED_GROUPS],
        compiler_params=pltpu.CompilerParams(
            dimension_semantics=("parallel", "parallel"), vmem_limit_bytes=VMEM_LIMIT),
        name="qkv_projection",
    )(x2, norm1, w_qkv)


def _attn_kernel(q_ref, kp_ref, kc_ref, vp_ref, vc_ref, bias0_ref, bias_ref, o_ref, lse_ref):
    q = q_ref[...] * (HEAD_DIM ** -0.5)
    k = jnp.concatenate([kp_ref[...], kc_ref[...]], axis=0)
    v = jnp.concatenate([vp_ref[...], vc_ref[...]], axis=0)
    pair_w = 2 * HEAD_DIM
    first_head = lax.broadcasted_iota(jnp.int32, (1, pair_w), 1) < HEAD_DIM
    def one_head(x, first):
        bits = pltpu.bitcast(x, jnp.uint32)
        keep = first_head if first else jnp.logical_not(first_head)
        return pltpu.bitcast(jnp.where(keep, bits, jnp.uint32(0)), jnp.bfloat16)

    ones = jnp.ones_like(k[:, :pair_w])
    operands = []
    for pair in range(HEADS_PER_GROUP // 2):
        cols = slice(pair * pair_w, (pair + 1) * pair_w)
        k2, v2 = k[:, cols], v[:, cols]
        k_heads = (one_head(k2, True), one_head(k2, False))
        v_heads = (jnp.concatenate([one_head(v2, True), one_head(ones, True)], axis=1),
                   jnp.concatenate([one_head(v2, False), one_head(ones, False)], axis=1))
        operands.append((cols, k_heads, v_heads))
    done = {}

    def head_pair(b, pair):
        rows = slice(b * ATTN_BLOCK, (b + 1) * ATTN_BLOCK)
        keys = slice(b * ATTN_BLOCK, (b + 2) * ATTN_BLOCK)
        cols, k_heads, v_heads = operands[pair]
        bias_ref_b = bias0_ref if b == 0 else bias_ref
        bias = jnp.concatenate([bias_ref_b[2 * pair], bias_ref_b[2 * pair + 1]], axis=1)
        k_both = jnp.concatenate([k_heads[0][keys], k_heads[1][keys]], axis=0)
        s = lax.dot_general(q[rows, cols], k_both, (((1,), (1,)), ((), ())),
                            preferred_element_type=jnp.float32) + bias
        yield
        n_keys = 2 * ATTN_BLOCK
        s0, s1 = s[:, :n_keys], s[:, n_keys:]
        m0 = jnp.max(s0, axis=-1, keepdims=True)
        m1 = jnp.max(s1, axis=-1, keepdims=True)
        p0 = jnp.exp(s0 - m0).astype(jnp.bfloat16)
        p1 = jnp.exp(s1 - m1).astype(jnp.bfloat16)
        yield
        acc = (jnp.dot(p0, v_heads[0][keys], preferred_element_type=jnp.float32)
               + jnp.dot(p1, v_heads[1][keys], preferred_element_type=jnp.float32))
        yield
        denom = acc[:, pair_w:]
        done[b, pair] = (acc[:, :pair_w] / denom, jnp.where(first_head, m0, m1) + jnp.log(denom))

    waiting = [head_pair(b, pair) for b in range(ATTN_QB) for pair in range(HEADS_PER_GROUP // 2)]
    live = []
    while waiting or live:
        if waiting:
            live.append(waiting.pop(0))
        for gen in list(live):
            if next(gen, _DONE) is _DONE:
                live.remove(gen)

    for b in range(ATTN_QB):
        rows = slice(b * ATTN_BLOCK, (b + 1) * ATTN_BLOCK)
        pairs = [done[b, pair] for pair in range(HEADS_PER_GROUP // 2)]
        o_ref[rows, :] = jnp.concatenate([o for o, _ in pairs], axis=1).astype(o_ref.dtype)
        lse_ref[rows, :] = jnp.concatenate([lse for _, lse in pairs], axis=1)


def _attention_group(qkv_g, bias, gi, dilation):
    B, _, L, _ = qkv_g.shape
    rows = ATTN_QB * ATTN_BLOCK
    cur = lambda which: pl.BlockSpec((None, None, rows, GROUP_WIDTH), lambda b, r, n: (b, r, n, which))
    prev = lambda which: pl.BlockSpec((None, None, ATTN_BLOCK, GROUP_WIDTH),
                                      lambda b, r, n: (b, r, jnp.maximum(n * ATTN_QB - 1, 0), which))
    bias_blk = (None, HEADS_PER_GROUP, ATTN_BLOCK, 2 * ATTN_BLOCK)
    return pl.pallas_call(
        _attn_kernel,
        grid=(B, dilation, L // rows),
        in_specs=[
            cur(0), prev(1), cur(1), prev(2), cur(2),
            pl.BlockSpec(bias_blk, lambda b, r, n: (jnp.minimum(n, 1), 0, 0, 0)),
            pl.BlockSpec(bias_blk, lambda b, r, n: (1, 0, 0, 0)),
        ],
        out_specs=[cur(0), cur(0)],
        out_shape=[
            jax.ShapeDtypeStruct((B, dilation, L, GROUP_WIDTH), jnp.bfloat16),
            jax.ShapeDtypeStruct((B, dilation, L, GROUP_WIDTH), jnp.float32),
        ],
        compiler_params=pltpu.CompilerParams(
            dimension_semantics=("parallel", "parallel", "arbitrary"), vmem_limit_bytes=VMEM_LIMIT),
        name=f"dilated_attention_g{gi}",
    )(qkv_g, qkv_g, qkv_g, qkv_g, qkv_g, bias, bias)


def _t5_causal_bucket(dist):
    max_exact = N_REL_BUCKETS // 2
    d_f = jnp.maximum(dist, max_exact).astype(jnp.float32)
    large = max_exact + (jnp.log(d_f / max_exact) / math.log(REL_MAX_DISTANCE / max_exact)
                         * (N_REL_BUCKETS - max_exact)).astype(jnp.int32)
    large = jnp.minimum(large, N_REL_BUCKETS - 1)
    return jnp.where(dist < max_exact, dist, large)


def _attention_bias_table(rel_bias_g, window, dilation):
    nw = window // dilation
    qi = jnp.arange(ATTN_BLOCK)[:, None]
    ki = jnp.arange(2 * ATTN_BLOCK)[None, :]
    dist = ATTN_BLOCK + qi - ki
    band = (dist >= 0) & (dist <= nw)
    bucket = _t5_causal_bucket(jnp.maximum(dist, 0) * dilation)
    onehot = (bucket[None, :, :, None] == jnp.arange(N_REL_BUCKETS)).astype(jnp.float32)
    bias = jnp.sum(onehot * rel_bias_g.astype(jnp.float32).T[:, None, None, :], axis=-1)
    later = jnp.where(band[None], bias, NEG_INF)
    first = jnp.where((band & (ki >= ATTN_BLOCK))[None], bias, NEG_INF)
    return jnp.stack([first, later])


def _gelu_tanh(x):
    return 0.5 * x * (1.0 + jnp.tanh(math.sqrt(2.0 / math.pi) * (x + 0.044715 * (x * x * x))))


def _softplus(x):
    return jnp.maximum(x, 0.0) + jnp.log(1.0 + jnp.exp(-jnp.abs(x)))


def _natural_order(blk_ref, d, slabs):
    if d == 1:
        return blk_ref[0].astype(jnp.float32)
    per = MIX_TILE // d
    for r in range(d):
        rows = blk_ref[r].astype(jnp.float32)
        for c in range(GROUP_WIDTH // LANES):
            slabs[c, pl.ds(r, per, stride=d), :] = rows[:, c * LANES:(c + 1) * LANES]
    return jnp.concatenate([slabs[c] for c in range(GROUP_WIDTH // LANES)], axis=1)


def _mix_tile(x_ref, o1_ref, o2_ref, o3_ref, l1_ref, l2_ref, l3_ref,
              n1_ref, win_ref, cw_ref, cb_ref, wg_ref, brg_ref, big_ref, lam_ref,
              wpa_ref, wpl_ref, wout_ref, n2_ref, wrh_ref, wrl_ref,
              x1_ref, xs_ref, route_ref, cnt_ref,
              xbuf, a_sc, b_sc, h_sc, *slabs):
    tm = MIX_TILE
    first_tile = pl.program_id(0) == 0

    @pl.when(first_tile)
    def _():
        xbuf[0:SUBLANES, :] = jnp.zeros((SUBLANES, LRU_WIDTH), jnp.float32)
        h_sc[...] = jnp.zeros_like(h_sc)

    x = x_ref[...]
    h = _rmsnorm(x, n1_ref[...]).astype(jnp.bfloat16)
    proj = jnp.dot(h, win_ref[...], preferred_element_type=jnp.float32)
    xr = proj[:, 0:LRU_WIDTH]
    g_lru = proj[:, LRU_WIDTH:2 * LRU_WIDTH]
    g_a = proj[:, 2 * LRU_WIDTH:2 * LRU_WIDTH + D_MODEL]
    g_b = proj[:, 2 * LRU_WIDTH + D_MODEL:]

    yield
    xbuf[SUBLANES:SUBLANES + tm, :] = xr
    xc = xr * cw_ref[CONV_WIDTH - 1:CONV_WIDTH, :] + cb_ref[...]
    for j in range(CONV_WIDTH - 1):
        back = CONV_WIDTH - 1 - j
        xc = xc + xbuf[SUBLANES - back:SUBLANES - back + tm, :] * cw_ref[j:j + 1, :]
    xbuf[0:SUBLANES, :] = xbuf[tm:tm + SUBLANES, :]

    xcb = xc.astype(jnp.bfloat16)
    r_parts, i_parts = [], []
    for c in range(N_LRU_CHUNKS):
        ri = jnp.dot(xcb[:, c * LRU_CHUNK:(c + 1) * LRU_CHUNK], wg_ref[c],
                     preferred_element_type=jnp.float32)
        r_parts.append(ri[:, :LRU_CHUNK])
        i_parts.append(ri[:, LRU_CHUNK:])
    yield
    r = jax.nn.sigmoid(jnp.concatenate(r_parts, axis=1) + brg_ref[...])
    ig = jax.nn.sigmoid(jnp.concatenate(i_parts, axis=1) + big_ref[...])
    log_a = (-LRU_C * _softplus(-lam_ref[...])) * r
    a = jnp.exp(log_a)
    gap = 1.0 - a * a
    root = jnp.where(gap > 0.0, gap * lax.rsqrt(gap), 0.0)
    bb = root * (ig * xc)

    yield
    row8 = lax.broadcasted_iota(jnp.int32, (tm, 1), 0) % SUBLANES
    for k in (1, 2, 4):
        a_s = pltpu.roll(a, k, axis=0)
        b_s = pltpu.roll(bb, k, axis=0)
        take = row8 >= k
        bb = jnp.where(take, a * b_s + bb, bb)
        a = jnp.where(take, a * a_s, a)
    a_sc[...] = a
    b_sc[...] = bb

    yield
    carry = h_sc[...]
    for g in range(tm // SUBLANES):
        rows = slice(g * SUBLANES, (g + 1) * SUBLANES)
        hh = a_sc[rows, :] * carry + b_sc[rows, :]
        b_sc[rows, :] = hh
        carry = hh[SUBLANES - 1:SUBLANES, :]
    h_sc[...] = carry
    yield
    lru = (b_sc[...] * _gelu_tanh(g_lru)).astype(jnp.bfloat16)
    p_lru = jnp.dot(lru, wpl_ref[...], preferred_element_type=jnp.float32)

    yield
    dil = [d for _, d in DILATED_GROUPS]
    o1, o2, o3 = (_natural_order(ref, d, slabs[2 * g]) for g, (ref, d) in
                  enumerate(zip((o1_ref, o2_ref, o3_ref), dil)))
    l1, l2, l3 = (_natural_order(ref, d, slabs[2 * g + 1]) for g, (ref, d) in
                  enumerate(zip((l1_ref, l2_ref, l3_ref), dil)))
    lm = jnp.maximum(jnp.maximum(l1, l2), l3)
    e1, e2, e3 = jnp.exp(l1 - lm), jnp.exp(l2 - lm), jnp.exp(l3 - lm)
    attn = (e1 * o1 + e2 * o2 + e3 * o3) / (e1 + e2 + e3)
    p_attn = jnp.dot(attn.astype(jnp.bfloat16), wpa_ref[...], preferred_element_type=jnp.float32)

    yield
    merged = jax.nn.sigmoid(g_a) * p_attn + jax.nn.sigmoid(g_b) * p_lru
    x1 = x + jnp.dot(merged.astype(jnp.bfloat16), wout_ref[...], preferred_element_type=jnp.float32)
    x1_ref[...] = x1
    h2 = _rmsnorm(x1, n2_ref[...])

    yield
    h2_hi = h2.astype(jnp.bfloat16)
    h2_lo = (h2 - h2_hi.astype(jnp.float32)).astype(jnp.bfloat16)
    logits = (jnp.dot(h2_hi, wrh_ref[...], preferred_element_type=jnp.float32)
              + (jnp.dot(h2_lo, wrh_ref[...], preferred_element_type=jnp.float32)
                 + jnp.dot(h2_hi, wrl_ref[...], preferred_element_type=jnp.float32)))
    lane = lax.broadcasted_iota(jnp.int32, (tm, LANES), 1)
    big = jnp.int32(LANES)
    lowest = jnp.float32(-3.0e38)
    is_g = (lane >= GROUP_LOGIT_LANE0) & (lane < GROUP_LOGIT_LANE0 + N_EXPERT_GROUPS)
    gl = jnp.where(is_g, logits, lowest)
    gmax = jnp.max(gl, axis=-1, keepdims=True)
    gsel = jnp.min(jnp.where(gl == gmax, lane, big), axis=-1, keepdims=True) - GROUP_LOGIT_LANE0
    p_sel = 1.0 / jnp.sum(jnp.where(is_g, jnp.exp(gl - gmax), 0.0), axis=-1, keepdims=True)
    is_e = (lane >= gsel * EXPERTS_PER_GROUP) & (lane < (gsel + 1) * EXPERTS_PER_GROUP)
    el = jnp.where(is_e, logits, lowest)
    v1 = jnp.max(el, axis=-1, keepdims=True)
    i1 = jnp.min(jnp.where(el == v1, lane, big), axis=-1, keepdims=True)
    el2 = jnp.where(lane == i1, lowest, el)
    v2 = jnp.max(el2, axis=-1, keepdims=True)
    i2 = jnp.min(jnp.where(el2 == v2, lane, big), axis=-1, keepdims=True)
    t = jnp.exp(v2 - v1)
    g0 = p_sel / (1.0 + t)
    g1 = p_sel * t / (1.0 + t)

    yield
    oh0 = lane == i1
    oh1 = lane == i2
    both = (oh0 | oh1).astype(jnp.float32)
    tri = (lax.broadcasted_iota(jnp.int32, (tm, tm), 0)
           > lax.broadcasted_iota(jnp.int32, (tm, tm), 1)).astype(jnp.bfloat16)
    before = jnp.dot(tri, both.astype(jnp.bfloat16), preferred_element_type=jnp.float32)
    cnt = jnp.sum(both, axis=0, keepdims=True)
    padded = jnp.floor((cnt + (MOE_CHUNK - 1.0)) * (1.0 / MOE_CHUNK)) * MOE_CHUNK
    upper = (lax.broadcasted_iota(jnp.int32, (LANES, LANES), 0)
             < lax.broadcasted_iota(jnp.int32, (LANES, LANES), 1)).astype(jnp.bfloat16)
    loff = jnp.dot(jnp.broadcast_to(padded, (SUBLANES, LANES)).astype(jnp.bfloat16), upper,
                   preferred_element_type=jnp.float32)[0:1, :]
    base = before + loff
    slot0 = jnp.sum(jnp.where(oh0, base, 0.0), axis=-1, keepdims=True)
    slot1 = jnp.sum(jnp.where(oh1, base, 0.0), axis=-1, keepdims=True)
    cnt_ref[...] = jnp.broadcast_to(cnt, cnt_ref.shape)

    slot_id = lax.broadcasted_iota(jnp.int32, (tm, MOE_SLOTS), 1)
    place = ((slot_id == slot0.astype(jnp.int32)) | (slot_id == slot1.astype(jnp.int32)))
    xs_ref[...] = _pack_bf16_pairs(
        lax.dot_general(place.astype(jnp.bfloat16), h2_hi, (((0,), (0,)), ((), ())),
                        preferred_element_type=jnp.float32))

    route = jnp.zeros((tm, LANES), jnp.float32)
    for lane_id, val in ((ROUTE_S0, slot0), (ROUTE_S1, slot1), (ROUTE_G0, g0), (ROUTE_G1, g1)):
        route = jnp.where(lane == lane_id, val, route)
    route_ref[...] = route


N_MIXER_TILED_INPUTS = 1 + 2 * N_GROUPS
N_MIXER_OUTPUTS = 4


def _mixer_kernel(*refs):
    n_batch = refs[0].shape[0]
    n_weights = len(refs) - N_MIXER_TILED_INPUTS - N_MIXER_OUTPUTS - n_batch * _MIXER_SCRATCH_PER_SEQ
    tiled = refs[:N_MIXER_TILED_INPUTS]
    weights = refs[N_MIXER_TILED_INPUTS:N_MIXER_TILED_INPUTS + n_weights]
    outs = refs[N_MIXER_TILED_INPUTS + n_weights:N_MIXER_TILED_INPUTS + n_weights + N_MIXER_OUTPUTS]
    scratch = refs[N_MIXER_TILED_INPUTS + n_weights + N_MIXER_OUTPUTS:]
    stages = []
    for b in range(n_batch):
        own = scratch[b * _MIXER_SCRATCH_PER_SEQ:(b + 1) * _MIXER_SCRATCH_PER_SEQ]
        stages.append(_mix_tile(*[r.at[b] for r in tiled], *weights, *[r.at[b] for r in outs], *own))
    while stages:
        for gen in list(stages):
            if next(gen, _DONE) is _DONE:
                stages.remove(gen)


def _mixer_scratch(tm):
    return [
        pltpu.VMEM((tm + 2 * SUBLANES, LRU_WIDTH), jnp.float32),
        pltpu.VMEM((tm, LRU_WIDTH), jnp.float32),
        pltpu.VMEM((tm, LRU_WIDTH), jnp.float32),
        pltpu.VMEM((1, LRU_WIDTH), jnp.float32),
    ] + [pltpu.VMEM((GROUP_WIDTH // LANES, tm, LANES), jnp.float32)] * (2 * N_GROUPS)


_MIXER_SCRATCH_PER_SEQ = len(_mixer_scratch(MIX_TILE))


def _token_mixer(x, attn_o, attn_lse, weights):
    B, S, _ = x.shape
    tm = MIX_TILE
    nt = S // tm
    rows = lambda j: (0, j, 0)

    def resident(shape):
        return pl.BlockSpec(shape, lambda j: (0,) * len(shape), pipeline_mode=pl.Buffered(1))

    in_specs = [pl.BlockSpec((B, tm, D_MODEL), rows)]
    group_specs = [pl.BlockSpec((B, d, tm // d, GROUP_WIDTH), lambda j: (0, 0, j, 0))
                   for _, d in DILATED_GROUPS]
    in_specs += group_specs + group_specs
    in_specs += [resident(w.shape) for w in weights]
    x1, xs, route, counts = pl.pallas_call(
        _mixer_kernel,
        grid=(nt,),
        in_specs=in_specs,
        out_specs=[
            pl.BlockSpec((B, tm, D_MODEL), rows),
            pl.BlockSpec((B, MOE_SLOTS, D_MODEL // 2), rows),
            pl.BlockSpec((B, tm, LANES), rows),
            pl.BlockSpec((B, None, SUBLANES, LANES), lambda j: (0, j, 0, 0)),
        ],
        out_shape=[
            jax.ShapeDtypeStruct((B, S, D_MODEL), jnp.float32),
            jax.ShapeDtypeStruct((B, nt * MOE_SLOTS, D_MODEL // 2), jnp.uint32),
            jax.ShapeDtypeStruct((B, S, LANES), jnp.float32),
            jax.ShapeDtypeStruct((B, nt, SUBLANES, LANES), jnp.float32),
        ],
        scratch_shapes=_mixer_scratch(tm) * B,
        compiler_params=pltpu.CompilerParams(
            dimension_semantics=("arbitrary",), vmem_limit_bytes=VMEM_LIMIT),
        name="token_mixer",
    )(x, *attn_o, *attn_lse, *weights)
    return (x1.reshape(B * S, D_MODEL), xs.reshape(B * nt * MOE_SLOTS, D_MODEL // 2),
            route.reshape(B * S, LANES), counts.reshape(B * nt, SUBLANES, LANES))


def _moe_tables(cnt_tile):
    ntiles = cnt_tile.shape[0]
    nch = (cnt_tile + MOE_CHUNK - 1) // MOE_CHUNK
    lo_c = jnp.cumsum(nch, axis=1) - nch
    per_expert = jnp.sum(nch, axis=0)
    region = ((per_expert + MOE_BLOCK_CHUNKS - 1) // MOE_BLOCK_CHUNKS) * MOE_BLOCK_CHUNKS
    pend = jnp.cumsum(region)
    glob = (pend - region)[None, :] + jnp.cumsum(nch, axis=0) - nch

    max_rows = TOP_K * ntiles * MIX_TILE + ntiles * N_EXPERTS * (MOE_CHUNK - 1)
    max_chunks = -(-max_rows // MOE_CHUNK) + N_EXPERTS * (MOE_BLOCK_CHUNKS - 1)
    max_blocks = -(-max_chunks // MOE_BLOCK_CHUNKS)
    max_chunks = max_blocks * MOE_BLOCK_CHUNKS

    seg_start = glob.T.reshape(-1)
    seg_src = (jnp.arange(ntiles, dtype=jnp.int32)[:, None] * TILE_CHUNKS + lo_c).T.reshape(-1)
    step = jnp.diff(seg_src - seg_start, prepend=0)
    g = jnp.arange(max_chunks, dtype=jnp.int32)
    src = g + jnp.sum(jnp.where(seg_start[None, :] <= g[:, None], step[None, :], 0), axis=1)
    src = jnp.clip(src, 0, ntiles * TILE_CHUNKS - 1)

    lstep = jnp.diff(glob - lo_c, axis=1, prepend=0)
    lc = jnp.arange(TILE_CHUNKS, dtype=jnp.int32)
    comb = lc[None, :] + jnp.sum(
        jnp.where(lo_c[:, None, :] <= lc[None, :, None], lstep[:, None, :], 0), axis=2)
    comb = jnp.clip(comb, 0, max_chunks - 1).reshape(-1)

    eidx = jnp.arange(N_EXPERTS, dtype=jnp.int32)
    nonempty = region > 0
    later = jnp.where(nonempty[None, :] & (eidx[None, :] > eidx[:, None]), eidx[None, :], N_EXPERTS)
    next_expert = jnp.min(later, axis=1)
    next_expert = jnp.where(next_expert == N_EXPERTS, -1, next_expert)
    wslot = (jnp.cumsum(nonempty.astype(jnp.int32)) - nonempty.astype(jnp.int32)) % 2
    last_expert = jnp.max(jnp.where(nonempty, eidx, 0))
    blk0 = jnp.arange(max_blocks, dtype=jnp.int32) * MOE_BLOCK_CHUNKS
    block_expert = jnp.minimum(jnp.sum((pend[None, :] <= blk0[:, None]).astype(jnp.int32), axis=1),
                               last_expert)
    of_block = lambda per_expert: jnp.sum(
        jnp.where(block_expert[:, None] == eidx[None, :], per_expert[None, :], 0), axis=1)
    n_valid = pend[-1:] // MOE_BLOCK_CHUNKS
    as_i32 = lambda v: v.astype(jnp.int32)
    expert_tables = (as_i32(block_expert), as_i32(of_block(next_expert)), as_i32(of_block(wslot)),
                     as_i32(n_valid), as_i32(src))
    return expert_tables, as_i32(comb)


def _chunk_copy(src_hbm, src_chunk, dst, dst_chunk, sem):
    rows = lambda c: pl.ds(pl.multiple_of(c * MOE_CHUNK, MOE_CHUNK), MOE_CHUNK)
    return pltpu.make_async_copy(src_hbm.at[rows(src_chunk), :], dst.at[rows(dst_chunk), :], sem)


def _expert_kernel(be_ref, nxt_ref, ws_ref, nv_ref, src_ref, xs_hbm, wgu_hbm, wd_hbm, ys_ref,
                   xbuf, wgu_f32, wd_f32, wgu_bf, wd_bf, sems, wsems):
    i = pl.program_id(0)
    valid = i < nv_ref[0]

    def gather(blk, slot):
        return [_chunk_copy(xs_hbm, src_ref[blk * MOE_BLOCK_CHUNKS + c], xbuf.at[slot], c, sems.at[slot])
                for c in range(MOE_BLOCK_CHUNKS)]

    def fetch_weights(e, slot):
        return [pltpu.make_async_copy(wgu_hbm.at[e], wgu_f32.at[slot], wsems.at[slot]),
                pltpu.make_async_copy(wd_hbm.at[e], wd_f32.at[slot], wsems.at[slot])]

    @pl.when(i == 0)
    def _():
        for cp in gather(0, 0) + fetch_weights(be_ref[0], 0):
            cp.start()

    @pl.when(i + 1 < nv_ref[0])
    def _():
        for cp in gather(i + 1, (i + 1) % 2):
            cp.start()

    new_expert = valid & ((i == 0) | (be_ref[i] != be_ref[jnp.maximum(i - 1, 0)]))

    @pl.when(new_expert)
    def _():
        ws = ws_ref[i]
        for cp in fetch_weights(be_ref[i], ws):
            cp.wait()
        wgu_bf[...] = wgu_f32[ws].astype(jnp.bfloat16)
        wd_bf[...] = wd_f32[ws].astype(jnp.bfloat16)

        @pl.when(nxt_ref[i] >= 0)
        def _():
            for cp in fetch_weights(nxt_ref[i], 1 - ws):
                cp.start()

    @pl.when(valid)
    def _():
        slot = i % 2
        for cp in gather(i, slot):
            cp.wait()
        x_lo, x_hi = _unpack_bf16_pairs(xbuf[slot])
        half = D_MODEL // 2
        gu = (jnp.dot(x_lo, wgu_bf[:half, :], preferred_element_type=jnp.float32)
              + jnp.dot(x_hi, wgu_bf[half:, :], preferred_element_type=jnp.float32))
        g = gu[:, :D_EXPERT]
        u = gu[:, D_EXPERT:]
        act = (g * jax.nn.sigmoid(g) * u).astype(jnp.bfloat16)
        y = jnp.dot(act, wd_bf[...], preferred_element_type=jnp.float32)
        ys_ref[...] = _pack_bf16_pairs(y.astype(jnp.bfloat16).astype(jnp.float32))

    @pl.when(jnp.logical_not(valid))
    def _():
        ys_ref[...] = jnp.zeros_like(ys_ref)


def _experts(tables, xs, w_gate_up, w_down):
    block_expert, block_next, block_wslot, n_valid, src = tables
    nblk = block_expert.shape[0]
    return pl.pallas_call(
        _expert_kernel,
        grid_spec=pltpu.PrefetchScalarGridSpec(
            num_scalar_prefetch=5,
            grid=(nblk,),
            in_specs=[pl.BlockSpec(memory_space=pl.ANY)] * 3,
            out_specs=pl.BlockSpec((MOE_TILE, D_MODEL // 2), lambda i, *_: (i, 0)),
            scratch_shapes=[
                pltpu.VMEM((2, MOE_TILE, D_MODEL // 2), jnp.uint32),
                pltpu.VMEM((2, D_MODEL, 2 * D_EXPERT), jnp.float32),
                pltpu.VMEM((2, D_EXPERT, D_MODEL), jnp.float32),
                pltpu.VMEM((D_MODEL, 2 * D_EXPERT), jnp.bfloat16),
                pltpu.VMEM((D_EXPERT, D_MODEL), jnp.bfloat16),
                pltpu.SemaphoreType.DMA((2,)),
                pltpu.SemaphoreType.DMA((2,)),
            ],
        ),
        out_shape=jax.ShapeDtypeStruct((nblk * MOE_TILE, D_MODEL // 2), jnp.uint32),
        compiler_params=pltpu.CompilerParams(
            dimension_semantics=("arbitrary",), vmem_limit_bytes=VMEM_LIMIT),
        name="moe_experts",
    )(block_expert, block_next, block_wslot, n_valid, src, xs, w_gate_up, w_down)


def _combine_kernel(comb_ref, ys_hbm, x1_ref, route_ref, nf_ref, out_ref, ybuf, sems):
    tc = MIX_TILE
    i = pl.program_id(0)
    last = pl.num_programs(0) - 1

    def gather(tile, slot):
        return [_chunk_copy(ys_hbm, comb_ref[tile * TILE_CHUNKS + lc], ybuf.at[slot], lc, sems.at[slot])
                for lc in range(TILE_CHUNKS)]

    @pl.when(i == 0)
    def _():
        for cp in gather(0, 0):
            cp.start()

    nxt = jnp.minimum(i + 1, last)
    for cp in gather(nxt, (i + 1) % 2):
        cp.start()

    slot = i % 2
    for cp in gather(i, slot):
        cp.wait()

    y_lo, y_hi = _unpack_bf16_pairs(ybuf[slot])
    route = route_ref[...]
    slot_id = lax.broadcasted_iota(jnp.int32, (tc, MOE_SLOTS), 1)

    def pick_rows(lane):
        sel = (slot_id == route[:, lane:lane + 1].astype(jnp.int32)).astype(jnp.bfloat16)
        return jnp.concatenate([jnp.dot(sel, y_lo, preferred_element_type=jnp.float32),
                                jnp.dot(sel, y_hi, preferred_element_type=jnp.float32)], axis=1)

    y0 = pick_rows(ROUTE_S0)
    y1 = pick_rows(ROUTE_S1)
    g0 = route[:, ROUTE_G0:ROUTE_G0 + 1]
    g1 = route[:, ROUTE_G1:ROUTE_G1 + 1]
    out_ref[...] = _rmsnorm(x1_ref[...] + g0 * y0 + g1 * y1, nf_ref[...])

    @pl.when(i == last)
    def _():
        for cp in gather(nxt, (i + 1) % 2):
            cp.wait()


def _combine(comb, ys, x1, route, norm_f):
    T = x1.shape[0]
    tc = MIX_TILE
    return pl.pallas_call(
        _combine_kernel,
        grid_spec=pltpu.PrefetchScalarGridSpec(
            num_scalar_prefetch=1,
            grid=(T // tc,),
            in_specs=[
                pl.BlockSpec(memory_space=pl.ANY),
                pl.BlockSpec((tc, D_MODEL), lambda i, c: (i, 0)),
                pl.BlockSpec((tc, LANES), lambda i, c: (i, 0)),
                pl.BlockSpec((1, D_MODEL), lambda i, c: (0, 0)),
            ],
            out_specs=pl.BlockSpec((tc, D_MODEL), lambda i, c: (i, 0)),
            scratch_shapes=[
                pltpu.VMEM((2, MOE_SLOTS, D_MODEL // 2), jnp.uint32),
                pltpu.SemaphoreType.DMA((2,)),
            ],
        ),
        out_shape=jax.ShapeDtypeStruct((T, D_MODEL), jnp.float32),
        compiler_params=pltpu.CompilerParams(
            dimension_semantics=("arbitrary",), vmem_limit_bytes=VMEM_LIMIT),
        name="moe_combine",
    )(comb, ys, x1, route, norm_f)


def _block_diag_gates(w_rg, w_ig):
    def bd(w):
        w4 = w.reshape(N_LRU_CHUNKS, LRU_CHUNK_HEADS, LRU_HEAD_DIM, LRU_HEAD_DIM)
        eye = jnp.eye(LRU_CHUNK_HEADS, dtype=w.dtype)
        return jnp.einsum('chij,hk->chikj', w4, eye).reshape(N_LRU_CHUNKS, LRU_CHUNK, LRU_CHUNK)
    return jnp.concatenate([bd(w_rg), bd(w_ig)], axis=-1).astype(jnp.bfloat16)


def kernel(x, rel_bias, norm1, w_in, conv_w, conv_b, w_rg, b_rg, w_ig, b_ig, lru_lambda,
           w_proj_attn, w_proj_lru, w_out, norm2, w_router_group, w_router_expert,
           w_gate_up, w_down, norm_f):
    B, S, D = x.shape
    T = B * S
    assert w_in.shape[0] == 1, "single-layer block"
    layer = 0
    bf16 = jnp.bfloat16
    x2 = x.reshape(T, D)
    row = lambda v: v[layer].reshape(1, -1)
    w_qkv = (w_in[layer][:, :QKV_WIDTH].reshape(D, 3, N_GROUPS, GROUP_WIDTH)
             .transpose(0, 2, 1, 3).reshape(D, QKV_WIDTH).astype(bf16))
    w_rest = w_in[layer][:, QKV_WIDTH:].astype(bf16)

    qkv_groups = _qkv_projection(x2, row(norm1), w_qkv, B, S)
    attn_o, attn_lse = [], []
    for gi, (window, dilation) in enumerate(DILATED_GROUPS):
        hs = slice(gi * HEADS_PER_GROUP, (gi + 1) * HEADS_PER_GROUP)
        bias = _attention_bias_table(rel_bias[:, hs], window, dilation)
        o, lse = _attention_group(qkv_groups[gi], bias, gi, dilation)
        attn_o.append(o)
        attn_lse.append(lse)

    w_router = jnp.zeros((D, LANES), jnp.float32)
    w_router = w_router.at[:, :N_EXPERTS].set(w_router_expert[layer].astype(jnp.float32))
    w_router = w_router.at[:, GROUP_LOGIT_LANE0:GROUP_LOGIT_LANE0 + N_EXPERT_GROUPS].set(
        w_router_group[layer].astype(jnp.float32))
    w_router_hi = w_router.astype(bf16)
    weights = [
        row(norm1), w_rest, conv_w[layer], row(conv_b),
        _block_diag_gates(w_rg[layer], w_ig[layer]), row(b_rg), row(b_ig), row(lru_lambda),
        w_proj_attn[layer].astype(bf16), w_proj_lru[layer].astype(bf16), w_out[layer].astype(bf16),
        row(norm2), w_router_hi, (w_router - w_router_hi.astype(jnp.float32)).astype(bf16),
    ]
    x1, xs, route, counts = _token_mixer(x, attn_o, attn_lse, weights)

    expert_tables, comb = _moe_tables(counts[:, 0, :N_EXPERTS].astype(jnp.int32))
    ys = _experts(expert_tables, xs, w_gate_up[layer], w_down[layer])
    out = _combine(comb, ys, x1, route, norm_f.reshape(1, -1))
    return out.reshape(B, S, D)
```

```python
import math

import jax
import jax.numpy as jnp
from jax import lax
from jax.experimental import pallas as pl
from jax.experimental.pallas import tpu as pltpu

D_MODEL = 1024
HEAD_DIM = 64
HEADS_PER_GROUP = 4
DILATED_GROUPS = ((128, 1), (512, 4), (2048, 16))
N_GROUPS = len(DILATED_GROUPS)
N_ATTN_HEADS = HEADS_PER_GROUP * N_GROUPS
ATTN_WIDTH = N_ATTN_HEADS * HEAD_DIM
GROUP_WIDTH = HEADS_PER_GROUP * HEAD_DIM
QKV_WIDTH = 3 * ATTN_WIDTH
ATTN_BLOCK = 128
N_REL_BUCKETS = 32
REL_MAX_DISTANCE = 2048
NEG_INF = -1e30

LRU_WIDTH = D_MODEL
LRU_HEADS = 16
LRU_HEAD_DIM = LRU_WIDTH // LRU_HEADS
CONV_WIDTH = 4
LRU_C = 8.0
LRU_CHUNK_HEADS = 4
LRU_CHUNK = LRU_CHUNK_HEADS * LRU_HEAD_DIM
N_LRU_CHUNKS = LRU_WIDTH // LRU_CHUNK

N_EXPERT_GROUPS = 4
EXPERTS_PER_GROUP = 8
N_EXPERTS = N_EXPERT_GROUPS * EXPERTS_PER_GROUP
TOP_K = 2
D_EXPERT = 512
EPS = 1e-6

REST_WIDTH = 2 * LRU_WIDTH + 2 * D_MODEL
LANES = 128
SUBLANES = 8

ATTN_TILE = 256
ATTN_QB = 4
MIX_TILE = 256
MOE_TILE = 256
MOE_CHUNK = SUBLANES
MOE_BLOCK_CHUNKS = MOE_TILE // MOE_CHUNK
MOE_SLOTS = 768
assert MOE_SLOTS >= TOP_K * MIX_TILE + N_EXPERTS * (MOE_CHUNK - 1) and MOE_SLOTS % MOE_CHUNK == 0
TILE_CHUNKS = MOE_SLOTS // MOE_CHUNK
VMEM_LIMIT = 56 * 1024 * 1024

ROUTE_S0, ROUTE_S1, ROUTE_G0, ROUTE_G1 = range(4)
GROUP_LOGIT_LANE0 = N_EXPERTS


_DONE = object()


def _rmsnorm(x, g):
    return x * lax.rsqrt(jnp.mean(x * x, axis=-1, keepdims=True) + EPS) * g


def _pack_bf16_pairs(v):
    w = v.shape[1] // 2
    bits = lax.bitcast_convert_type(v, jnp.uint32)
    return bits[:, w:] | (bits[:, :w] >> 16)


def _unpack_bf16_pairs(p):
    lo = lax.bitcast_convert_type(p << 16, jnp.float32)
    hi = lax.bitcast_convert_type(p & jnp.uint32(0xFFFF0000), jnp.float32)
    return lo.astype(jnp.bfloat16), hi.astype(jnp.bfloat16)


def _residue_major_perm(n, d):
    per = n // d
    out = lax.broadcasted_iota(jnp.int32, (n, n), 0)
    src = lax.broadcasted_iota(jnp.int32, (n, n), 1)
    m = jnp.bitwise_and(out, per - 1)
    r = jnp.right_shift(out, per.bit_length() - 1)
    return (src == m * d + r).astype(jnp.bfloat16)


def _qkv_kernel(x_ref, g_ref, w_ref, *o_refs):
    tm = ATTN_TILE
    h = _rmsnorm(x_ref[...], g_ref[...]).astype(jnp.bfloat16)
    qkv = jnp.dot(h, w_ref[...], preferred_element_type=jnp.float32).astype(jnp.bfloat16)
    for gi, (o_ref, (_, d)) in enumerate(zip(o_refs, DILATED_GROUPS)):
        part = qkv[:, gi * ATTN_WIDTH:(gi + 1) * ATTN_WIDTH]
        if d > 1:
            part = jnp.dot(_residue_major_perm(tm, d), part,
                           preferred_element_type=jnp.float32).astype(jnp.bfloat16)
        o_ref[...] = part.reshape(d, tm // d, ATTN_WIDTH)


def _qkv_projection(x2, norm1, w_qkv, B, S):
    tm = ATTN_TILE
    nt = S // tm
    return pl.pallas_call(
        _qkv_kernel,
        grid=(B, nt),
        in_specs=[
            pl.BlockSpec((tm, D_MODEL), lambda b, j: (b * nt + j, 0)),
            pl.BlockSpec((1, D_MODEL), lambda b, j: (0, 0)),
            pl.BlockSpec((D_MODEL, QKV_WIDTH), lambda b, j: (0, 0)),
        ],
        out_specs=[pl.BlockSpec((None, d, tm // d, ATTN_WIDTH), lambda b, j: (b, 0, j, 0))
                   for _, d in DILATED_GROUPS],
        out_shape=[jax.ShapeDtypeStruct((B, d, S // d, ATTN_WIDTH), jnp.bfloat16)
                   for _, d in DILATED_GROUPS],
        compiler_params=pltpu.CompilerParams(
            dimension_semantics=("parallel", "parallel"), vmem_limit_bytes=VMEM_LIMIT),
        name="qkv_projection",
    )(x2, norm1, w_qkv)


def _attn_kernel(q_ref, kp_ref, kc_ref, vp_ref, vc_ref, bias0_ref, bias_ref, o_ref, lse_ref):
    q = q_ref[...] * (HEAD_DIM ** -0.5)
    k = jnp.concatenate([kp_ref[...], kc_ref[...]], axis=0)
    v = jnp.concatenate([vp_ref[...], vc_ref[...]], axis=0)
    pair_w = 2 * HEAD_DIM
    first_head = lax.broadcasted_iota(jnp.int32, (1, pair_w), 1) < HEAD_DIM
    def one_head(x, first):
        bits = pltpu.bitcast(x, jnp.uint32)
        keep = first_head if first else jnp.logical_not(first_head)
        return pltpu.bitcast(jnp.where(keep, bits, jnp.uint32(0)), jnp.bfloat16)

    ones = jnp.ones_like(k[:, :pair_w])
    operands = []
    for pair in range(HEADS_PER_GROUP // 2):
        cols = slice(pair * pair_w, (pair + 1) * pair_w)
        k2, v2 = k[:, cols], v[:, cols]
        k_heads = (one_head(k2, True), one_head(k2, False))
        v_heads = (jnp.concatenate([one_head(v2, True), one_head(ones, True)], axis=1),
                   jnp.concatenate([one_head(v2, False), one_head(ones, False)], axis=1))
        operands.append((cols, k_heads, v_heads))
    done = {}

    def head_pair(b, pair):
        rows = slice(b * ATTN_BLOCK, (b + 1) * ATTN_BLOCK)
        keys = slice(b * ATTN_BLOCK, (b + 2) * ATTN_BLOCK)
        cols, k_heads, v_heads = operands[pair]
        bias_ref_b = bias0_ref if b == 0 else bias_ref
        bias = jnp.concatenate([bias_ref_b[2 * pair], bias_ref_b[2 * pair + 1]], axis=1)
        k_both = jnp.concatenate([k_heads[0][keys], k_heads[1][keys]], axis=0)
        s = lax.dot_general(q[rows, cols], k_both, (((1,), (1,)), ((), ())),
                            preferred_element_type=jnp.float32) + bias
        yield
        n_keys = 2 * ATTN_BLOCK
        s0, s1 = s[:, :n_keys], s[:, n_keys:]
        m0 = jnp.max(s0, axis=-1, keepdims=True)
        m1 = jnp.max(s1, axis=-1, keepdims=True)
        p0 = jnp.exp(s0 - m0).astype(jnp.bfloat16)
        p1 = jnp.exp(s1 - m1).astype(jnp.bfloat16)
        yield
        acc = (jnp.dot(p0, v_heads[0][keys], preferred_element_type=jnp.float32)
               + jnp.dot(p1, v_heads[1][keys], preferred_element_type=jnp.float32))
        yield
        denom = acc[:, pair_w:]
        done[b, pair] = (acc[:, :pair_w] / denom, jnp.where(first_head, m0, m1) + jnp.log(denom))

    waiting = [head_pair(b, pair) for b in range(ATTN_QB) for pair in range(HEADS_PER_GROUP // 2)]
    live = []
    while waiting or live:
        if waiting:
            live.append(waiting.pop(0))
        for gen in list(live):
            if next(gen, _DONE) is _DONE:
                live.remove(gen)

    for b in range(ATTN_QB):
        rows = slice(b * ATTN_BLOCK, (b + 1) * ATTN_BLOCK)
        pairs = [done[b, pair] for pair in range(HEADS_PER_GROUP // 2)]
        o_ref[rows, :] = jnp.concatenate([o for o, _ in pairs], axis=1).astype(o_ref.dtype)
        lse_ref[rows, :] = jnp.concatenate([lse for _, lse in pairs], axis=1)


def _attention_group(qkv_g, bias, gi, dilation):
    B, _, L, _ = qkv_g.shape
    rows = ATTN_QB * ATTN_BLOCK
    cur = lambda which: pl.BlockSpec((None, None, rows, GROUP_WIDTH), lambda b, r, n: (b, r, n, which))
    prev = lambda which: pl.BlockSpec((None, None, ATTN_BLOCK, GROUP_WIDTH),
                                      lambda b, r, n: (b, r, jnp.maximum(n * ATTN_QB - 1, 0), which))
    bias_blk = (None, HEADS_PER_GROUP, ATTN_BLOCK, 2 * ATTN_BLOCK)
    return pl.pallas_call(
        _attn_kernel,
        grid=(B, dilation, L // rows),
        in_specs=[
            cur(0), prev(1), cur(1), prev(2), cur(2),
            pl.BlockSpec(bias_blk, lambda b, r, n: (jnp.minimum(n, 1), 0, 0, 0)),
            pl.BlockSpec(bias_blk, lambda b, r, n: (1, 0, 0, 0)),
        ],
        out_specs=[cur(0), cur(0)],
        out_shape=[
            jax.ShapeDtypeStruct((B, dilation, L, GROUP_WIDTH), jnp.bfloat16),
            jax.ShapeDtypeStruct((B, dilation, L, GROUP_WIDTH), jnp.float32),
        ],
        compiler_params=pltpu.CompilerParams(
            dimension_semantics=("parallel", "parallel", "arbitrary"), vmem_limit_bytes=VMEM_LIMIT),
        name=f"dilated_attention_g{gi}",
    )(qkv_g, qkv_g, qkv_g, qkv_g, qkv_g, bias, bias)


def _t5_causal_bucket(dist):
    max_exact = N_REL_BUCKETS // 2
    d_f = jnp.maximum(dist, max_exact).astype(jnp.float32)
    large = max_exact + (jnp.log(d_f / max_exact) / math.log(REL_MAX_DISTANCE / max_exact)
                         * (N_REL_BUCKETS - max_exact)).astype(jnp.int32)
    large = jnp.minimum(large, N_REL_BUCKETS - 1)
    return jnp.where(dist < max_exact, dist, large)


def _attention_bias_table(rel_bias_g, window, dilation):
    nw = window // dilation
    qi = jnp.arange(ATTN_BLOCK)[:, None]
    ki = jnp.arange(2 * ATTN_BLOCK)[None, :]
    dist = ATTN_BLOCK + qi - ki
    band = (dist >= 0) & (dist <= nw)
    bucket = _t5_causal_bucket(jnp.maximum(dist, 0) * dilation)
    onehot = (bucket[None, :, :, None] == jnp.arange(N_REL_BUCKETS)).astype(jnp.float32)
    bias = jnp.sum(onehot * rel_bias_g.astype(jnp.float32).T[:, None, None, :], axis=-1)
    later = jnp.where(band[None], bias, NEG_INF)
    first = jnp.where((band & (ki >= ATTN_BLOCK))[None], bias, NEG_INF)
    return jnp.stack([first, later])


def _gelu_tanh(x):
    return 0.5 * x * (1.0 + jnp.tanh(math.sqrt(2.0 / math.pi) * (x + 0.044715 * (x * x * x))))


def _softplus(x):
    return jnp.maximum(x, 0.0) + jnp.log(1.0 + jnp.exp(-jnp.abs(x)))


def _natural_order(blk_ref, d, slabs):
    if d == 1:
        return blk_ref[0].astype(jnp.float32)
    per = MIX_TILE // d
    for r in range(d):
        rows = blk_ref[r].astype(jnp.float32)
        for c in range(GROUP_WIDTH // LANES):
            slabs[c, pl.ds(r, per, stride=d), :] = rows[:, c * LANES:(c + 1) * LANES]
    return jnp.concatenate([slabs[c] for c in range(GROUP_WIDTH // LANES)], axis=1)


def _mix_tile(x_ref, o1_ref, o2_ref, o3_ref, l1_ref, l2_ref, l3_ref,
              n1_ref, win_ref, cw_ref, cb_ref, wg_ref, brg_ref, big_ref, lam_ref,
              wpa_ref, wpl_ref, wout_ref, n2_ref, wrh_ref, wrl_ref,
              x1_ref, xs_ref, route_ref, cnt_ref,
              xbuf, a_sc, b_sc, h_sc, *slabs):
    tm = MIX_TILE
    first_tile = pl.program_id(0) == 0

    @pl.when(first_tile)
    def _():
        xbuf[0:SUBLANES, :] = jnp.zeros((SUBLANES, LRU_WIDTH), jnp.float32)
        h_sc[...] = jnp.zeros_like(h_sc)

    x = x_ref[...]
    h = _rmsnorm(x, n1_ref[...]).astype(jnp.bfloat16)
    def project(first_col):
        return jnp.dot(h, win_ref[:, first_col:first_col + D_MODEL], preferred_element_type=jnp.float32)

    xr = project(0)

    yield
    g_lru = project(LRU_WIDTH)
    xbuf[SUBLANES:SUBLANES + tm, :] = xr
    xc = xr * cw_ref[CONV_WIDTH - 1:CONV_WIDTH, :] + cb_ref[...]
    for j in range(CONV_WIDTH - 1):
        back = CONV_WIDTH - 1 - j
        xc = xc + xbuf[SUBLANES - back:SUBLANES - back + tm, :] * cw_ref[j:j + 1, :]
    xbuf[0:SUBLANES, :] = xbuf[tm:tm + SUBLANES, :]

    xcb = xc.astype(jnp.bfloat16)
    r_parts, i_parts = [], []
    for c in range(N_LRU_CHUNKS):
        ri = jnp.dot(xcb[:, c * LRU_CHUNK:(c + 1) * LRU_CHUNK], wg_ref[c],
                     preferred_element_type=jnp.float32)
        r_parts.append(ri[:, :LRU_CHUNK])
        i_parts.append(ri[:, LRU_CHUNK:])
    yield
    g_a = project(2 * LRU_WIDTH)
    r = jax.nn.sigmoid(jnp.concatenate(r_parts, axis=1) + brg_ref[...])
    ig = jax.nn.sigmoid(jnp.concatenate(i_parts, axis=1) + big_ref[...])
    log_a = (-LRU_C * _softplus(-lam_ref[...])) * r
    a = jnp.exp(log_a)
    gap = 1.0 - a * a
    root = jnp.where(gap > 0.0, gap * lax.rsqrt(gap), 0.0)
    bb = root * (ig * xc)

    yield
    g_b = project(2 * LRU_WIDTH + D_MODEL)
    row8 = lax.broadcasted_iota(jnp.int32, (tm, 1), 0) % SUBLANES
    for k in (1, 2, 4):
        a_s = pltpu.roll(a, k, axis=0)
        b_s = pltpu.roll(bb, k, axis=0)
        take = row8 >= k
        bb = jnp.where(take, a * b_s + bb, bb)
        a = jnp.where(take, a * a_s, a)
    a_sc[...] = a
    b_sc[...] = bb

    yield
    carry = h_sc[...]
    for g in range(tm // SUBLANES):
        rows = slice(g * SUBLANES, (g + 1) * SUBLANES)
        hh = a_sc[rows, :] * carry + b_sc[rows, :]
        b_sc[rows, :] = hh
        carry = hh[SUBLANES - 1:SUBLANES, :]
    h_sc[...] = carry
    yield
    lru = (b_sc[...] * _gelu_tanh(g_lru)).astype(jnp.bfloat16)
    p_lru = jnp.dot(lru, wpl_ref[...], preferred_element_type=jnp.float32)

    yield
    dil = [d for _, d in DILATED_GROUPS]
    o1, o2, o3 = (_natural_order(ref, d, slabs[2 * g]) for g, (ref, d) in
                  enumerate(zip((o1_ref, o2_ref, o3_ref), dil)))
    l1, l2, l3 = (_natural_order(ref, d, slabs[2 * g + 1]) for g, (ref, d) in
                  enumerate(zip((l1_ref, l2_ref, l3_ref), dil)))
    lm = jnp.maximum(jnp.maximum(l1, l2), l3)
    e1, e2, e3 = jnp.exp(l1 - lm), jnp.exp(l2 - lm), jnp.exp(l3 - lm)
    attn = (e1 * o1 + e2 * o2 + e3 * o3) / (e1 + e2 + e3)
    p_attn = jnp.dot(attn.astype(jnp.bfloat16), wpa_ref[...], preferred_element_type=jnp.float32)

    yield
    merged = jax.nn.sigmoid(g_a) * p_attn + jax.nn.sigmoid(g_b) * p_lru
    x1 = x + jnp.dot(merged.astype(jnp.bfloat16), wout_ref[...], preferred_element_type=jnp.float32)
    x1_ref[...] = x1
    h2 = _rmsnorm(x1, n2_ref[...])

    yield
    h2_hi = h2.astype(jnp.bfloat16)
    h2_lo = (h2 - h2_hi.astype(jnp.float32)).astype(jnp.bfloat16)
    logits = (jnp.dot(h2_hi, wrh_ref[...], preferred_element_type=jnp.float32)
              + (jnp.dot(h2_lo, wrh_ref[...], preferred_element_type=jnp.float32)
                 + jnp.dot(h2_hi, wrl_ref[...], preferred_element_type=jnp.float32)))
    lane = lax.broadcasted_iota(jnp.int32, (tm, LANES), 1)
    big = jnp.int32(LANES)
    lowest = jnp.float32(-3.0e38)
    is_g = (lane >= GROUP_LOGIT_LANE0) & (lane < GROUP_LOGIT_LANE0 + N_EXPERT_GROUPS)
    gl = jnp.where(is_g, logits, lowest)
    gmax = jnp.max(gl, axis=-1, keepdims=True)
    gsel = jnp.min(jnp.where(gl == gmax, lane, big), axis=-1, keepdims=True) - GROUP_LOGIT_LANE0
    p_sel = 1.0 / jnp.sum(jnp.where(is_g, jnp.exp(gl - gmax), 0.0), axis=-1, keepdims=True)
    is_e = (lane >= gsel * EXPERTS_PER_GROUP) & (lane < (gsel + 1) * EXPERTS_PER_GROUP)
    el = jnp.where(is_e, logits, lowest)
    v1 = jnp.max(el, axis=-1, keepdims=True)
    i1 = jnp.min(jnp.where(el == v1, lane, big), axis=-1, keepdims=True)
    el2 = jnp.where(lane == i1, lowest, el)
    v2 = jnp.max(el2, axis=-1, keepdims=True)
    i2 = jnp.min(jnp.where(el2 == v2, lane, big), axis=-1, keepdims=True)
    t = jnp.exp(v2 - v1)
    g0 = p_sel / (1.0 + t)
    g1 = p_sel * t / (1.0 + t)

    yield
    oh0 = lane == i1
    oh1 = lane == i2
    both = (oh0 | oh1).astype(jnp.float32)
    tri = (lax.broadcasted_iota(jnp.int32, (tm, tm), 0)
           > lax.broadcasted_iota(jnp.int32, (tm, tm), 1)).astype(jnp.bfloat16)
    before = jnp.dot(tri, both.astype(jnp.bfloat16), preferred_element_type=jnp.float32)
    cnt = jnp.sum(both, axis=0, keepdims=True)
    padded = jnp.floor((cnt + (MOE_CHUNK - 1.0)) * (1.0 / MOE_CHUNK)) * MOE_CHUNK
    upper = (lax.broadcasted_iota(jnp.int32, (LANES, LANES), 0)
             < lax.broadcasted_iota(jnp.int32, (LANES, LANES), 1)).astype(jnp.bfloat16)
    loff = jnp.dot(jnp.broadcast_to(padded, (SUBLANES, LANES)).astype(jnp.bfloat16), upper,
                   preferred_element_type=jnp.float32)[0:1, :]
    base = before + loff
    slot0 = jnp.sum(jnp.where(oh0, base, 0.0), axis=-1, keepdims=True)
    slot1 = jnp.sum(jnp.where(oh1, base, 0.0), axis=-1, keepdims=True)
    cnt_ref[...] = jnp.broadcast_to(cnt, cnt_ref.shape)

    slot_id = lax.broadcasted_iota(jnp.int32, (tm, MOE_SLOTS), 1)
    place = ((slot_id == slot0.astype(jnp.int32)) | (slot_id == slot1.astype(jnp.int32)))
    xs_ref[...] = _pack_bf16_pairs(
        lax.dot_general(place.astype(jnp.bfloat16), h2_hi, (((0,), (0,)), ((), ())),
                        preferred_element_type=jnp.float32))

    route = jnp.zeros((tm, LANES), jnp.float32)
    for lane_id, val in ((ROUTE_S0, slot0), (ROUTE_S1, slot1), (ROUTE_G0, g0), (ROUTE_G1, g1)):
        route = jnp.where(lane == lane_id, val, route)
    route_ref[...] = route


N_MIXER_TILED_INPUTS = 1 + 2 * N_GROUPS
N_MIXER_OUTPUTS = 4


def _mixer_kernel(*refs):
    n_batch = refs[0].shape[0]
    n_weights = len(refs) - N_MIXER_TILED_INPUTS - N_MIXER_OUTPUTS - n_batch * _MIXER_SCRATCH_PER_SEQ
    tiled = refs[:N_MIXER_TILED_INPUTS]
    weights = refs[N_MIXER_TILED_INPUTS:N_MIXER_TILED_INPUTS + n_weights]
    outs = refs[N_MIXER_TILED_INPUTS + n_weights:N_MIXER_TILED_INPUTS + n_weights + N_MIXER_OUTPUTS]
    scratch = refs[N_MIXER_TILED_INPUTS + n_weights + N_MIXER_OUTPUTS:]
    stages = []
    for b in range(n_batch):
        own = scratch[b * _MIXER_SCRATCH_PER_SEQ:(b + 1) * _MIXER_SCRATCH_PER_SEQ]
        stages.append(_mix_tile(*[r.at[b] for r in tiled], *weights, *[r.at[b] for r in outs], *own))
    while stages:
        for gen in list(stages):
            if next(gen, _DONE) is _DONE:
                stages.remove(gen)


def _mixer_scratch(tm):
    return [
        pltpu.VMEM((tm + 2 * SUBLANES, LRU_WIDTH), jnp.float32),
        pltpu.VMEM((tm, LRU_WIDTH), jnp.float32),
        pltpu.VMEM((tm, LRU_WIDTH), jnp.float32),
        pltpu.VMEM((1, LRU_WIDTH), jnp.float32),
    ] + [pltpu.VMEM((GROUP_WIDTH // LANES, tm, LANES), jnp.float32)] * (2 * N_GROUPS)


_MIXER_SCRATCH_PER_SEQ = len(_mixer_scratch(MIX_TILE))


def _token_mixer(x, attn_o, attn_lse, weights):
    B, S, _ = x.shape
    tm = MIX_TILE
    nt = S // tm
    rows = lambda j: (0, j, 0)

    def resident(shape):
        return pl.BlockSpec(shape, lambda j: (0,) * len(shape), pipeline_mode=pl.Buffered(1))

    in_specs = [pl.BlockSpec((B, tm, D_MODEL), rows)]
    group_specs = [pl.BlockSpec((B, d, tm // d, GROUP_WIDTH), lambda j: (0, 0, j, 0))
                   for _, d in DILATED_GROUPS]
    in_specs += group_specs + group_specs
    in_specs += [resident(w.shape) for w in weights]
    x1, xs, route, counts = pl.pallas_call(
        _mixer_kernel,
        grid=(nt,),
        in_specs=in_specs,
        out_specs=[
            pl.BlockSpec((B, tm, D_MODEL), rows),
            pl.BlockSpec((B, MOE_SLOTS, D_MODEL // 2), rows),
            pl.BlockSpec((B, tm, LANES), rows),
            pl.BlockSpec((B, None, SUBLANES, LANES), lambda j: (0, j, 0, 0)),
        ],
        out_shape=[
            jax.ShapeDtypeStruct((B, S, D_MODEL), jnp.float32),
            jax.ShapeDtypeStruct((B, nt * MOE_SLOTS, D_MODEL // 2), jnp.uint32),
            jax.ShapeDtypeStruct((B, S, LANES), jnp.float32),
            jax.ShapeDtypeStruct((B, nt, SUBLANES, LANES), jnp.float32),
        ],
        scratch_shapes=_mixer_scratch(tm) * B,
        compiler_params=pltpu.CompilerParams(
            dimension_semantics=("arbitrary",), vmem_limit_bytes=VMEM_LIMIT),
        name="token_mixer",
    )(x, *attn_o, *attn_lse, *weights)
    return (x1.reshape(B * S, D_MODEL), xs.reshape(B * nt * MOE_SLOTS, D_MODEL // 2),
            route.reshape(B * S, LANES), counts.reshape(B * nt, SUBLANES, LANES))


def _moe_tables(cnt_tile):
    ntiles = cnt_tile.shape[0]
    nch = (cnt_tile + MOE_CHUNK - 1) // MOE_CHUNK
    lo_c = jnp.cumsum(nch, axis=1) - nch
    per_expert = jnp.sum(nch, axis=0)
    region = ((per_expert + MOE_BLOCK_CHUNKS - 1) // MOE_BLOCK_CHUNKS) * MOE_BLOCK_CHUNKS
    pend = jnp.cumsum(region)
    glob = (pend - region)[None, :] + jnp.cumsum(nch, axis=0) - nch

    max_rows = TOP_K * ntiles * MIX_TILE + ntiles * N_EXPERTS * (MOE_CHUNK - 1)
    max_chunks = -(-max_rows // MOE_CHUNK) + N_EXPERTS * (MOE_BLOCK_CHUNKS - 1)
    max_blocks = -(-max_chunks // MOE_BLOCK_CHUNKS)
    max_chunks = max_blocks * MOE_BLOCK_CHUNKS

    seg_start = glob.T.reshape(-1)
    seg_src = (jnp.arange(ntiles, dtype=jnp.int32)[:, None] * TILE_CHUNKS + lo_c).T.reshape(-1)
    step = jnp.diff(seg_src - seg_start, prepend=0)
    g = jnp.arange(max_chunks, dtype=jnp.int32)
    src = g + jnp.sum(jnp.where(seg_start[None, :] <= g[:, None], step[None, :], 0), axis=1)
    src = jnp.clip(src, 0, ntiles * TILE_CHUNKS - 1)

    lstep = jnp.diff(glob - lo_c, axis=1, prepend=0)
    lc = jnp.arange(TILE_CHUNKS, dtype=jnp.int32)
    comb = lc[None, :] + jnp.sum(
        jnp.where(lo_c[:, None, :] <= lc[None, :, None], lstep[:, None, :], 0), axis=2)
    comb = jnp.clip(comb, 0, max_chunks - 1).reshape(-1)

    eidx = jnp.arange(N_EXPERTS, dtype=jnp.int32)
    nonempty = region > 0
    later = jnp.where(nonempty[None, :] & (eidx[None, :] > eidx[:, None]), eidx[None, :], N_EXPERTS)
    next_expert = jnp.min(later, axis=1)
    next_expert = jnp.where(next_expert == N_EXPERTS, -1, next_expert)
    wslot = (jnp.cumsum(nonempty.astype(jnp.int32)) - nonempty.astype(jnp.int32)) % 2
    last_expert = jnp.max(jnp.where(nonempty, eidx, 0))
    blk0 = jnp.arange(max_blocks, dtype=jnp.int32) * MOE_BLOCK_CHUNKS
    block_expert = jnp.minimum(jnp.sum((pend[None, :] <= blk0[:, None]).astype(jnp.int32), axis=1),
                               last_expert)
    of_block = lambda per_expert: jnp.sum(
        jnp.where(block_expert[:, None] == eidx[None, :], per_expert[None, :], 0), axis=1)
    n_valid = pend[-1:] // MOE_BLOCK_CHUNKS
    as_i32 = lambda v: v.astype(jnp.int32)
    expert_tables = (as_i32(block_expert), as_i32(of_block(next_expert)), as_i32(of_block(wslot)),
                     as_i32(n_valid), as_i32(src))
    return expert_tables, as_i32(comb)


def _chunk_copy(src_hbm, src_chunk, dst, dst_chunk, sem):
    rows = lambda c: pl.ds(pl.multiple_of(c * MOE_CHUNK, MOE_CHUNK), MOE_CHUNK)
    return pltpu.make_async_copy(src_hbm.at[rows(src_chunk), :], dst.at[rows(dst_chunk), :], sem)


def _expert_kernel(be_ref, nxt_ref, ws_ref, nv_ref, src_ref, xs_hbm, wgu_hbm, wd_hbm, ys_ref,
                   xbuf, wgu_f32, wd_f32, wgu_bf, wd_bf, sems, wsems):
    i = pl.program_id(0)
    valid = i < nv_ref[0]

    def gather(blk, slot):
        return [_chunk_copy(xs_hbm, src_ref[blk * MOE_BLOCK_CHUNKS + c], xbuf.at[slot], c, sems.at[slot])
                for c in range(MOE_BLOCK_CHUNKS)]

    def fetch_weights(e, slot):
        return [pltpu.make_async_copy(wgu_hbm.at[e], wgu_f32.at[slot], wsems.at[slot]),
                pltpu.make_async_copy(wd_hbm.at[e], wd_f32.at[slot], wsems.at[slot])]

    @pl.when(i == 0)
    def _():
        for cp in gather(0, 0) + fetch_weights(be_ref[0], 0):
            cp.start()

    @pl.when(i + 1 < nv_ref[0])
    def _():
        for cp in gather(i + 1, (i + 1) % 2):
            cp.start()

    new_expert = valid & ((i == 0) | (be_ref[i] != be_ref[jnp.maximum(i - 1, 0)]))

    @pl.when(new_expert)
    def _():
        ws = ws_ref[i]
        for cp in fetch_weights(be_ref[i], ws):
            cp.wait()
        wgu_bf[...] = wgu_f32[ws].astype(jnp.bfloat16)
        wd_bf[...] = wd_f32[ws].astype(jnp.bfloat16)

        @pl.when(nxt_ref[i] >= 0)
        def _():
            for cp in fetch_weights(nxt_ref[i], 1 - ws):
                cp.start()

    @pl.when(valid)
    def _():
        slot = i % 2
        for cp in gather(i, slot):
            cp.wait()
        x_lo, x_hi = _unpack_bf16_pairs(xbuf[slot])
        half = D_MODEL // 2
        gu = (jnp.dot(x_lo, wgu_bf[:half, :], preferred_element_type=jnp.float32)
              + jnp.dot(x_hi, wgu_bf[half:, :], preferred_element_type=jnp.float32))
        g = gu[:, :D_EXPERT]
        u = gu[:, D_EXPERT:]
        act = (g * jax.nn.sigmoid(g) * u).astype(jnp.bfloat16)
        y = jnp.dot(act, wd_bf[...], preferred_element_type=jnp.float32)
        ys_ref[...] = _pack_bf16_pairs(y.astype(jnp.bfloat16).astype(jnp.float32))

    @pl.when(jnp.logical_not(valid))
    def _():
        ys_ref[...] = jnp.zeros_like(ys_ref)


def _experts(tables, xs, w_gate_up, w_down):
    block_expert, block_next, block_wslot, n_valid, src = tables
    nblk = block_expert.shape[0]
    return pl.pallas_call(
        _expert_kernel,
        grid_spec=pltpu.PrefetchScalarGridSpec(
            num_scalar_prefetch=5,
            grid=(nblk,),
            in_specs=[pl.BlockSpec(memory_space=pl.ANY)] * 3,
            out_specs=pl.BlockSpec((MOE_TILE, D_MODEL // 2), lambda i, *_: (i, 0)),
            scratch_shapes=[
                pltpu.VMEM((2, MOE_TILE, D_MODEL // 2), jnp.uint32),
                pltpu.VMEM((2, D_MODEL, 2 * D_EXPERT), jnp.float32),
                pltpu.VMEM((2, D_EXPERT, D_MODEL), jnp.float32),
                pltpu.VMEM((D_MODEL, 2 * D_EXPERT), jnp.bfloat16),
                pltpu.VMEM((D_EXPERT, D_MODEL), jnp.bfloat16),
                pltpu.SemaphoreType.DMA((2,)),
                pltpu.SemaphoreType.DMA((2,)),
            ],
        ),
        out_shape=jax.ShapeDtypeStruct((nblk * MOE_TILE, D_MODEL // 2), jnp.uint32),
        compiler_params=pltpu.CompilerParams(
            dimension_semantics=("arbitrary",), vmem_limit_bytes=VMEM_LIMIT),
        name="moe_experts",
    )(block_expert, block_next, block_wslot, n_valid, src, xs, w_gate_up, w_down)


def _combine_kernel(comb_ref, ys_hbm, x1_ref, route_ref, nf_ref, out_ref, ybuf, sems):
    tc = MIX_TILE
    i = pl.program_id(0)
    last = pl.num_programs(0) - 1

    def gather(tile, slot):
        return [_chunk_copy(ys_hbm, comb_ref[tile * TILE_CHUNKS + lc], ybuf.at[slot], lc, sems.at[slot])
                for lc in range(TILE_CHUNKS)]

    @pl.when(i == 0)
    def _():
        for cp in gather(0, 0):
            cp.start()

    nxt = jnp.minimum(i + 1, last)
    for cp in gather(nxt, (i + 1) % 2):
        cp.start()

    slot = i % 2
    for cp in gather(i, slot):
        cp.wait()

    y_lo, y_hi = _unpack_bf16_pairs(ybuf[slot])
    route = route_ref[...]
    slot_id = lax.broadcasted_iota(jnp.int32, (tc, MOE_SLOTS), 1)

    def pick_rows(lane):
        sel = (slot_id == route[:, lane:lane + 1].astype(jnp.int32)).astype(jnp.bfloat16)
        return jnp.concatenate([jnp.dot(sel, y_lo, preferred_element_type=jnp.float32),
                                jnp.dot(sel, y_hi, preferred_element_type=jnp.float32)], axis=1)

    y0 = pick_rows(ROUTE_S0)
    y1 = pick_rows(ROUTE_S1)
    g0 = route[:, ROUTE_G0:ROUTE_G0 + 1]
    g1 = route[:, ROUTE_G1:ROUTE_G1 + 1]
    out_ref[...] = _rmsnorm(x1_ref[...] + g0 * y0 + g1 * y1, nf_ref[...])

    @pl.when(i == last)
    def _():
        for cp in gather(nxt, (i + 1) % 2):
            cp.wait()


def _combine(comb, ys, x1, route, norm_f):
    T = x1.shape[0]
    tc = MIX_TILE
    return pl.pallas_call(
        _combine_kernel,
        grid_spec=pltpu.PrefetchScalarGridSpec(
            num_scalar_prefetch=1,
            grid=(T // tc,),
            in_specs=[
                pl.BlockSpec(memory_space=pl.ANY),
                pl.BlockSpec((tc, D_MODEL), lambda i, c: (i, 0)),
                pl.BlockSpec((tc, LANES), lambda i, c: (i, 0)),
                pl.BlockSpec((1, D_MODEL), lambda i, c: (0, 0)),
            ],
            out_specs=pl.BlockSpec((tc, D_MODEL), lambda i, c: (i, 0)),
            scratch_shapes=[
                pltpu.VMEM((2, MOE_SLOTS, D_MODEL // 2), jnp.uint32),
                pltpu.SemaphoreType.DMA((2,)),
            ],
        ),
        out_shape=jax.ShapeDtypeStruct((T, D_MODEL), jnp.float32),
        compiler_params=pltpu.CompilerParams(
            dimension_semantics=("arbitrary",), vmem_limit_bytes=VMEM_LIMIT),
        name="moe_combine",
    )(comb, ys, x1, route, norm_f)


def _block_diag_gates(w_rg, w_ig):
    def bd(w):
        w4 = w.reshape(N_LRU_CHUNKS, LRU_CHUNK_HEADS, LRU_HEAD_DIM, LRU_HEAD_DIM)
        eye = jnp.eye(LRU_CHUNK_HEADS, dtype=w.dtype)
        return jnp.einsum('chij,hk->chikj', w4, eye).reshape(N_LRU_CHUNKS, LRU_CHUNK, LRU_CHUNK)
    return jnp.concatenate([bd(w_rg), bd(w_ig)], axis=-1).astype(jnp.bfloat16)


def kernel(x, rel_bias, norm1, w_in, conv_w, conv_b, w_rg, b_rg, w_ig, b_ig, lru_lambda,
           w_proj_attn, w_proj_lru, w_out, norm2, w_router_group, w_router_expert,
           w_gate_up, w_down, norm_f):
    B, S, D = x.shape
    T = B * S
    assert w_in.shape[0] == 1, "single-layer block"
    layer = 0
    bf16 = jnp.bfloat16
    x2 = x.reshape(T, D)
    row = lambda v: v[layer].reshape(1, -1)
    w_qkv = (w_in[layer][:, :QKV_WIDTH].reshape(D, 3, N_GROUPS, GROUP_WIDTH)
             .transpose(0, 2, 1, 3).reshape(D, QKV_WIDTH).astype(bf16))
    w_rest = w_in[layer][:, QKV_WIDTH:].astype(bf16)

    qkv_groups = _qkv_projection(x2, row(norm1), w_qkv, B, S)
    attn_o, attn_lse = [], []
    for gi, (window, dilation) in enumerate(DILATED_GROUPS):
        hs = slice(gi * HEADS_PER_GROUP, (gi + 1) * HEADS_PER_GROUP)
        bias = _attention_bias_table(rel_bias[:, hs], window, dilation)
        o, lse = _attention_group(qkv_groups[gi], bias, gi, dilation)
        attn_o.append(o)
        attn_lse.append(lse)

    w_router = jnp.zeros((D, LANES), jnp.float32)
    w_router = w_router.at[:, :N_EXPERTS].set(w_router_expert[layer].astype(jnp.float32))
    w_router = w_router.at[:, GROUP_LOGIT_LANE0:GROUP_LOGIT_LANE0 + N_EXPERT_GROUPS].set(
        w_router_group[layer].astype(jnp.float32))
    w_router_hi = w_router.astype(bf16)
    weights = [
        row(norm1), w_rest, conv_w[layer], row(conv_b),
        _block_diag_gates(w_rg[layer], w_ig[layer]), row(b_rg), row(b_ig), row(lru_lambda),
        w_proj_attn[layer].astype(bf16), w_proj_lru[layer].astype(bf16), w_out[layer].astype(bf16),
        row(norm2), w_router_hi, (w_router - w_router_hi.astype(jnp.float32)).astype(bf16),
    ]
    x1, xs, route, counts = _token_mixer(x, attn_o, attn_lse, weights)

    expert_tables, comb = _moe_tables(counts[:, 0, :N_EXPERTS].astype(jnp.int32))
    ys = _experts(expert_tables, xs, w_gate_up[layer], w_down[layer])
    out = _combine(comb, ys, x1, route, norm_f.reshape(1, -1))
    return out.reshape(B, S, D)
```

```python
import math

import jax
import jax.numpy as jnp
from jax import lax
from jax.experimental import pallas as pl
from jax.experimental.pallas import tpu as pltpu

D_MODEL = 1024
HEAD_DIM = 64
HEADS_PER_GROUP = 4
DILATED_GROUPS = ((128, 1), (512, 4), (2048, 16))
N_GROUPS = len(DILATED_GROUPS)
N_ATTN_HEADS = HEADS_PER_GROUP * N_GROUPS
ATTN_WIDTH = N_ATTN_HEADS * HEAD_DIM
GROUP_WIDTH = HEADS_PER_GROUP * HEAD_DIM
QKV_WIDTH = 3 * ATTN_WIDTH
ATTN_BLOCK = 128
N_REL_BUCKETS = 32
REL_MAX_DISTANCE = 2048
NEG_INF = -1e30

LRU_WIDTH = D_MODEL
LRU_HEADS = 16
LRU_HEAD_DIM = LRU_WIDTH // LRU_HEADS
CONV_WIDTH = 4
LRU_C = 8.0
LRU_CHUNK_HEADS = 4
LRU_CHUNK = LRU_CHUNK_HEADS * LRU_HEAD_DIM
N_LRU_CHUNKS = LRU_WIDTH // LRU_CHUNK

N_EXPERT_GROUPS = 4
EXPERTS_PER_GROUP = 8
N_EXPERTS = N_EXPERT_GROUPS * EXPERTS_PER_GROUP
TOP_K = 2
D_EXPERT = 512
EPS = 1e-6

REST_WIDTH = 2 * LRU_WIDTH + 2 * D_MODEL
LANES = 128
SUBLANES = 8

ATTN_TILE = 256
ATTN_TILES_PER_STEP = 2
ATTN_QB = 4
MIX_TILE = 256
MOE_TILE = 256
COMBINE_TILES_PER_STEP = 2
WEIGHT_DMA_PRIORITY = 1
EXPERT_GROUP = 256
MOE_CHUNK = SUBLANES
MOE_BLOCK_CHUNKS = MOE_TILE // MOE_CHUNK
MOE_SLOTS = 768
assert MOE_SLOTS >= TOP_K * MIX_TILE + N_EXPERTS * (MOE_CHUNK - 1) and MOE_SLOTS % MOE_CHUNK == 0
TILE_CHUNKS = MOE_SLOTS // MOE_CHUNK
VMEM_LIMIT = 56 * 1024 * 1024

ROUTE_S0, ROUTE_S1, ROUTE_G0, ROUTE_G1 = range(4)
GROUP_LOGIT_LANE0 = N_EXPERTS


_DONE = object()


def _rmsnorm(x, g):
    return x * lax.rsqrt(jnp.mean(x * x, axis=-1, keepdims=True) + EPS) * g


def _pack_bf16_pairs(v):
    w = v.shape[1] // 2
    bits = lax.bitcast_convert_type(v, jnp.uint32)
    return bits[:, w:] | (bits[:, :w] >> 16)


def _unpack_bf16_pairs(p):
    lo = lax.bitcast_convert_type(p << 16, jnp.float32)
    hi = lax.bitcast_convert_type(p & jnp.uint32(0xFFFF0000), jnp.float32)
    return lo.astype(jnp.bfloat16), hi.astype(jnp.bfloat16)


def _residue_major_perm(n, d):
    per = n // d
    out = lax.broadcasted_iota(jnp.int32, (n, n), 0)
    src = lax.broadcasted_iota(jnp.int32, (n, n), 1)
    m = jnp.bitwise_and(out, per - 1)
    r = jnp.right_shift(out, per.bit_length() - 1)
    return (src == m * d + r).astype(jnp.bfloat16)


def _qkv_kernel(x_ref, g_ref, w_ref, *o_refs):
    tm = ATTN_TILE

    def tile(t):
        rows = slice(t * tm, (t + 1) * tm)
        h = _rmsnorm(x_ref[rows, :], g_ref[...]).astype(jnp.bfloat16)
        yield
        for gi, (o_ref, (_, d)) in enumerate(zip(o_refs, DILATED_GROUPS)):
            cols = slice(gi * ATTN_WIDTH, (gi + 1) * ATTN_WIDTH)
            part = jnp.dot(h, w_ref[:, cols], preferred_element_type=jnp.float32).astype(jnp.bfloat16)
            yield
            if d > 1:
                part = jnp.dot(_residue_major_perm(tm, d), part,
                               preferred_element_type=jnp.float32).astype(jnp.bfloat16)
            o_ref[:, t * (tm // d):(t + 1) * (tm // d), :] = part.reshape(d, tm // d, ATTN_WIDTH)
            yield

    waiting = [tile(t) for t in range(ATTN_TILES_PER_STEP)]
    live = []
    while waiting or live:
        if waiting:
            live.append(waiting.pop(0))
        for gen in list(live):
            if next(gen, _DONE) is _DONE:
                live.remove(gen)


def _qkv_projection(x2, norm1, w_qkv, B, S):
    tm = ATTN_TILE * ATTN_TILES_PER_STEP
    nt = S // tm
    return pl.pallas_call(
        _qkv_kernel,
        grid=(B, nt),
        in_specs=[
            pl.BlockSpec((tm, D_MODEL), lambda b, j: (b * nt + j, 0)),
            pl.BlockSpec((1, D_MODEL), lambda b, j: (0, 0)),
            pl.BlockSpec((D_MODEL, QKV_WIDTH), lambda b, j: (0, 0)),
        ],
        out_specs=[pl.BlockSpec((None, d, tm // d, ATTN_WIDTH), lambda b, j: (b, 0, j, 0))
                   for _, d in DILATED_GROUPS],
        out_shape=[jax.ShapeDtypeStruct((B, d, S // d, ATTN_WIDTH), jnp.bfloat16)
                   for _, d in DILATED_GROUPS],
        compiler_params=pltpu.CompilerParams(
            dimension_semantics=("parallel", "parallel"), vmem_limit_bytes=VMEM_LIMIT),
        name="qkv_projection",
    )(x2, norm1, w_qkv)


def _attn_kernel(q_ref, kp_ref, kc_ref, vp_ref, vc_ref, bias0_ref, bias_ref, o_ref, lse_ref):
    q = q_ref[...] * (HEAD_DIM ** -0.5)
    k = jnp.concatenate([kp_ref[...], kc_ref[...]], axis=0)
    v = jnp.concatenate([vp_ref[...], vc_ref[...]], axis=0)
    pair_w = 2 * HEAD_DIM
    first_head = lax.broadcasted_iota(jnp.int32, (1, pair_w), 1) < HEAD_DIM
    def one_head(x, first):
        bits = pltpu.bitcast(x, jnp.uint32)
        keep = first_head if first else jnp.logical_not(first_head)
        return pltpu.bitcast(jnp.where(keep, bits, jnp.uint32(0)), jnp.bfloat16)

    ones = jnp.ones_like(k[:, :pair_w])
    operands = []
    for pair in range(HEADS_PER_GROUP // 2):
        cols = slice(pair * pair_w, (pair + 1) * pair_w)
        k2, v2 = k[:, cols], v[:, cols]
        k_heads = (one_head(k2, True), one_head(k2, False))
        v_heads = (jnp.concatenate([one_head(v2, True), one_head(ones, True)], axis=1),
                   jnp.concatenate([one_head(v2, False), one_head(ones, False)], axis=1))
        operands.append((cols, k_heads, v_heads))
    done = {}

    def head_pair(b, pair):
        rows = slice(b * ATTN_BLOCK, (b + 1) * ATTN_BLOCK)
        keys = slice(b * ATTN_BLOCK, (b + 2) * ATTN_BLOCK)
        cols, k_heads, v_heads = operands[pair]
        bias_ref_b = bias0_ref if b == 0 else bias_ref
        bias = jnp.concatenate([bias_ref_b[2 * pair], bias_ref_b[2 * pair + 1]], axis=1)
        k_both = jnp.concatenate([k_heads[0][keys], k_heads[1][keys]], axis=0)
        s = lax.dot_general(q[rows, cols], k_both, (((1,), (1,)), ((), ())),
                            preferred_element_type=jnp.float32) + bias
        yield
        n_keys = 2 * ATTN_BLOCK
        s0, s1 = s[:, :n_keys], s[:, n_keys:]
        m0 = jnp.max(s0, axis=-1, keepdims=True)
        m1 = jnp.max(s1, axis=-1, keepdims=True)
        p0 = jnp.exp(s0 - m0).astype(jnp.bfloat16)
        p1 = jnp.exp(s1 - m1).astype(jnp.bfloat16)
        yield
        acc = (jnp.dot(p0, v_heads[0][keys], preferred_element_type=jnp.float32)
               + jnp.dot(p1, v_heads[1][keys], preferred_element_type=jnp.float32))
        yield
        denom = acc[:, pair_w:]
        done[b, pair] = (acc[:, :pair_w] / denom, jnp.where(first_head, m0, m1) + jnp.log(denom))

    waiting = [head_pair(b, pair) for b in range(ATTN_QB) for pair in range(HEADS_PER_GROUP // 2)]
    live = []
    while waiting or live:
        if waiting:
            live.append(waiting.pop(0))
        for gen in list(live):
            if next(gen, _DONE) is _DONE:
                live.remove(gen)

    for b in range(ATTN_QB):
        rows = slice(b * ATTN_BLOCK, (b + 1) * ATTN_BLOCK)
        pairs = [done[b, pair] for pair in range(HEADS_PER_GROUP // 2)]
        o_ref[rows, :] = jnp.concatenate([o for o, _ in pairs], axis=1).astype(o_ref.dtype)
        lse_ref[rows, :] = jnp.concatenate([lse for _, lse in pairs], axis=1)


def _attention_group(qkv_g, bias, gi, dilation):
    B, _, L, _ = qkv_g.shape
    rows = ATTN_QB * ATTN_BLOCK
    cur = lambda which: pl.BlockSpec((None, None, rows, GROUP_WIDTH), lambda b, r, n: (b, r, n, which))
    prev = lambda which: pl.BlockSpec((None, None, ATTN_BLOCK, GROUP_WIDTH),
                                      lambda b, r, n: (b, r, jnp.maximum(n * ATTN_QB - 1, 0), which))
    bias_blk = (None, HEADS_PER_GROUP, ATTN_BLOCK, 2 * ATTN_BLOCK)
    return pl.pallas_call(
        _attn_kernel,
        grid=(B, dilation, L // rows),
        in_specs=[
            cur(0), prev(1), cur(1), prev(2), cur(2),
            pl.BlockSpec(bias_blk, lambda b, r, n: (jnp.minimum(n, 1), 0, 0, 0)),
            pl.BlockSpec(bias_blk, lambda b, r, n: (1, 0, 0, 0)),
        ],
        out_specs=[cur(0), cur(0)],
        out_shape=[
            jax.ShapeDtypeStruct((B, dilation, L, GROUP_WIDTH), jnp.bfloat16),
            jax.ShapeDtypeStruct((B, dilation, L, GROUP_WIDTH), jnp.float32),
        ],
        compiler_params=pltpu.CompilerParams(
            dimension_semantics=("parallel", "parallel", "arbitrary"), vmem_limit_bytes=VMEM_LIMIT),
        name=f"dilated_attention_g{gi}",
    )(qkv_g, qkv_g, qkv_g, qkv_g, qkv_g, bias, bias)


def _t5_causal_bucket(dist):
    max_exact = N_REL_BUCKETS // 2
    d_f = jnp.maximum(dist, max_exact).astype(jnp.float32)
    large = max_exact + (jnp.log(d_f / max_exact) / math.log(REL_MAX_DISTANCE / max_exact)
                         * (N_REL_BUCKETS - max_exact)).astype(jnp.int32)
    large = jnp.minimum(large, N_REL_BUCKETS - 1)
    return jnp.where(dist < max_exact, dist, large)


def _attention_bias_table(rel_bias_g, window, dilation):
    nw = window // dilation
    qi = jnp.arange(ATTN_BLOCK)[:, None]
    ki = jnp.arange(2 * ATTN_BLOCK)[None, :]
    dist = ATTN_BLOCK + qi - ki
    band = (dist >= 0) & (dist <= nw)
    bucket = _t5_causal_bucket(jnp.maximum(dist, 0) * dilation)
    onehot = (bucket[:, :, None] == jnp.arange(N_REL_BUCKETS)).astype(jnp.float32)
    bias = jnp.einsum('qkb,bh->hqk', onehot, rel_bias_g.astype(jnp.float32),
                      precision=lax.Precision.HIGHEST)
    later = jnp.where(band[None], bias, NEG_INF)
    first = jnp.where((band & (ki >= ATTN_BLOCK))[None], bias, NEG_INF)
    return jnp.stack([first, later])


def _gelu_tanh(x):
    return 0.5 * x * (1.0 + jnp.tanh(math.sqrt(2.0 / math.pi) * (x + 0.044715 * (x * x * x))))


def _softplus(x):
    return jnp.maximum(x, 0.0) + jnp.log(1.0 + jnp.exp(-jnp.abs(x)))


def _natural_order(blk_ref, d, slabs):
    if d == 1:
        return blk_ref[0].astype(jnp.float32)
    per = MIX_TILE // d
    for r in range(d):
        rows = blk_ref[r].astype(jnp.float32)
        for c in range(GROUP_WIDTH // LANES):
            slabs[c, pl.ds(r, per, stride=d), :] = rows[:, c * LANES:(c + 1) * LANES]
    return jnp.concatenate([slabs[c] for c in range(GROUP_WIDTH // LANES)], axis=1)


def _mix_tile(x_ref, o1_ref, o2_ref, o3_ref, l1_ref, l2_ref, l3_ref,
              n1_ref, win_ref, cw_ref, cb_ref, wg_ref, brg_ref, big_ref, lam_ref,
              wpa_ref, wpl_ref, wout_ref, n2_ref, wrh_ref, wrl_ref,
              x1_ref, xs_ref, route_ref, cnt_ref,
              xbuf, a_sc, b_sc, h_sc, *slabs):
    tm = MIX_TILE
    first_tile = pl.program_id(0) == 0

    @pl.when(first_tile)
    def _():
        xbuf[0:SUBLANES, :] = jnp.zeros((SUBLANES, LRU_WIDTH), jnp.float32)
        h_sc[...] = jnp.zeros_like(h_sc)

    x = x_ref[...]
    h = _rmsnorm(x, n1_ref[...]).astype(jnp.bfloat16)
    def project(first_col):
        return jnp.dot(h, win_ref[:, first_col:first_col + D_MODEL], preferred_element_type=jnp.float32)

    xr = project(0)

    yield
    g_lru = project(LRU_WIDTH)
    xbuf[SUBLANES:SUBLANES + tm, :] = xr
    xc = xr * cw_ref[CONV_WIDTH - 1:CONV_WIDTH, :] + cb_ref[...]
    for j in range(CONV_WIDTH - 1):
        back = CONV_WIDTH - 1 - j
        xc = xc + xbuf[SUBLANES - back:SUBLANES - back + tm, :] * cw_ref[j:j + 1, :]
    xbuf[0:SUBLANES, :] = xbuf[tm:tm + SUBLANES, :]

    xcb = xc.astype(jnp.bfloat16)
    r_parts, i_parts = [], []
    for c in range(N_LRU_CHUNKS):
        ri = jnp.dot(xcb[:, c * LRU_CHUNK:(c + 1) * LRU_CHUNK], wg_ref[c],
                     preferred_element_type=jnp.float32)
        r_parts.append(ri[:, :LRU_CHUNK])
        i_parts.append(ri[:, LRU_CHUNK:])
    yield
    g_a = project(2 * LRU_WIDTH)
    r = jax.nn.sigmoid(jnp.concatenate(r_parts, axis=1) + brg_ref[...])
    ig = jax.nn.sigmoid(jnp.concatenate(i_parts, axis=1) + big_ref[...])
    log_a = (-LRU_C * _softplus(-lam_ref[...])) * r
    a = jnp.exp(log_a)
    gap = 1.0 - a * a
    root = jnp.where(gap > 0.0, gap * lax.rsqrt(gap), 0.0)
    bb = root * (ig * xc)

    yield
    g_b = project(2 * LRU_WIDTH + D_MODEL)
    row8 = lax.broadcasted_iota(jnp.int32, (tm, 1), 0) % SUBLANES
    for k in (1, 2, 4):
        a_s = pltpu.roll(a, k, axis=0)
        b_s = pltpu.roll(bb, k, axis=0)
        take = row8 >= k
        bb = jnp.where(take, a * b_s + bb, bb)
        a = jnp.where(take, a * a_s, a)
    a_sc[...] = a
    b_sc[...] = bb

    yield
    carry = h_sc[...]
    for g in range(tm // SUBLANES):
        rows = slice(g * SUBLANES, (g + 1) * SUBLANES)
        hh = a_sc[rows, :] * carry + b_sc[rows, :]
        b_sc[rows, :] = hh
        carry = hh[SUBLANES - 1:SUBLANES, :]
    h_sc[...] = carry
    yield
    lru = (b_sc[...] * _gelu_tanh(g_lru)).astype(jnp.bfloat16)
    p_lru = jnp.dot(lru, wpl_ref[...], preferred_element_type=jnp.float32)

    yield
    dil = [d for _, d in DILATED_GROUPS]
    o1, o2, o3 = (_natural_order(ref, d, slabs[2 * g]) for g, (ref, d) in
                  enumerate(zip((o1_ref, o2_ref, o3_ref), dil)))
    l1, l2, l3 = (_natural_order(ref, d, slabs[2 * g + 1]) for g, (ref, d) in
                  enumerate(zip((l1_ref, l2_ref, l3_ref), dil)))
    lm = jnp.maximum(jnp.maximum(l1, l2), l3)
    e1, e2, e3 = jnp.exp(l1 - lm), jnp.exp(l2 - lm), jnp.exp(l3 - lm)
    attn = (e1 * o1 + e2 * o2 + e3 * o3) / (e1 + e2 + e3)
    p_attn = jnp.dot(attn.astype(jnp.bfloat16), wpa_ref[...], preferred_element_type=jnp.float32)

    yield
    merged = jax.nn.sigmoid(g_a) * p_attn + jax.nn.sigmoid(g_b) * p_lru
    x1 = x + jnp.dot(merged.astype(jnp.bfloat16), wout_ref[...], preferred_element_type=jnp.float32)
    x1_ref[...] = x1
    h2 = _rmsnorm(x1, n2_ref[...])

    yield
    h2_hi = h2.astype(jnp.bfloat16)
    h2_lo = (h2 - h2_hi.astype(jnp.float32)).astype(jnp.bfloat16)
    logits = (jnp.dot(h2_hi, wrh_ref[...], preferred_element_type=jnp.float32)
              + (jnp.dot(h2_lo, wrh_ref[...], preferred_element_type=jnp.float32)
                 + jnp.dot(h2_hi, wrl_ref[...], preferred_element_type=jnp.float32)))
    lane = lax.broadcasted_iota(jnp.int32, (tm, LANES), 1)
    big = jnp.int32(LANES)
    lowest = jnp.float32(-3.0e38)
    is_g = (lane >= GROUP_LOGIT_LANE0) & (lane < GROUP_LOGIT_LANE0 + N_EXPERT_GROUPS)
    gl = jnp.where(is_g, logits, lowest)
    gmax = jnp.max(gl, axis=-1, keepdims=True)
    gsel = jnp.min(jnp.where(gl == gmax, lane, big), axis=-1, keepdims=True) - GROUP_LOGIT_LANE0
    p_sel = 1.0 / jnp.sum(jnp.where(is_g, jnp.exp(gl - gmax), 0.0), axis=-1, keepdims=True)
    is_e = (lane >= gsel * EXPERTS_PER_GROUP) & (lane < (gsel + 1) * EXPERTS_PER_GROUP)
    el = jnp.where(is_e, logits, lowest)
    v1 = jnp.max(el, axis=-1, keepdims=True)
    i1 = jnp.min(jnp.where(el == v1, lane, big), axis=-1, keepdims=True)
    el2 = jnp.where(lane == i1, lowest, el)
    v2 = jnp.max(el2, axis=-1, keepdims=True)
    i2 = jnp.min(jnp.where(el2 == v2, lane, big), axis=-1, keepdims=True)
    t = jnp.exp(v2 - v1)
    g0 = p_sel / (1.0 + t)
    g1 = p_sel * t / (1.0 + t)

    yield
    oh0 = lane == i1
    oh1 = lane == i2
    both = (oh0 | oh1).astype(jnp.float32)
    tri = (lax.broadcasted_iota(jnp.int32, (tm, tm), 0)
           > lax.broadcasted_iota(jnp.int32, (tm, tm), 1)).astype(jnp.bfloat16)
    before = jnp.dot(tri, both.astype(jnp.bfloat16), preferred_element_type=jnp.float32)
    cnt = jnp.sum(both, axis=0, keepdims=True)
    padded = jnp.floor((cnt + (MOE_CHUNK - 1.0)) * (1.0 / MOE_CHUNK)) * MOE_CHUNK
    upper = (lax.broadcasted_iota(jnp.int32, (LANES, LANES), 0)
             < lax.broadcasted_iota(jnp.int32, (LANES, LANES), 1)).astype(jnp.bfloat16)
    loff = jnp.dot(jnp.broadcast_to(padded, (SUBLANES, LANES)).astype(jnp.bfloat16), upper,
                   preferred_element_type=jnp.float32)[0:1, :]
    base = before + loff
    slot0 = jnp.sum(jnp.where(oh0, base, 0.0), axis=-1, keepdims=True)
    slot1 = jnp.sum(jnp.where(oh1, base, 0.0), axis=-1, keepdims=True)
    cnt_ref[...] = jnp.broadcast_to(cnt, cnt_ref.shape)

    slot_id = lax.broadcasted_iota(jnp.int32, (tm, MOE_SLOTS), 1)
    place = ((slot_id == slot0.astype(jnp.int32)) | (slot_id == slot1.astype(jnp.int32)))
    xs_ref[...] = _pack_bf16_pairs(
        lax.dot_general(place.astype(jnp.bfloat16), h2_hi, (((0,), (0,)), ((), ())),
                        preferred_element_type=jnp.float32))

    route = jnp.zeros((tm, LANES), jnp.float32)
    for lane_id, val in ((ROUTE_S0, slot0), (ROUTE_S1, slot1), (ROUTE_G0, g0), (ROUTE_G1, g1)):
        route = jnp.where(lane == lane_id, val, route)
    route_ref[...] = route


N_MIXER_TILED_INPUTS = 1 + 2 * N_GROUPS
N_MIXER_OUTPUTS = 4


def _mixer_kernel(*refs):
    n_batch = refs[0].shape[0]
    n_weights = len(refs) - N_MIXER_TILED_INPUTS - N_MIXER_OUTPUTS - n_batch * _MIXER_SCRATCH_PER_SEQ
    tiled = refs[:N_MIXER_TILED_INPUTS]
    weights = refs[N_MIXER_TILED_INPUTS:N_MIXER_TILED_INPUTS + n_weights]
    outs = refs[N_MIXER_TILED_INPUTS + n_weights:N_MIXER_TILED_INPUTS + n_weights + N_MIXER_OUTPUTS]
    scratch = refs[N_MIXER_TILED_INPUTS + n_weights + N_MIXER_OUTPUTS:]
    stages = []
    for b in range(n_batch):
        own = scratch[b * _MIXER_SCRATCH_PER_SEQ:(b + 1) * _MIXER_SCRATCH_PER_SEQ]
        stages.append(_mix_tile(*[r.at[b] for r in tiled], *weights, *[r.at[b] for r in outs], *own))
    while stages:
        for gen in list(stages):
            if next(gen, _DONE) is _DONE:
                stages.remove(gen)


def _mixer_scratch(tm):
    return [
        pltpu.VMEM((tm + 2 * SUBLANES, LRU_WIDTH), jnp.float32),
        pltpu.VMEM((tm, LRU_WIDTH), jnp.float32),
        pltpu.VMEM((tm, LRU_WIDTH), jnp.float32),
        pltpu.VMEM((1, LRU_WIDTH), jnp.float32),
    ] + [pltpu.VMEM((GROUP_WIDTH // LANES, tm, LANES), jnp.float32)] * (2 * N_GROUPS)


_MIXER_SCRATCH_PER_SEQ = len(_mixer_scratch(MIX_TILE))


def _token_mixer(x, attn_o, attn_lse, weights):
    B, S, _ = x.shape
    tm = MIX_TILE
    nt = S // tm
    rows = lambda j: (0, j, 0)

    def resident(shape):
        return pl.BlockSpec(shape, lambda j: (0,) * len(shape), pipeline_mode=pl.Buffered(1))

    in_specs = [pl.BlockSpec((B, tm, D_MODEL), rows)]
    group_specs = [pl.BlockSpec((B, d, tm // d, GROUP_WIDTH), lambda j: (0, 0, j, 0))
                   for _, d in DILATED_GROUPS]
    in_specs += group_specs + group_specs
    in_specs += [resident(w.shape) for w in weights]
    x1, xs, route, counts = pl.pallas_call(
        _mixer_kernel,
        grid=(nt,),
        in_specs=in_specs,
        out_specs=[
            pl.BlockSpec((B, tm, D_MODEL), rows),
            pl.BlockSpec((B, MOE_SLOTS, D_MODEL // 2), rows),
            pl.BlockSpec((B, tm, LANES), rows),
            pl.BlockSpec((B, None, SUBLANES, LANES), lambda j: (0, j, 0, 0)),
        ],
        out_shape=[
            jax.ShapeDtypeStruct((B, S, D_MODEL), jnp.float32),
            jax.ShapeDtypeStruct((B, nt * MOE_SLOTS, D_MODEL // 2), jnp.uint32),
            jax.ShapeDtypeStruct((B, S, LANES), jnp.float32),
            jax.ShapeDtypeStruct((B, nt, SUBLANES, LANES), jnp.float32),
        ],
        scratch_shapes=_mixer_scratch(tm) * B,
        compiler_params=pltpu.CompilerParams(
            dimension_semantics=("arbitrary",), vmem_limit_bytes=VMEM_LIMIT),
        name="token_mixer",
    )(x, *attn_o, *attn_lse, *weights)
    return (x1.reshape(B * S, D_MODEL), xs.reshape(B * nt * MOE_SLOTS, D_MODEL // 2),
            route.reshape(B * S, LANES), counts.reshape(B * nt, SUBLANES, LANES))


def _moe_tables(cnt_tile):
    ntiles = cnt_tile.shape[0]
    nch = (cnt_tile + MOE_CHUNK - 1) // MOE_CHUNK
    lo_c = jnp.cumsum(nch, axis=1) - nch
    per_expert = jnp.sum(nch, axis=0)
    region = ((per_expert + MOE_BLOCK_CHUNKS - 1) // MOE_BLOCK_CHUNKS) * MOE_BLOCK_CHUNKS
    pend = jnp.cumsum(region)
    glob = (pend - region)[None, :] + jnp.cumsum(nch, axis=0) - nch

    max_rows = TOP_K * ntiles * MIX_TILE + ntiles * N_EXPERTS * (MOE_CHUNK - 1)
    max_chunks = -(-max_rows // MOE_CHUNK) + N_EXPERTS * (MOE_BLOCK_CHUNKS - 1)
    max_blocks = -(-max_chunks // MOE_BLOCK_CHUNKS)
    max_chunks = max_blocks * MOE_BLOCK_CHUNKS

    seg_start = glob.T.reshape(-1)
    seg_src = (jnp.arange(ntiles, dtype=jnp.int32)[:, None] * TILE_CHUNKS + lo_c).T.reshape(-1)
    step = jnp.diff(seg_src - seg_start, prepend=0)
    g = jnp.arange(max_chunks, dtype=jnp.int32)
    src = g + jnp.sum(jnp.where(seg_start[None, :] <= g[:, None], step[None, :], 0), axis=1)
    src = jnp.clip(src, 0, ntiles * TILE_CHUNKS - 1)

    lstep = jnp.diff(glob - lo_c, axis=1, prepend=0)
    lc = jnp.arange(TILE_CHUNKS, dtype=jnp.int32)
    comb = lc[None, :] + jnp.sum(
        jnp.where(lo_c[:, None, :] <= lc[None, :, None], lstep[:, None, :], 0), axis=2)
    comb = jnp.clip(comb, 0, max_chunks - 1).reshape(-1)

    eidx = jnp.arange(N_EXPERTS, dtype=jnp.int32)
    nonempty = region > 0
    later = jnp.where(nonempty[None, :] & (eidx[None, :] > eidx[:, None]), eidx[None, :], N_EXPERTS)
    next_expert = jnp.min(later, axis=1)
    next_expert = jnp.where(next_expert == N_EXPERTS, -1, next_expert)
    wslot = (jnp.cumsum(nonempty.astype(jnp.int32)) - nonempty.astype(jnp.int32)) % 2
    last_expert = jnp.max(jnp.where(nonempty, eidx, 0))
    blk0 = jnp.arange(max_blocks, dtype=jnp.int32) * MOE_BLOCK_CHUNKS
    block_expert = jnp.minimum(jnp.sum((pend[None, :] <= blk0[:, None]).astype(jnp.int32), axis=1),
                               last_expert)
    of_block = lambda per_expert: jnp.sum(
        jnp.where(block_expert[:, None] == eidx[None, :], per_expert[None, :], 0), axis=1)
    n_valid = pend[-1:] // MOE_BLOCK_CHUNKS
    as_i32 = lambda v: v.astype(jnp.int32)
    expert_tables = (as_i32(block_expert), as_i32(of_block(next_expert)), as_i32(of_block(wslot)),
                     as_i32(n_valid), as_i32(src))
    return expert_tables, as_i32(comb)


def _chunk_copy(src_hbm, src_chunk, dst, dst_chunk, sem):
    rows = lambda c: pl.ds(pl.multiple_of(c * MOE_CHUNK, MOE_CHUNK), MOE_CHUNK)
    return pltpu.make_async_copy(src_hbm.at[rows(src_chunk), :], dst.at[rows(dst_chunk), :], sem)


def _expert_kernel(be_ref, nxt_ref, ws_ref, nv_ref, src_ref, xs_hbm, wgu_hbm, wd_hbm, ys_ref,
                   xbuf, wgu_f32, wd_f32, wgu_bf, wd_bf, sems, wsems):
    i = pl.program_id(0)
    valid = i < nv_ref[0]

    def gather(blk, slot):
        return [_chunk_copy(xs_hbm, src_ref[blk * MOE_BLOCK_CHUNKS + c], xbuf.at[slot], c, sems.at[slot])
                for c in range(MOE_BLOCK_CHUNKS)]

    def fetch_weights(e, slot):
        return [pltpu.make_async_copy(wgu_hbm.at[e], wgu_f32.at[slot], wsems.at[slot]),
                pltpu.make_async_copy(wd_hbm.at[e], wd_f32.at[slot], wsems.at[slot])]

    @pl.when(i == 0)
    def _():
        for cp in gather(0, 0):
            cp.start()
        for cp in fetch_weights(be_ref[0], 0):
            cp.start(priority=WEIGHT_DMA_PRIORITY)

    @pl.when(i + 1 < nv_ref[0])
    def _():
        for cp in gather(i + 1, (i + 1) % 2):
            cp.start()

    new_expert = valid & ((i == 0) | (be_ref[i] != be_ref[jnp.maximum(i - 1, 0)]))

    @pl.when(new_expert)
    def _():
        ws = ws_ref[i]
        for cp in fetch_weights(be_ref[i], ws):
            cp.wait()
        wgu_bf[...] = wgu_f32[ws].astype(jnp.bfloat16)
        wd_bf[...] = wd_f32[ws].astype(jnp.bfloat16)

        @pl.when(nxt_ref[i] >= 0)
        def _():
            for cp in fetch_weights(nxt_ref[i], 1 - ws):
                cp.start(priority=WEIGHT_DMA_PRIORITY)

    @pl.when(valid)
    def _():
        slot = i % 2
        for cp in gather(i, slot):
            cp.wait()
        x_lo, x_hi = _unpack_bf16_pairs(xbuf[slot])
        half = D_MODEL // 2
        partial = []

        def up_proj(cols):
            return (jnp.dot(x_lo, wgu_bf[:half, cols], preferred_element_type=jnp.float32)
                    + jnp.dot(x_hi, wgu_bf[half:, cols], preferred_element_type=jnp.float32))

        def hidden_group(c):
            cols = slice(c * EXPERT_GROUP, (c + 1) * EXPERT_GROUP)
            g = up_proj(cols)
            u = up_proj(slice(D_EXPERT + cols.start, D_EXPERT + cols.stop))
            yield
            act = (g * jax.nn.sigmoid(g) * u).astype(jnp.bfloat16)
            yield
            partial.append(jnp.dot(act, wd_bf[cols, :], preferred_element_type=jnp.float32))

        waiting = [hidden_group(c) for c in range(D_EXPERT // EXPERT_GROUP)]
        live = []
        while waiting or live:
            if waiting:
                live.append(waiting.pop(0))
            for gen in list(live):
                if next(gen, _DONE) is _DONE:
                    live.remove(gen)
        y = sum(partial[1:], partial[0])
        ys_ref[...] = _pack_bf16_pairs(y.astype(jnp.bfloat16).astype(jnp.float32))

    @pl.when(jnp.logical_not(valid))
    def _():
        ys_ref[...] = jnp.zeros_like(ys_ref)


def _experts(tables, xs, w_gate_up, w_down):
    block_expert, block_next, block_wslot, n_valid, src = tables
    nblk = block_expert.shape[0]
    return pl.pallas_call(
        _expert_kernel,
        grid_spec=pltpu.PrefetchScalarGridSpec(
            num_scalar_prefetch=5,
            grid=(nblk,),
            in_specs=[pl.BlockSpec(memory_space=pl.ANY)] * 3,
            out_specs=pl.BlockSpec((MOE_TILE, D_MODEL // 2), lambda i, *_: (i, 0)),
            scratch_shapes=[
                pltpu.VMEM((2, MOE_TILE, D_MODEL // 2), jnp.uint32),
                pltpu.VMEM((2, D_MODEL, 2 * D_EXPERT), jnp.float32),
                pltpu.VMEM((2, D_EXPERT, D_MODEL), jnp.float32),
                pltpu.VMEM((D_MODEL, 2 * D_EXPERT), jnp.bfloat16),
                pltpu.VMEM((D_EXPERT, D_MODEL), jnp.bfloat16),
                pltpu.SemaphoreType.DMA((2,)),
                pltpu.SemaphoreType.DMA((2,)),
            ],
        ),
        out_shape=jax.ShapeDtypeStruct((nblk * MOE_TILE, D_MODEL // 2), jnp.uint32),
        compiler_params=pltpu.CompilerParams(
            dimension_semantics=("arbitrary",), vmem_limit_bytes=VMEM_LIMIT),
        name="moe_experts",
    )(block_expert, block_next, block_wslot, n_valid, src, xs, w_gate_up, w_down)


def _combine_kernel(comb_ref, ys_hbm, x1_ref, route_ref, nf_ref, out_ref, ybuf, sems):
    tc = MIX_TILE
    per_step = COMBINE_TILES_PER_STEP
    i = pl.program_id(0)
    last = pl.num_programs(0) - 1

    def gather(step, slot):
        first = step * per_step * TILE_CHUNKS
        return [_chunk_copy(ys_hbm, comb_ref[first + lc], ybuf.at[slot], lc, sems.at[slot])
                for lc in range(per_step * TILE_CHUNKS)]

    @pl.when(i == 0)
    def _():
        for cp in gather(0, 0):
            cp.start()

    nxt = jnp.minimum(i + 1, last)
    for cp in gather(nxt, (i + 1) % 2):
        cp.start()

    slot = i % 2
    for cp in gather(i, slot):
        cp.wait()

    def tile(t):
        rows = slice(t * tc, (t + 1) * tc)
        y_lo, y_hi = _unpack_bf16_pairs(ybuf[slot, t * MOE_SLOTS:(t + 1) * MOE_SLOTS, :])
        route = route_ref[rows, :]
        slot_id = lax.broadcasted_iota(jnp.int32, (tc, MOE_SLOTS), 1)
        picked = []
        for lane in (ROUTE_S0, ROUTE_S1):
            sel = (slot_id == route[:, lane:lane + 1].astype(jnp.int32)).astype(jnp.bfloat16)
            yield
            picked.append(jnp.concatenate([jnp.dot(sel, y_lo, preferred_element_type=jnp.float32),
                                           jnp.dot(sel, y_hi, preferred_element_type=jnp.float32)], axis=1))
            yield
        g0 = route[:, ROUTE_G0:ROUTE_G0 + 1]
        g1 = route[:, ROUTE_G1:ROUTE_G1 + 1]
        out_ref[rows, :] = _rmsnorm(x1_ref[rows, :] + g0 * picked[0] + g1 * picked[1], nf_ref[...])

    waiting = [tile(t) for t in range(per_step)]
    live = []
    while waiting or live:
        if waiting:
            live.append(waiting.pop(0))
        for gen in list(live):
            if next(gen, _DONE) is _DONE:
                live.remove(gen)

    @pl.when(i == last)
    def _():
        for cp in gather(nxt, (i + 1) % 2):
            cp.wait()


def _combine(comb, ys, x1, route, norm_f):
    T = x1.shape[0]
    tc = MIX_TILE * COMBINE_TILES_PER_STEP
    return pl.pallas_call(
        _combine_kernel,
        grid_spec=pltpu.PrefetchScalarGridSpec(
            num_scalar_prefetch=1,
            grid=(T // tc,),
            in_specs=[
                pl.BlockSpec(memory_space=pl.ANY),
                pl.BlockSpec((tc, D_MODEL), lambda i, c: (i, 0)),
                pl.BlockSpec((tc, LANES), lambda i, c: (i, 0)),
                pl.BlockSpec((1, D_MODEL), lambda i, c: (0, 0)),
            ],
            out_specs=pl.BlockSpec((tc, D_MODEL), lambda i, c: (i, 0)),
            scratch_shapes=[
                pltpu.VMEM((2, COMBINE_TILES_PER_STEP * MOE_SLOTS, D_MODEL // 2), jnp.uint32),
                pltpu.SemaphoreType.DMA((2,)),
            ],
        ),
        out_shape=jax.ShapeDtypeStruct((T, D_MODEL), jnp.float32),
        compiler_params=pltpu.CompilerParams(
            dimension_semantics=("arbitrary",), vmem_limit_bytes=VMEM_LIMIT),
        name="moe_combine",
    )(comb, ys, x1, route, norm_f)


def _block_diag_gates(w_rg, w_ig):
    def bd(w):
        w4 = w.reshape(N_LRU_CHUNKS, LRU_CHUNK_HEADS, LRU_HEAD_DIM, LRU_HEAD_DIM)
        eye = jnp.eye(LRU_CHUNK_HEADS, dtype=w.dtype)
        return jnp.einsum('chij,hk->chikj', w4, eye).reshape(N_LRU_CHUNKS, LRU_CHUNK, LRU_CHUNK)
    return jnp.concatenate([bd(w_rg), bd(w_ig)], axis=-1).astype(jnp.bfloat16)


def kernel(x, rel_bias, norm1, w_in, conv_w, conv_b, w_rg, b_rg, w_ig, b_ig, lru_lambda,
           w_proj_attn, w_proj_lru, w_out, norm2, w_router_group, w_router_expert,
           w_gate_up, w_down, norm_f):
    B, S, D = x.shape
    T = B * S
    assert w_in.shape[0] == 1, "single-layer block"
    layer = 0
    bf16 = jnp.bfloat16
    x2 = x.reshape(T, D)
    row = lambda v: v[layer].reshape(1, -1)
    w_qkv = (w_in[layer][:, :QKV_WIDTH].reshape(D, 3, N_GROUPS, GROUP_WIDTH)
             .transpose(0, 2, 1, 3).reshape(D, QKV_WIDTH).astype(bf16))
    w_rest = w_in[layer][:, QKV_WIDTH:].astype(bf16)

    qkv_groups = _qkv_projection(x2, row(norm1), w_qkv, B, S)
    attn_o, attn_lse = [], []
    for gi, (window, dilation) in enumerate(DILATED_GROUPS):
        hs = slice(gi * HEADS_PER_GROUP, (gi + 1) * HEADS_PER_GROUP)
        bias = _attention_bias_table(rel_bias[:, hs], window, dilation)
        o, lse = _attention_group(qkv_groups[gi], bias, gi, dilation)
        attn_o.append(o)
        attn_lse.append(lse)

    w_router = jnp.zeros((D, LANES), jnp.float32)
    w_router = w_router.at[:, :N_EXPERTS].set(w_router_expert[layer].astype(jnp.float32))
    w_router = w_router.at[:, GROUP_LOGIT_LANE0:GROUP_LOGIT_LANE0 + N_EXPERT_GROUPS].set(
        w_router_group[layer].astype(jnp.float32))
    w_router_hi = w_router.astype(bf16)
    weights = [
        row(norm1), w_rest, conv_w[layer], row(conv_b),
        _block_diag_gates(w_rg[layer], w_ig[layer]), row(b_rg), row(b_ig), row(lru_lambda),
        w_proj_attn[layer].astype(bf16), w_proj_lru[layer].astype(bf16), w_out[layer].astype(bf16),
        row(norm2), w_router_hi, (w_router - w_router_hi.astype(jnp.float32)).astype(bf16),
    ]
    x1, xs, route, counts = _token_mixer(x, attn_o, attn_lse, weights)

    expert_tables, comb = _moe_tables(counts[:, 0, :N_EXPERTS].astype(jnp.int32))
    ys = _experts(expert_tables, xs, w_gate_up[layer], w_down[layer])
    out = _combine(comb, ys, x1, route, norm_f.reshape(1, -1))
    return out.reshape(B, S, D)
```

```python
import functools
import math

import jax
import jax.numpy as jnp
from jax import lax
from jax.experimental import pallas as pl
from jax.experimental.pallas import tpu as pltpu

D_MODEL = 1024
HEAD_DIM = 64
HEADS_PER_GROUP = 4
DILATED_GROUPS = ((128, 1), (512, 4), (2048, 16))
N_GROUPS = len(DILATED_GROUPS)
N_ATTN_HEADS = HEADS_PER_GROUP * N_GROUPS
ATTN_WIDTH = N_ATTN_HEADS * HEAD_DIM
GROUP_WIDTH = HEADS_PER_GROUP * HEAD_DIM
QKV_WIDTH = 3 * ATTN_WIDTH
ATTN_BLOCK = 128
N_REL_BUCKETS = 32
REL_MAX_DISTANCE = 2048
NEG_INF = -1e30

LRU_WIDTH = D_MODEL
LRU_HEADS = 16
LRU_HEAD_DIM = LRU_WIDTH // LRU_HEADS
CONV_WIDTH = 4
LRU_C = 8.0
LRU_CHUNK_HEADS = 4
LRU_CHUNK = LRU_CHUNK_HEADS * LRU_HEAD_DIM
N_LRU_CHUNKS = LRU_WIDTH // LRU_CHUNK

N_EXPERT_GROUPS = 4
EXPERTS_PER_GROUP = 8
N_EXPERTS = N_EXPERT_GROUPS * EXPERTS_PER_GROUP
TOP_K = 2
D_EXPERT = 512
EPS = 1e-6

REST_WIDTH = 2 * LRU_WIDTH + 2 * D_MODEL
LANES = 128
SUBLANES = 8

ATTN_TILE = 256
ATTN_TILES_PER_STEP = 2
ATTN_QB = 8
MIX_TILE = 256
MOE_TILE = 256
EXPERT_BLOCKS_PER_STEP = 4
COMBINE_TILES_PER_STEP = 2
WEIGHT_DMA_PRIORITY = 1
EXPERT_GROUP = 256
MOE_CHUNK = SUBLANES
MOE_BLOCK_CHUNKS = MOE_TILE // MOE_CHUNK
MOE_SLOTS = 768
assert MOE_SLOTS >= TOP_K * MIX_TILE + N_EXPERTS * (MOE_CHUNK - 1) and MOE_SLOTS % MOE_CHUNK == 0
TILE_CHUNKS = MOE_SLOTS // MOE_CHUNK
VMEM_LIMIT = 56 * 1024 * 1024

ROUTE_S0, ROUTE_S1, ROUTE_G0, ROUTE_G1 = range(4)
GROUP_LOGIT_LANE0 = N_EXPERTS


_DONE = object()


def _rmsnorm(x, g):
    return x * lax.rsqrt(jnp.mean(x * x, axis=-1, keepdims=True) + EPS) * g


def _pack_bf16_pairs(v):
    w = v.shape[1] // 2
    bits = lax.bitcast_convert_type(v, jnp.uint32)
    return bits[:, w:] | (bits[:, :w] >> 16)


def _unpack_bf16_pairs(p):
    lo = lax.bitcast_convert_type(p << 16, jnp.float32)
    hi = lax.bitcast_convert_type(p & jnp.uint32(0xFFFF0000), jnp.float32)
    return lo.astype(jnp.bfloat16), hi.astype(jnp.bfloat16)


def _residue_major_perm(n, d):
    per = n // d
    out = lax.broadcasted_iota(jnp.int32, (n, n), 0)
    src = lax.broadcasted_iota(jnp.int32, (n, n), 1)
    m = jnp.bitwise_and(out, per - 1)
    r = jnp.right_shift(out, per.bit_length() - 1)
    return (src == m * d + r).astype(jnp.bfloat16)


def _qkv_kernel(x_ref, g_ref, w_ref, *o_refs):
    tm = ATTN_TILE

    def tile(t):
        rows = slice(t * tm, (t + 1) * tm)
        h = _rmsnorm(x_ref[rows, :], g_ref[...]).astype(jnp.bfloat16)
        yield
        for gi, (o_ref, (_, d)) in enumerate(zip(o_refs, DILATED_GROUPS)):
            cols = slice(gi * ATTN_WIDTH, (gi + 1) * ATTN_WIDTH)
            part = jnp.dot(h, w_ref[:, cols], preferred_element_type=jnp.float32).astype(jnp.bfloat16)
            yield
            if d > 1:
                part = jnp.dot(_residue_major_perm(tm, d), part,
                               preferred_element_type=jnp.float32).astype(jnp.bfloat16)
            o_ref[:, t * (tm // d):(t + 1) * (tm // d), :] = part.reshape(d, tm // d, ATTN_WIDTH)
            yield

    waiting = [tile(t) for t in range(ATTN_TILES_PER_STEP)]
    live = []
    while waiting or live:
        if waiting:
            live.append(waiting.pop(0))
        for gen in list(live):
            if next(gen, _DONE) is _DONE:
                live.remove(gen)


def _qkv_projection(x2, norm1, w_qkv, B, S):
    tm = ATTN_TILE * ATTN_TILES_PER_STEP
    nt = S // tm
    return pl.pallas_call(
        _qkv_kernel,
        grid=(B, nt),
        in_specs=[
            pl.BlockSpec((tm, D_MODEL), lambda b, j: (b * nt + j, 0)),
            pl.BlockSpec((1, D_MODEL), lambda b, j: (0, 0)),
            pl.BlockSpec((D_MODEL, QKV_WIDTH), lambda b, j: (0, 0)),
        ],
        out_specs=[pl.BlockSpec((None, d, tm // d, ATTN_WIDTH), lambda b, j: (b, 0, j, 0))
                   for _, d in DILATED_GROUPS],
        out_shape=[jax.ShapeDtypeStruct((B, d, S // d, ATTN_WIDTH), jnp.bfloat16)
                   for _, d in DILATED_GROUPS],
        compiler_params=pltpu.CompilerParams(
            dimension_semantics=("parallel", "parallel"), vmem_limit_bytes=VMEM_LIMIT),
        name="qkv_projection",
    )(x2, norm1, w_qkv)


def _attn_kernel(n_qb, q_ref, kp_ref, kc_ref, vp_ref, vc_ref, bias0_ref, bias_ref, o_ref, lse_ref):
    q = q_ref[...] * (HEAD_DIM ** -0.5)
    k = jnp.concatenate([kp_ref[...], kc_ref[...]], axis=0)
    v = jnp.concatenate([vp_ref[...], vc_ref[...]], axis=0)
    pair_w = 2 * HEAD_DIM
    first_head = lax.broadcasted_iota(jnp.int32, (1, pair_w), 1) < HEAD_DIM
    def one_head(x, first):
        bits = pltpu.bitcast(x, jnp.uint32)
        keep = first_head if first else jnp.logical_not(first_head)
        return pltpu.bitcast(jnp.where(keep, bits, jnp.uint32(0)), jnp.bfloat16)

    ones = jnp.ones_like(k[:, :pair_w])
    operands = []
    for pair in range(HEADS_PER_GROUP // 2):
        cols = slice(pair * pair_w, (pair + 1) * pair_w)
        k2, v2 = k[:, cols], v[:, cols]
        k_heads = (one_head(k2, True), one_head(k2, False))
        v_heads = (jnp.concatenate([one_head(v2, True), one_head(ones, True)], axis=1),
                   jnp.concatenate([one_head(v2, False), one_head(ones, False)], axis=1))
        operands.append((cols, k_heads, v_heads))
    done = {}

    def head_pair(b, pair):
        rows = slice(b * ATTN_BLOCK, (b + 1) * ATTN_BLOCK)
        keys = slice(b * ATTN_BLOCK, (b + 2) * ATTN_BLOCK)
        cols, k_heads, v_heads = operands[pair]
        bias_ref_b = bias0_ref if b == 0 else bias_ref
        bias = jnp.concatenate([bias_ref_b[2 * pair], bias_ref_b[2 * pair + 1]], axis=1)
        k_both = jnp.concatenate([k_heads[0][keys], k_heads[1][keys]], axis=0)
        s = lax.dot_general(q[rows, cols], k_both, (((1,), (1,)), ((), ())),
                            preferred_element_type=jnp.float32) + bias
        yield
        n_keys = 2 * ATTN_BLOCK
        s0, s1 = s[:, :n_keys], s[:, n_keys:]
        m0 = jnp.max(s0, axis=-1, keepdims=True)
        m1 = jnp.max(s1, axis=-1, keepdims=True)
        p0 = jnp.exp(s0 - m0).astype(jnp.bfloat16)
        p1 = jnp.exp(s1 - m1).astype(jnp.bfloat16)
        yield
        acc = (jnp.dot(p0, v_heads[0][keys], preferred_element_type=jnp.float32)
               + jnp.dot(p1, v_heads[1][keys], preferred_element_type=jnp.float32))
        yield
        denom = acc[:, pair_w:]
        done[b, pair] = (acc[:, :pair_w] / denom, jnp.where(first_head, m0, m1) + jnp.log(denom))

    waiting = [head_pair(b, pair) for b in range(n_qb) for pair in range(HEADS_PER_GROUP // 2)]
    live = []
    while waiting or live:
        if waiting:
            live.append(waiting.pop(0))
        for gen in list(live):
            if next(gen, _DONE) is _DONE:
                live.remove(gen)

    for b in range(n_qb):
        rows = slice(b * ATTN_BLOCK, (b + 1) * ATTN_BLOCK)
        pairs = [done[b, pair] for pair in range(HEADS_PER_GROUP // 2)]
        o_ref[rows, :] = jnp.concatenate([o for o, _ in pairs], axis=1).astype(o_ref.dtype)
        lse_ref[rows, :] = jnp.concatenate([lse for _, lse in pairs], axis=1)


def _attention_group(qkv_g, bias, gi, dilation):
    B, _, L, _ = qkv_g.shape
    n_qb = min(ATTN_QB, L // ATTN_BLOCK)
    rows = n_qb * ATTN_BLOCK
    cur = lambda which: pl.BlockSpec((None, None, rows, GROUP_WIDTH), lambda b, r, n: (b, r, n, which))
    prev = lambda which: pl.BlockSpec((None, None, ATTN_BLOCK, GROUP_WIDTH),
                                      lambda b, r, n: (b, r, jnp.maximum(n * n_qb - 1, 0), which))
    bias_blk = (None, HEADS_PER_GROUP, ATTN_BLOCK, 2 * ATTN_BLOCK)
    return pl.pallas_call(
        functools.partial(_attn_kernel, n_qb),
        grid=(B, dilation, L // rows),
        in_specs=[
            cur(0), prev(1), cur(1), prev(2), cur(2),
            pl.BlockSpec(bias_blk, lambda b, r, n: (jnp.minimum(n, 1), 0, 0, 0)),
            pl.BlockSpec(bias_blk, lambda b, r, n: (1, 0, 0, 0)),
        ],
        out_specs=[cur(0), cur(0)],
        out_shape=[
            jax.ShapeDtypeStruct((B, dilation, L, GROUP_WIDTH), jnp.bfloat16),
            jax.ShapeDtypeStruct((B, dilation, L, GROUP_WIDTH), jnp.float32),
        ],
        compiler_params=pltpu.CompilerParams(
            dimension_semantics=("parallel", "parallel", "arbitrary"), vmem_limit_bytes=VMEM_LIMIT),
        name=f"dilated_attention_g{gi}",
    )(qkv_g, qkv_g, qkv_g, qkv_g, qkv_g, bias, bias)


def _t5_causal_bucket(dist):
    max_exact = N_REL_BUCKETS // 2
    d_f = jnp.maximum(dist, max_exact).astype(jnp.float32)
    large = max_exact + (jnp.log(d_f / max_exact) / math.log(REL_MAX_DISTANCE / max_exact)
                         * (N_REL_BUCKETS - max_exact)).astype(jnp.int32)
    large = jnp.minimum(large, N_REL_BUCKETS - 1)
    return jnp.where(dist < max_exact, dist, large)


def _attention_bias_table(rel_bias_g, window, dilation):
    nw = window // dilation
    qi = jnp.arange(ATTN_BLOCK)[:, None]
    ki = jnp.arange(2 * ATTN_BLOCK)[None, :]
    dist = ATTN_BLOCK + qi - ki
    band = (dist >= 0) & (dist <= nw)
    bucket = _t5_causal_bucket(jnp.maximum(dist, 0) * dilation)
    onehot = (bucket[:, :, None] == jnp.arange(N_REL_BUCKETS)).astype(jnp.float32)
    bias = jnp.einsum('qkb,bh->hqk', onehot, rel_bias_g.astype(jnp.float32),
                      precision=lax.Precision.HIGHEST)
    later = jnp.where(band[None], bias, NEG_INF)
    first = jnp.where((band & (ki >= ATTN_BLOCK))[None], bias, NEG_INF)
    return jnp.stack([first, later])


def _gelu_tanh(x):
    return 0.5 * x * (1.0 + jnp.tanh(math.sqrt(2.0 / math.pi) * (x + 0.044715 * (x * x * x))))


def _softplus(x):
    return jnp.maximum(x, 0.0) + jnp.log(1.0 + jnp.exp(-jnp.abs(x)))


def _natural_order(blk_ref, d, slabs):
    if d == 1:
        return blk_ref[0].astype(jnp.float32)
    per = MIX_TILE // d
    for r in range(d):
        rows = blk_ref[r].astype(jnp.float32)
        for c in range(GROUP_WIDTH // LANES):
            slabs[c, pl.ds(r, per, stride=d), :] = rows[:, c * LANES:(c + 1) * LANES]
    return jnp.concatenate([slabs[c] for c in range(GROUP_WIDTH // LANES)], axis=1)


def _mix_tile(x_ref, o1_ref, o2_ref, o3_ref, l1_ref, l2_ref, l3_ref,
              n1_ref, win_ref, cw_ref, cb_ref, wg_ref, brg_ref, big_ref, lam_ref,
              wpa_ref, wpl_ref, wout_ref, n2_ref, wrh_ref, wrl_ref,
              x1_ref, xs_ref, route_ref, cnt_ref,
              xbuf, a_sc, b_sc, h_sc, *slabs):
    tm = MIX_TILE
    first_tile = pl.program_id(0) == 0

    @pl.when(first_tile)
    def _():
        xbuf[0:SUBLANES, :] = jnp.zeros((SUBLANES, LRU_WIDTH), jnp.float32)
        h_sc[...] = jnp.zeros_like(h_sc)

    x = x_ref[...]
    h = _rmsnorm(x, n1_ref[...]).astype(jnp.bfloat16)
    def project(first_col):
        return jnp.dot(h, win_ref[:, first_col:first_col + D_MODEL], preferred_element_type=jnp.float32)

    xr = project(0)

    yield
    g_lru = project(LRU_WIDTH)
    xbuf[SUBLANES:SUBLANES + tm, :] = xr
    xc = xr * cw_ref[CONV_WIDTH - 1:CONV_WIDTH, :] + cb_ref[...]
    for j in range(CONV_WIDTH - 1):
        back = CONV_WIDTH - 1 - j
        xc = xc + xbuf[SUBLANES - back:SUBLANES - back + tm, :] * cw_ref[j:j + 1, :]
    xbuf[0:SUBLANES, :] = xbuf[tm:tm + SUBLANES, :]

    xcb = xc.astype(jnp.bfloat16)
    r_parts, i_parts = [], []
    for c in range(N_LRU_CHUNKS):
        ri = jnp.dot(xcb[:, c * LRU_CHUNK:(c + 1) * LRU_CHUNK], wg_ref[c],
                     preferred_element_type=jnp.float32)
        r_parts.append(ri[:, :LRU_CHUNK])
        i_parts.append(ri[:, LRU_CHUNK:])
    yield
    g_a = project(2 * LRU_WIDTH)
    r = jax.nn.sigmoid(jnp.concatenate(r_parts, axis=1) + brg_ref[...])
    ig = jax.nn.sigmoid(jnp.concatenate(i_parts, axis=1) + big_ref[...])
    log_a = (-LRU_C * _softplus(-lam_ref[...])) * r
    a = jnp.exp(log_a)
    gap = 1.0 - a * a
    root = jnp.where(gap > 0.0, gap * lax.rsqrt(gap), 0.0)
    bb = root * (ig * xc)

    yield
    g_b = project(2 * LRU_WIDTH + D_MODEL)
    row8 = lax.broadcasted_iota(jnp.int32, (tm, 1), 0) % SUBLANES
    for k in (1, 2, 4):
        a_s = pltpu.roll(a, k, axis=0)
        b_s = pltpu.roll(bb, k, axis=0)
        take = row8 >= k
        bb = jnp.where(take, a * b_s + bb, bb)
        a = jnp.where(take, a * a_s, a)
    a_sc[...] = a
    b_sc[...] = bb

    yield
    carry = h_sc[...]
    for g in range(tm // SUBLANES):
        rows = slice(g * SUBLANES, (g + 1) * SUBLANES)
        hh = a_sc[rows, :] * carry + b_sc[rows, :]
        b_sc[rows, :] = hh
        carry = hh[SUBLANES - 1:SUBLANES, :]
    h_sc[...] = carry
    yield
    lru = (b_sc[...] * _gelu_tanh(g_lru)).astype(jnp.bfloat16)
    p_lru = jnp.dot(lru, wpl_ref[...], preferred_element_type=jnp.float32)

    yield
    dil = [d for _, d in DILATED_GROUPS]
    o1, o2, o3 = (_natural_order(ref, d, slabs[2 * g]) for g, (ref, d) in
                  enumerate(zip((o1_ref, o2_ref, o3_ref), dil)))
    l1, l2, l3 = (_natural_order(ref, d, slabs[2 * g + 1]) for g, (ref, d) in
                  enumerate(zip((l1_ref, l2_ref, l3_ref), dil)))
    lm = jnp.maximum(jnp.maximum(l1, l2), l3)
    e1, e2, e3 = jnp.exp(l1 - lm), jnp.exp(l2 - lm), jnp.exp(l3 - lm)
    attn = (e1 * o1 + e2 * o2 + e3 * o3) / (e1 + e2 + e3)
    p_attn = jnp.dot(attn.astype(jnp.bfloat16), wpa_ref[...], preferred_element_type=jnp.float32)

    yield
    merged = jax.nn.sigmoid(g_a) * p_attn + jax.nn.sigmoid(g_b) * p_lru
    x1 = x + jnp.dot(merged.astype(jnp.bfloat16), wout_ref[...], preferred_element_type=jnp.float32)
    x1_ref[...] = x1
    h2 = _rmsnorm(x1, n2_ref[...])

    yield
    h2_hi = h2.astype(jnp.bfloat16)
    h2_lo = (h2 - h2_hi.astype(jnp.float32)).astype(jnp.bfloat16)
    logits = (jnp.dot(h2_hi, wrh_ref[...], preferred_element_type=jnp.float32)
              + (jnp.dot(h2_lo, wrh_ref[...], preferred_element_type=jnp.float32)
                 + jnp.dot(h2_hi, wrl_ref[...], preferred_element_type=jnp.float32)))
    lane = lax.broadcasted_iota(jnp.int32, (tm, LANES), 1)
    big = jnp.int32(LANES)
    lowest = jnp.float32(-3.0e38)
    is_g = (lane >= GROUP_LOGIT_LANE0) & (lane < GROUP_LOGIT_LANE0 + N_EXPERT_GROUPS)
    gl = jnp.where(is_g, logits, lowest)
    gmax = jnp.max(gl, axis=-1, keepdims=True)
    gsel = jnp.min(jnp.where(gl == gmax, lane, big), axis=-1, keepdims=True) - GROUP_LOGIT_LANE0
    p_sel = 1.0 / jnp.sum(jnp.where(is_g, jnp.exp(gl - gmax), 0.0), axis=-1, keepdims=True)
    is_e = (lane >= gsel * EXPERTS_PER_GROUP) & (lane < (gsel + 1) * EXPERTS_PER_GROUP)
    el = jnp.where(is_e, logits, lowest)
    v1 = jnp.max(el, axis=-1, keepdims=True)
    i1 = jnp.min(jnp.where(el == v1, lane, big), axis=-1, keepdims=True)
    el2 = jnp.where(lane == i1, lowest, el)
    v2 = jnp.max(el2, axis=-1, keepdims=True)
    i2 = jnp.min(jnp.where(el2 == v2, lane, big), axis=-1, keepdims=True)
    t = jnp.exp(v2 - v1)
    g0 = p_sel / (1.0 + t)
    g1 = p_sel * t / (1.0 + t)

    yield
    oh0 = lane == i1
    oh1 = lane == i2
    both = (oh0 | oh1).astype(jnp.float32)
    tri = (lax.broadcasted_iota(jnp.int32, (tm, tm), 0)
           > lax.broadcasted_iota(jnp.int32, (tm, tm), 1)).astype(jnp.bfloat16)
    before = jnp.dot(tri, both.astype(jnp.bfloat16), preferred_element_type=jnp.float32)
    cnt = jnp.sum(both, axis=0, keepdims=True)
    padded = jnp.floor((cnt + (MOE_CHUNK - 1.0)) * (1.0 / MOE_CHUNK)) * MOE_CHUNK
    upper = (lax.broadcasted_iota(jnp.int32, (LANES, LANES), 0)
             < lax.broadcasted_iota(jnp.int32, (LANES, LANES), 1)).astype(jnp.bfloat16)
    loff = jnp.dot(jnp.broadcast_to(padded, (SUBLANES, LANES)).astype(jnp.bfloat16), upper,
                   preferred_element_type=jnp.float32)[0:1, :]
    base = before + loff
    slot0 = jnp.sum(jnp.where(oh0, base, 0.0), axis=-1, keepdims=True)
    slot1 = jnp.sum(jnp.where(oh1, base, 0.0), axis=-1, keepdims=True)
    cnt_ref[...] = jnp.broadcast_to(cnt, cnt_ref.shape)

    slot_id = lax.broadcasted_iota(jnp.int32, (tm, MOE_SLOTS), 1)
    place = ((slot_id == slot0.astype(jnp.int32)) | (slot_id == slot1.astype(jnp.int32)))
    xs_ref[...] = _pack_bf16_pairs(
        lax.dot_general(place.astype(jnp.bfloat16), h2_hi, (((0,), (0,)), ((), ())),
                        preferred_element_type=jnp.float32))

    route = jnp.zeros((tm, LANES), jnp.float32)
    for lane_id, val in ((ROUTE_S0, slot0), (ROUTE_S1, slot1), (ROUTE_G0, g0), (ROUTE_G1, g1)):
        route = jnp.where(lane == lane_id, val, route)
    route_ref[...] = route


N_MIXER_TILED_INPUTS = 1 + 2 * N_GROUPS
N_MIXER_OUTPUTS = 4


def _mixer_kernel(*refs):
    n_batch = refs[0].shape[0]
    n_weights = len(refs) - N_MIXER_TILED_INPUTS - N_MIXER_OUTPUTS - n_batch * _MIXER_SCRATCH_PER_SEQ
    tiled = refs[:N_MIXER_TILED_INPUTS]
    weights = refs[N_MIXER_TILED_INPUTS:N_MIXER_TILED_INPUTS + n_weights]
    outs = refs[N_MIXER_TILED_INPUTS + n_weights:N_MIXER_TILED_INPUTS + n_weights + N_MIXER_OUTPUTS]
    scratch = refs[N_MIXER_TILED_INPUTS + n_weights + N_MIXER_OUTPUTS:]
    stages = []
    for b in range(n_batch):
        own = scratch[b * _MIXER_SCRATCH_PER_SEQ:(b + 1) * _MIXER_SCRATCH_PER_SEQ]
        stages.append(_mix_tile(*[r.at[b] for r in tiled], *weights, *[r.at[b] for r in outs], *own))
    while stages:
        for gen in list(stages):
            if next(gen, _DONE) is _DONE:
                stages.remove(gen)


def _mixer_scratch(tm):
    return [
        pltpu.VMEM((tm + 2 * SUBLANES, LRU_WIDTH), jnp.float32),
        pltpu.VMEM((tm, LRU_WIDTH), jnp.float32),
        pltpu.VMEM((tm, LRU_WIDTH), jnp.float32),
        pltpu.VMEM((1, LRU_WIDTH), jnp.float32),
    ] + [pltpu.VMEM((GROUP_WIDTH // LANES, tm, LANES), jnp.float32)] * (2 * N_GROUPS)


_MIXER_SCRATCH_PER_SEQ = len(_mixer_scratch(MIX_TILE))


def _token_mixer(x, attn_o, attn_lse, weights):
    B, S, _ = x.shape
    tm = MIX_TILE
    nt = S // tm
    rows = lambda j: (0, j, 0)

    def resident(shape):
        return pl.BlockSpec(shape, lambda j: (0,) * len(shape), pipeline_mode=pl.Buffered(1))

    in_specs = [pl.BlockSpec((B, tm, D_MODEL), rows)]
    group_specs = [pl.BlockSpec((B, d, tm // d, GROUP_WIDTH), lambda j: (0, 0, j, 0))
                   for _, d in DILATED_GROUPS]
    in_specs += group_specs + group_specs
    in_specs += [resident(w.shape) for w in weights]
    x1, xs, route, counts = pl.pallas_call(
        _mixer_kernel,
        grid=(nt,),
        in_specs=in_specs,
        out_specs=[
            pl.BlockSpec((B, tm, D_MODEL), rows),
            pl.BlockSpec((B, MOE_SLOTS, D_MODEL // 2), rows),
            pl.BlockSpec((B, tm, LANES), rows),
            pl.BlockSpec((B, None, SUBLANES, LANES), lambda j: (0, j, 0, 0)),
        ],
        out_shape=[
            jax.ShapeDtypeStruct((B, S, D_MODEL), jnp.float32),
            jax.ShapeDtypeStruct((B, nt * MOE_SLOTS, D_MODEL // 2), jnp.uint32),
            jax.ShapeDtypeStruct((B, S, LANES), jnp.float32),
            jax.ShapeDtypeStruct((B, nt, SUBLANES, LANES), jnp.float32),
        ],
        scratch_shapes=_mixer_scratch(tm) * B,
        compiler_params=pltpu.CompilerParams(
            dimension_semantics=("arbitrary",), vmem_limit_bytes=VMEM_LIMIT),
        name="token_mixer",
    )(x, *attn_o, *attn_lse, *weights)
    return (x1.reshape(B * S, D_MODEL), xs.reshape(B * nt * MOE_SLOTS, D_MODEL // 2),
            route.reshape(B * S, LANES), counts.reshape(B * nt, SUBLANES, LANES))


def _moe_tables(cnt_tile):
    ntiles = cnt_tile.shape[0]
    nch = (cnt_tile + MOE_CHUNK - 1) // MOE_CHUNK
    lo_c = jnp.cumsum(nch, axis=1) - nch
    per_expert = jnp.sum(nch, axis=0)
    region = ((per_expert + MOE_BLOCK_CHUNKS - 1) // MOE_BLOCK_CHUNKS) * MOE_BLOCK_CHUNKS
    pend = jnp.cumsum(region)
    glob = (pend - region)[None, :] + jnp.cumsum(nch, axis=0) - nch

    max_rows = TOP_K * ntiles * MIX_TILE + ntiles * N_EXPERTS * (MOE_CHUNK - 1)
    max_chunks = -(-max_rows // MOE_CHUNK) + N_EXPERTS * (MOE_BLOCK_CHUNKS - 1)
    max_blocks = -(-max_chunks // MOE_BLOCK_CHUNKS)
    max_blocks = -(-max_blocks // EXPERT_BLOCKS_PER_STEP) * EXPERT_BLOCKS_PER_STEP
    max_chunks = max_blocks * MOE_BLOCK_CHUNKS

    seg_start = glob.T.reshape(-1)
    seg_src = (jnp.arange(ntiles, dtype=jnp.int32)[:, None] * TILE_CHUNKS + lo_c).T.reshape(-1)
    step = jnp.diff(seg_src - seg_start, prepend=0)
    g = jnp.arange(max_chunks, dtype=jnp.int32)
    src = g + jnp.sum(jnp.where(seg_start[None, :] <= g[:, None], step[None, :], 0), axis=1)
    src = jnp.clip(src, 0, ntiles * TILE_CHUNKS - 1)

    lstep = jnp.diff(glob - lo_c, axis=1, prepend=0)
    lc = jnp.arange(TILE_CHUNKS, dtype=jnp.int32)
    comb = lc[None, :] + jnp.sum(
        jnp.where(lo_c[:, None, :] <= lc[None, :, None], lstep[:, None, :], 0), axis=2)
    comb = jnp.clip(comb, 0, max_chunks - 1).reshape(-1)

    eidx = jnp.arange(N_EXPERTS, dtype=jnp.int32)
    nonempty = region > 0
    later = jnp.where(nonempty[None, :] & (eidx[None, :] > eidx[:, None]), eidx[None, :], N_EXPERTS)
    next_expert = jnp.min(later, axis=1)
    next_expert = jnp.where(next_expert == N_EXPERTS, -1, next_expert)
    wslot = (jnp.cumsum(nonempty.astype(jnp.int32)) - nonempty.astype(jnp.int32)) % 2
    last_expert = jnp.max(jnp.where(nonempty, eidx, 0))
    blk0 = jnp.arange(max_blocks, dtype=jnp.int32) * MOE_BLOCK_CHUNKS
    block_expert = jnp.minimum(jnp.sum((pend[None, :] <= blk0[:, None]).astype(jnp.int32), axis=1),
                               last_expert)
    of_block = lambda per_expert: jnp.sum(
        jnp.where(block_expert[:, None] == eidx[None, :], per_expert[None, :], 0), axis=1)
    n_valid = pend[-1:] // MOE_BLOCK_CHUNKS
    as_i32 = lambda v: v.astype(jnp.int32)
    expert_tables = (as_i32(block_expert), as_i32(of_block(next_expert)), as_i32(of_block(wslot)),
                     as_i32(n_valid), as_i32(src))
    return expert_tables, as_i32(comb)


def _chunk_copy(src_hbm, src_chunk, dst, dst_chunk, sem):
    rows = lambda c: pl.ds(pl.multiple_of(c * MOE_CHUNK, MOE_CHUNK), MOE_CHUNK)
    return pltpu.make_async_copy(src_hbm.at[rows(src_chunk), :], dst.at[rows(dst_chunk), :], sem)


def _expert_kernel(be_ref, nxt_ref, ws_ref, nv_ref, src_ref, xs_hbm, wgu_hbm, wd_hbm, ys_ref, *scratch):
    for k in range(EXPERT_BLOCKS_PER_STEP):
        _expert_block(pl.program_id(0) * EXPERT_BLOCKS_PER_STEP + k,
                      be_ref, nxt_ref, ws_ref, nv_ref, src_ref, xs_hbm, wgu_hbm, wd_hbm,
                      ys_ref.at[pl.ds(k * MOE_TILE, MOE_TILE), :], *scratch)


def _expert_block(i, be_ref, nxt_ref, ws_ref, nv_ref, src_ref, xs_hbm, wgu_hbm, wd_hbm, ys_ref,
                  xbuf, wgu_f32, wd_f32, wgu_bf, wd_bf, sems, wsems):
    valid = i < nv_ref[0]

    def gather(blk, slot):
        return [_chunk_copy(xs_hbm, src_ref[blk * MOE_BLOCK_CHUNKS + c], xbuf.at[slot], c, sems.at[slot])
                for c in range(MOE_BLOCK_CHUNKS)]

    def fetch_weights(e, slot):
        return [pltpu.make_async_copy(wgu_hbm.at[e], wgu_f32.at[slot], wsems.at[slot]),
                pltpu.make_async_copy(wd_hbm.at[e], wd_f32.at[slot], wsems.at[slot])]

    @pl.when(i == 0)
    def _():
        for cp in gather(0, 0):
            cp.start()
        for cp in fetch_weights(be_ref[0], 0):
            cp.start(priority=WEIGHT_DMA_PRIORITY)

    @pl.when(i + 1 < nv_ref[0])
    def _():
        for cp in gather(i + 1, (i + 1) % 2):
            cp.start()

    new_expert = valid & ((i == 0) | (be_ref[i] != be_ref[jnp.maximum(i - 1, 0)]))

    @pl.when(new_expert)
    def _():
        ws = ws_ref[i]
        for cp in fetch_weights(be_ref[i], ws):
            cp.wait()
        wgu_bf[...] = wgu_f32[ws].astype(jnp.bfloat16)
        wd_bf[...] = wd_f32[ws].astype(jnp.bfloat16)

        @pl.when(nxt_ref[i] >= 0)
        def _():
            for cp in fetch_weights(nxt_ref[i], 1 - ws):
                cp.start(priority=WEIGHT_DMA_PRIORITY)

    @pl.when(valid)
    def _():
        slot = i % 2
        for cp in gather(i, slot):
            cp.wait()
        x_lo, x_hi = _unpack_bf16_pairs(xbuf[slot])
        half = D_MODEL // 2
        partial = []

        def up_proj(cols):
            return (jnp.dot(x_lo, wgu_bf[:half, cols], preferred_element_type=jnp.float32)
                    + jnp.dot(x_hi, wgu_bf[half:, cols], preferred_element_type=jnp.float32))

        def hidden_group(c):
            cols = slice(c * EXPERT_GROUP, (c + 1) * EXPERT_GROUP)
            g = up_proj(cols)
            u = up_proj(slice(D_EXPERT + cols.start, D_EXPERT + cols.stop))
            yield
            act = (g * jax.nn.sigmoid(g) * u).astype(jnp.bfloat16)
            yield
            partial.append(jnp.dot(act, wd_bf[cols, :], preferred_element_type=jnp.float32))

        waiting = [hidden_group(c) for c in range(D_EXPERT // EXPERT_GROUP)]
        live = []
        while waiting or live:
            if waiting:
                live.append(waiting.pop(0))
            for gen in list(live):
                if next(gen, _DONE) is _DONE:
                    live.remove(gen)
        y = sum(partial[1:], partial[0])
        ys_ref[...] = _pack_bf16_pairs(y.astype(jnp.bfloat16).astype(jnp.float32))

    @pl.when(jnp.logical_not(valid))
    def _():
        ys_ref[...] = jnp.zeros_like(ys_ref)


def _experts(tables, xs, w_gate_up, w_down):
    block_expert, block_next, block_wslot, n_valid, src = tables
    nblk = block_expert.shape[0]
    return pl.pallas_call(
        _expert_kernel,
        grid_spec=pltpu.PrefetchScalarGridSpec(
            num_scalar_prefetch=5,
            grid=(nblk // EXPERT_BLOCKS_PER_STEP,),
            in_specs=[pl.BlockSpec(memory_space=pl.ANY)] * 3,
            out_specs=pl.BlockSpec((EXPERT_BLOCKS_PER_STEP * MOE_TILE, D_MODEL // 2),
                                   lambda i, *_: (i, 0)),
            scratch_shapes=[
                pltpu.VMEM((2, MOE_TILE, D_MODEL // 2), jnp.uint32),
                pltpu.VMEM((2, D_MODEL, 2 * D_EXPERT), jnp.float32),
                pltpu.VMEM((2, D_EXPERT, D_MODEL), jnp.float32),
                pltpu.VMEM((D_MODEL, 2 * D_EXPERT), jnp.bfloat16),
                pltpu.VMEM((D_EXPERT, D_MODEL), jnp.bfloat16),
                pltpu.SemaphoreType.DMA((2,)),
                pltpu.SemaphoreType.DMA((2,)),
            ],
        ),
        out_shape=jax.ShapeDtypeStruct((nblk * MOE_TILE, D_MODEL // 2), jnp.uint32),
        compiler_params=pltpu.CompilerParams(
            dimension_semantics=("arbitrary",), vmem_limit_bytes=VMEM_LIMIT),
        name="moe_experts",
    )(block_expert, block_next, block_wslot, n_valid, src, xs, w_gate_up, w_down)


def _combine_kernel(comb_ref, ys_hbm, x1_ref, route_ref, nf_ref, out_ref, ybuf, sems):
    tc = MIX_TILE
    per_step = COMBINE_TILES_PER_STEP
    i = pl.program_id(0)
    last = pl.num_programs(0) - 1

    def gather(step, slot):
        first = step * per_step * TILE_CHUNKS
        return [_chunk_copy(ys_hbm, comb_ref[first + lc], ybuf.at[slot], lc, sems.at[slot])
                for lc in range(per_step * TILE_CHUNKS)]

    @pl.when(i == 0)
    def _():
        for cp in gather(0, 0):
            cp.start()

    nxt = jnp.minimum(i + 1, last)
    for cp in gather(nxt, (i + 1) % 2):
        cp.start()

    slot = i % 2
    for cp in gather(i, slot):
        cp.wait()

    def tile(t):
        rows = slice(t * tc, (t + 1) * tc)
        y_lo, y_hi = _unpack_bf16_pairs(ybuf[slot, t * MOE_SLOTS:(t + 1) * MOE_SLOTS, :])
        route = route_ref[rows, :]
        slot_id = lax.broadcasted_iota(jnp.int32, (tc, MOE_SLOTS), 1)
        picked = []
        for lane in (ROUTE_S0, ROUTE_S1):
            sel = (slot_id == route[:, lane:lane + 1].astype(jnp.int32)).astype(jnp.bfloat16)
            yield
            picked.append(jnp.concatenate([jnp.dot(sel, y_lo, preferred_element_type=jnp.float32),
                                           jnp.dot(sel, y_hi, preferred_element_type=jnp.float32)], axis=1))
            yield
        g0 = route[:, ROUTE_G0:ROUTE_G0 + 1]
        g1 = route[:, ROUTE_G1:ROUTE_G1 + 1]
        out_ref[rows, :] = _rmsnorm(x1_ref[rows, :] + g0 * picked[0] + g1 * picked[1], nf_ref[...])

    waiting = [tile(t) for t in range(per_step)]
    live = []
    while waiting or live:
        if waiting:
            live.append(waiting.pop(0))
        for gen in list(live):
            if next(gen, _DONE) is _DONE:
                live.remove(gen)

    @pl.when(i == last)
    def _():
        for cp in gather(nxt, (i + 1) % 2):
            cp.wait()


def _combine(comb, ys, x1, route, norm_f):
    T = x1.shape[0]
    tc = MIX_TILE * COMBINE_TILES_PER_STEP
    return pl.pallas_call(
        _combine_kernel,
        grid_spec=pltpu.PrefetchScalarGridSpec(
            num_scalar_prefetch=1,
            grid=(T // tc,),
            in_specs=[
                pl.BlockSpec(memory_space=pl.ANY),
                pl.BlockSpec((tc, D_MODEL), lambda i, c: (i, 0)),
                pl.BlockSpec((tc, LANES), lambda i, c: (i, 0)),
                pl.BlockSpec((1, D_MODEL), lambda i, c: (0, 0)),
            ],
            out_specs=pl.BlockSpec((tc, D_MODEL), lambda i, c: (i, 0)),
            scratch_shapes=[
                pltpu.VMEM((2, COMBINE_TILES_PER_STEP * MOE_SLOTS, D_MODEL // 2), jnp.uint32),
                pltpu.SemaphoreType.DMA((2,)),
            ],
        ),
        out_shape=jax.ShapeDtypeStruct((T, D_MODEL), jnp.float32),
        compiler_params=pltpu.CompilerParams(
            dimension_semantics=("arbitrary",), vmem_limit_bytes=VMEM_LIMIT),
        name="moe_combine",
    )(comb, ys, x1, route, norm_f)


def _block_diag_gates(w_rg, w_ig):
    def bd(w):
        w4 = w.reshape(N_LRU_CHUNKS, LRU_CHUNK_HEADS, LRU_HEAD_DIM, LRU_HEAD_DIM)
        eye = jnp.eye(LRU_CHUNK_HEADS, dtype=w.dtype)
        return jnp.einsum('chij,hk->chikj', w4, eye).reshape(N_LRU_CHUNKS, LRU_CHUNK, LRU_CHUNK)
    return jnp.concatenate([bd(w_rg), bd(w_ig)], axis=-1).astype(jnp.bfloat16)


def kernel(x, rel_bias, norm1, w_in, conv_w, conv_b, w_rg, b_rg, w_ig, b_ig, lru_lambda,
           w_proj_attn, w_proj_lru, w_out, norm2, w_router_group, w_router_expert,
           w_gate_up, w_down, norm_f):
    B, S, D = x.shape
    T = B * S
    assert w_in.shape[0] == 1, "single-layer block"
    layer = 0
    bf16 = jnp.bfloat16
    x2 = x.reshape(T, D)
    row = lambda v: v[layer].reshape(1, -1)
    w_qkv = (w_in[layer][:, :QKV_WIDTH].reshape(D, 3, N_GROUPS, GROUP_WIDTH)
             .transpose(0, 2, 1, 3).reshape(D, QKV_WIDTH).astype(bf16))
    w_rest = w_in[layer][:, QKV_WIDTH:].astype(bf16)

    qkv_groups = _qkv_projection(x2, row(norm1), w_qkv, B, S)
    attn_o, attn_lse = [], []
    for gi, (window, dilation) in enumerate(DILATED_GROUPS):
        hs = slice(gi * HEADS_PER_GROUP, (gi + 1) * HEADS_PER_GROUP)
        bias = _attention_bias_table(rel_bias[:, hs], window, dilation)
        o, lse = _attention_group(qkv_groups[gi], bias, gi, dilation)
        attn_o.append(o)
        attn_lse.append(lse)

    w_router = jnp.zeros((D, LANES), jnp.float32)
    w_router = w_router.at[:, :N_EXPERTS].set(w_router_expert[layer].astype(jnp.float32))
    w_router = w_router.at[:, GROUP_LOGIT_LANE0:GROUP_LOGIT_LANE0 + N_EXPERT_GROUPS].set(
        w_router_group[layer].astype(jnp.float32))
    w_router_hi = w_router.astype(bf16)
    weights = [
        row(norm1), w_rest, conv_w[layer], row(conv_b),
        _block_diag_gates(w_rg[layer], w_ig[layer]), row(b_rg), row(b_ig), row(lru_lambda),
        w_proj_attn[layer].astype(bf16), w_proj_lru[layer].astype(bf16), w_out[layer].astype(bf16),
        row(norm2), w_router_hi, (w_router - w_router_hi.astype(jnp.float32)).astype(bf16),
    ]
    x1, xs, route, counts = _token_mixer(x, attn_o, attn_lse, weights)

    expert_tables, comb = _moe_tables(counts[:, 0, :N_EXPERTS].astype(jnp.int32))
    ys = _experts(expert_tables, xs, w_gate_up[layer], w_down[layer])
    out = _combine(comb, ys, x1, route, norm_f.reshape(1, -1))
    return out.reshape(B, S, D)
```

```python
import functools
import math

import jax
import jax.numpy as jnp
from jax import lax
from jax.experimental import pallas as pl
from jax.experimental.pallas import tpu as pltpu

D_MODEL = 1024
HEAD_DIM = 64
HEADS_PER_GROUP = 4
DILATED_GROUPS = ((128, 1), (512, 4), (2048, 16))
N_GROUPS = len(DILATED_GROUPS)
N_ATTN_HEADS = HEADS_PER_GROUP * N_GROUPS
ATTN_WIDTH = N_ATTN_HEADS * HEAD_DIM
GROUP_WIDTH = HEADS_PER_GROUP * HEAD_DIM
QKV_WIDTH = 3 * ATTN_WIDTH
ATTN_BLOCK = 128
N_REL_BUCKETS = 32
REL_MAX_DISTANCE = 2048
NEG_INF = -1e30

LRU_WIDTH = D_MODEL
LRU_HEADS = 16
LRU_HEAD_DIM = LRU_WIDTH // LRU_HEADS
CONV_WIDTH = 4
LRU_C = 8.0
LRU_CHUNK_HEADS = 4
LRU_CHUNK = LRU_CHUNK_HEADS * LRU_HEAD_DIM
N_LRU_CHUNKS = LRU_WIDTH // LRU_CHUNK

N_EXPERT_GROUPS = 4
EXPERTS_PER_GROUP = 8
N_EXPERTS = N_EXPERT_GROUPS * EXPERTS_PER_GROUP
TOP_K = 2
D_EXPERT = 512
EPS = 1e-6

REST_WIDTH = 2 * LRU_WIDTH + 2 * D_MODEL
LANES = 128
SUBLANES = 8

ATTN_TILE = 256
ATTN_TILES_PER_STEP = 2
ATTN_QB = 8
MIX_TILE = 256
MOE_TILE = 256
EXPERT_GATHER_SLOTS = 4
EXPERT_BLOCKS_PER_STEP = 4
COMBINE_TILES_PER_STEP = 2
WEIGHT_DMA_PRIORITY = 1
EXPERT_GROUP = 256
MOE_CHUNK = SUBLANES
MOE_BLOCK_CHUNKS = MOE_TILE // MOE_CHUNK
MOE_SLOTS = 768
assert MOE_SLOTS >= TOP_K * MIX_TILE + N_EXPERTS * (MOE_CHUNK - 1) and MOE_SLOTS % MOE_CHUNK == 0
TILE_CHUNKS = MOE_SLOTS // MOE_CHUNK
VMEM_LIMIT = 56 * 1024 * 1024

ROUTE_S0, ROUTE_S1, ROUTE_G0, ROUTE_G1 = range(4)
GROUP_LOGIT_LANE0 = N_EXPERTS


_DONE = object()


def _rmsnorm(x, g):
    return x * lax.rsqrt(jnp.mean(x * x, axis=-1, keepdims=True) + EPS) * g


def _pack_bf16_pairs(v):
    w = v.shape[1] // 2
    bits = lax.bitcast_convert_type(v, jnp.uint32)
    return bits[:, w:] | (bits[:, :w] >> 16)


def _unpack_bf16_pairs(p):
    lo = lax.bitcast_convert_type(p << 16, jnp.float32)
    hi = lax.bitcast_convert_type(p & jnp.uint32(0xFFFF0000), jnp.float32)
    return lo.astype(jnp.bfloat16), hi.astype(jnp.bfloat16)


def _residue_major_perm(n, d):
    per = n // d
    out = lax.broadcasted_iota(jnp.int32, (n, n), 0)
    src = lax.broadcasted_iota(jnp.int32, (n, n), 1)
    m = jnp.bitwise_and(out, per - 1)
    r = jnp.right_shift(out, per.bit_length() - 1)
    return (src == m * d + r).astype(jnp.bfloat16)


def _qkv_kernel(x_ref, g_ref, w_ref, *o_refs):
    tm = ATTN_TILE

    def tile(t):
        rows = slice(t * tm, (t + 1) * tm)
        h = _rmsnorm(x_ref[rows, :], g_ref[...]).astype(jnp.bfloat16)
        yield
        for gi, (o_ref, (_, d)) in enumerate(zip(o_refs, DILATED_GROUPS)):
            cols = slice(gi * ATTN_WIDTH, (gi + 1) * ATTN_WIDTH)
            part = jnp.dot(h, w_ref[:, cols], preferred_element_type=jnp.float32).astype(jnp.bfloat16)
            yield
            if d > 1:
                part = jnp.dot(_residue_major_perm(tm, d), part,
                               preferred_element_type=jnp.float32).astype(jnp.bfloat16)
            o_ref[:, t * (tm // d):(t + 1) * (tm // d), :] = part.reshape(d, tm // d, ATTN_WIDTH)
            yield

    waiting = [tile(t) for t in range(ATTN_TILES_PER_STEP)]
    live = []
    while waiting or live:
        if waiting:
            live.append(waiting.pop(0))
        for gen in list(live):
            if next(gen, _DONE) is _DONE:
                live.remove(gen)


def _qkv_projection(x2, norm1, w_qkv, B, S):
    tm = ATTN_TILE * ATTN_TILES_PER_STEP
    nt = S // tm
    return pl.pallas_call(
        _qkv_kernel,
        grid=(B, nt),
        in_specs=[
            pl.BlockSpec((tm, D_MODEL), lambda b, j: (b * nt + j, 0)),
            pl.BlockSpec((1, D_MODEL), lambda b, j: (0, 0)),
            pl.BlockSpec((D_MODEL, QKV_WIDTH), lambda b, j: (0, 0)),
        ],
        out_specs=[pl.BlockSpec((None, d, tm // d, ATTN_WIDTH), lambda b, j: (b, 0, j, 0))
                   for _, d in DILATED_GROUPS],
        out_shape=[jax.ShapeDtypeStruct((B, d, S // d, ATTN_WIDTH), jnp.bfloat16)
                   for _, d in DILATED_GROUPS],
        compiler_params=pltpu.CompilerParams(
            dimension_semantics=("parallel", "parallel"), vmem_limit_bytes=VMEM_LIMIT),
        name="qkv_projection",
    )(x2, norm1, w_qkv)


def _attn_kernel(n_qb, q_ref, kp_ref, kc_ref, vp_ref, vc_ref, bias0_ref, bias_ref, o_ref, lse_ref):
    q = q_ref[...] * (HEAD_DIM ** -0.5)
    k = jnp.concatenate([kp_ref[...], kc_ref[...]], axis=0)
    v = jnp.concatenate([vp_ref[...], vc_ref[...]], axis=0)
    pair_w = 2 * HEAD_DIM
    first_head = lax.broadcasted_iota(jnp.int32, (1, pair_w), 1) < HEAD_DIM
    def one_head(x, first):
        bits = pltpu.bitcast(x, jnp.uint32)
        keep = first_head if first else jnp.logical_not(first_head)
        return pltpu.bitcast(jnp.where(keep, bits, jnp.uint32(0)), jnp.bfloat16)

    ones = jnp.ones_like(k[:, :pair_w])
    operands = []
    for pair in range(HEADS_PER_GROUP // 2):
        cols = slice(pair * pair_w, (pair + 1) * pair_w)
        k2, v2 = k[:, cols], v[:, cols]
        k_heads = (one_head(k2, True), one_head(k2, False))
        v_heads = (jnp.concatenate([one_head(v2, True), one_head(ones, True)], axis=1),
                   jnp.concatenate([one_head(v2, False), one_head(ones, False)], axis=1))
        operands.append((cols, k_heads, v_heads))
    done = {}

    def head_pair(b, pair):
        rows = slice(b * ATTN_BLOCK, (b + 1) * ATTN_BLOCK)
        keys = slice(b * ATTN_BLOCK, (b + 2) * ATTN_BLOCK)
        cols, k_heads, v_heads = operands[pair]
        bias_ref_b = bias0_ref if b == 0 else bias_ref
        bias = jnp.concatenate([bias_ref_b[2 * pair], bias_ref_b[2 * pair + 1]], axis=1)
        k_both = jnp.concatenate([k_heads[0][keys], k_heads[1][keys]], axis=0)
        s = lax.dot_general(q[rows, cols], k_both, (((1,), (1,)), ((), ())),
                            preferred_element_type=jnp.float32) + bias
        yield
        n_keys = 2 * ATTN_BLOCK
        s0, s1 = s[:, :n_keys], s[:, n_keys:]
        m0 = jnp.max(s0, axis=-1, keepdims=True)
        m1 = jnp.max(s1, axis=-1, keepdims=True)
        p0 = jnp.exp(s0 - m0).astype(jnp.bfloat16)
        p1 = jnp.exp(s1 - m1).astype(jnp.bfloat16)
        yield
        acc = (jnp.dot(p0, v_heads[0][keys], preferred_element_type=jnp.float32)
               + jnp.dot(p1, v_heads[1][keys], preferred_element_type=jnp.float32))
        yield
        denom = acc[:, pair_w:]
        done[b, pair] = (acc[:, :pair_w] / denom, jnp.where(first_head, m0, m1) + jnp.log(denom))

    waiting = [head_pair(b, pair) for b in range(n_qb) for pair in range(HEADS_PER_GROUP // 2)]
    live = []
    while waiting or live:
        if waiting:
            live.append(waiting.pop(0))
        for gen in list(live):
            if next(gen, _DONE) is _DONE:
                live.remove(gen)

    for b in range(n_qb):
        rows = slice(b * ATTN_BLOCK, (b + 1) * ATTN_BLOCK)
        pairs = [done[b, pair] for pair in range(HEADS_PER_GROUP // 2)]
        o_ref[rows, :] = jnp.concatenate([o for o, _ in pairs], axis=1).astype(o_ref.dtype)
        lse_ref[rows, :] = jnp.concatenate([lse for _, lse in pairs], axis=1)


def _attention_group(qkv_g, bias, gi, dilation):
    B, _, L, _ = qkv_g.shape
    n_qb = min(ATTN_QB, L // ATTN_BLOCK)
    rows = n_qb * ATTN_BLOCK
    cur = lambda which: pl.BlockSpec((None, None, rows, GROUP_WIDTH), lambda b, r, n: (b, r, n, which))
    prev = lambda which: pl.BlockSpec((None, None, ATTN_BLOCK, GROUP_WIDTH),
                                      lambda b, r, n: (b, r, jnp.maximum(n * n_qb - 1, 0), which))
    bias_blk = (None, HEADS_PER_GROUP, ATTN_BLOCK, 2 * ATTN_BLOCK)
    return pl.pallas_call(
        functools.partial(_attn_kernel, n_qb),
        grid=(B, dilation, L // rows),
        in_specs=[
            cur(0), prev(1), cur(1), prev(2), cur(2),
            pl.BlockSpec(bias_blk, lambda b, r, n: (jnp.minimum(n, 1), 0, 0, 0)),
            pl.BlockSpec(bias_blk, lambda b, r, n: (1, 0, 0, 0)),
        ],
        out_specs=[cur(0), cur(0)],
        out_shape=[
            jax.ShapeDtypeStruct((B, dilation, L, GROUP_WIDTH), jnp.bfloat16),
            jax.ShapeDtypeStruct((B, dilation, L, GROUP_WIDTH), jnp.float32),
        ],
        compiler_params=pltpu.CompilerParams(
            dimension_semantics=("parallel", "parallel", "arbitrary"), vmem_limit_bytes=VMEM_LIMIT),
        name=f"dilated_attention_g{gi}",
    )(qkv_g, qkv_g, qkv_g, qkv_g, qkv_g, bias, bias)


def _t5_causal_bucket(dist):
    max_exact = N_REL_BUCKETS // 2
    d_f = jnp.maximum(dist, max_exact).astype(jnp.float32)
    large = max_exact + (jnp.log(d_f / max_exact) / math.log(REL_MAX_DISTANCE / max_exact)
                         * (N_REL_BUCKETS - max_exact)).astype(jnp.int32)
    large = jnp.minimum(large, N_REL_BUCKETS - 1)
    return jnp.where(dist < max_exact, dist, large)


def _attention_bias_table(rel_bias_g, window, dilation):
    nw = window // dilation
    qi = jnp.arange(ATTN_BLOCK)[:, None]
    ki = jnp.arange(2 * ATTN_BLOCK)[None, :]
    dist = ATTN_BLOCK + qi - ki
    band = (dist >= 0) & (dist <= nw)
    bucket = _t5_causal_bucket(jnp.maximum(dist, 0) * dilation)
    onehot = (bucket[:, :, None] == jnp.arange(N_REL_BUCKETS)).astype(jnp.float32)
    bias = jnp.einsum('qkb,bh->hqk', onehot, rel_bias_g.astype(jnp.float32),
                      precision=lax.Precision.HIGHEST)
    later = jnp.where(band[None], bias, NEG_INF)
    first = jnp.where((band & (ki >= ATTN_BLOCK))[None], bias, NEG_INF)
    return jnp.stack([first, later])


def _gelu_tanh(x):
    return 0.5 * x * (1.0 + jnp.tanh(math.sqrt(2.0 / math.pi) * (x + 0.044715 * (x * x * x))))


def _softplus(x):
    return jnp.maximum(x, 0.0) + jnp.log(1.0 + jnp.exp(-jnp.abs(x)))


def _natural_order(blk_ref, d, slabs):
    if d == 1:
        return blk_ref[0].astype(jnp.float32)
    per = MIX_TILE // d
    for r in range(d):
        rows = blk_ref[r].astype(jnp.float32)
        for c in range(GROUP_WIDTH // LANES):
            slabs[c, pl.ds(r, per, stride=d), :] = rows[:, c * LANES:(c + 1) * LANES]
    return jnp.concatenate([slabs[c] for c in range(GROUP_WIDTH // LANES)], axis=1)


def _mix_tile(x_ref, o1_ref, o2_ref, o3_ref, l1_ref, l2_ref, l3_ref,
              n1_ref, win_ref, cw_ref, cb_ref, wg_ref, brg_ref, big_ref, lam_ref,
              wpa_ref, wpl_ref, wout_ref, n2_ref, wr_ref,
              x1_ref, xs_ref, route_ref, cnt_ref,
              xbuf, a_sc, b_sc, h_sc, *slabs):
    tm = MIX_TILE
    first_tile = pl.program_id(0) == 0

    @pl.when(first_tile)
    def _():
        xbuf[0:SUBLANES, :] = jnp.zeros((SUBLANES, LRU_WIDTH), jnp.float32)
        h_sc[...] = jnp.zeros_like(h_sc)

    x = x_ref[...]
    h = _rmsnorm(x, n1_ref[...]).astype(jnp.bfloat16)
    def project(first_col):
        return jnp.dot(h, win_ref[:, first_col:first_col + D_MODEL], preferred_element_type=jnp.float32)

    xr = project(0)

    yield
    g_lru = project(LRU_WIDTH)
    xbuf[SUBLANES:SUBLANES + tm, :] = xr
    xc = xr * cw_ref[CONV_WIDTH - 1:CONV_WIDTH, :] + cb_ref[...]
    for j in range(CONV_WIDTH - 1):
        back = CONV_WIDTH - 1 - j
        xc = xc + xbuf[SUBLANES - back:SUBLANES - back + tm, :] * cw_ref[j:j + 1, :]
    xbuf[0:SUBLANES, :] = xbuf[tm:tm + SUBLANES, :]

    xcb = xc.astype(jnp.bfloat16)
    r_parts, i_parts = [], []
    for c in range(N_LRU_CHUNKS):
        ri = jnp.dot(xcb[:, c * LRU_CHUNK:(c + 1) * LRU_CHUNK], wg_ref[c],
                     preferred_element_type=jnp.float32)
        r_parts.append(ri[:, :LRU_CHUNK])
        i_parts.append(ri[:, LRU_CHUNK:])
    yield
    g_a = project(2 * LRU_WIDTH)
    r = jax.nn.sigmoid(jnp.concatenate(r_parts, axis=1) + brg_ref[...])
    ig = jax.nn.sigmoid(jnp.concatenate(i_parts, axis=1) + big_ref[...])
    log_a = (-LRU_C * _softplus(-lam_ref[...])) * r
    a = jnp.exp(log_a)
    gap = 1.0 - a * a
    root = jnp.where(gap > 0.0, gap * lax.rsqrt(gap), 0.0)
    bb = root * (ig * xc)

    yield
    g_b = project(2 * LRU_WIDTH + D_MODEL)
    row8 = lax.broadcasted_iota(jnp.int32, (tm, 1), 0) % SUBLANES
    for k in (1, 2, 4):
        a_s = pltpu.roll(a, k, axis=0)
        b_s = pltpu.roll(bb, k, axis=0)
        take = row8 >= k
        bb = jnp.where(take, a * b_s + bb, bb)
        a = jnp.where(take, a * a_s, a)
    a_sc[...] = a
    b_sc[...] = bb

    yield
    carry = h_sc[...]
    for g in range(tm // SUBLANES):
        rows = slice(g * SUBLANES, (g + 1) * SUBLANES)
        hh = a_sc[rows, :] * carry + b_sc[rows, :]
        b_sc[rows, :] = hh
        carry = hh[SUBLANES - 1:SUBLANES, :]
    h_sc[...] = carry
    yield
    lru = (b_sc[...] * _gelu_tanh(g_lru)).astype(jnp.bfloat16)
    p_lru = jnp.dot(lru, wpl_ref[...], preferred_element_type=jnp.float32)

    yield
    dil = [d for _, d in DILATED_GROUPS]
    o1, o2, o3 = (_natural_order(ref, d, slabs[2 * g]) for g, (ref, d) in
                  enumerate(zip((o1_ref, o2_ref, o3_ref), dil)))
    l1, l2, l3 = (_natural_order(ref, d, slabs[2 * g + 1]) for g, (ref, d) in
                  enumerate(zip((l1_ref, l2_ref, l3_ref), dil)))
    lm = jnp.maximum(jnp.maximum(l1, l2), l3)
    e1, e2, e3 = jnp.exp(l1 - lm), jnp.exp(l2 - lm), jnp.exp(l3 - lm)
    attn = (e1 * o1 + e2 * o2 + e3 * o3) / (e1 + e2 + e3)
    p_attn = jnp.dot(attn.astype(jnp.bfloat16), wpa_ref[...], preferred_element_type=jnp.float32)

    yield
    merged = jax.nn.sigmoid(g_a) * p_attn + jax.nn.sigmoid(g_b) * p_lru
    x1 = x + jnp.dot(merged.astype(jnp.bfloat16), wout_ref[...], preferred_element_type=jnp.float32)
    x1_ref[...] = x1
    h2 = _rmsnorm(x1, n2_ref[...])

    yield
    h2_hi = h2.astype(jnp.bfloat16)
    h2_lo = (h2 - h2_hi.astype(jnp.float32)).astype(jnp.bfloat16)
    by_hi = jnp.dot(h2_hi, wr_ref[...], preferred_element_type=jnp.float32)
    by_lo = jnp.dot(h2_lo, wr_ref[...], preferred_element_type=jnp.float32)
    logits = (by_hi[:, :LANES] + (by_hi[:, LANES:] + by_lo[:, :LANES])) + by_lo[:, LANES:]
    lane = lax.broadcasted_iota(jnp.int32, (tm, LANES), 1)
    big = jnp.int32(LANES)
    lowest = jnp.float32(-3.0e38)
    is_g = (lane >= GROUP_LOGIT_LANE0) & (lane < GROUP_LOGIT_LANE0 + N_EXPERT_GROUPS)
    gl = jnp.where(is_g, logits, lowest)
    gmax = jnp.max(gl, axis=-1, keepdims=True)
    gsel = jnp.min(jnp.where(gl == gmax, lane, big), axis=-1, keepdims=True) - GROUP_LOGIT_LANE0
    p_sel = 1.0 / jnp.sum(jnp.where(is_g, jnp.exp(gl - gmax), 0.0), axis=-1, keepdims=True)
    is_e = (lane >= gsel * EXPERTS_PER_GROUP) & (lane < (gsel + 1) * EXPERTS_PER_GROUP)
    el = jnp.where(is_e, logits, lowest)
    v1 = jnp.max(el, axis=-1, keepdims=True)
    i1 = jnp.min(jnp.where(el == v1, lane, big), axis=-1, keepdims=True)
    el2 = jnp.where(lane == i1, lowest, el)
    v2 = jnp.max(el2, axis=-1, keepdims=True)
    i2 = jnp.min(jnp.where(el2 == v2, lane, big), axis=-1, keepdims=True)
    t = jnp.exp(v2 - v1)
    g0 = p_sel / (1.0 + t)
    g1 = p_sel * t / (1.0 + t)

    yield
    oh0 = lane == i1
    oh1 = lane == i2
    both = (oh0 | oh1).astype(jnp.float32)
    tri = (lax.broadcasted_iota(jnp.int32, (tm, tm), 0)
           > lax.broadcasted_iota(jnp.int32, (tm, tm), 1)).astype(jnp.bfloat16)
    before = jnp.dot(tri, both.astype(jnp.bfloat16), preferred_element_type=jnp.float32)
    cnt = jnp.sum(both, axis=0, keepdims=True)
    padded = jnp.floor((cnt + (MOE_CHUNK - 1.0)) * (1.0 / MOE_CHUNK)) * MOE_CHUNK
    upper = (lax.broadcasted_iota(jnp.int32, (LANES, LANES), 0)
             < lax.broadcasted_iota(jnp.int32, (LANES, LANES), 1)).astype(jnp.bfloat16)
    loff = jnp.dot(jnp.broadcast_to(padded, (SUBLANES, LANES)).astype(jnp.bfloat16), upper,
                   preferred_element_type=jnp.float32)[0:1, :]
    base = before + loff
    slot0 = jnp.sum(jnp.where(oh0, base, 0.0), axis=-1, keepdims=True)
    slot1 = jnp.sum(jnp.where(oh1, base, 0.0), axis=-1, keepdims=True)
    cnt_ref[...] = jnp.broadcast_to(cnt, cnt_ref.shape)

    slot_id = lax.broadcasted_iota(jnp.int32, (tm, MOE_SLOTS), 1)
    place = ((slot_id == slot0.astype(jnp.int32)) | (slot_id == slot1.astype(jnp.int32)))
    xs_ref[...] = _pack_bf16_pairs(
        lax.dot_general(place.astype(jnp.bfloat16), h2_hi, (((0,), (0,)), ((), ())),
                        preferred_element_type=jnp.float32))

    route = jnp.zeros((tm, LANES), jnp.float32)
    for lane_id, val in ((ROUTE_S0, slot0), (ROUTE_S1, slot1), (ROUTE_G0, g0), (ROUTE_G1, g1)):
        route = jnp.where(lane == lane_id, val, route)
    route_ref[...] = route


N_MIXER_TILED_INPUTS = 1 + 2 * N_GROUPS
N_MIXER_OUTPUTS = 4


def _mixer_kernel(*refs):
    n_batch = refs[0].shape[0]
    n_weights = len(refs) - N_MIXER_TILED_INPUTS - N_MIXER_OUTPUTS - n_batch * _MIXER_SCRATCH_PER_SEQ
    tiled = refs[:N_MIXER_TILED_INPUTS]
    weights = refs[N_MIXER_TILED_INPUTS:N_MIXER_TILED_INPUTS + n_weights]
    outs = refs[N_MIXER_TILED_INPUTS + n_weights:N_MIXER_TILED_INPUTS + n_weights + N_MIXER_OUTPUTS]
    scratch = refs[N_MIXER_TILED_INPUTS + n_weights + N_MIXER_OUTPUTS:]
    stages = []
    for b in range(n_batch):
        own = scratch[b * _MIXER_SCRATCH_PER_SEQ:(b + 1) * _MIXER_SCRATCH_PER_SEQ]
        stages.append(_mix_tile(*[r.at[b] for r in tiled], *weights, *[r.at[b] for r in outs], *own))
    while stages:
        for gen in list(stages):
            if next(gen, _DONE) is _DONE:
                stages.remove(gen)


def _mixer_scratch(tm):
    return [
        pltpu.VMEM((tm + 2 * SUBLANES, LRU_WIDTH), jnp.float32),
        pltpu.VMEM((tm, LRU_WIDTH), jnp.float32),
        pltpu.VMEM((tm, LRU_WIDTH), jnp.float32),
        pltpu.VMEM((1, LRU_WIDTH), jnp.float32),
    ] + [pltpu.VMEM((GROUP_WIDTH // LANES, tm, LANES), jnp.float32)] * (2 * N_GROUPS)


_MIXER_SCRATCH_PER_SEQ = len(_mixer_scratch(MIX_TILE))


def _token_mixer(x, attn_o, attn_lse, weights):
    B, S, _ = x.shape
    tm = MIX_TILE
    nt = S // tm
    rows = lambda j: (0, j, 0)

    def resident(shape):
        return pl.BlockSpec(shape, lambda j: (0,) * len(shape), pipeline_mode=pl.Buffered(1))

    in_specs = [pl.BlockSpec((B, tm, D_MODEL), rows)]
    group_specs = [pl.BlockSpec((B, d, tm // d, GROUP_WIDTH), lambda j: (0, 0, j, 0))
                   for _, d in DILATED_GROUPS]
    in_specs += group_specs + group_specs
    in_specs += [resident(w.shape) for w in weights]
    x1, xs, route, counts = pl.pallas_call(
        _mixer_kernel,
        grid=(nt,),
        in_specs=in_specs,
        out_specs=[
            pl.BlockSpec((B, tm, D_MODEL), rows),
            pl.BlockSpec((B, MOE_SLOTS, D_MODEL // 2), rows),
            pl.BlockSpec((B, tm, LANES), rows),
            pl.BlockSpec((B, None, SUBLANES, LANES), lambda j: (0, j, 0, 0)),
        ],
        out_shape=[
            jax.ShapeDtypeStruct((B, S, D_MODEL), jnp.float32),
            jax.ShapeDtypeStruct((B, nt * MOE_SLOTS, D_MODEL // 2), jnp.uint32),
            jax.ShapeDtypeStruct((B, S, LANES), jnp.float32),
            jax.ShapeDtypeStruct((B, nt, SUBLANES, LANES), jnp.float32),
        ],
        scratch_shapes=_mixer_scratch(tm) * B,
        compiler_params=pltpu.CompilerParams(
            dimension_semantics=("arbitrary",), vmem_limit_bytes=VMEM_LIMIT),
        name="token_mixer",
    )(x, *attn_o, *attn_lse, *weights)
    return (x1.reshape(B * S, D_MODEL), xs.reshape(B * nt * MOE_SLOTS, D_MODEL // 2),
            route.reshape(B * S, LANES), counts.reshape(B * nt, SUBLANES, LANES))


def _moe_tables(cnt_tile):
    ntiles = cnt_tile.shape[0]
    nch = (cnt_tile + MOE_CHUNK - 1) // MOE_CHUNK
    lo_c = jnp.cumsum(nch, axis=1) - nch
    per_expert = jnp.sum(nch, axis=0)
    region = ((per_expert + MOE_BLOCK_CHUNKS - 1) // MOE_BLOCK_CHUNKS) * MOE_BLOCK_CHUNKS
    pend = jnp.cumsum(region)
    glob = (pend - region)[None, :] + jnp.cumsum(nch, axis=0) - nch

    max_rows = TOP_K * ntiles * MIX_TILE + ntiles * N_EXPERTS * (MOE_CHUNK - 1)
    max_chunks = -(-max_rows // MOE_CHUNK) + N_EXPERTS * (MOE_BLOCK_CHUNKS - 1)
    max_blocks = -(-max_chunks // MOE_BLOCK_CHUNKS)
    max_blocks = -(-max_blocks // EXPERT_BLOCKS_PER_STEP) * EXPERT_BLOCKS_PER_STEP
    max_chunks = max_blocks * MOE_BLOCK_CHUNKS

    seg_start = glob.T.reshape(-1)
    seg_src = (jnp.arange(ntiles, dtype=jnp.int32)[:, None] * TILE_CHUNKS + lo_c).T.reshape(-1)
    step = jnp.diff(seg_src - seg_start, prepend=0)
    g = jnp.arange(max_chunks, dtype=jnp.int32)
    src = g + jnp.sum(jnp.where(seg_start[None, :] <= g[:, None], step[None, :], 0), axis=1)
    src = jnp.clip(src, 0, ntiles * TILE_CHUNKS - 1)

    lstep = jnp.diff(glob - lo_c, axis=1, prepend=0)
    lc = jnp.arange(TILE_CHUNKS, dtype=jnp.int32)
    comb = lc[None, :] + jnp.sum(
        jnp.where(lo_c[:, None, :] <= lc[None, :, None], lstep[:, None, :], 0), axis=2)
    comb = jnp.clip(comb, 0, max_chunks - 1).reshape(-1)

    eidx = jnp.arange(N_EXPERTS, dtype=jnp.int32)
    nonempty = region > 0
    later = jnp.where(nonempty[None, :] & (eidx[None, :] > eidx[:, None]), eidx[None, :], N_EXPERTS)
    next_expert = jnp.min(later, axis=1)
    next_expert = jnp.where(next_expert == N_EXPERTS, -1, next_expert)
    wslot = (jnp.cumsum(nonempty.astype(jnp.int32)) - nonempty.astype(jnp.int32)) % 2
    last_expert = jnp.max(jnp.where(nonempty, eidx, 0))
    blk0 = jnp.arange(max_blocks, dtype=jnp.int32) * MOE_BLOCK_CHUNKS
    block_expert = jnp.minimum(jnp.sum((pend[None, :] <= blk0[:, None]).astype(jnp.int32), axis=1),
                               last_expert)
    of_block = lambda per_expert: jnp.sum(
        jnp.where(block_expert[:, None] == eidx[None, :], per_expert[None, :], 0), axis=1)
    n_valid = pend[-1:] // MOE_BLOCK_CHUNKS
    as_i32 = lambda v: v.astype(jnp.int32)
    expert_tables = (as_i32(block_expert), as_i32(of_block(next_expert)), as_i32(of_block(wslot)),
                     as_i32(n_valid), as_i32(src))
    return expert_tables, as_i32(comb)


def _chunk_copy(src_hbm, src_chunk, dst, dst_chunk, sem):
    rows = lambda c: pl.ds(pl.multiple_of(c * MOE_CHUNK, MOE_CHUNK), MOE_CHUNK)
    return pltpu.make_async_copy(src_hbm.at[rows(src_chunk), :], dst.at[rows(dst_chunk), :], sem)


def _expert_kernel(be_ref, nxt_ref, ws_ref, nv_ref, src_ref, xs_hbm, wgu_hbm, wd_hbm, ys_ref, *scratch):
    for k in range(EXPERT_BLOCKS_PER_STEP):
        _expert_block(pl.program_id(0) * EXPERT_BLOCKS_PER_STEP + k,
                      be_ref, nxt_ref, ws_ref, nv_ref, src_ref, xs_hbm, wgu_hbm, wd_hbm,
                      ys_ref.at[pl.ds(k * MOE_TILE, MOE_TILE), :], *scratch)


def _expert_block(i, be_ref, nxt_ref, ws_ref, nv_ref, src_ref, xs_hbm, wgu_hbm, wd_hbm, ys_ref,
                  xbuf, wgu_f32, wd_f32, wgu_bf, wd_bf, sems, wsems):
    valid = i < nv_ref[0]

    def gather(blk, slot):
        return [_chunk_copy(xs_hbm, src_ref[blk * MOE_BLOCK_CHUNKS + c], xbuf.at[slot], c, sems.at[slot])
                for c in range(MOE_BLOCK_CHUNKS)]

    def fetch_weights(e, slot):
        return [pltpu.make_async_copy(wgu_hbm.at[e], wgu_f32.at[slot], wsems.at[slot]),
                pltpu.make_async_copy(wd_hbm.at[e], wd_f32.at[slot], wsems.at[slot])]

    ahead = EXPERT_GATHER_SLOTS - 1

    @pl.when(i == 0)
    def _():
        for cp in fetch_weights(be_ref[0], 0):
            cp.start(priority=WEIGHT_DMA_PRIORITY)
        for first in range(ahead):
            @pl.when(first < nv_ref[0])
            def _():
                for cp in gather(first, first):
                    cp.start()

    @pl.when(i + ahead < nv_ref[0])
    def _():
        for cp in gather(i + ahead, (i + ahead) % EXPERT_GATHER_SLOTS):
            cp.start()

    new_expert = valid & ((i == 0) | (be_ref[i] != be_ref[jnp.maximum(i - 1, 0)]))

    @pl.when(new_expert)
    def _():
        ws = ws_ref[i]
        for cp in fetch_weights(be_ref[i], ws):
            cp.wait()
        wgu_bf[...] = wgu_f32[ws].astype(jnp.bfloat16)
        wd_bf[...] = wd_f32[ws].astype(jnp.bfloat16)

        @pl.when(nxt_ref[i] >= 0)
        def _():
            for cp in fetch_weights(nxt_ref[i], 1 - ws):
                cp.start(priority=WEIGHT_DMA_PRIORITY)

    @pl.when(valid)
    def _():
        slot = i % EXPERT_GATHER_SLOTS
        for cp in gather(i, slot):
            cp.wait()
        x_lo, x_hi = _unpack_bf16_pairs(xbuf[slot])
        half = D_MODEL // 2
        partial = []

        def up_proj(cols):
            return (jnp.dot(x_lo, wgu_bf[:half, cols], preferred_element_type=jnp.float32)
                    + jnp.dot(x_hi, wgu_bf[half:, cols], preferred_element_type=jnp.float32))

        def hidden_group(c):
            cols = slice(c * EXPERT_GROUP, (c + 1) * EXPERT_GROUP)
            g = up_proj(cols)
            u = up_proj(slice(D_EXPERT + cols.start, D_EXPERT + cols.stop))
            yield
            act = (g * jax.nn.sigmoid(g) * u).astype(jnp.bfloat16)
            yield
            partial.append(jnp.dot(act, wd_bf[cols, :], preferred_element_type=jnp.float32))

        waiting = [hidden_group(c) for c in range(D_EXPERT // EXPERT_GROUP)]
        live = []
        while waiting or live:
            if waiting:
                live.append(waiting.pop(0))
            for gen in list(live):
                if next(gen, _DONE) is _DONE:
                    live.remove(gen)
        y = sum(partial[1:], partial[0])
        ys_ref[...] = _pack_bf16_pairs(y.astype(jnp.bfloat16).astype(jnp.float32))

    @pl.when(jnp.logical_not(valid))
    def _():
        ys_ref[...] = jnp.zeros_like(ys_ref)


def _experts(tables, xs, w_gate_up, w_down):
    block_expert, block_next, block_wslot, n_valid, src = tables
    nblk = block_expert.shape[0]
    return pl.pallas_call(
        _expert_kernel,
        grid_spec=pltpu.PrefetchScalarGridSpec(
            num_scalar_prefetch=5,
            grid=(nblk // EXPERT_BLOCKS_PER_STEP,),
            in_specs=[pl.BlockSpec(memory_space=pl.ANY)] * 3,
            out_specs=pl.BlockSpec((EXPERT_BLOCKS_PER_STEP * MOE_TILE, D_MODEL // 2),
                                   lambda i, *_: (i, 0)),
            scratch_shapes=[
                pltpu.VMEM((EXPERT_GATHER_SLOTS, MOE_TILE, D_MODEL // 2), jnp.uint32),
                pltpu.VMEM((2, D_MODEL, 2 * D_EXPERT), jnp.float32),
                pltpu.VMEM((2, D_EXPERT, D_MODEL), jnp.float32),
                pltpu.VMEM((D_MODEL, 2 * D_EXPERT), jnp.bfloat16),
                pltpu.VMEM((D_EXPERT, D_MODEL), jnp.bfloat16),
                pltpu.SemaphoreType.DMA((EXPERT_GATHER_SLOTS,)),
                pltpu.SemaphoreType.DMA((2,)),
            ],
        ),
        out_shape=jax.ShapeDtypeStruct((nblk * MOE_TILE, D_MODEL // 2), jnp.uint32),
        compiler_params=pltpu.CompilerParams(
            dimension_semantics=("arbitrary",), vmem_limit_bytes=VMEM_LIMIT),
        name="moe_experts",
    )(block_expert, block_next, block_wslot, n_valid, src, xs, w_gate_up, w_down)


def _combine_kernel(comb_ref, ys_hbm, x1_ref, route_ref, nf_ref, out_ref, ybuf, sems):
    tc = MIX_TILE
    per_step = COMBINE_TILES_PER_STEP
    i = pl.program_id(0)
    last = pl.num_programs(0) - 1

    def gather(step, slot):
        first = step * per_step * TILE_CHUNKS
        return [_chunk_copy(ys_hbm, comb_ref[first + lc], ybuf.at[slot], lc, sems.at[slot])
                for lc in range(per_step * TILE_CHUNKS)]

    @pl.when(i == 0)
    def _():
        for cp in gather(0, 0):
            cp.start()

    nxt = jnp.minimum(i + 1, last)
    for cp in gather(nxt, (i + 1) % 2):
        cp.start()

    slot = i % 2
    for cp in gather(i, slot):
        cp.wait()

    def tile(t):
        rows = slice(t * tc, (t + 1) * tc)
        y_lo, y_hi = _unpack_bf16_pairs(ybuf[slot, t * MOE_SLOTS:(t + 1) * MOE_SLOTS, :])
        route = route_ref[rows, :]
        slot_id = lax.broadcasted_iota(jnp.int32, (tc, MOE_SLOTS), 1)
        picked = []
        for lane in (ROUTE_S0, ROUTE_S1):
            sel = (slot_id == route[:, lane:lane + 1].astype(jnp.int32)).astype(jnp.bfloat16)
            yield
            picked.append(jnp.concatenate([jnp.dot(sel, y_lo, preferred_element_type=jnp.float32),
                                           jnp.dot(sel, y_hi, preferred_element_type=jnp.float32)], axis=1))
            yield
        g0 = route[:, ROUTE_G0:ROUTE_G0 + 1]
        g1 = route[:, ROUTE_G1:ROUTE_G1 + 1]
        out_ref[rows, :] = _rmsnorm(x1_ref[rows, :] + g0 * picked[0] + g1 * picked[1], nf_ref[...])

    waiting = [tile(t) for t in range(per_step)]
    live = []
    while waiting or live:
        if waiting:
            live.append(waiting.pop(0))
        for gen in list(live):
            if next(gen, _DONE) is _DONE:
                live.remove(gen)

    @pl.when(i == last)
    def _():
        for cp in gather(nxt, (i + 1) % 2):
            cp.wait()


def _combine(comb, ys, x1, route, norm_f):
    T = x1.shape[0]
    tc = MIX_TILE * COMBINE_TILES_PER_STEP
    return pl.pallas_call(
        _combine_kernel,
        grid_spec=pltpu.PrefetchScalarGridSpec(
            num_scalar_prefetch=1,
            grid=(T // tc,),
            in_specs=[
                pl.BlockSpec(memory_space=pl.ANY),
                pl.BlockSpec((tc, D_MODEL), lambda i, c: (i, 0)),
                pl.BlockSpec((tc, LANES), lambda i, c: (i, 0)),
                pl.BlockSpec((1, D_MODEL), lambda i, c: (0, 0)),
            ],
            out_specs=pl.BlockSpec((tc, D_MODEL), lambda i, c: (i, 0)),
            scratch_shapes=[
                pltpu.VMEM((2, COMBINE_TILES_PER_STEP * MOE_SLOTS, D_MODEL // 2), jnp.uint32),
                pltpu.SemaphoreType.DMA((2,)),
            ],
        ),
        out_shape=jax.ShapeDtypeStruct((T, D_MODEL), jnp.float32),
        compiler_params=pltpu.CompilerParams(
            dimension_semantics=("arbitrary",), vmem_limit_bytes=VMEM_LIMIT),
        name="moe_combine",
    )(comb, ys, x1, route, norm_f)


def _block_diag_gates(w_rg, w_ig):
    def bd(w):
        w4 = w.reshape(N_LRU_CHUNKS, LRU_CHUNK_HEADS, LRU_HEAD_DIM, LRU_HEAD_DIM)
        eye = jnp.eye(LRU_CHUNK_HEADS, dtype=w.dtype)
        return jnp.einsum('chij,hk->chikj', w4, eye).reshape(N_LRU_CHUNKS, LRU_CHUNK, LRU_CHUNK)
    return jnp.concatenate([bd(w_rg), bd(w_ig)], axis=-1).astype(jnp.bfloat16)


def kernel(x, rel_bias, norm1, w_in, conv_w, conv_b, w_rg, b_rg, w_ig, b_ig, lru_lambda,
           w_proj_attn, w_proj_lru, w_out, norm2, w_router_group, w_router_expert,
           w_gate_up, w_down, norm_f):
    B, S, D = x.shape
    T = B * S
    assert w_in.shape[0] == 1, "single-layer block"
    layer = 0
    bf16 = jnp.bfloat16
    x2 = x.reshape(T, D)
    row = lambda v: v[layer].reshape(1, -1)
    w_qkv = (w_in[layer][:, :QKV_WIDTH].reshape(D, 3, N_GROUPS, GROUP_WIDTH)
             .transpose(0, 2, 1, 3).reshape(D, QKV_WIDTH).astype(bf16))
    w_rest = w_in[layer][:, QKV_WIDTH:].astype(bf16)

    qkv_groups = _qkv_projection(x2, row(norm1), w_qkv, B, S)
    attn_o, attn_lse = [], []
    for gi, (window, dilation) in enumerate(DILATED_GROUPS):
        hs = slice(gi * HEADS_PER_GROUP, (gi + 1) * HEADS_PER_GROUP)
        bias = _attention_bias_table(rel_bias[:, hs], window, dilation)
        o, lse = _attention_group(qkv_groups[gi], bias, gi, dilation)
        attn_o.append(o)
        attn_lse.append(lse)

    w_router = jnp.zeros((D, LANES), jnp.float32)
    w_router = w_router.at[:, :N_EXPERTS].set(w_router_expert[layer].astype(jnp.float32))
    w_router = w_router.at[:, GROUP_LOGIT_LANE0:GROUP_LOGIT_LANE0 + N_EXPERT_GROUPS].set(
        w_router_group[layer].astype(jnp.float32))
    w_router_hi = w_router.astype(bf16)
    weights = [
        row(norm1), w_rest, conv_w[layer], row(conv_b),
        _block_diag_gates(w_rg[layer], w_ig[layer]), row(b_rg), row(b_ig), row(lru_lambda),
        w_proj_attn[layer].astype(bf16), w_proj_lru[layer].astype(bf16), w_out[layer].astype(bf16),
        row(norm2),
        jnp.concatenate([w_router_hi, (w_router - w_router_hi.astype(jnp.float32)).astype(bf16)], axis=1),
    ]
    x1, xs, route, counts = _token_mixer(x, attn_o, attn_lse, weights)

    expert_tables, comb = _moe_tables(counts[:, 0, :N_EXPERTS].astype(jnp.int32))
    ys = _experts(expert_tables, xs, w_gate_up[layer], w_down[layer])
    out = _combine(comb, ys, x1, route, norm_f.reshape(1, -1))
    return out.reshape(B, S, D)
```

```python
import functools
import math

import jax
import jax.numpy as jnp
from jax import lax
from jax.experimental import pallas as pl
from jax.experimental.pallas import tpu as pltpu

D_MODEL = 1024
HEAD_DIM = 64
HEADS_PER_GROUP = 4
DILATED_GROUPS = ((128, 1), (512, 4), (2048, 16))
N_GROUPS = len(DILATED_GROUPS)
N_ATTN_HEADS = HEADS_PER_GROUP * N_GROUPS
ATTN_WIDTH = N_ATTN_HEADS * HEAD_DIM
GROUP_WIDTH = HEADS_PER_GROUP * HEAD_DIM
QKV_WIDTH = 3 * ATTN_WIDTH
ATTN_BLOCK = 128
N_REL_BUCKETS = 32
REL_MAX_DISTANCE = 2048
NEG_INF = -1e30

LRU_WIDTH = D_MODEL
LRU_HEADS = 16
LRU_HEAD_DIM = LRU_WIDTH // LRU_HEADS
CONV_WIDTH = 4
LRU_C = 8.0
LRU_CHUNK_HEADS = 4
LRU_CHUNK = LRU_CHUNK_HEADS * LRU_HEAD_DIM
N_LRU_CHUNKS = LRU_WIDTH // LRU_CHUNK

N_EXPERT_GROUPS = 4
EXPERTS_PER_GROUP = 8
N_EXPERTS = N_EXPERT_GROUPS * EXPERTS_PER_GROUP
TOP_K = 2
D_EXPERT = 512
EPS = 1e-6

REST_WIDTH = 2 * LRU_WIDTH + 2 * D_MODEL
LANES = 128
SUBLANES = 8

ATTN_TILE = 256
ATTN_TILES_PER_STEP = 2
ATTN_QB = 8
MIX_TILE = 256
SCAN_SEG = MIX_TILE // SUBLANES
SCAN_PITCH = SCAN_SEG + SUBLANES
MOE_TILE = 256
EXPERT_GATHER_SLOTS = 4
EXPERT_BLOCKS_PER_STEP = 4
COMBINE_TILES_PER_STEP = 2
WEIGHT_DMA_PRIORITY = 1
EXPERT_GROUP = 256
MOE_CHUNK = SUBLANES
MOE_BLOCK_CHUNKS = MOE_TILE // MOE_CHUNK
MOE_SLOTS = 768
assert MOE_SLOTS >= TOP_K * MIX_TILE + N_EXPERTS * (MOE_CHUNK - 1) and MOE_SLOTS % MOE_CHUNK == 0
TILE_CHUNKS = MOE_SLOTS // MOE_CHUNK
VMEM_LIMIT = 56 * 1024 * 1024

ROUTE_S0, ROUTE_S1, ROUTE_G0, ROUTE_G1 = range(4)
GROUP_LOGIT_LANE0 = N_EXPERTS


_DONE = object()


def _rmsnorm(x, g):
    return x * lax.rsqrt(jnp.mean(x * x, axis=-1, keepdims=True) + EPS) * g


def _pack_bf16_pairs(v):
    w = v.shape[1] // 2
    bits = lax.bitcast_convert_type(v, jnp.uint32)
    return bits[:, w:] | (bits[:, :w] >> 16)


def _unpack_bf16_pairs(p):
    lo = lax.bitcast_convert_type(p << 16, jnp.float32)
    hi = lax.bitcast_convert_type(p & jnp.uint32(0xFFFF0000), jnp.float32)
    return lo.astype(jnp.bfloat16), hi.astype(jnp.bfloat16)


def _residue_major_perm(n, d):
    per = n // d
    out = lax.broadcasted_iota(jnp.int32, (n, n), 0)
    src = lax.broadcasted_iota(jnp.int32, (n, n), 1)
    m = jnp.bitwise_and(out, per - 1)
    r = jnp.right_shift(out, per.bit_length() - 1)
    return (src == m * d + r).astype(jnp.bfloat16)


def _qkv_kernel(x_ref, g_ref, w_ref, *o_refs):
    tm = ATTN_TILE

    def tile(t):
        rows = slice(t * tm, (t + 1) * tm)
        h = _rmsnorm(x_ref[rows, :], g_ref[...]).astype(jnp.bfloat16)
        yield
        for gi, (o_ref, (_, d)) in enumerate(zip(o_refs, DILATED_GROUPS)):
            cols = slice(gi * ATTN_WIDTH, (gi + 1) * ATTN_WIDTH)
            part = jnp.dot(h, w_ref[:, cols], preferred_element_type=jnp.float32).astype(jnp.bfloat16)
            yield
            if d > 1:
                part = jnp.dot(_residue_major_perm(tm, d), part,
                               preferred_element_type=jnp.float32).astype(jnp.bfloat16)
            o_ref[:, t * (tm // d):(t + 1) * (tm // d), :] = part.reshape(d, tm // d, ATTN_WIDTH)
            yield

    waiting = [tile(t) for t in range(ATTN_TILES_PER_STEP)]
    live = []
    while waiting or live:
        if waiting:
            live.append(waiting.pop(0))
        for gen in list(live):
            if next(gen, _DONE) is _DONE:
                live.remove(gen)


def _qkv_projection(x2, norm1, w_qkv, B, S):
    tm = ATTN_TILE * ATTN_TILES_PER_STEP
    nt = S // tm
    return pl.pallas_call(
        _qkv_kernel,
        grid=(B, nt),
        in_specs=[
            pl.BlockSpec((tm, D_MODEL), lambda b, j: (b * nt + j, 0)),
            pl.BlockSpec((1, D_MODEL), lambda b, j: (0, 0)),
            pl.BlockSpec((D_MODEL, QKV_WIDTH), lambda b, j: (0, 0)),
        ],
        out_specs=[pl.BlockSpec((None, d, tm // d, ATTN_WIDTH), lambda b, j: (b, 0, j, 0))
                   for _, d in DILATED_GROUPS],
        out_shape=[jax.ShapeDtypeStruct((B, d, S // d, ATTN_WIDTH), jnp.bfloat16)
                   for _, d in DILATED_GROUPS],
        compiler_params=pltpu.CompilerParams(
            dimension_semantics=("parallel", "parallel"), vmem_limit_bytes=VMEM_LIMIT),
        name="qkv_projection",
    )(x2, norm1, w_qkv)


def _attn_kernel(n_qb, q_ref, kp_ref, kc_ref, vp_ref, vc_ref, bias0_ref, bias_ref, o_ref, lse_ref):
    q = q_ref[...] * (HEAD_DIM ** -0.5)
    k = jnp.concatenate([kp_ref[...], kc_ref[...]], axis=0)
    v = jnp.concatenate([vp_ref[...], vc_ref[...]], axis=0)
    pair_w = 2 * HEAD_DIM
    first_head = lax.broadcasted_iota(jnp.int32, (1, pair_w), 1) < HEAD_DIM
    def one_head(x, first):
        bits = pltpu.bitcast(x, jnp.uint32)
        keep = first_head if first else jnp.logical_not(first_head)
        return pltpu.bitcast(jnp.where(keep, bits, jnp.uint32(0)), jnp.bfloat16)

    ones = jnp.ones_like(k[:, :pair_w])
    operands = []
    for pair in range(HEADS_PER_GROUP // 2):
        cols = slice(pair * pair_w, (pair + 1) * pair_w)
        k2, v2 = k[:, cols], v[:, cols]
        k_heads = (one_head(k2, True), one_head(k2, False))
        v_heads = (jnp.concatenate([one_head(v2, True), one_head(ones, True)], axis=1),
                   jnp.concatenate([one_head(v2, False), one_head(ones, False)], axis=1))
        operands.append((cols, k_heads, v_heads))
    done = {}

    def head_pair(b, pair):
        rows = slice(b * ATTN_BLOCK, (b + 1) * ATTN_BLOCK)
        keys = slice(b * ATTN_BLOCK, (b + 2) * ATTN_BLOCK)
        cols, k_heads, v_heads = operands[pair]
        bias_ref_b = bias0_ref if b == 0 else bias_ref
        bias = jnp.concatenate([bias_ref_b[2 * pair], bias_ref_b[2 * pair + 1]], axis=1)
        k_both = jnp.concatenate([k_heads[0][keys], k_heads[1][keys]], axis=0)
        s = lax.dot_general(q[rows, cols], k_both, (((1,), (1,)), ((), ())),
                            preferred_element_type=jnp.float32) + bias
        yield
        n_keys = 2 * ATTN_BLOCK
        s0, s1 = s[:, :n_keys], s[:, n_keys:]
        m0 = jnp.max(s0, axis=-1, keepdims=True)
        m1 = jnp.max(s1, axis=-1, keepdims=True)
        p0 = jnp.exp(s0 - m0).astype(jnp.bfloat16)
        p1 = jnp.exp(s1 - m1).astype(jnp.bfloat16)
        yield
        acc = (jnp.dot(p0, v_heads[0][keys], preferred_element_type=jnp.float32)
               + jnp.dot(p1, v_heads[1][keys], preferred_element_type=jnp.float32))
        yield
        denom = acc[:, pair_w:]
        done[b, pair] = (acc[:, :pair_w] / denom, jnp.where(first_head, m0, m1) + jnp.log(denom))

    waiting = [head_pair(b, pair) for b in range(n_qb) for pair in range(HEADS_PER_GROUP // 2)]
    live = []
    while waiting or live:
        if waiting:
            live.append(waiting.pop(0))
        for gen in list(live):
            if next(gen, _DONE) is _DONE:
                live.remove(gen)

    for b in range(n_qb):
        rows = slice(b * ATTN_BLOCK, (b + 1) * ATTN_BLOCK)
        pairs = [done[b, pair] for pair in range(HEADS_PER_GROUP // 2)]
        o_ref[rows, :] = jnp.concatenate([o for o, _ in pairs], axis=1).astype(o_ref.dtype)
        lse_ref[rows, :] = jnp.concatenate([lse for _, lse in pairs], axis=1)


def _attention_group(qkv_g, bias, gi, dilation):
    B, _, L, _ = qkv_g.shape
    n_qb = min(ATTN_QB, L // ATTN_BLOCK)
    rows = n_qb * ATTN_BLOCK
    cur = lambda which: pl.BlockSpec((None, None, rows, GROUP_WIDTH), lambda b, r, n: (b, r, n, which))
    prev = lambda which: pl.BlockSpec((None, None, ATTN_BLOCK, GROUP_WIDTH),
                                      lambda b, r, n: (b, r, jnp.maximum(n * n_qb - 1, 0), which))
    bias_blk = (None, HEADS_PER_GROUP, ATTN_BLOCK, 2 * ATTN_BLOCK)
    return pl.pallas_call(
        functools.partial(_attn_kernel, n_qb),
        grid=(B, dilation, L // rows),
        in_specs=[
            cur(0), prev(1), cur(1), prev(2), cur(2),
            pl.BlockSpec(bias_blk, lambda b, r, n: (jnp.minimum(n, 1), 0, 0, 0)),
            pl.BlockSpec(bias_blk, lambda b, r, n: (1, 0, 0, 0)),
        ],
        out_specs=[cur(0), cur(0)],
        out_shape=[
            jax.ShapeDtypeStruct((B, dilation, L, GROUP_WIDTH), jnp.bfloat16),
            jax.ShapeDtypeStruct((B, dilation, L, GROUP_WIDTH), jnp.float32),
        ],
        compiler_params=pltpu.CompilerParams(
            dimension_semantics=("parallel", "parallel", "arbitrary"), vmem_limit_bytes=VMEM_LIMIT),
        name=f"dilated_attention_g{gi}",
    )(qkv_g, qkv_g, qkv_g, qkv_g, qkv_g, bias, bias)


def _t5_causal_bucket(dist):
    max_exact = N_REL_BUCKETS // 2
    d_f = jnp.maximum(dist, max_exact).astype(jnp.float32)
    large = max_exact + (jnp.log(d_f / max_exact) / math.log(REL_MAX_DISTANCE / max_exact)
                         * (N_REL_BUCKETS - max_exact)).astype(jnp.int32)
    large = jnp.minimum(large, N_REL_BUCKETS - 1)
    return jnp.where(dist < max_exact, dist, large)


def _attention_bias_table(rel_bias_g, window, dilation):
    nw = window // dilation
    qi = jnp.arange(ATTN_BLOCK)[:, None]
    ki = jnp.arange(2 * ATTN_BLOCK)[None, :]
    dist = ATTN_BLOCK + qi - ki
    band = (dist >= 0) & (dist <= nw)
    bucket = _t5_causal_bucket(jnp.maximum(dist, 0) * dilation)
    onehot = (bucket[:, :, None] == jnp.arange(N_REL_BUCKETS)).astype(jnp.float32)
    bias = jnp.einsum('qkb,bh->hqk', onehot, rel_bias_g.astype(jnp.float32),
                      precision=lax.Precision.HIGHEST)
    later = jnp.where(band[None], bias, NEG_INF)
    first = jnp.where((band & (ki >= ATTN_BLOCK))[None], bias, NEG_INF)
    return jnp.stack([first, later])


def _gelu_tanh(x):
    return 0.5 * x * (1.0 + jnp.tanh(math.sqrt(2.0 / math.pi) * (x + 0.044715 * (x * x * x))))


def _softplus(x):
    return jnp.maximum(x, 0.0) + jnp.log(1.0 + jnp.exp(-jnp.abs(x)))


def _natural_order(blk_ref, d, slabs):
    if d == 1:
        return blk_ref[0].astype(jnp.float32)
    per = MIX_TILE // d
    for r in range(d):
        rows = blk_ref[r].astype(jnp.float32)
        for c in range(GROUP_WIDTH // LANES):
            slabs[c, pl.ds(r, per, stride=d), :] = rows[:, c * LANES:(c + 1) * LANES]
    return jnp.concatenate([slabs[c] for c in range(GROUP_WIDTH // LANES)], axis=1)


def _mix_tile(x_ref, o1_ref, o2_ref, o3_ref, l1_ref, l2_ref, l3_ref,
              n1_ref, win_ref, cw_ref, cb_ref, wg_ref, brg_ref, big_ref, lam_ref,
              wpa_ref, wpl_ref, wout_ref, n2_ref, wr_ref,
              x1_ref, xs_ref, route_ref, cnt_ref,
              xbuf, a_sc, b_sc, h_sc, *slabs):
    tm = MIX_TILE
    first_tile = pl.program_id(0) == 0

    @pl.when(first_tile)
    def _():
        xbuf[0:SUBLANES, :] = jnp.zeros((SUBLANES, LRU_WIDTH), jnp.float32)
        h_sc[...] = jnp.zeros_like(h_sc)

    x = x_ref[...]
    h = _rmsnorm(x, n1_ref[...]).astype(jnp.bfloat16)
    def project(first_col):
        return jnp.dot(h, win_ref[:, first_col:first_col + D_MODEL], preferred_element_type=jnp.float32)

    xr = project(0)

    yield
    g_lru = project(LRU_WIDTH)
    xbuf[SUBLANES:SUBLANES + tm, :] = xr
    xc = xr * cw_ref[CONV_WIDTH - 1:CONV_WIDTH, :] + cb_ref[...]
    for j in range(CONV_WIDTH - 1):
        back = CONV_WIDTH - 1 - j
        xc = xc + xbuf[SUBLANES - back:SUBLANES - back + tm, :] * cw_ref[j:j + 1, :]
    xbuf[0:SUBLANES, :] = xbuf[tm:tm + SUBLANES, :]

    xcb = xc.astype(jnp.bfloat16)
    r_parts, i_parts = [], []
    for c in range(N_LRU_CHUNKS):
        ri = jnp.dot(xcb[:, c * LRU_CHUNK:(c + 1) * LRU_CHUNK], wg_ref[c],
                     preferred_element_type=jnp.float32)
        r_parts.append(ri[:, :LRU_CHUNK])
        i_parts.append(ri[:, LRU_CHUNK:])
    yield
    g_a = project(2 * LRU_WIDTH)
    r = jax.nn.sigmoid(jnp.concatenate(r_parts, axis=1) + brg_ref[...])
    ig = jax.nn.sigmoid(jnp.concatenate(i_parts, axis=1) + big_ref[...])
    log_a = (-LRU_C * _softplus(-lam_ref[...])) * r
    a = jnp.exp(log_a)
    gap = 1.0 - a * a
    root = jnp.where(gap > 0.0, gap * lax.rsqrt(gap), 0.0)
    bb = root * (ig * xc)

    yield
    g_b = project(2 * LRU_WIDTH + D_MODEL)
    n_slab = LRU_WIDTH // LANES
    for c in range(n_slab):
        lanes = slice(c * LANES, (c + 1) * LANES)
        for s in range(SUBLANES):
            rows = slice(s * SCAN_SEG, (s + 1) * SCAN_SEG)
            a_sc[c, s * SCAN_PITCH:s * SCAN_PITCH + SCAN_SEG, :] = a[rows, lanes]
            b_sc[c, s * SCAN_PITCH:s * SCAN_PITCH + SCAN_SEG, :] = bb[rows, lanes]

    yield
    local = [jnp.zeros((SUBLANES, LANES), jnp.float32)] * n_slab
    decay = [jnp.ones((SUBLANES, LANES), jnp.float32)] * n_slab
    for j in range(SCAN_SEG):
        for c in range(n_slab):
            step = pl.ds(j, SUBLANES, stride=SCAN_PITCH)
            a_j = a_sc[c, step, :]
            local[c] = a_j * local[c] + b_sc[c, step, :]
            decay[c] = a_j * decay[c]
            b_sc[c, step, :] = local[c]
            a_sc[c, step, :] = decay[c]

    yield
    h_parts = []
    for c in range(n_slab):
        lanes = slice(c * LANES, (c + 1) * LANES)
        entry = h_sc[:, lanes]
        segs = []
        for s in range(SUBLANES):
            rows = slice(s * SCAN_PITCH, s * SCAN_PITCH + SCAN_SEG)
            segs.append(b_sc[c, rows, :] + a_sc[c, rows, :] * entry)
            entry = decay[c][s:s + 1, :] * entry + local[c][s:s + 1, :]
        h_sc[:, lanes] = entry
        h_parts.append(jnp.concatenate(segs, axis=0))
    h_all = jnp.concatenate(h_parts, axis=1)
    yield
    lru = (h_all * _gelu_tanh(g_lru)).astype(jnp.bfloat16)
    p_lru = jnp.dot(lru, wpl_ref[...], preferred_element_type=jnp.float32)

    yield
    dil = [d for _, d in DILATED_GROUPS]
    o1, o2, o3 = (_natural_order(ref, d, slabs[2 * g]) for g, (ref, d) in
                  enumerate(zip((o1_ref, o2_ref, o3_ref), dil)))
    l1, l2, l3 = (_natural_order(ref, d, slabs[2 * g + 1]) for g, (ref, d) in
                  enumerate(zip((l1_ref, l2_ref, l3_ref), dil)))
    lm = jnp.maximum(jnp.maximum(l1, l2), l3)
    e1, e2, e3 = jnp.exp(l1 - lm), jnp.exp(l2 - lm), jnp.exp(l3 - lm)
    attn = (e1 * o1 + e2 * o2 + e3 * o3) / (e1 + e2 + e3)
    p_attn = jnp.dot(attn.astype(jnp.bfloat16), wpa_ref[...], preferred_element_type=jnp.float32)

    yield
    merged = jax.nn.sigmoid(g_a) * p_attn + jax.nn.sigmoid(g_b) * p_lru
    x1 = x + jnp.dot(merged.astype(jnp.bfloat16), wout_ref[...], preferred_element_type=jnp.float32)
    x1_ref[...] = x1
    h2 = _rmsnorm(x1, n2_ref[...])

    yield
    h2_hi = h2.astype(jnp.bfloat16)
    h2_lo = (h2 - h2_hi.astype(jnp.float32)).astype(jnp.bfloat16)
    by_hi = jnp.dot(h2_hi, wr_ref[...], preferred_element_type=jnp.float32)
    by_lo = jnp.dot(h2_lo, wr_ref[...], preferred_element_type=jnp.float32)
    logits = (by_hi[:, :LANES] + (by_hi[:, LANES:] + by_lo[:, :LANES])) + by_lo[:, LANES:]
    lane = lax.broadcasted_iota(jnp.int32, (tm, LANES), 1)
    big = jnp.int32(LANES)
    lowest = jnp.float32(-3.0e38)
    is_g = (lane >= GROUP_LOGIT_LANE0) & (lane < GROUP_LOGIT_LANE0 + N_EXPERT_GROUPS)
    gl = jnp.where(is_g, logits, lowest)
    gmax = jnp.max(gl, axis=-1, keepdims=True)
    gsel = jnp.min(jnp.where(gl == gmax, lane, big), axis=-1, keepdims=True) - GROUP_LOGIT_LANE0
    p_sel = 1.0 / jnp.sum(jnp.where(is_g, jnp.exp(gl - gmax), 0.0), axis=-1, keepdims=True)
    is_e = (lane >= gsel * EXPERTS_PER_GROUP) & (lane < (gsel + 1) * EXPERTS_PER_GROUP)
    el = jnp.where(is_e, logits, lowest)
    v1 = jnp.max(el, axis=-1, keepdims=True)
    i1 = jnp.min(jnp.where(el == v1, lane, big), axis=-1, keepdims=True)
    el2 = jnp.where(lane == i1, lowest, el)
    v2 = jnp.max(el2, axis=-1, keepdims=True)
    i2 = jnp.min(jnp.where(el2 == v2, lane, big), axis=-1, keepdims=True)
    t = jnp.exp(v2 - v1)
    g0 = p_sel / (1.0 + t)
    g1 = p_sel * t / (1.0 + t)

    yield
    oh0 = lane == i1
    oh1 = lane == i2
    both = (oh0 | oh1).astype(jnp.float32)
    tri = (lax.broadcasted_iota(jnp.int32, (tm, tm), 0)
           > lax.broadcasted_iota(jnp.int32, (tm, tm), 1)).astype(jnp.bfloat16)
    before = jnp.dot(tri, both.astype(jnp.bfloat16), preferred_element_type=jnp.float32)
    cnt = jnp.sum(both, axis=0, keepdims=True)
    padded = jnp.floor((cnt + (MOE_CHUNK - 1.0)) * (1.0 / MOE_CHUNK)) * MOE_CHUNK
    upper = (lax.broadcasted_iota(jnp.int32, (LANES, LANES), 0)
             < lax.broadcasted_iota(jnp.int32, (LANES, LANES), 1)).astype(jnp.bfloat16)
    loff = jnp.dot(jnp.broadcast_to(padded, (SUBLANES, LANES)).astype(jnp.bfloat16), upper,
                   preferred_element_type=jnp.float32)[0:1, :]
    base = before + loff
    slot0 = jnp.sum(jnp.where(oh0, base, 0.0), axis=-1, keepdims=True)
    slot1 = jnp.sum(jnp.where(oh1, base, 0.0), axis=-1, keepdims=True)
    cnt_ref[...] = jnp.broadcast_to(cnt, cnt_ref.shape)

    slot_id = lax.broadcasted_iota(jnp.int32, (tm, MOE_SLOTS), 1)
    place = ((slot_id == slot0.astype(jnp.int32)) | (slot_id == slot1.astype(jnp.int32)))
    xs_ref[...] = _pack_bf16_pairs(
        lax.dot_general(place.astype(jnp.bfloat16), h2_hi, (((0,), (0,)), ((), ())),
                        preferred_element_type=jnp.float32))

    route = jnp.zeros((tm, LANES), jnp.float32)
    for lane_id, val in ((ROUTE_S0, slot0), (ROUTE_S1, slot1), (ROUTE_G0, g0), (ROUTE_G1, g1)):
        route = jnp.where(lane == lane_id, val, route)
    route_ref[...] = route


N_MIXER_TILED_INPUTS = 1 + 2 * N_GROUPS
N_MIXER_OUTPUTS = 4


def _mixer_kernel(*refs):
    n_batch = refs[0].shape[0]
    n_weights = len(refs) - N_MIXER_TILED_INPUTS - N_MIXER_OUTPUTS - n_batch * _MIXER_SCRATCH_PER_SEQ
    tiled = refs[:N_MIXER_TILED_INPUTS]
    weights = refs[N_MIXER_TILED_INPUTS:N_MIXER_TILED_INPUTS + n_weights]
    outs = refs[N_MIXER_TILED_INPUTS + n_weights:N_MIXER_TILED_INPUTS + n_weights + N_MIXER_OUTPUTS]
    scratch = refs[N_MIXER_TILED_INPUTS + n_weights + N_MIXER_OUTPUTS:]
    stages = []
    for b in range(n_batch):
        own = scratch[b * _MIXER_SCRATCH_PER_SEQ:(b + 1) * _MIXER_SCRATCH_PER_SEQ]
        stages.append(_mix_tile(*[r.at[b] for r in tiled], *weights, *[r.at[b] for r in outs], *own))
    while stages:
        for gen in list(stages):
            if next(gen, _DONE) is _DONE:
                stages.remove(gen)


def _mixer_scratch(tm):
    return [
        pltpu.VMEM((tm + 2 * SUBLANES, LRU_WIDTH), jnp.float32),
        pltpu.VMEM((LRU_WIDTH // LANES, SUBLANES * SCAN_PITCH, LANES), jnp.float32),
        pltpu.VMEM((LRU_WIDTH // LANES, SUBLANES * SCAN_PITCH, LANES), jnp.float32),
        pltpu.VMEM((1, LRU_WIDTH), jnp.float32),
    ] + [pltpu.VMEM((GROUP_WIDTH // LANES, tm, LANES), jnp.float32)] * (2 * N_GROUPS)


_MIXER_SCRATCH_PER_SEQ = len(_mixer_scratch(MIX_TILE))


def _token_mixer(x, attn_o, attn_lse, weights):
    B, S, _ = x.shape
    tm = MIX_TILE
    nt = S // tm
    rows = lambda j: (0, j, 0)

    def resident(shape):
        return pl.BlockSpec(shape, lambda j: (0,) * len(shape), pipeline_mode=pl.Buffered(1))

    in_specs = [pl.BlockSpec((B, tm, D_MODEL), rows)]
    group_specs = [pl.BlockSpec((B, d, tm // d, GROUP_WIDTH), lambda j: (0, 0, j, 0))
                   for _, d in DILATED_GROUPS]
    in_specs += group_specs + group_specs
    in_specs += [resident(w.shape) for w in weights]
    x1, xs, route, counts = pl.pallas_call(
        _mixer_kernel,
        grid=(nt,),
        in_specs=in_specs,
        out_specs=[
            pl.BlockSpec((B, tm, D_MODEL), rows),
            pl.BlockSpec((B, MOE_SLOTS, D_MODEL // 2), rows),
            pl.BlockSpec((B, tm, LANES), rows),
            pl.BlockSpec((B, None, SUBLANES, LANES), lambda j: (0, j, 0, 0)),
        ],
        out_shape=[
            jax.ShapeDtypeStruct((B, S, D_MODEL), jnp.float32),
            jax.ShapeDtypeStruct((B, nt * MOE_SLOTS, D_MODEL // 2), jnp.uint32),
            jax.ShapeDtypeStruct((B, S, LANES), jnp.float32),
            jax.ShapeDtypeStruct((B, nt, SUBLANES, LANES), jnp.float32),
        ],
        scratch_shapes=_mixer_scratch(tm) * B,
        compiler_params=pltpu.CompilerParams(
            dimension_semantics=("arbitrary",), vmem_limit_bytes=VMEM_LIMIT),
        name="token_mixer",
    )(x, *attn_o, *attn_lse, *weights)
    return (x1.reshape(B * S, D_MODEL), xs.reshape(B * nt * MOE_SLOTS, D_MODEL // 2),
            route.reshape(B * S, LANES), counts.reshape(B * nt, SUBLANES, LANES))


def _moe_tables(cnt_tile):
    ntiles = cnt_tile.shape[0]
    nch = (cnt_tile + MOE_CHUNK - 1) // MOE_CHUNK
    lo_c = jnp.cumsum(nch, axis=1) - nch
    per_expert = jnp.sum(nch, axis=0)
    region = ((per_expert + MOE_BLOCK_CHUNKS - 1) // MOE_BLOCK_CHUNKS) * MOE_BLOCK_CHUNKS
    pend = jnp.cumsum(region)
    glob = (pend - region)[None, :] + jnp.cumsum(nch, axis=0) - nch

    max_rows = TOP_K * ntiles * MIX_TILE + ntiles * N_EXPERTS * (MOE_CHUNK - 1)
    max_chunks = -(-max_rows // MOE_CHUNK) + N_EXPERTS * (MOE_BLOCK_CHUNKS - 1)
    max_blocks = -(-max_chunks // MOE_BLOCK_CHUNKS)
    max_blocks = -(-max_blocks // EXPERT_BLOCKS_PER_STEP) * EXPERT_BLOCKS_PER_STEP
    max_chunks = max_blocks * MOE_BLOCK_CHUNKS

    seg_start = glob.T.reshape(-1)
    seg_src = (jnp.arange(ntiles, dtype=jnp.int32)[:, None] * TILE_CHUNKS + lo_c).T.reshape(-1)
    step = jnp.diff(seg_src - seg_start, prepend=0)
    g = jnp.arange(max_chunks, dtype=jnp.int32)
    src = g + jnp.sum(jnp.where(seg_start[None, :] <= g[:, None], step[None, :], 0), axis=1)
    src = jnp.clip(src, 0, ntiles * TILE_CHUNKS - 1)

    lstep = jnp.diff(glob - lo_c, axis=1, prepend=0)
    lc = jnp.arange(TILE_CHUNKS, dtype=jnp.int32)
    comb = lc[None, :] + jnp.sum(
        jnp.where(lo_c[:, None, :] <= lc[None, :, None], lstep[:, None, :], 0), axis=2)
    comb = jnp.clip(comb, 0, max_chunks - 1).reshape(-1)

    eidx = jnp.arange(N_EXPERTS, dtype=jnp.int32)
    nonempty = region > 0
    later = jnp.where(nonempty[None, :] & (eidx[None, :] > eidx[:, None]), eidx[None, :], N_EXPERTS)
    next_expert = jnp.min(later, axis=1)
    next_expert = jnp.where(next_expert == N_EXPERTS, -1, next_expert)
    wslot = (jnp.cumsum(nonempty.astype(jnp.int32)) - nonempty.astype(jnp.int32)) % 2
    last_expert = jnp.max(jnp.where(nonempty, eidx, 0))
    blk0 = jnp.arange(max_blocks, dtype=jnp.int32) * MOE_BLOCK_CHUNKS
    block_expert = jnp.minimum(jnp.sum((pend[None, :] <= blk0[:, None]).astype(jnp.int32), axis=1),
                               last_expert)
    of_block = lambda per_expert: jnp.sum(
        jnp.where(block_expert[:, None] == eidx[None, :], per_expert[None, :], 0), axis=1)
    n_valid = pend[-1:] // MOE_BLOCK_CHUNKS
    as_i32 = lambda v: v.astype(jnp.int32)
    expert_tables = (as_i32(block_expert), as_i32(of_block(next_expert)), as_i32(of_block(wslot)),
                     as_i32(n_valid), as_i32(src))
    return expert_tables, as_i32(comb)


def _chunk_copy(src_hbm, src_chunk, dst, dst_chunk, sem):
    rows = lambda c: pl.ds(pl.multiple_of(c * MOE_CHUNK, MOE_CHUNK), MOE_CHUNK)
    return pltpu.make_async_copy(src_hbm.at[rows(src_chunk), :], dst.at[rows(dst_chunk), :], sem)


def _expert_kernel(be_ref, nxt_ref, ws_ref, nv_ref, src_ref, xs_hbm, wgu_hbm, wd_hbm, ys_ref, *scratch):
    for k in range(EXPERT_BLOCKS_PER_STEP):
        _expert_block(pl.program_id(0) * EXPERT_BLOCKS_PER_STEP + k,
                      be_ref, nxt_ref, ws_ref, nv_ref, src_ref, xs_hbm, wgu_hbm, wd_hbm,
                      ys_ref.at[pl.ds(k * MOE_TILE, MOE_TILE), :], *scratch)


def _expert_block(i, be_ref, nxt_ref, ws_ref, nv_ref, src_ref, xs_hbm, wgu_hbm, wd_hbm, ys_ref,
                  xbuf, wgu_f32, wd_f32, wgu_bf, wd_bf, sems, wsems):
    valid = i < nv_ref[0]

    def gather(blk, slot):
        return [_chunk_copy(xs_hbm, src_ref[blk * MOE_BLOCK_CHUNKS + c], xbuf.at[slot], c, sems.at[slot])
                for c in range(MOE_BLOCK_CHUNKS)]

    def fetch_weights(e, slot):
        return [pltpu.make_async_copy(wgu_hbm.at[e], wgu_f32.at[slot], wsems.at[slot]),
                pltpu.make_async_copy(wd_hbm.at[e], wd_f32.at[slot], wsems.at[slot])]

    ahead = EXPERT_GATHER_SLOTS - 1

    @pl.when(i == 0)
    def _():
        for cp in fetch_weights(be_ref[0], 0):
            cp.start(priority=WEIGHT_DMA_PRIORITY)
        for first in range(ahead):
            @pl.when(first < nv_ref[0])
            def _():
                for cp in gather(first, first):
                    cp.start()

    @pl.when(i + ahead < nv_ref[0])
    def _():
        for cp in gather(i + ahead, (i + ahead) % EXPERT_GATHER_SLOTS):
            cp.start()

    new_expert = valid & ((i == 0) | (be_ref[i] != be_ref[jnp.maximum(i - 1, 0)]))

    @pl.when(new_expert)
    def _():
        ws = ws_ref[i]
        for cp in fetch_weights(be_ref[i], ws):
            cp.wait()
        wgu_bf[...] = wgu_f32[ws].astype(jnp.bfloat16)
        wd_bf[...] = wd_f32[ws].astype(jnp.bfloat16)

        @pl.when(nxt_ref[i] >= 0)
        def _():
            for cp in fetch_weights(nxt_ref[i], 1 - ws):
                cp.start(priority=WEIGHT_DMA_PRIORITY)

    @pl.when(valid)
    def _():
        slot = i % EXPERT_GATHER_SLOTS
        for cp in gather(i, slot):
            cp.wait()
        x_lo, x_hi = _unpack_bf16_pairs(xbuf[slot])
        half = D_MODEL // 2
        partial = []

        def up_proj(cols):
            return (jnp.dot(x_lo, wgu_bf[:half, cols], preferred_element_type=jnp.float32)
                    + jnp.dot(x_hi, wgu_bf[half:, cols], preferred_element_type=jnp.float32))

        def hidden_group(c):
            cols = slice(c * EXPERT_GROUP, (c + 1) * EXPERT_GROUP)
            g = up_proj(cols)
            u = up_proj(slice(D_EXPERT + cols.start, D_EXPERT + cols.stop))
            yield
            act = (g * jax.nn.sigmoid(g) * u).astype(jnp.bfloat16)
            yield
            partial.append(jnp.dot(act, wd_bf[cols, :], preferred_element_type=jnp.float32))

        waiting = [hidden_group(c) for c in range(D_EXPERT // EXPERT_GROUP)]
        live = []
        while waiting or live:
            if waiting:
                live.append(waiting.pop(0))
            for gen in list(live):
                if next(gen, _DONE) is _DONE:
                    live.remove(gen)
        y = sum(partial[1:], partial[0])
        ys_ref[...] = _pack_bf16_pairs(y.astype(jnp.bfloat16).astype(jnp.float32))

    @pl.when(jnp.logical_not(valid))
    def _():
        ys_ref[...] = jnp.zeros_like(ys_ref)


def _experts(tables, xs, w_gate_up, w_down):
    block_expert, block_next, block_wslot, n_valid, src = tables
    nblk = block_expert.shape[0]
    return pl.pallas_call(
        _expert_kernel,
        grid_spec=pltpu.PrefetchScalarGridSpec(
            num_scalar_prefetch=5,
            grid=(nblk // EXPERT_BLOCKS_PER_STEP,),
            in_specs=[pl.BlockSpec(memory_space=pl.ANY)] * 3,
            out_specs=pl.BlockSpec((EXPERT_BLOCKS_PER_STEP * MOE_TILE, D_MODEL // 2),
                                   lambda i, *_: (i, 0)),
            scratch_shapes=[
                pltpu.VMEM((EXPERT_GATHER_SLOTS, MOE_TILE, D_MODEL // 2), jnp.uint32),
                pltpu.VMEM((2, D_MODEL, 2 * D_EXPERT), jnp.float32),
                pltpu.VMEM((2, D_EXPERT, D_MODEL), jnp.float32),
                pltpu.VMEM((D_MODEL, 2 * D_EXPERT), jnp.bfloat16),
                pltpu.VMEM((D_EXPERT, D_MODEL), jnp.bfloat16),
                pltpu.SemaphoreType.DMA((EXPERT_GATHER_SLOTS,)),
                pltpu.SemaphoreType.DMA((2,)),
            ],
        ),
        out_shape=jax.ShapeDtypeStruct((nblk * MOE_TILE, D_MODEL // 2), jnp.uint32),
        compiler_params=pltpu.CompilerParams(
            dimension_semantics=("arbitrary",), vmem_limit_bytes=VMEM_LIMIT),
        name="moe_experts",
    )(block_expert, block_next, block_wslot, n_valid, src, xs, w_gate_up, w_down)


def _combine_kernel(comb_ref, ys_hbm, x1_ref, route_ref, nf_ref, out_ref, ybuf, sems):
    tc = MIX_TILE
    per_step = COMBINE_TILES_PER_STEP
    i = pl.program_id(0)
    last = pl.num_programs(0) - 1

    def gather(step, slot):
        first = step * per_step * TILE_CHUNKS
        return [_chunk_copy(ys_hbm, comb_ref[first + lc], ybuf.at[slot], lc, sems.at[slot])
                for lc in range(per_step * TILE_CHUNKS)]

    @pl.when(i == 0)
    def _():
        for cp in gather(0, 0):
            cp.start()

    nxt = jnp.minimum(i + 1, last)
    for cp in gather(nxt, (i + 1) % 2):
        cp.start()

    slot = i % 2
    for cp in gather(i, slot):
        cp.wait()

    def tile(t):
        rows = slice(t * tc, (t + 1) * tc)
        y_lo, y_hi = _unpack_bf16_pairs(ybuf[slot, t * MOE_SLOTS:(t + 1) * MOE_SLOTS, :])
        route = route_ref[rows, :]
        slot_id = lax.broadcasted_iota(jnp.int32, (tc, MOE_SLOTS), 1)
        picked = []
        for lane in (ROUTE_S0, ROUTE_S1):
            sel = (slot_id == route[:, lane:lane + 1].astype(jnp.int32)).astype(jnp.bfloat16)
            yield
            picked.append(jnp.concatenate([jnp.dot(sel, y_lo, preferred_element_type=jnp.float32),
                                           jnp.dot(sel, y_hi, preferred_element_type=jnp.float32)], axis=1))
            yield
        g0 = route[:, ROUTE_G0:ROUTE_G0 + 1]
        g1 = route[:, ROUTE_G1:ROUTE_G1 + 1]
        out_ref[rows, :] = _rmsnorm(x1_ref[rows, :] + g0 * picked[0] + g1 * picked[1], nf_ref[...])

    waiting = [tile(t) for t in range(per_step)]
    live = []
    while waiting or live:
        if waiting:
            live.append(waiting.pop(0))
        for gen in list(live):
            if next(gen, _DONE) is _DONE:
                live.remove(gen)

    @pl.when(i == last)
    def _():
        for cp in gather(nxt, (i + 1) % 2):
            cp.wait()


def _combine(comb, ys, x1, route, norm_f):
    T = x1.shape[0]
    tc = MIX_TILE * COMBINE_TILES_PER_STEP
    return pl.pallas_call(
        _combine_kernel,
        grid_spec=pltpu.PrefetchScalarGridSpec(
            num_scalar_prefetch=1,
            grid=(T // tc,),
            in_specs=[
                pl.BlockSpec(memory_space=pl.ANY),
                pl.BlockSpec((tc, D_MODEL), lambda i, c: (i, 0)),
                pl.BlockSpec((tc, LANES), lambda i, c: (i, 0)),
                pl.BlockSpec((1, D_MODEL), lambda i, c: (0, 0)),
            ],
            out_specs=pl.BlockSpec((tc, D_MODEL), lambda i, c: (i, 0)),
            scratch_shapes=[
                pltpu.VMEM((2, COMBINE_TILES_PER_STEP * MOE_SLOTS, D_MODEL // 2), jnp.uint32),
                pltpu.SemaphoreType.DMA((2,)),
            ],
        ),
        out_shape=jax.ShapeDtypeStruct((T, D_MODEL), jnp.float32),
        compiler_params=pltpu.CompilerParams(
            dimension_semantics=("arbitrary",), vmem_limit_bytes=VMEM_LIMIT),
        name="moe_combine",
    )(comb, ys, x1, route, norm_f)


def _block_diag_gates(w_rg, w_ig):
    def bd(w):
        w4 = w.reshape(N_LRU_CHUNKS, LRU_CHUNK_HEADS, LRU_HEAD_DIM, LRU_HEAD_DIM)
        eye = jnp.eye(LRU_CHUNK_HEADS, dtype=w.dtype)
        return jnp.einsum('chij,hk->chikj', w4, eye).reshape(N_LRU_CHUNKS, LRU_CHUNK, LRU_CHUNK)
    return jnp.concatenate([bd(w_rg), bd(w_ig)], axis=-1).astype(jnp.bfloat16)


def kernel(x, rel_bias, norm1, w_in, conv_w, conv_b, w_rg, b_rg, w_ig, b_ig, lru_lambda,
           w_proj_attn, w_proj_lru, w_out, norm2, w_router_group, w_router_expert,
           w_gate_up, w_down, norm_f):
    B, S, D = x.shape
    T = B * S
    assert w_in.shape[0] == 1, "single-layer block"
    layer = 0
    bf16 = jnp.bfloat16
    x2 = x.reshape(T, D)
    row = lambda v: v[layer].reshape(1, -1)
    w_qkv = (w_in[layer][:, :QKV_WIDTH].reshape(D, 3, N_GROUPS, GROUP_WIDTH)
             .transpose(0, 2, 1, 3).reshape(D, QKV_WIDTH).astype(bf16))
    w_rest = w_in[layer][:, QKV_WIDTH:].astype(bf16)

    qkv_groups = _qkv_projection(x2, row(norm1), w_qkv, B, S)
    attn_o, attn_lse = [], []
    for gi, (window, dilation) in enumerate(DILATED_GROUPS):
        hs = slice(gi * HEADS_PER_GROUP, (gi + 1) * HEADS_PER_GROUP)
        bias = _attention_bias_table(rel_bias[:, hs], window, dilation)
        o, lse = _attention_group(qkv_groups[gi], bias, gi, dilation)
        attn_o.append(o)
        attn_lse.append(lse)

    w_router = jnp.zeros((D, LANES), jnp.float32)
    w_router = w_router.at[:, :N_EXPERTS].set(w_router_expert[layer].astype(jnp.float32))
    w_router = w_router.at[:, GROUP_LOGIT_LANE0:GROUP_LOGIT_LANE0 + N_EXPERT_GROUPS].set(
        w_router_group[layer].astype(jnp.float32))
    w_router_hi = w_router.astype(bf16)
    weights = [
        row(norm1), w_rest, conv_w[layer], row(conv_b),
        _block_diag_gates(w_rg[layer], w_ig[layer]), row(b_rg), row(b_ig), row(lru_lambda),
        w_proj_attn[layer].astype(bf16), w_proj_lru[layer].astype(bf16), w_out[layer].astype(bf16),
        row(norm2),
        jnp.concatenate([w_router_hi, (w_router - w_router_hi.astype(jnp.float32)).astype(bf16)], axis=1),
    ]
    x1, xs, route, counts = _token_mixer(x, attn_o, attn_lse, weights)

    expert_tables, comb = _moe_tables(counts[:, 0, :N_EXPERTS].astype(jnp.int32))
    ys = _experts(expert_tables, xs, w_gate_up[layer], w_down[layer])
    out = _combine(comb, ys, x1, route, norm_f.reshape(1, -1))
    return out.reshape(B, S, D)
```

```python
import functools
import math

import jax
import jax.numpy as jnp
from jax import lax
from jax.experimental import pallas as pl
from jax.experimental.pallas import tpu as pltpu

D_MODEL = 1024
HEAD_DIM = 64
HEADS_PER_GROUP = 4
DILATED_GROUPS = ((128, 1), (512, 4), (2048, 16))
N_GROUPS = len(DILATED_GROUPS)
N_ATTN_HEADS = HEADS_PER_GROUP * N_GROUPS
ATTN_WIDTH = N_ATTN_HEADS * HEAD_DIM
GROUP_WIDTH = HEADS_PER_GROUP * HEAD_DIM
QKV_WIDTH = 3 * ATTN_WIDTH
ATTN_BLOCK = 128
N_REL_BUCKETS = 32
REL_MAX_DISTANCE = 2048
NEG_INF = -1e30

LRU_WIDTH = D_MODEL
LRU_HEADS = 16
LRU_HEAD_DIM = LRU_WIDTH // LRU_HEADS
CONV_WIDTH = 4
LRU_C = 8.0
LRU_CHUNK_HEADS = 4
LRU_CHUNK = LRU_CHUNK_HEADS * LRU_HEAD_DIM
N_LRU_CHUNKS = LRU_WIDTH // LRU_CHUNK

N_EXPERT_GROUPS = 4
EXPERTS_PER_GROUP = 8
N_EXPERTS = N_EXPERT_GROUPS * EXPERTS_PER_GROUP
TOP_K = 2
D_EXPERT = 512
EPS = 1e-6

REST_WIDTH = 2 * LRU_WIDTH + 2 * D_MODEL
LANES = 128
SUBLANES = 8

ATTN_TILE = 256
ATTN_TILES_PER_STEP = 2
ATTN_QB = 8
MIX_TILE = 256
SCAN_SEG = MIX_TILE // SUBLANES
SCAN_PITCH = SCAN_SEG + SUBLANES
MOE_TILE = 256
EXPERT_GATHER_SLOTS = 4
EXPERT_BLOCKS_PER_STEP = 4
COMBINE_TILES_PER_STEP = 4
WEIGHT_DMA_PRIORITY = 1
EXPERT_GROUP = 256
MOE_CHUNK = SUBLANES
MOE_BLOCK_CHUNKS = MOE_TILE // MOE_CHUNK
MOE_SLOTS = 768
assert MOE_SLOTS >= TOP_K * MIX_TILE + N_EXPERTS * (MOE_CHUNK - 1) and MOE_SLOTS % MOE_CHUNK == 0
TILE_CHUNKS = MOE_SLOTS // MOE_CHUNK
VMEM_LIMIT = 56 * 1024 * 1024

ROUTE_S0, ROUTE_S1, ROUTE_G0, ROUTE_G1 = range(4)
GROUP_LOGIT_LANE0 = N_EXPERTS


_DONE = object()


def _rmsnorm(x, g):
    return x * lax.rsqrt(jnp.mean(x * x, axis=-1, keepdims=True) + EPS) * g


def _pack_bf16_pairs(v):
    w = v.shape[1] // 2
    bits = lax.bitcast_convert_type(v, jnp.uint32)
    return bits[:, w:] | (bits[:, :w] >> 16)


def _unpack_bf16_pairs(p):
    lo = lax.bitcast_convert_type(p << 16, jnp.float32)
    hi = lax.bitcast_convert_type(p & jnp.uint32(0xFFFF0000), jnp.float32)
    return lo.astype(jnp.bfloat16), hi.astype(jnp.bfloat16)


def _qkv_kernel(x_ref, g_ref, w_ref, *refs):
    tm = ATTN_TILE
    o_refs, slab_refs = refs[:N_GROUPS], refs[N_GROUPS:]
    n_slab = ATTN_WIDTH // LANES

    def tile(t):
        rows = slice(t * tm, (t + 1) * tm)
        slabs = slab_refs[t]
        h = _rmsnorm(x_ref[rows, :], g_ref[...]).astype(jnp.bfloat16)
        yield
        for gi, (o_ref, (_, d)) in enumerate(zip(o_refs, DILATED_GROUPS)):
            cols = slice(gi * ATTN_WIDTH, (gi + 1) * ATTN_WIDTH)
            part = jnp.dot(h, w_ref[:, cols], preferred_element_type=jnp.float32)
            out_rows = slice(t * (tm // d), (t + 1) * (tm // d))
            if d == 1:
                o_ref[0, out_rows, :] = part.astype(jnp.bfloat16)
                yield
                continue
            for c in range(n_slab):
                slabs[c] = part[:, c * LANES:(c + 1) * LANES]
            yield
            for r in range(d):
                o_ref[r, out_rows, :] = jnp.concatenate(
                    [slabs[c, pl.ds(r, tm // d, stride=d), :] for c in range(n_slab)],
                    axis=1).astype(jnp.bfloat16)
            yield

    waiting = [tile(t) for t in range(ATTN_TILES_PER_STEP)]
    live = []
    while waiting or live:
        if waiting:
            live.append(waiting.pop(0))
        for gen in list(live):
            if next(gen, _DONE) is _DONE:
                live.remove(gen)


def _qkv_projection(x2, norm1, w_qkv, B, S):
    tm = ATTN_TILE * ATTN_TILES_PER_STEP
    nt = S // tm
    return pl.pallas_call(
        _qkv_kernel,
        grid=(B, nt),
        in_specs=[
            pl.BlockSpec((tm, D_MODEL), lambda b, j: (b * nt + j, 0)),
            pl.BlockSpec((1, D_MODEL), lambda b, j: (0, 0)),
            pl.BlockSpec((D_MODEL, QKV_WIDTH), lambda b, j: (0, 0)),
        ],
        out_specs=[pl.BlockSpec((None, d, tm // d, ATTN_WIDTH), lambda b, j: (b, 0, j, 0))
                   for _, d in DILATED_GROUPS],
        out_shape=[jax.ShapeDtypeStruct((B, d, S // d, ATTN_WIDTH), jnp.bfloat16)
                   for _, d in DILATED_GROUPS],
        scratch_shapes=[pltpu.VMEM((ATTN_WIDTH // LANES, ATTN_TILE, LANES), jnp.float32)]
        * ATTN_TILES_PER_STEP,
        compiler_params=pltpu.CompilerParams(
            dimension_semantics=("parallel", "parallel"), vmem_limit_bytes=VMEM_LIMIT),
        name="qkv_projection",
    )(x2, norm1, w_qkv)


def _attn_kernel(n_qb, q_ref, kp_ref, kc_ref, vp_ref, vc_ref, bias0_ref, bias_ref, o_ref, lse_ref):
    q = q_ref[...] * (HEAD_DIM ** -0.5)
    k = jnp.concatenate([kp_ref[...], kc_ref[...]], axis=0)
    v = jnp.concatenate([vp_ref[...], vc_ref[...]], axis=0)
    pair_w = 2 * HEAD_DIM
    first_head = lax.broadcasted_iota(jnp.int32, (1, pair_w), 1) < HEAD_DIM
    def one_head(x, first):
        bits = pltpu.bitcast(x, jnp.uint32)
        keep = first_head if first else jnp.logical_not(first_head)
        return pltpu.bitcast(jnp.where(keep, bits, jnp.uint32(0)), jnp.bfloat16)

    ones = jnp.ones_like(k[:, :pair_w])
    operands = []
    for pair in range(HEADS_PER_GROUP // 2):
        cols = slice(pair * pair_w, (pair + 1) * pair_w)
        k2, v2 = k[:, cols], v[:, cols]
        k_heads = (one_head(k2, True), one_head(k2, False))
        v_heads = (jnp.concatenate([one_head(v2, True), one_head(ones, True)], axis=1),
                   jnp.concatenate([one_head(v2, False), one_head(ones, False)], axis=1))
        operands.append((cols, k_heads, v_heads))
    done = {}

    def head_pair(b, pair):
        rows = slice(b * ATTN_BLOCK, (b + 1) * ATTN_BLOCK)
        keys = slice(b * ATTN_BLOCK, (b + 2) * ATTN_BLOCK)
        cols, k_heads, v_heads = operands[pair]
        bias_ref_b = bias0_ref if b == 0 else bias_ref
        bias = jnp.concatenate([bias_ref_b[2 * pair], bias_ref_b[2 * pair + 1]], axis=1)
        k_both = jnp.concatenate([k_heads[0][keys], k_heads[1][keys]], axis=0)
        s = lax.dot_general(q[rows, cols], k_both, (((1,), (1,)), ((), ())),
                            preferred_element_type=jnp.float32) + bias
        yield
        n_keys = 2 * ATTN_BLOCK
        s0, s1 = s[:, :n_keys], s[:, n_keys:]
        m0 = jnp.max(s0, axis=-1, keepdims=True)
        m1 = jnp.max(s1, axis=-1, keepdims=True)
        p0 = jnp.exp(s0 - m0).astype(jnp.bfloat16)
        p1 = jnp.exp(s1 - m1).astype(jnp.bfloat16)
        yield
        acc = (jnp.dot(p0, v_heads[0][keys], preferred_element_type=jnp.float32)
               + jnp.dot(p1, v_heads[1][keys], preferred_element_type=jnp.float32))
        yield
        denom = acc[:, pair_w:]
        done[b, pair] = (acc[:, :pair_w] / denom, jnp.where(first_head, m0, m1) + jnp.log(denom))

    waiting = [head_pair(b, pair) for b in range(n_qb) for pair in range(HEADS_PER_GROUP // 2)]
    live = []
    while waiting or live:
        if waiting:
            live.append(waiting.pop(0))
        for gen in list(live):
            if next(gen, _DONE) is _DONE:
                live.remove(gen)

    for b in range(n_qb):
        rows = slice(b * ATTN_BLOCK, (b + 1) * ATTN_BLOCK)
        pairs = [done[b, pair] for pair in range(HEADS_PER_GROUP // 2)]
        o_ref[rows, :] = jnp.concatenate([o for o, _ in pairs], axis=1).astype(o_ref.dtype)
        lse_ref[rows, :] = jnp.concatenate([lse for _, lse in pairs], axis=1)


def _attention_group(qkv_g, bias, gi, dilation):
    B, _, L, _ = qkv_g.shape
    n_qb = min(ATTN_QB, L // ATTN_BLOCK)
    rows = n_qb * ATTN_BLOCK
    cur = lambda which: pl.BlockSpec((None, None, rows, GROUP_WIDTH), lambda b, r, n: (b, r, n, which))
    prev = lambda which: pl.BlockSpec((None, None, ATTN_BLOCK, GROUP_WIDTH),
                                      lambda b, r, n: (b, r, jnp.maximum(n * n_qb - 1, 0), which))
    bias_blk = (None, HEADS_PER_GROUP, ATTN_BLOCK, 2 * ATTN_BLOCK)
    return pl.pallas_call(
        functools.partial(_attn_kernel, n_qb),
        grid=(B, dilation, L // rows),
        in_specs=[
            cur(0), prev(1), cur(1), prev(2), cur(2),
            pl.BlockSpec(bias_blk, lambda b, r, n: (jnp.minimum(n, 1), 0, 0, 0)),
            pl.BlockSpec(bias_blk, lambda b, r, n: (1, 0, 0, 0)),
        ],
        out_specs=[cur(0), cur(0)],
        out_shape=[
            jax.ShapeDtypeStruct((B, dilation, L, GROUP_WIDTH), jnp.bfloat16),
            jax.ShapeDtypeStruct((B, dilation, L, GROUP_WIDTH), jnp.float32),
        ],
        compiler_params=pltpu.CompilerParams(
            dimension_semantics=("parallel", "parallel", "arbitrary"), vmem_limit_bytes=VMEM_LIMIT),
        name=f"dilated_attention_g{gi}",
    )(qkv_g, qkv_g, qkv_g, qkv_g, qkv_g, bias, bias)


def _t5_causal_bucket(dist):
    max_exact = N_REL_BUCKETS // 2
    d_f = jnp.maximum(dist, max_exact).astype(jnp.float32)
    large = max_exact + (jnp.log(d_f / max_exact) / math.log(REL_MAX_DISTANCE / max_exact)
                         * (N_REL_BUCKETS - max_exact)).astype(jnp.int32)
    large = jnp.minimum(large, N_REL_BUCKETS - 1)
    return jnp.where(dist < max_exact, dist, large)


def _attention_bias_table(rel_bias_g, window, dilation):
    nw = window // dilation
    qi = jnp.arange(ATTN_BLOCK)[:, None]
    ki = jnp.arange(2 * ATTN_BLOCK)[None, :]
    dist = ATTN_BLOCK + qi - ki
    band = (dist >= 0) & (dist <= nw)
    bucket = _t5_causal_bucket(jnp.maximum(dist, 0) * dilation)
    onehot = (bucket[:, :, None] == jnp.arange(N_REL_BUCKETS)).astype(jnp.float32)
    bias = jnp.einsum('qkb,bh->hqk', onehot, rel_bias_g.astype(jnp.float32),
                      precision=lax.Precision.HIGHEST)
    later = jnp.where(band[None], bias, NEG_INF)
    first = jnp.where((band & (ki >= ATTN_BLOCK))[None], bias, NEG_INF)
    return jnp.stack([first, later])


def _gelu_tanh(x):
    return 0.5 * x * (1.0 + jnp.tanh(math.sqrt(2.0 / math.pi) * (x + 0.044715 * (x * x * x))))


def _softplus(x):
    return jnp.maximum(x, 0.0) + jnp.log(1.0 + jnp.exp(-jnp.abs(x)))


def _natural_order(blk_ref, d, slabs):
    if d == 1:
        return blk_ref[0].astype(jnp.float32)
    per = MIX_TILE // d
    for r in range(d):
        rows = blk_ref[r].astype(jnp.float32)
        for c in range(GROUP_WIDTH // LANES):
            slabs[c, pl.ds(r, per, stride=d), :] = rows[:, c * LANES:(c + 1) * LANES]
    return jnp.concatenate([slabs[c] for c in range(GROUP_WIDTH // LANES)], axis=1)


def _mix_tile(x_ref, o1_ref, o2_ref, o3_ref, l1_ref, l2_ref, l3_ref,
              n1_ref, win_ref, cw_ref, cb_ref, wg_ref, brg_ref, big_ref, lam_ref,
              wpa_ref, wpl_ref, wout_ref, n2_ref, wr_ref,
              x1_ref, xs_ref, route_ref, cnt_ref,
              xbuf, a_sc, b_sc, h_sc, *slabs):
    tm = MIX_TILE
    first_tile = pl.program_id(0) == 0

    @pl.when(first_tile)
    def _():
        xbuf[0:SUBLANES, :] = jnp.zeros((SUBLANES, LRU_WIDTH), jnp.float32)
        h_sc[...] = jnp.zeros_like(h_sc)

    x = x_ref[...]
    h = _rmsnorm(x, n1_ref[...]).astype(jnp.bfloat16)
    def project(first_col):
        return jnp.dot(h, win_ref[:, first_col:first_col + D_MODEL], preferred_element_type=jnp.float32)

    xr = project(0)

    yield
    g_lru = project(LRU_WIDTH)
    xbuf[SUBLANES:SUBLANES + tm, :] = xr
    xc = xr * cw_ref[CONV_WIDTH - 1:CONV_WIDTH, :] + cb_ref[...]
    for j in range(CONV_WIDTH - 1):
        back = CONV_WIDTH - 1 - j
        xc = xc + xbuf[SUBLANES - back:SUBLANES - back + tm, :] * cw_ref[j:j + 1, :]
    xbuf[0:SUBLANES, :] = xbuf[tm:tm + SUBLANES, :]

    xcb = xc.astype(jnp.bfloat16)
    r_parts, i_parts = [], []
    for c in range(N_LRU_CHUNKS):
        ri = jnp.dot(xcb[:, c * LRU_CHUNK:(c + 1) * LRU_CHUNK], wg_ref[c],
                     preferred_element_type=jnp.float32)
        r_parts.append(ri[:, :LRU_CHUNK])
        i_parts.append(ri[:, LRU_CHUNK:])
    yield
    g_a = project(2 * LRU_WIDTH)
    r = jax.nn.sigmoid(jnp.concatenate(r_parts, axis=1) + brg_ref[...])
    ig = jax.nn.sigmoid(jnp.concatenate(i_parts, axis=1) + big_ref[...])
    log_a = (-LRU_C * _softplus(-lam_ref[...])) * r
    a = jnp.exp(log_a)
    gap = 1.0 - a * a
    root = jnp.where(gap > 0.0, gap * lax.rsqrt(gap), 0.0)
    bb = root * (ig * xc)

    yield
    g_b = project(2 * LRU_WIDTH + D_MODEL)
    n_slab = LRU_WIDTH // LANES
    for c in range(n_slab):
        lanes = slice(c * LANES, (c + 1) * LANES)
        for s in range(SUBLANES):
            rows = slice(s * SCAN_SEG, (s + 1) * SCAN_SEG)
            a_sc[c, s * SCAN_PITCH:s * SCAN_PITCH + SCAN_SEG, :] = a[rows, lanes]
            b_sc[c, s * SCAN_PITCH:s * SCAN_PITCH + SCAN_SEG, :] = bb[rows, lanes]

    yield
    local = [jnp.zeros((SUBLANES, LANES), jnp.float32)] * n_slab
    decay = [jnp.ones((SUBLANES, LANES), jnp.float32)] * n_slab
    for j in range(SCAN_SEG):
        for c in range(n_slab):
            step = pl.ds(j, SUBLANES, stride=SCAN_PITCH)
            a_j = a_sc[c, step, :]
            local[c] = a_j * local[c] + b_sc[c, step, :]
            decay[c] = a_j * decay[c]
            b_sc[c, step, :] = local[c]
            a_sc[c, step, :] = decay[c]

    yield
    h_parts = []
    for c in range(n_slab):
        lanes = slice(c * LANES, (c + 1) * LANES)
        entry = h_sc[:, lanes]
        segs = []
        for s in range(SUBLANES):
            rows = slice(s * SCAN_PITCH, s * SCAN_PITCH + SCAN_SEG)
            segs.append(b_sc[c, rows, :] + a_sc[c, rows, :] * entry)
            entry = decay[c][s:s + 1, :] * entry + local[c][s:s + 1, :]
        h_sc[:, lanes] = entry
        h_parts.append(jnp.concatenate(segs, axis=0))
    h_all = jnp.concatenate(h_parts, axis=1)
    yield
    lru = (h_all * _gelu_tanh(g_lru)).astype(jnp.bfloat16)
    p_lru = jnp.dot(lru, wpl_ref[...], preferred_element_type=jnp.float32)

    yield
    dil = [d for _, d in DILATED_GROUPS]
    o1, o2, o3 = (_natural_order(ref, d, slabs[2 * g]) for g, (ref, d) in
                  enumerate(zip((o1_ref, o2_ref, o3_ref), dil)))
    l1, l2, l3 = (_natural_order(ref, d, slabs[2 * g + 1]) for g, (ref, d) in
                  enumerate(zip((l1_ref, l2_ref, l3_ref), dil)))
    lm = jnp.maximum(jnp.maximum(l1, l2), l3)
    e1, e2, e3 = jnp.exp(l1 - lm), jnp.exp(l2 - lm), jnp.exp(l3 - lm)
    attn = (e1 * o1 + e2 * o2 + e3 * o3) / (e1 + e2 + e3)
    p_attn = jnp.dot(attn.astype(jnp.bfloat16), wpa_ref[...], preferred_element_type=jnp.float32)

    yield
    merged = jax.nn.sigmoid(g_a) * p_attn + jax.nn.sigmoid(g_b) * p_lru
    x1 = x + jnp.dot(merged.astype(jnp.bfloat16), wout_ref[...], preferred_element_type=jnp.float32)
    x1_ref[...] = x1
    h2 = _rmsnorm(x1, n2_ref[...])

    yield
    h2_hi = h2.astype(jnp.bfloat16)
    h2_lo = (h2 - h2_hi.astype(jnp.float32)).astype(jnp.bfloat16)
    by_hi = jnp.dot(h2_hi, wr_ref[...], preferred_element_type=jnp.float32)
    by_lo = jnp.dot(h2_lo, wr_ref[...], preferred_element_type=jnp.float32)
    logits = (by_hi[:, :LANES] + (by_hi[:, LANES:] + by_lo[:, :LANES])) + by_lo[:, LANES:]
    lane = lax.broadcasted_iota(jnp.int32, (tm, LANES), 1)
    big = jnp.int32(LANES)
    lowest = jnp.float32(-3.0e38)
    is_g = (lane >= GROUP_LOGIT_LANE0) & (lane < GROUP_LOGIT_LANE0 + N_EXPERT_GROUPS)
    gl = jnp.where(is_g, logits, lowest)
    gmax = jnp.max(gl, axis=-1, keepdims=True)
    gsel = jnp.min(jnp.where(gl == gmax, lane, big), axis=-1, keepdims=True) - GROUP_LOGIT_LANE0
    p_sel = 1.0 / jnp.sum(jnp.where(is_g, jnp.exp(gl - gmax), 0.0), axis=-1, keepdims=True)
    is_e = (lane >= gsel * EXPERTS_PER_GROUP) & (lane < (gsel + 1) * EXPERTS_PER_GROUP)
    el = jnp.where(is_e, logits, lowest)
    v1 = jnp.max(el, axis=-1, keepdims=True)
    i1 = jnp.min(jnp.where(el == v1, lane, big), axis=-1, keepdims=True)
    el2 = jnp.where(lane == i1, lowest, el)
    v2 = jnp.max(el2, axis=-1, keepdims=True)
    i2 = jnp.min(jnp.where(el2 == v2, lane, big), axis=-1, keepdims=True)
    t = jnp.exp(v2 - v1)
    g0 = p_sel / (1.0 + t)
    g1 = p_sel * t / (1.0 + t)

    yield
    oh0 = lane == i1
    oh1 = lane == i2
    both = (oh0 | oh1).astype(jnp.float32)
    tri = (lax.broadcasted_iota(jnp.int32, (tm, tm), 0)
           > lax.broadcasted_iota(jnp.int32, (tm, tm), 1)).astype(jnp.bfloat16)
    before = jnp.dot(tri, both.astype(jnp.bfloat16), preferred_element_type=jnp.float32)
    cnt = jnp.sum(both, axis=0, keepdims=True)
    padded = jnp.floor((cnt + (MOE_CHUNK - 1.0)) * (1.0 / MOE_CHUNK)) * MOE_CHUNK
    upper = (lax.broadcasted_iota(jnp.int32, (LANES, LANES), 0)
             < lax.broadcasted_iota(jnp.int32, (LANES, LANES), 1)).astype(jnp.bfloat16)
    loff = jnp.dot(jnp.broadcast_to(padded, (SUBLANES, LANES)).astype(jnp.bfloat16), upper,
                   preferred_element_type=jnp.float32)[0:1, :]
    base = before + loff
    slot0 = jnp.sum(jnp.where(oh0, base, 0.0), axis=-1, keepdims=True)
    slot1 = jnp.sum(jnp.where(oh1, base, 0.0), axis=-1, keepdims=True)
    cnt_ref[...] = jnp.broadcast_to(cnt, cnt_ref.shape)

    slot_id = lax.broadcasted_iota(jnp.int32, (tm, MOE_SLOTS), 1)
    place = ((slot_id == slot0.astype(jnp.int32)) | (slot_id == slot1.astype(jnp.int32)))
    xs_ref[...] = _pack_bf16_pairs(
        lax.dot_general(place.astype(jnp.bfloat16), h2_hi, (((0,), (0,)), ((), ())),
                        preferred_element_type=jnp.float32))

    route = jnp.zeros((tm, LANES), jnp.float32)
    for lane_id, val in ((ROUTE_S0, slot0), (ROUTE_S1, slot1), (ROUTE_G0, g0), (ROUTE_G1, g1)):
        route = jnp.where(lane == lane_id, val, route)
    route_ref[...] = route


N_MIXER_TILED_INPUTS = 1 + 2 * N_GROUPS
N_MIXER_OUTPUTS = 4


def _mixer_kernel(*refs):
    n_batch = refs[0].shape[0]
    n_weights = len(refs) - N_MIXER_TILED_INPUTS - N_MIXER_OUTPUTS - n_batch * _MIXER_SCRATCH_PER_SEQ
    tiled = refs[:N_MIXER_TILED_INPUTS]
    weights = refs[N_MIXER_TILED_INPUTS:N_MIXER_TILED_INPUTS + n_weights]
    outs = refs[N_MIXER_TILED_INPUTS + n_weights:N_MIXER_TILED_INPUTS + n_weights + N_MIXER_OUTPUTS]
    scratch = refs[N_MIXER_TILED_INPUTS + n_weights + N_MIXER_OUTPUTS:]
    stages = []
    for b in range(n_batch):
        own = scratch[b * _MIXER_SCRATCH_PER_SEQ:(b + 1) * _MIXER_SCRATCH_PER_SEQ]
        stages.append(_mix_tile(*[r.at[b] for r in tiled], *weights, *[r.at[b] for r in outs], *own))
    while stages:
        for gen in list(stages):
            if next(gen, _DONE) is _DONE:
                stages.remove(gen)


def _mixer_scratch(tm):
    return [
        pltpu.VMEM((tm + 2 * SUBLANES, LRU_WIDTH), jnp.float32),
        pltpu.VMEM((LRU_WIDTH // LANES, SUBLANES * SCAN_PITCH, LANES), jnp.float32),
        pltpu.VMEM((LRU_WIDTH // LANES, SUBLANES * SCAN_PITCH, LANES), jnp.float32),
        pltpu.VMEM((1, LRU_WIDTH), jnp.float32),
    ] + [pltpu.VMEM((GROUP_WIDTH // LANES, tm, LANES), jnp.float32)] * (2 * N_GROUPS)


_MIXER_SCRATCH_PER_SEQ = len(_mixer_scratch(MIX_TILE))


def _token_mixer(x, attn_o, attn_lse, weights):
    B, S, _ = x.shape
    tm = MIX_TILE
    nt = S // tm
    rows = lambda j: (0, j, 0)

    def resident(shape):
        return pl.BlockSpec(shape, lambda j: (0,) * len(shape), pipeline_mode=pl.Buffered(1))

    in_specs = [pl.BlockSpec((B, tm, D_MODEL), rows)]
    group_specs = [pl.BlockSpec((B, d, tm // d, GROUP_WIDTH), lambda j: (0, 0, j, 0))
                   for _, d in DILATED_GROUPS]
    in_specs += group_specs + group_specs
    in_specs += [resident(w.shape) for w in weights]
    x1, xs, route, counts = pl.pallas_call(
        _mixer_kernel,
        grid=(nt,),
        in_specs=in_specs,
        out_specs=[
            pl.BlockSpec((B, tm, D_MODEL), rows),
            pl.BlockSpec((B, MOE_SLOTS, D_MODEL // 2), rows),
            pl.BlockSpec((B, tm, LANES), rows),
            pl.BlockSpec((B, None, SUBLANES, LANES), lambda j: (0, j, 0, 0)),
        ],
        out_shape=[
            jax.ShapeDtypeStruct((B, S, D_MODEL), jnp.float32),
            jax.ShapeDtypeStruct((B, nt * MOE_SLOTS, D_MODEL // 2), jnp.uint32),
            jax.ShapeDtypeStruct((B, S, LANES), jnp.float32),
            jax.ShapeDtypeStruct((B, nt, SUBLANES, LANES), jnp.float32),
        ],
        scratch_shapes=_mixer_scratch(tm) * B,
        compiler_params=pltpu.CompilerParams(
            dimension_semantics=("arbitrary",), vmem_limit_bytes=VMEM_LIMIT),
        name="token_mixer",
    )(x, *attn_o, *attn_lse, *weights)
    return (x1.reshape(B * S, D_MODEL), xs.reshape(B * nt * MOE_SLOTS, D_MODEL // 2),
            route.reshape(B * S, LANES), counts.reshape(B * nt, SUBLANES, LANES))


def _moe_tables(cnt_tile):
    ntiles = cnt_tile.shape[0]
    nch = (cnt_tile + MOE_CHUNK - 1) // MOE_CHUNK
    lo_c = jnp.cumsum(nch, axis=1) - nch
    per_expert = jnp.sum(nch, axis=0)
    region = ((per_expert + MOE_BLOCK_CHUNKS - 1) // MOE_BLOCK_CHUNKS) * MOE_BLOCK_CHUNKS
    pend = jnp.cumsum(region)
    glob = (pend - region)[None, :] + jnp.cumsum(nch, axis=0) - nch

    max_rows = TOP_K * ntiles * MIX_TILE + ntiles * N_EXPERTS * (MOE_CHUNK - 1)
    max_chunks = -(-max_rows // MOE_CHUNK) + N_EXPERTS * (MOE_BLOCK_CHUNKS - 1)
    max_blocks = -(-max_chunks // MOE_BLOCK_CHUNKS)
    max_blocks = -(-max_blocks // EXPERT_BLOCKS_PER_STEP) * EXPERT_BLOCKS_PER_STEP
    max_chunks = max_blocks * MOE_BLOCK_CHUNKS

    seg_start = glob.T.reshape(-1)
    seg_src = (jnp.arange(ntiles, dtype=jnp.int32)[:, None] * TILE_CHUNKS + lo_c).T.reshape(-1)
    step = jnp.diff(seg_src - seg_start, prepend=0)
    g = jnp.arange(max_chunks, dtype=jnp.int32)
    src = g + jnp.sum(jnp.where(seg_start[None, :] <= g[:, None], step[None, :], 0), axis=1)
    src = jnp.clip(src, 0, ntiles * TILE_CHUNKS - 1)

    lstep = jnp.diff(glob - lo_c, axis=1, prepend=0)
    lc = jnp.arange(TILE_CHUNKS, dtype=jnp.int32)
    comb = lc[None, :] + jnp.sum(
        jnp.where(lo_c[:, None, :] <= lc[None, :, None], lstep[:, None, :], 0), axis=2)
    comb = jnp.clip(comb, 0, max_chunks - 1).reshape(-1)

    eidx = jnp.arange(N_EXPERTS, dtype=jnp.int32)
    nonempty = region > 0
    later = jnp.where(nonempty[None, :] & (eidx[None, :] > eidx[:, None]), eidx[None, :], N_EXPERTS)
    next_expert = jnp.min(later, axis=1)
    next_expert = jnp.where(next_expert == N_EXPERTS, -1, next_expert)
    wslot = (jnp.cumsum(nonempty.astype(jnp.int32)) - nonempty.astype(jnp.int32)) % 2
    last_expert = jnp.max(jnp.where(nonempty, eidx, 0))
    blk0 = jnp.arange(max_blocks, dtype=jnp.int32) * MOE_BLOCK_CHUNKS
    block_expert = jnp.minimum(jnp.sum((pend[None, :] <= blk0[:, None]).astype(jnp.int32), axis=1),
                               last_expert)
    of_block = lambda per_expert: jnp.sum(
        jnp.where(block_expert[:, None] == eidx[None, :], per_expert[None, :], 0), axis=1)
    n_valid = pend[-1:] // MOE_BLOCK_CHUNKS
    as_i32 = lambda v: v.astype(jnp.int32)
    expert_tables = (as_i32(block_expert), as_i32(of_block(next_expert)), as_i32(of_block(wslot)),
                     as_i32(n_valid), as_i32(src))
    return expert_tables, as_i32(comb)


def _chunk_copy(src_hbm, src_chunk, dst, dst_chunk, sem):
    rows = lambda c: pl.ds(pl.multiple_of(c * MOE_CHUNK, MOE_CHUNK), MOE_CHUNK)
    return pltpu.make_async_copy(src_hbm.at[rows(src_chunk), :], dst.at[rows(dst_chunk), :], sem)


def _expert_kernel(be_ref, nxt_ref, ws_ref, nv_ref, src_ref, xs_hbm, wgu_hbm, wd_hbm, ys_ref, *scratch):
    for k in range(EXPERT_BLOCKS_PER_STEP):
        _expert_block(pl.program_id(0) * EXPERT_BLOCKS_PER_STEP + k,
                      be_ref, nxt_ref, ws_ref, nv_ref, src_ref, xs_hbm, wgu_hbm, wd_hbm,
                      ys_ref.at[pl.ds(k * MOE_TILE, MOE_TILE), :], *scratch)


def _expert_block(i, be_ref, nxt_ref, ws_ref, nv_ref, src_ref, xs_hbm, wgu_hbm, wd_hbm, ys_ref,
                  xbuf, wgu_f32, wd_f32, wgu_bf, wd_bf, sems, wsems):
    valid = i < nv_ref[0]

    def gather(blk, slot):
        return [_chunk_copy(xs_hbm, src_ref[blk * MOE_BLOCK_CHUNKS + c], xbuf.at[slot], c, sems.at[slot])
                for c in range(MOE_BLOCK_CHUNKS)]

    def fetch_weights(e, slot):
        return [pltpu.make_async_copy(wgu_hbm.at[e], wgu_f32.at[slot], wsems.at[slot]),
                pltpu.make_async_copy(wd_hbm.at[e], wd_f32.at[slot], wsems.at[slot])]

    ahead = EXPERT_GATHER_SLOTS - 1

    @pl.when(i == 0)
    def _():
        for cp in fetch_weights(be_ref[0], 0):
            cp.start(priority=WEIGHT_DMA_PRIORITY)
        for first in range(ahead):
            @pl.when(first < nv_ref[0])
            def _():
                for cp in gather(first, first):
                    cp.start()

    @pl.when(i + ahead < nv_ref[0])
    def _():
        for cp in gather(i + ahead, (i + ahead) % EXPERT_GATHER_SLOTS):
            cp.start()

    new_expert = valid & ((i == 0) | (be_ref[i] != be_ref[jnp.maximum(i - 1, 0)]))

    @pl.when(new_expert)
    def _():
        ws = ws_ref[i]
        for cp in fetch_weights(be_ref[i], ws):
            cp.wait()
        wgu_bf[...] = wgu_f32[ws].astype(jnp.bfloat16)
        wd_bf[...] = wd_f32[ws].astype(jnp.bfloat16)

        @pl.when(nxt_ref[i] >= 0)
        def _():
            for cp in fetch_weights(nxt_ref[i], 1 - ws):
                cp.start(priority=WEIGHT_DMA_PRIORITY)

    @pl.when(valid)
    def _():
        slot = i % EXPERT_GATHER_SLOTS
        for cp in gather(i, slot):
            cp.wait()
        x_lo, x_hi = _unpack_bf16_pairs(xbuf[slot])
        half = D_MODEL // 2
        partial = []

        def up_proj(cols):
            return (jnp.dot(x_lo, wgu_bf[:half, cols], preferred_element_type=jnp.float32)
                    + jnp.dot(x_hi, wgu_bf[half:, cols], preferred_element_type=jnp.float32))

        def hidden_group(c):
            cols = slice(c * EXPERT_GROUP, (c + 1) * EXPERT_GROUP)
            g = up_proj(cols)
            u = up_proj(slice(D_EXPERT + cols.start, D_EXPERT + cols.stop))
            yield
            act = (g * jax.nn.sigmoid(g) * u).astype(jnp.bfloat16)
            yield
            partial.append(jnp.dot(act, wd_bf[cols, :], preferred_element_type=jnp.float32))

        waiting = [hidden_group(c) for c in range(D_EXPERT // EXPERT_GROUP)]
        live = []
        while waiting or live:
            if waiting:
                live.append(waiting.pop(0))
            for gen in list(live):
                if next(gen, _DONE) is _DONE:
                    live.remove(gen)
        y = sum(partial[1:], partial[0])
        ys_ref[...] = _pack_bf16_pairs(y.astype(jnp.bfloat16).astype(jnp.float32))

    @pl.when(jnp.logical_not(valid))
    def _():
        ys_ref[...] = jnp.zeros_like(ys_ref)


def _experts(tables, xs, w_gate_up, w_down):
    block_expert, block_next, block_wslot, n_valid, src = tables
    nblk = block_expert.shape[0]
    return pl.pallas_call(
        _expert_kernel,
        grid_spec=pltpu.PrefetchScalarGridSpec(
            num_scalar_prefetch=5,
            grid=(nblk // EXPERT_BLOCKS_PER_STEP,),
            in_specs=[pl.BlockSpec(memory_space=pl.ANY)] * 3,
            out_specs=pl.BlockSpec((EXPERT_BLOCKS_PER_STEP * MOE_TILE, D_MODEL // 2),
                                   lambda i, *_: (i, 0)),
            scratch_shapes=[
                pltpu.VMEM((EXPERT_GATHER_SLOTS, MOE_TILE, D_MODEL // 2), jnp.uint32),
                pltpu.VMEM((2, D_MODEL, 2 * D_EXPERT), jnp.float32),
                pltpu.VMEM((2, D_EXPERT, D_MODEL), jnp.float32),
                pltpu.VMEM((D_MODEL, 2 * D_EXPERT), jnp.bfloat16),
                pltpu.VMEM((D_EXPERT, D_MODEL), jnp.bfloat16),
                pltpu.SemaphoreType.DMA((EXPERT_GATHER_SLOTS,)),
                pltpu.SemaphoreType.DMA((2,)),
            ],
        ),
        out_shape=jax.ShapeDtypeStruct((nblk * MOE_TILE, D_MODEL // 2), jnp.uint32),
        compiler_params=pltpu.CompilerParams(
            dimension_semantics=("arbitrary",), vmem_limit_bytes=VMEM_LIMIT),
        name="moe_experts",
    )(block_expert, block_next, block_wslot, n_valid, src, xs, w_gate_up, w_down)


def _combine_kernel(comb_ref, ys_hbm, x1_ref, route_ref, nf_ref, out_ref, ybuf, sems):
    tc = MIX_TILE
    per_step = COMBINE_TILES_PER_STEP
    i = pl.program_id(0)
    last = pl.num_programs(0) - 1

    def gather(step, slot):
        first = step * per_step * TILE_CHUNKS
        return [_chunk_copy(ys_hbm, comb_ref[first + lc], ybuf.at[slot], lc, sems.at[slot])
                for lc in range(per_step * TILE_CHUNKS)]

    @pl.when(i == 0)
    def _():
        for cp in gather(0, 0):
            cp.start()

    nxt = jnp.minimum(i + 1, last)
    for cp in gather(nxt, (i + 1) % 2):
        cp.start()

    slot = i % 2
    for cp in gather(i, slot):
        cp.wait()

    def tile(t):
        rows = slice(t * tc, (t + 1) * tc)
        y_lo, y_hi = _unpack_bf16_pairs(ybuf[slot, t * MOE_SLOTS:(t + 1) * MOE_SLOTS, :])
        route = route_ref[rows, :]
        slot_id = lax.broadcasted_iota(jnp.int32, (tc, MOE_SLOTS), 1)
        picked = []
        for lane in (ROUTE_S0, ROUTE_S1):
            sel = (slot_id == route[:, lane:lane + 1].astype(jnp.int32)).astype(jnp.bfloat16)
            yield
            picked.append(jnp.concatenate([jnp.dot(sel, y_lo, preferred_element_type=jnp.float32),
                                           jnp.dot(sel, y_hi, preferred_element_type=jnp.float32)], axis=1))
            yield
        g0 = route[:, ROUTE_G0:ROUTE_G0 + 1]
        g1 = route[:, ROUTE_G1:ROUTE_G1 + 1]
        out_ref[rows, :] = _rmsnorm(x1_ref[rows, :] + g0 * picked[0] + g1 * picked[1], nf_ref[...])

    waiting = [tile(t) for t in range(per_step)]
    live = []
    while waiting or live:
        if waiting:
            live.append(waiting.pop(0))
        for gen in list(live):
            if next(gen, _DONE) is _DONE:
                live.remove(gen)

    @pl.when(i == last)
    def _():
        for cp in gather(nxt, (i + 1) % 2):
            cp.wait()


def _combine(comb, ys, x1, route, norm_f):
    T = x1.shape[0]
    tc = MIX_TILE * COMBINE_TILES_PER_STEP
    return pl.pallas_call(
        _combine_kernel,
        grid_spec=pltpu.PrefetchScalarGridSpec(
            num_scalar_prefetch=1,
            grid=(T // tc,),
            in_specs=[
                pl.BlockSpec(memory_space=pl.ANY),
                pl.BlockSpec((tc, D_MODEL), lambda i, c: (i, 0)),
                pl.BlockSpec((tc, LANES), lambda i, c: (i, 0)),
                pl.BlockSpec((1, D_MODEL), lambda i, c: (0, 0)),
            ],
            out_specs=pl.BlockSpec((tc, D_MODEL), lambda i, c: (i, 0)),
            scratch_shapes=[
                pltpu.VMEM((2, COMBINE_TILES_PER_STEP * MOE_SLOTS, D_MODEL // 2), jnp.uint32),
                pltpu.SemaphoreType.DMA((2,)),
            ],
        ),
        out_shape=jax.ShapeDtypeStruct((T, D_MODEL), jnp.float32),
        compiler_params=pltpu.CompilerParams(
            dimension_semantics=("arbitrary",), vmem_limit_bytes=VMEM_LIMIT),
        name="moe_combine",
    )(comb, ys, x1, route, norm_f)


def _block_diag_gates(w_rg, w_ig):
    def bd(w):
        w4 = w.reshape(N_LRU_CHUNKS, LRU_CHUNK_HEADS, LRU_HEAD_DIM, LRU_HEAD_DIM)
        eye = jnp.eye(LRU_CHUNK_HEADS, dtype=w.dtype)
        return jnp.einsum('chij,hk->chikj', w4, eye).reshape(N_LRU_CHUNKS, LRU_CHUNK, LRU_CHUNK)
    return jnp.concatenate([bd(w_rg), bd(w_ig)], axis=-1).astype(jnp.bfloat16)


def kernel(x, rel_bias, norm1, w_in, conv_w, conv_b, w_rg, b_rg, w_ig, b_ig, lru_lambda,
           w_proj_attn, w_proj_lru, w_out, norm2, w_router_group, w_router_expert,
           w_gate_up, w_down, norm_f):
    B, S, D = x.shape
    T = B * S
    assert w_in.shape[0] == 1, "single-layer block"
    layer = 0
    bf16 = jnp.bfloat16
    x2 = x.reshape(T, D)
    row = lambda v: v[layer].reshape(1, -1)
    w_in_bf = w_in[layer].astype(bf16)
    w_qkv = (w_in_bf[:, :QKV_WIDTH].reshape(D, 3, N_GROUPS, GROUP_WIDTH)
             .transpose(0, 2, 1, 3).reshape(D, QKV_WIDTH))
    w_rest = w_in_bf[:, QKV_WIDTH:]

    qkv_groups = _qkv_projection(x2, row(norm1), w_qkv, B, S)
    attn_o, attn_lse = [], []
    for gi, (window, dilation) in enumerate(DILATED_GROUPS):
        hs = slice(gi * HEADS_PER_GROUP, (gi + 1) * HEADS_PER_GROUP)
        bias = _attention_bias_table(rel_bias[:, hs], window, dilation)
        o, lse = _attention_group(qkv_groups[gi], bias, gi, dilation)
        attn_o.append(o)
        attn_lse.append(lse)

    w_router = jnp.concatenate(
        [w_router_expert[layer].astype(jnp.float32), w_router_group[layer].astype(jnp.float32),
         jnp.zeros((D, LANES - N_EXPERTS - N_EXPERT_GROUPS), jnp.float32)], axis=1)
    w_router_hi = w_router.astype(bf16)
    weights = [
        row(norm1), w_rest, conv_w[layer], row(conv_b),
        _block_diag_gates(w_rg[layer], w_ig[layer]), row(b_rg), row(b_ig), row(lru_lambda),
        w_proj_attn[layer].astype(bf16), w_proj_lru[layer].astype(bf16), w_out[layer].astype(bf16),
        row(norm2),
        jnp.concatenate([w_router_hi, (w_router - w_router_hi.astype(jnp.float32)).astype(bf16)], axis=1),
    ]
    x1, xs, route, counts = _token_mixer(x, attn_o, attn_lse, weights)

    expert_tables, comb = _moe_tables(counts[:, 0, :N_EXPERTS].astype(jnp.int32))
    ys = _experts(expert_tables, xs, w_gate_up[layer], w_down[layer])
    out = _combine(comb, ys, x1, route, norm_f.reshape(1, -1))
    return out.reshape(B, S, D)
```

```python
import functools
import math

import jax
import jax.numpy as jnp
import numpy as np
from jax import lax
from jax.experimental import pallas as pl
from jax.experimental.pallas import tpu as pltpu

D_MODEL = 1024
HEAD_DIM = 64
HEADS_PER_GROUP = 4
DILATED_GROUPS = ((128, 1), (512, 4), (2048, 16))
N_GROUPS = len(DILATED_GROUPS)
N_ATTN_HEADS = HEADS_PER_GROUP * N_GROUPS
ATTN_WIDTH = N_ATTN_HEADS * HEAD_DIM
GROUP_WIDTH = HEADS_PER_GROUP * HEAD_DIM
QKV_WIDTH = 3 * ATTN_WIDTH
ATTN_BLOCK = 128
N_REL_BUCKETS = 32
REL_MAX_DISTANCE = 2048
NEG_INF = -1e30

LRU_WIDTH = D_MODEL
LRU_HEADS = 16
LRU_HEAD_DIM = LRU_WIDTH // LRU_HEADS
CONV_WIDTH = 4
LRU_C = 8.0
LRU_CHUNK_HEADS = 4
LRU_CHUNK = LRU_CHUNK_HEADS * LRU_HEAD_DIM
N_LRU_CHUNKS = LRU_WIDTH // LRU_CHUNK

N_EXPERT_GROUPS = 4
EXPERTS_PER_GROUP = 8
N_EXPERTS = N_EXPERT_GROUPS * EXPERTS_PER_GROUP
TOP_K = 2
D_EXPERT = 512
EPS = 1e-6

REST_WIDTH = 2 * LRU_WIDTH + 2 * D_MODEL
LANES = 128
SUBLANES = 8

ATTN_TILE = 256
ATTN_TILES_PER_STEP = 2
ATTN_QB = 8
MIX_TILE = 256
SCAN_SEG = MIX_TILE // SUBLANES
SCAN_PITCH = SCAN_SEG + SUBLANES
MOE_TILE = 256
EXPERT_GATHER_SLOTS = 4
EXPERT_BLOCKS_PER_STEP = 8
COMBINE_TILES_PER_STEP = 4
WEIGHT_DMA_PRIORITY = 1
EXPERT_GROUP = 256
MOE_CHUNK = SUBLANES
MOE_BLOCK_CHUNKS = MOE_TILE // MOE_CHUNK
MOE_SLOTS = 768
assert MOE_SLOTS >= TOP_K * MIX_TILE + N_EXPERTS * (MOE_CHUNK - 1) and MOE_SLOTS % MOE_CHUNK == 0
TILE_CHUNKS = MOE_SLOTS // MOE_CHUNK
VMEM_LIMIT = 56 * 1024 * 1024

ROUTE_S0, ROUTE_S1, ROUTE_G0, ROUTE_G1 = range(4)
GROUP_LOGIT_LANE0 = N_EXPERTS


_DONE = object()


def _rmsnorm(x, g):
    return x * lax.rsqrt(jnp.mean(x * x, axis=-1, keepdims=True) + EPS) * g


def _pack_bf16_pairs(v):
    w = v.shape[1] // 2
    bits = lax.bitcast_convert_type(v, jnp.uint32)
    return bits[:, w:] | (bits[:, :w] >> 16)


def _unpack_bf16_pairs(p):
    lo = lax.bitcast_convert_type(p << 16, jnp.float32)
    hi = lax.bitcast_convert_type(p & jnp.uint32(0xFFFF0000), jnp.float32)
    return lo.astype(jnp.bfloat16), hi.astype(jnp.bfloat16)


def _qkv_kernel(x_ref, g_ref, w_ref, *refs):
    tm = ATTN_TILE
    o_refs, slab_refs = refs[:N_GROUPS], refs[N_GROUPS:]
    n_slab = ATTN_WIDTH // LANES

    def tile(t):
        rows = slice(t * tm, (t + 1) * tm)
        slabs = slab_refs[t]
        h = _rmsnorm(x_ref[rows, :], g_ref[...]).astype(jnp.bfloat16)
        yield
        for gi, (o_ref, (_, d)) in enumerate(zip(o_refs, DILATED_GROUPS)):
            cols = slice(gi * ATTN_WIDTH, (gi + 1) * ATTN_WIDTH)
            part = jnp.dot(h, w_ref[:, cols], preferred_element_type=jnp.float32)
            out_rows = slice(t * (tm // d), (t + 1) * (tm // d))
            if d == 1:
                o_ref[0, out_rows, :] = part.astype(jnp.bfloat16)
                yield
                continue
            for c in range(n_slab):
                slabs[c] = part[:, c * LANES:(c + 1) * LANES]
            yield
            for r in range(d):
                o_ref[r, out_rows, :] = jnp.concatenate(
                    [slabs[c, pl.ds(r, tm // d, stride=d), :] for c in range(n_slab)],
                    axis=1).astype(jnp.bfloat16)
            yield

    waiting = [tile(t) for t in range(ATTN_TILES_PER_STEP)]
    live = []
    while waiting or live:
        if waiting:
            live.append(waiting.pop(0))
        for gen in list(live):
            if next(gen, _DONE) is _DONE:
                live.remove(gen)


def _qkv_projection(x2, norm1, w_qkv, B, S):
    tm = ATTN_TILE * ATTN_TILES_PER_STEP
    nt = S // tm
    return pl.pallas_call(
        _qkv_kernel,
        grid=(B, nt),
        in_specs=[
            pl.BlockSpec((tm, D_MODEL), lambda b, j: (b * nt + j, 0)),
            pl.BlockSpec((1, D_MODEL), lambda b, j: (0, 0)),
            pl.BlockSpec((D_MODEL, QKV_WIDTH), lambda b, j: (0, 0)),
        ],
        out_specs=[pl.BlockSpec((None, d, tm // d, ATTN_WIDTH), lambda b, j: (b, 0, j, 0))
                   for _, d in DILATED_GROUPS],
        out_shape=[jax.ShapeDtypeStruct((B, d, S // d, ATTN_WIDTH), jnp.bfloat16)
                   for _, d in DILATED_GROUPS],
        scratch_shapes=[pltpu.VMEM((ATTN_WIDTH // LANES, ATTN_TILE, LANES), jnp.float32)]
        * ATTN_TILES_PER_STEP,
        compiler_params=pltpu.CompilerParams(
            dimension_semantics=("parallel", "parallel"), vmem_limit_bytes=VMEM_LIMIT),
        name="qkv_projection",
    )(x2, norm1, w_qkv)


def _attn_kernel(n_qb, q_ref, kp_ref, kc_ref, vp_ref, vc_ref, bias0_ref, bias_ref, o_ref, lse_ref):
    q = q_ref[...] * (HEAD_DIM ** -0.5)
    k = jnp.concatenate([kp_ref[...], kc_ref[...]], axis=0)
    v = jnp.concatenate([vp_ref[...], vc_ref[...]], axis=0)
    pair_w = 2 * HEAD_DIM
    first_head = lax.broadcasted_iota(jnp.int32, (1, pair_w), 1) < HEAD_DIM
    def one_head(x, first):
        bits = pltpu.bitcast(x, jnp.uint32)
        keep = first_head if first else jnp.logical_not(first_head)
        return pltpu.bitcast(jnp.where(keep, bits, jnp.uint32(0)), jnp.bfloat16)

    ones = jnp.ones_like(k[:, :pair_w])
    operands = []
    for pair in range(HEADS_PER_GROUP // 2):
        cols = slice(pair * pair_w, (pair + 1) * pair_w)
        k2, v2 = k[:, cols], v[:, cols]
        k_heads = (one_head(k2, True), one_head(k2, False))
        v_heads = (jnp.concatenate([one_head(v2, True), one_head(ones, True)], axis=1),
                   jnp.concatenate([one_head(v2, False), one_head(ones, False)], axis=1))
        operands.append((cols, k_heads, v_heads))
    done = {}

    def head_pair(b, pair):
        rows = slice(b * ATTN_BLOCK, (b + 1) * ATTN_BLOCK)
        keys = slice(b * ATTN_BLOCK, (b + 2) * ATTN_BLOCK)
        cols, k_heads, v_heads = operands[pair]
        bias_ref_b = bias0_ref if b == 0 else bias_ref
        bias = jnp.concatenate([bias_ref_b[2 * pair], bias_ref_b[2 * pair + 1]], axis=1)
        k_both = jnp.concatenate([k_heads[0][keys], k_heads[1][keys]], axis=0)
        s = lax.dot_general(q[rows, cols], k_both, (((1,), (1,)), ((), ())),
                            preferred_element_type=jnp.float32) + bias
        yield
        n_keys = 2 * ATTN_BLOCK
        s0, s1 = s[:, :n_keys], s[:, n_keys:]
        m0 = jnp.max(s0, axis=-1, keepdims=True)
        m1 = jnp.max(s1, axis=-1, keepdims=True)
        p0 = jnp.exp(s0 - m0).astype(jnp.bfloat16)
        p1 = jnp.exp(s1 - m1).astype(jnp.bfloat16)
        yield
        acc = (jnp.dot(p0, v_heads[0][keys], preferred_element_type=jnp.float32)
               + jnp.dot(p1, v_heads[1][keys], preferred_element_type=jnp.float32))
        yield
        denom = acc[:, pair_w:]
        done[b, pair] = (acc[:, :pair_w] / denom, jnp.where(first_head, m0, m1) + jnp.log(denom))

    waiting = [head_pair(b, pair) for b in range(n_qb) for pair in range(HEADS_PER_GROUP // 2)]
    live = []
    while waiting or live:
        if waiting:
            live.append(waiting.pop(0))
        for gen in list(live):
            if next(gen, _DONE) is _DONE:
                live.remove(gen)

    for b in range(n_qb):
        rows = slice(b * ATTN_BLOCK, (b + 1) * ATTN_BLOCK)
        pairs = [done[b, pair] for pair in range(HEADS_PER_GROUP // 2)]
        o_ref[rows, :] = jnp.concatenate([o for o, _ in pairs], axis=1).astype(o_ref.dtype)
        lse_ref[rows, :] = jnp.concatenate([lse for _, lse in pairs], axis=1)


def _attention_group(qkv_g, bias, gi, dilation):
    B, _, L, _ = qkv_g.shape
    n_qb = min(ATTN_QB, L // ATTN_BLOCK)
    rows = n_qb * ATTN_BLOCK
    cur = lambda which: pl.BlockSpec((None, None, rows, GROUP_WIDTH), lambda b, r, n: (b, r, n, which))
    prev = lambda which: pl.BlockSpec((None, None, ATTN_BLOCK, GROUP_WIDTH),
                                      lambda b, r, n: (b, r, jnp.maximum(n * n_qb - 1, 0), which))
    bias_blk = (None, HEADS_PER_GROUP, ATTN_BLOCK, 2 * ATTN_BLOCK)
    return pl.pallas_call(
        functools.partial(_attn_kernel, n_qb),
        grid=(B, dilation, L // rows),
        in_specs=[
            cur(0), prev(1), cur(1), prev(2), cur(2),
            pl.BlockSpec(bias_blk, lambda b, r, n: (jnp.minimum(n, 1), 0, 0, 0)),
            pl.BlockSpec(bias_blk, lambda b, r, n: (1, 0, 0, 0)),
        ],
        out_specs=[cur(0), cur(0)],
        out_shape=[
            jax.ShapeDtypeStruct((B, dilation, L, GROUP_WIDTH), jnp.bfloat16),
            jax.ShapeDtypeStruct((B, dilation, L, GROUP_WIDTH), jnp.float32),
        ],
        compiler_params=pltpu.CompilerParams(
            dimension_semantics=("parallel", "parallel", "arbitrary"), vmem_limit_bytes=VMEM_LIMIT),
        name=f"dilated_attention_g{gi}",
    )(qkv_g, qkv_g, qkv_g, qkv_g, qkv_g, bias, bias)


def _t5_causal_bucket(dist):
    max_exact = N_REL_BUCKETS // 2
    d_f = np.maximum(dist, max_exact).astype(np.float32)
    scaled = (np.log(d_f / np.float32(max_exact)) / np.float32(math.log(REL_MAX_DISTANCE / max_exact))
              * np.float32(N_REL_BUCKETS - max_exact))
    large = np.minimum(max_exact + scaled.astype(np.int32), N_REL_BUCKETS - 1)
    return np.where(dist < max_exact, dist, large).astype(np.int32)


def _attention_bias_table(rel_bias_g, window, dilation):
    nw = window // dilation
    qi = np.arange(ATTN_BLOCK)[:, None]
    ki = np.arange(2 * ATTN_BLOCK)[None, :]
    dist = ATTN_BLOCK + qi - ki
    band = (dist >= 0) & (dist <= nw)
    bucket = _t5_causal_bucket(np.maximum(dist, 0) * dilation)
    onehot = (jnp.asarray(bucket)[:, :, None] == jnp.arange(N_REL_BUCKETS)).astype(jnp.float32)
    bias = jnp.einsum('qkb,bh->hqk', onehot, rel_bias_g.astype(jnp.float32),
                      precision=lax.Precision.HIGHEST)
    masks = jnp.asarray(np.stack([band & (ki >= ATTN_BLOCK), band]))
    return jnp.where(masks[:, None], bias[None], NEG_INF)


def _gelu_tanh(x):
    return 0.5 * x * (1.0 + jnp.tanh(math.sqrt(2.0 / math.pi) * (x + 0.044715 * (x * x * x))))


def _softplus(x):
    return jnp.maximum(x, 0.0) + jnp.log(1.0 + jnp.exp(-jnp.abs(x)))


def _natural_order(blk_ref, d, slabs):
    if d == 1:
        return blk_ref[0].astype(jnp.float32)
    per = MIX_TILE // d
    for r in range(d):
        rows = blk_ref[r].astype(jnp.float32)
        for c in range(GROUP_WIDTH // LANES):
            slabs[c, pl.ds(r, per, stride=d), :] = rows[:, c * LANES:(c + 1) * LANES]
    return jnp.concatenate([slabs[c] for c in range(GROUP_WIDTH // LANES)], axis=1)


def _mix_tile(x_ref, o1_ref, o2_ref, o3_ref, l1_ref, l2_ref, l3_ref,
              n1_ref, win_ref, cw_ref, cb_ref, wg_ref, brg_ref, big_ref, lam_ref,
              wpa_ref, wpl_ref, wout_ref, n2_ref, wr_ref,
              x1_ref, xs_ref, route_ref, cnt_ref,
              xbuf, a_sc, b_sc, h_sc, *slabs):
    tm = MIX_TILE
    first_tile = pl.program_id(0) == 0

    @pl.when(first_tile)
    def _():
        xbuf[0:SUBLANES, :] = jnp.zeros((SUBLANES, LRU_WIDTH), jnp.float32)
        h_sc[...] = jnp.zeros_like(h_sc)

    x = x_ref[...]
    h = _rmsnorm(x, n1_ref[...]).astype(jnp.bfloat16)
    def project(first_col):
        return jnp.dot(h, win_ref[:, first_col:first_col + D_MODEL], preferred_element_type=jnp.float32)

    xr = project(0)

    yield
    g_lru = project(LRU_WIDTH)
    xbuf[SUBLANES:SUBLANES + tm, :] = xr
    xc = xr * cw_ref[CONV_WIDTH - 1:CONV_WIDTH, :] + cb_ref[...]
    for j in range(CONV_WIDTH - 1):
        back = CONV_WIDTH - 1 - j
        xc = xc + xbuf[SUBLANES - back:SUBLANES - back + tm, :] * cw_ref[j:j + 1, :]
    xbuf[0:SUBLANES, :] = xbuf[tm:tm + SUBLANES, :]

    xcb = xc.astype(jnp.bfloat16)
    r_parts, i_parts = [], []
    for c in range(N_LRU_CHUNKS):
        ri = jnp.dot(xcb[:, c * LRU_CHUNK:(c + 1) * LRU_CHUNK], wg_ref[c],
                     preferred_element_type=jnp.float32)
        r_parts.append(ri[:, :LRU_CHUNK])
        i_parts.append(ri[:, LRU_CHUNK:])
    yield
    g_a = project(2 * LRU_WIDTH)
    r = jax.nn.sigmoid(jnp.concatenate(r_parts, axis=1) + brg_ref[...])
    ig = jax.nn.sigmoid(jnp.concatenate(i_parts, axis=1) + big_ref[...])
    log_a = (-LRU_C * _softplus(-lam_ref[...])) * r
    a = jnp.exp(log_a)
    gap = 1.0 - a * a
    root = jnp.where(gap > 0.0, gap * lax.rsqrt(gap), 0.0)
    bb = root * (ig * xc)

    yield
    g_b = project(2 * LRU_WIDTH + D_MODEL)
    n_slab = LRU_WIDTH // LANES
    for c in range(n_slab):
        lanes = slice(c * LANES, (c + 1) * LANES)
        for s in range(SUBLANES):
            rows = slice(s * SCAN_SEG, (s + 1) * SCAN_SEG)
            a_sc[c, s * SCAN_PITCH:s * SCAN_PITCH + SCAN_SEG, :] = a[rows, lanes]
            b_sc[c, s * SCAN_PITCH:s * SCAN_PITCH + SCAN_SEG, :] = bb[rows, lanes]

    yield
    local = [jnp.zeros((SUBLANES, LANES), jnp.float32)] * n_slab
    decay = [jnp.ones((SUBLANES, LANES), jnp.float32)] * n_slab
    for j in range(SCAN_SEG):
        for c in range(n_slab):
            step = pl.ds(j, SUBLANES, stride=SCAN_PITCH)
            a_j = a_sc[c, step, :]
            local[c] = a_j * local[c] + b_sc[c, step, :]
            decay[c] = a_j * decay[c]
            b_sc[c, step, :] = local[c]
            a_sc[c, step, :] = decay[c]

    yield
    h_parts = []
    for c in range(n_slab):
        lanes = slice(c * LANES, (c + 1) * LANES)
        entry = h_sc[:, lanes]
        segs = []
        for s in range(SUBLANES):
            rows = slice(s * SCAN_PITCH, s * SCAN_PITCH + SCAN_SEG)
            segs.append(b_sc[c, rows, :] + a_sc[c, rows, :] * entry)
            entry = decay[c][s:s + 1, :] * entry + local[c][s:s + 1, :]
        h_sc[:, lanes] = entry
        h_parts.append(jnp.concatenate(segs, axis=0))
    h_all = jnp.concatenate(h_parts, axis=1)
    yield
    lru = (h_all * _gelu_tanh(g_lru)).astype(jnp.bfloat16)
    p_lru = jnp.dot(lru, wpl_ref[...], preferred_element_type=jnp.float32)

    yield
    dil = [d for _, d in DILATED_GROUPS]
    o1, o2, o3 = (_natural_order(ref, d, slabs[2 * g]) for g, (ref, d) in
                  enumerate(zip((o1_ref, o2_ref, o3_ref), dil)))
    l1, l2, l3 = (_natural_order(ref, d, slabs[2 * g + 1]) for g, (ref, d) in
                  enumerate(zip((l1_ref, l2_ref, l3_ref), dil)))
    lm = jnp.maximum(jnp.maximum(l1, l2), l3)
    e1, e2, e3 = jnp.exp(l1 - lm), jnp.exp(l2 - lm), jnp.exp(l3 - lm)
    attn = (e1 * o1 + e2 * o2 + e3 * o3) / (e1 + e2 + e3)
    p_attn = jnp.dot(attn.astype(jnp.bfloat16), wpa_ref[...], preferred_element_type=jnp.float32)

    yield
    merged = jax.nn.sigmoid(g_a) * p_attn + jax.nn.sigmoid(g_b) * p_lru
    x1 = x + jnp.dot(merged.astype(jnp.bfloat16), wout_ref[...], preferred_element_type=jnp.float32)
    x1_ref[...] = x1
    h2 = _rmsnorm(x1, n2_ref[...])

    yield
    h2_hi = h2.astype(jnp.bfloat16)
    h2_lo = (h2 - h2_hi.astype(jnp.float32)).astype(jnp.bfloat16)
    by_hi = jnp.dot(h2_hi, wr_ref[...], preferred_element_type=jnp.float32)
    by_lo = jnp.dot(h2_lo, wr_ref[...], preferred_element_type=jnp.float32)
    logits = (by_hi[:, :LANES] + (by_hi[:, LANES:] + by_lo[:, :LANES])) + by_lo[:, LANES:]
    lane = lax.broadcasted_iota(jnp.int32, (tm, LANES), 1)
    big = jnp.int32(LANES)
    lowest = jnp.float32(-3.0e38)
    is_g = (lane >= GROUP_LOGIT_LANE0) & (lane < GROUP_LOGIT_LANE0 + N_EXPERT_GROUPS)
    gl = jnp.where(is_g, logits, lowest)
    gmax = jnp.max(gl, axis=-1, keepdims=True)
    gsel = jnp.min(jnp.where(gl == gmax, lane, big), axis=-1, keepdims=True) - GROUP_LOGIT_LANE0
    p_sel = 1.0 / jnp.sum(jnp.where(is_g, jnp.exp(gl - gmax), 0.0), axis=-1, keepdims=True)
    is_e = (lane >= gsel * EXPERTS_PER_GROUP) & (lane < (gsel + 1) * EXPERTS_PER_GROUP)
    el = jnp.where(is_e, logits, lowest)
    v1 = jnp.max(el, axis=-1, keepdims=True)
    i1 = jnp.min(jnp.where(el == v1, lane, big), axis=-1, keepdims=True)
    el2 = jnp.where(lane == i1, lowest, el)
    v2 = jnp.max(el2, axis=-1, keepdims=True)
    i2 = jnp.min(jnp.where(el2 == v2, lane, big), axis=-1, keepdims=True)
    t = jnp.exp(v2 - v1)
    g0 = p_sel / (1.0 + t)
    g1 = p_sel * t / (1.0 + t)

    yield
    oh0 = lane == i1
    oh1 = lane == i2
    both = (oh0 | oh1).astype(jnp.float32)
    tri = (lax.broadcasted_iota(jnp.int32, (tm, tm), 0)
           > lax.broadcasted_iota(jnp.int32, (tm, tm), 1)).astype(jnp.bfloat16)
    before = jnp.dot(tri, both.astype(jnp.bfloat16), preferred_element_type=jnp.float32)
    cnt = jnp.sum(both, axis=0, keepdims=True)
    padded = jnp.floor((cnt + (MOE_CHUNK - 1.0)) * (1.0 / MOE_CHUNK)) * MOE_CHUNK
    upper = (lax.broadcasted_iota(jnp.int32, (LANES, LANES), 0)
             < lax.broadcasted_iota(jnp.int32, (LANES, LANES), 1)).astype(jnp.bfloat16)
    loff = jnp.dot(jnp.broadcast_to(padded, (SUBLANES, LANES)).astype(jnp.bfloat16), upper,
                   preferred_element_type=jnp.float32)[0:1, :]
    base = before + loff
    slot0 = jnp.sum(jnp.where(oh0, base, 0.0), axis=-1, keepdims=True)
    slot1 = jnp.sum(jnp.where(oh1, base, 0.0), axis=-1, keepdims=True)
    cnt_ref[...] = jnp.broadcast_to(cnt, cnt_ref.shape)

    slot_id = lax.broadcasted_iota(jnp.int32, (tm, MOE_SLOTS), 1)
    place = ((slot_id == slot0.astype(jnp.int32)) | (slot_id == slot1.astype(jnp.int32)))
    xs_ref[...] = _pack_bf16_pairs(
        lax.dot_general(place.astype(jnp.bfloat16), h2_hi, (((0,), (0,)), ((), ())),
                        preferred_element_type=jnp.float32))

    route = jnp.zeros((tm, LANES), jnp.float32)
    for lane_id, val in ((ROUTE_S0, slot0), (ROUTE_S1, slot1), (ROUTE_G0, g0), (ROUTE_G1, g1)):
        route = jnp.where(lane == lane_id, val, route)
    route_ref[...] = route


N_MIXER_TILED_INPUTS = 1 + 2 * N_GROUPS
N_MIXER_OUTPUTS = 4


def _mixer_kernel(*refs):
    n_batch = refs[0].shape[0]
    n_weights = len(refs) - N_MIXER_TILED_INPUTS - N_MIXER_OUTPUTS - n_batch * _MIXER_SCRATCH_PER_SEQ
    tiled = refs[:N_MIXER_TILED_INPUTS]
    weights = refs[N_MIXER_TILED_INPUTS:N_MIXER_TILED_INPUTS + n_weights]
    outs = refs[N_MIXER_TILED_INPUTS + n_weights:N_MIXER_TILED_INPUTS + n_weights + N_MIXER_OUTPUTS]
    scratch = refs[N_MIXER_TILED_INPUTS + n_weights + N_MIXER_OUTPUTS:]
    stages = []
    for b in range(n_batch):
        own = scratch[b * _MIXER_SCRATCH_PER_SEQ:(b + 1) * _MIXER_SCRATCH_PER_SEQ]
        stages.append(_mix_tile(*[r.at[b] for r in tiled], *weights, *[r.at[b] for r in outs], *own))
    while stages:
        for gen in list(stages):
            if next(gen, _DONE) is _DONE:
                stages.remove(gen)


def _mixer_scratch(tm):
    return [
        pltpu.VMEM((tm + 2 * SUBLANES, LRU_WIDTH), jnp.float32),
        pltpu.VMEM((LRU_WIDTH // LANES, SUBLANES * SCAN_PITCH, LANES), jnp.float32),
        pltpu.VMEM((LRU_WIDTH // LANES, SUBLANES * SCAN_PITCH, LANES), jnp.float32),
        pltpu.VMEM((1, LRU_WIDTH), jnp.float32),
    ] + [pltpu.VMEM((GROUP_WIDTH // LANES, tm, LANES), jnp.float32)] * (2 * N_GROUPS)


_MIXER_SCRATCH_PER_SEQ = len(_mixer_scratch(MIX_TILE))


def _token_mixer(x, attn_o, attn_lse, weights):
    B, S, _ = x.shape
    tm = MIX_TILE
    nt = S // tm
    rows = lambda j: (0, j, 0)

    def resident(shape):
        return pl.BlockSpec(shape, lambda j: (0,) * len(shape), pipeline_mode=pl.Buffered(1))

    in_specs = [pl.BlockSpec((B, tm, D_MODEL), rows)]
    group_specs = [pl.BlockSpec((B, d, tm // d, GROUP_WIDTH), lambda j: (0, 0, j, 0))
                   for _, d in DILATED_GROUPS]
    in_specs += group_specs + group_specs
    in_specs += [resident(w.shape) for w in weights]
    x1, xs, route, counts = pl.pallas_call(
        _mixer_kernel,
        grid=(nt,),
        in_specs=in_specs,
        out_specs=[
            pl.BlockSpec((B, tm, D_MODEL), rows),
            pl.BlockSpec((B, MOE_SLOTS, D_MODEL // 2), rows),
            pl.BlockSpec((B, tm, LANES), rows),
            pl.BlockSpec((B, None, SUBLANES, LANES), lambda j: (0, j, 0, 0)),
        ],
        out_shape=[
            jax.ShapeDtypeStruct((B, S, D_MODEL), jnp.float32),
            jax.ShapeDtypeStruct((B, nt * MOE_SLOTS, D_MODEL // 2), jnp.uint32),
            jax.ShapeDtypeStruct((B, S, LANES), jnp.float32),
            jax.ShapeDtypeStruct((B, nt, SUBLANES, LANES), jnp.float32),
        ],
        scratch_shapes=_mixer_scratch(tm) * B,
        compiler_params=pltpu.CompilerParams(
            dimension_semantics=("arbitrary",), vmem_limit_bytes=VMEM_LIMIT),
        name="token_mixer",
    )(x, *attn_o, *attn_lse, *weights)
    return (x1.reshape(B * S, D_MODEL), xs.reshape(B * nt * MOE_SLOTS, D_MODEL // 2),
            route.reshape(B * S, LANES), counts.reshape(B * nt, SUBLANES, LANES))


def _moe_tables(cnt_tile):
    ntiles = cnt_tile.shape[0]
    nch = (cnt_tile + MOE_CHUNK - 1) // MOE_CHUNK
    lo_c = jnp.cumsum(nch, axis=1) - nch
    per_expert = jnp.sum(nch, axis=0)
    region = ((per_expert + MOE_BLOCK_CHUNKS - 1) // MOE_BLOCK_CHUNKS) * MOE_BLOCK_CHUNKS
    pend = jnp.cumsum(region)
    glob = (pend - region)[None, :] + jnp.cumsum(nch, axis=0) - nch

    max_rows = TOP_K * ntiles * MIX_TILE + ntiles * N_EXPERTS * (MOE_CHUNK - 1)
    max_chunks = -(-max_rows // MOE_CHUNK) + N_EXPERTS * (MOE_BLOCK_CHUNKS - 1)
    max_blocks = -(-max_chunks // MOE_BLOCK_CHUNKS)
    max_blocks = -(-max_blocks // EXPERT_BLOCKS_PER_STEP) * EXPERT_BLOCKS_PER_STEP
    max_chunks = max_blocks * MOE_BLOCK_CHUNKS

    seg_start = glob.T.reshape(-1)
    seg_src = (jnp.arange(ntiles, dtype=jnp.int32)[:, None] * TILE_CHUNKS + lo_c).T.reshape(-1)
    step = jnp.diff(seg_src - seg_start, prepend=0)
    g = jnp.arange(max_chunks, dtype=jnp.int32)
    src = g + jnp.sum(jnp.where(seg_start[None, :] <= g[:, None], step[None, :], 0), axis=1)
    src = jnp.clip(src, 0, ntiles * TILE_CHUNKS - 1)

    lstep = jnp.diff(glob - lo_c, axis=1, prepend=0)
    lc = jnp.arange(TILE_CHUNKS, dtype=jnp.int32)
    comb = lc[None, :] + jnp.sum(
        jnp.where(lo_c[:, None, :] <= lc[None, :, None], lstep[:, None, :], 0), axis=2)
    comb = jnp.clip(comb, 0, max_chunks - 1).reshape(-1)

    eidx = jnp.arange(N_EXPERTS, dtype=jnp.int32)
    nonempty = region > 0
    later = jnp.where(nonempty[None, :] & (eidx[None, :] > eidx[:, None]), eidx[None, :], N_EXPERTS)
    next_expert = jnp.min(later, axis=1)
    next_expert = jnp.where(next_expert == N_EXPERTS, -1, next_expert)
    wslot = (jnp.cumsum(nonempty.astype(jnp.int32)) - nonempty.astype(jnp.int32)) % 2
    last_expert = jnp.max(jnp.where(nonempty, eidx, 0))
    blk0 = jnp.arange(max_blocks, dtype=jnp.int32) * MOE_BLOCK_CHUNKS
    block_expert = jnp.minimum(jnp.sum((pend[None, :] <= blk0[:, None]).astype(jnp.int32), axis=1),
                               last_expert)
    of_block = lambda per_expert: jnp.sum(
        jnp.where(block_expert[:, None] == eidx[None, :], per_expert[None, :], 0), axis=1)
    n_valid = pend[-1:] // MOE_BLOCK_CHUNKS
    as_i32 = lambda v: v.astype(jnp.int32)
    expert_tables = (as_i32(block_expert), as_i32(of_block(next_expert)), as_i32(of_block(wslot)),
                     as_i32(n_valid), as_i32(src))
    return expert_tables, as_i32(comb)


def _chunk_copy(src_hbm, src_chunk, dst, dst_chunk, sem):
    rows = lambda c: pl.ds(pl.multiple_of(c * MOE_CHUNK, MOE_CHUNK), MOE_CHUNK)
    return pltpu.make_async_copy(src_hbm.at[rows(src_chunk), :], dst.at[rows(dst_chunk), :], sem)


def _expert_kernel(be_ref, nxt_ref, ws_ref, nv_ref, src_ref, xs_hbm, wgu_hbm, wd_hbm, ys_ref, *scratch):
    for k in range(EXPERT_BLOCKS_PER_STEP):
        _expert_block(pl.program_id(0) * EXPERT_BLOCKS_PER_STEP + k,
                      be_ref, nxt_ref, ws_ref, nv_ref, src_ref, xs_hbm, wgu_hbm, wd_hbm,
                      ys_ref.at[pl.ds(k * MOE_TILE, MOE_TILE), :], *scratch)


def _expert_block(i, be_ref, nxt_ref, ws_ref, nv_ref, src_ref, xs_hbm, wgu_hbm, wd_hbm, ys_ref,
                  xbuf, wgu_f32, wd_f32, wgu_bf, wd_bf, sems, wsems):
    valid = i < nv_ref[0]

    def gather(blk, slot):
        return [_chunk_copy(xs_hbm, src_ref[blk * MOE_BLOCK_CHUNKS + c], xbuf.at[slot], c, sems.at[slot])
                for c in range(MOE_BLOCK_CHUNKS)]

    def fetch_weights(e, slot):
        return [pltpu.make_async_copy(wgu_hbm.at[e], wgu_f32.at[slot], wsems.at[slot]),
                pltpu.make_async_copy(wd_hbm.at[e], wd_f32.at[slot], wsems.at[slot])]

    ahead = EXPERT_GATHER_SLOTS - 1

    @pl.when(i == 0)
    def _():
        for cp in fetch_weights(be_ref[0], 0):
            cp.start(priority=WEIGHT_DMA_PRIORITY)
        for first in range(ahead):
            @pl.when(first < nv_ref[0])
            def _():
                for cp in gather(first, first):
                    cp.start()

    @pl.when(i + ahead < nv_ref[0])
    def _():
        for cp in gather(i + ahead, (i + ahead) % EXPERT_GATHER_SLOTS):
            cp.start()

    new_expert = valid & ((i == 0) | (be_ref[i] != be_ref[jnp.maximum(i - 1, 0)]))

    @pl.when(new_expert)
    def _():
        ws = ws_ref[i]
        for cp in fetch_weights(be_ref[i], ws):
            cp.wait()
        wgu_bf[...] = wgu_f32[ws].astype(jnp.bfloat16)
        wd_bf[...] = wd_f32[ws].astype(jnp.bfloat16)

        @pl.when(nxt_ref[i] >= 0)
        def _():
            for cp in fetch_weights(nxt_ref[i], 1 - ws):
                cp.start(priority=WEIGHT_DMA_PRIORITY)

    @pl.when(valid)
    def _():
        slot = i % EXPERT_GATHER_SLOTS
        for cp in gather(i, slot):
            cp.wait()
        x_lo, x_hi = _unpack_bf16_pairs(xbuf[slot])
        half = D_MODEL // 2
        partial = []

        def up_proj(cols):
            return (jnp.dot(x_lo, wgu_bf[:half, cols], preferred_element_type=jnp.float32)
                    + jnp.dot(x_hi, wgu_bf[half:, cols], preferred_element_type=jnp.float32))

        def hidden_group(c):
            cols = slice(c * EXPERT_GROUP, (c + 1) * EXPERT_GROUP)
            g = up_proj(cols)
            u = up_proj(slice(D_EXPERT + cols.start, D_EXPERT + cols.stop))
            yield
            act = (g * jax.nn.sigmoid(g) * u).astype(jnp.bfloat16)
            yield
            partial.append(jnp.dot(act, wd_bf[cols, :], preferred_element_type=jnp.float32))

        waiting = [hidden_group(c) for c in range(D_EXPERT // EXPERT_GROUP)]
        live = []
        while waiting or live:
            if waiting:
                live.append(waiting.pop(0))
            for gen in list(live):
                if next(gen, _DONE) is _DONE:
                    live.remove(gen)
        y = sum(partial[1:], partial[0])
        ys_ref[...] = _pack_bf16_pairs(y.astype(jnp.bfloat16).astype(jnp.float32))

    @pl.when(jnp.logical_not(valid))
    def _():
        ys_ref[...] = jnp.zeros_like(ys_ref)


def _experts(tables, xs, w_gate_up, w_down):
    block_expert, block_next, block_wslot, n_valid, src = tables
    nblk = block_expert.shape[0]
    return pl.pallas_call(
        _expert_kernel,
        grid_spec=pltpu.PrefetchScalarGridSpec(
            num_scalar_prefetch=5,
            grid=(nblk // EXPERT_BLOCKS_PER_STEP,),
            in_specs=[pl.BlockSpec(memory_space=pl.ANY)] * 3,
            out_specs=pl.BlockSpec((EXPERT_BLOCKS_PER_STEP * MOE_TILE, D_MODEL // 2),
                                   lambda i, *_: (i, 0)),
            scratch_shapes=[
                pltpu.VMEM((EXPERT_GATHER_SLOTS, MOE_TILE, D_MODEL // 2), jnp.uint32),
                pltpu.VMEM((2, D_MODEL, 2 * D_EXPERT), jnp.float32),
                pltpu.VMEM((2, D_EXPERT, D_MODEL), jnp.float32),
                pltpu.VMEM((D_MODEL, 2 * D_EXPERT), jnp.bfloat16),
                pltpu.VMEM((D_EXPERT, D_MODEL), jnp.bfloat16),
                pltpu.SemaphoreType.DMA((EXPERT_GATHER_SLOTS,)),
                pltpu.SemaphoreType.DMA((2,)),
            ],
        ),
        out_shape=jax.ShapeDtypeStruct((nblk * MOE_TILE, D_MODEL // 2), jnp.uint32),
        compiler_params=pltpu.CompilerParams(
            dimension_semantics=("arbitrary",), vmem_limit_bytes=VMEM_LIMIT),
        name="moe_experts",
    )(block_expert, block_next, block_wslot, n_valid, src, xs, w_gate_up, w_down)


def _combine_kernel(comb_ref, ys_hbm, x1_ref, route_ref, nf_ref, out_ref, ybuf, sems):
    tc = MIX_TILE
    per_step = COMBINE_TILES_PER_STEP
    i = pl.program_id(0)
    last = pl.num_programs(0) - 1

    def gather(step, slot):
        first = step * per_step * TILE_CHUNKS
        return [_chunk_copy(ys_hbm, comb_ref[first + lc], ybuf.at[slot], lc, sems.at[slot])
                for lc in range(per_step * TILE_CHUNKS)]

    @pl.when(i == 0)
    def _():
        for cp in gather(0, 0):
            cp.start()

    nxt = jnp.minimum(i + 1, last)
    for cp in gather(nxt, (i + 1) % 2):
        cp.start()

    slot = i % 2
    for cp in gather(i, slot):
        cp.wait()

    def tile(t):
        rows = slice(t * tc, (t + 1) * tc)
        y_lo, y_hi = _unpack_bf16_pairs(ybuf[slot, t * MOE_SLOTS:(t + 1) * MOE_SLOTS, :])
        route = route_ref[rows, :]
        slot_id = lax.broadcasted_iota(jnp.int32, (tc, MOE_SLOTS), 1)
        picked = []
        for lane in (ROUTE_S0, ROUTE_S1):
            sel = (slot_id == route[:, lane:lane + 1].astype(jnp.int32)).astype(jnp.bfloat16)
            yield
            picked.append(jnp.concatenate([jnp.dot(sel, y_lo, preferred_element_type=jnp.float32),
                                           jnp.dot(sel, y_hi, preferred_element_type=jnp.float32)], axis=1))
            yield
        g0 = route[:, ROUTE_G0:ROUTE_G0 + 1]
        g1 = route[:, ROUTE_G1:ROUTE_G1 + 1]
        out_ref[rows, :] = _rmsnorm(x1_ref[rows, :] + g0 * picked[0] + g1 * picked[1], nf_ref[...])

    waiting = [tile(t) for t in range(per_step)]
    live = []
    while waiting or live:
        if waiting:
            live.append(waiting.pop(0))
        for gen in list(live):
            if next(gen, _DONE) is _DONE:
                live.remove(gen)

    @pl.when(i == last)
    def _():
        for cp in gather(nxt, (i + 1) % 2):
            cp.wait()


def _combine(comb, ys, x1, route, norm_f):
    T = x1.shape[0]
    tc = MIX_TILE * COMBINE_TILES_PER_STEP
    return pl.pallas_call(
        _combine_kernel,
        grid_spec=pltpu.PrefetchScalarGridSpec(
            num_scalar_prefetch=1,
            grid=(T // tc,),
            in_specs=[
                pl.BlockSpec(memory_space=pl.ANY),
                pl.BlockSpec((tc, D_MODEL), lambda i, c: (i, 0)),
                pl.BlockSpec((tc, LANES), lambda i, c: (i, 0)),
                pl.BlockSpec((1, D_MODEL), lambda i, c: (0, 0)),
            ],
            out_specs=pl.BlockSpec((tc, D_MODEL), lambda i, c: (i, 0)),
            scratch_shapes=[
                pltpu.VMEM((2, COMBINE_TILES_PER_STEP * MOE_SLOTS, D_MODEL // 2), jnp.uint32),
                pltpu.SemaphoreType.DMA((2,)),
            ],
        ),
        out_shape=jax.ShapeDtypeStruct((T, D_MODEL), jnp.float32),
        compiler_params=pltpu.CompilerParams(
            dimension_semantics=("arbitrary",), vmem_limit_bytes=VMEM_LIMIT),
        name="moe_combine",
    )(comb, ys, x1, route, norm_f)


def _block_diag_gates(w_rg, w_ig):
    def bd(w):
        w4 = w.reshape(N_LRU_CHUNKS, LRU_CHUNK_HEADS, LRU_HEAD_DIM, LRU_HEAD_DIM)
        eye = jnp.eye(LRU_CHUNK_HEADS, dtype=w.dtype)
        return jnp.einsum('chij,hk->chikj', w4, eye).reshape(N_LRU_CHUNKS, LRU_CHUNK, LRU_CHUNK)
    return jnp.concatenate([bd(w_rg), bd(w_ig)], axis=-1).astype(jnp.bfloat16)


def kernel(x, rel_bias, norm1, w_in, conv_w, conv_b, w_rg, b_rg, w_ig, b_ig, lru_lambda,
           w_proj_attn, w_proj_lru, w_out, norm2, w_router_group, w_router_expert,
           w_gate_up, w_down, norm_f):
    B, S, D = x.shape
    T = B * S
    assert w_in.shape[0] == 1, "single-layer block"
    layer = 0
    bf16 = jnp.bfloat16
    x2 = x.reshape(T, D)
    row = lambda v: v[layer].reshape(1, -1)
    w_in_bf = w_in[layer].astype(bf16)
    w_qkv = (w_in_bf[:, :QKV_WIDTH].reshape(D, 3, N_GROUPS, GROUP_WIDTH)
             .transpose(0, 2, 1, 3).reshape(D, QKV_WIDTH))
    w_rest = w_in_bf[:, QKV_WIDTH:]

    qkv_groups = _qkv_projection(x2, row(norm1), w_qkv, B, S)
    attn_o, attn_lse = [], []
    for gi, (window, dilation) in enumerate(DILATED_GROUPS):
        hs = slice(gi * HEADS_PER_GROUP, (gi + 1) * HEADS_PER_GROUP)
        bias = _attention_bias_table(rel_bias[:, hs], window, dilation)
        o, lse = _attention_group(qkv_groups[gi], bias, gi, dilation)
        attn_o.append(o)
        attn_lse.append(lse)

    w_router = jnp.concatenate(
        [w_router_expert[layer].astype(jnp.float32), w_router_group[layer].astype(jnp.float32),
         jnp.zeros((D, LANES - N_EXPERTS - N_EXPERT_GROUPS), jnp.float32)], axis=1)
    w_router_hi = w_router.astype(bf16)
    weights = [
        row(norm1), w_rest, conv_w[layer], row(conv_b),
        _block_diag_gates(w_rg[layer], w_ig[layer]), row(b_rg), row(b_ig), row(lru_lambda),
        w_proj_attn[layer].astype(bf16), w_proj_lru[layer].astype(bf16), w_out[layer].astype(bf16),
        row(norm2),
        jnp.concatenate([w_router_hi, (w_router - w_router_hi.astype(jnp.float32)).astype(bf16)], axis=1),
    ]
    x1, xs, route, counts = _token_mixer(x, attn_o, attn_lse, weights)

    expert_tables, comb = _moe_tables(counts[:, 0, :N_EXPERTS].astype(jnp.int32))
    ys = _experts(expert_tables, xs, w_gate_up[layer], w_down[layer])
    out = _combine(comb, ys, x1, route, norm_f.reshape(1, -1))
    return out.reshape(B, S, D)
```

```python
import functools
import math

import jax
import jax.numpy as jnp
import numpy as np
from jax import lax
from jax.experimental import pallas as pl
from jax.experimental.pallas import tpu as pltpu

D_MODEL = 1024
HEAD_DIM = 64
HEADS_PER_GROUP = 4
DILATED_GROUPS = ((128, 1), (512, 4), (2048, 16))
N_GROUPS = len(DILATED_GROUPS)
N_ATTN_HEADS = HEADS_PER_GROUP * N_GROUPS
ATTN_WIDTH = N_ATTN_HEADS * HEAD_DIM
GROUP_WIDTH = HEADS_PER_GROUP * HEAD_DIM
QKV_WIDTH = 3 * ATTN_WIDTH
ATTN_BLOCK = 128
N_REL_BUCKETS = 32
REL_MAX_DISTANCE = 2048
NEG_INF = -1e30

LRU_WIDTH = D_MODEL
LRU_HEADS = 16
LRU_HEAD_DIM = LRU_WIDTH // LRU_HEADS
CONV_WIDTH = 4
LRU_C = 8.0
LRU_CHUNK_HEADS = 4
LRU_CHUNK = LRU_CHUNK_HEADS * LRU_HEAD_DIM
N_LRU_CHUNKS = LRU_WIDTH // LRU_CHUNK

N_EXPERT_GROUPS = 4
EXPERTS_PER_GROUP = 8
N_EXPERTS = N_EXPERT_GROUPS * EXPERTS_PER_GROUP
TOP_K = 2
D_EXPERT = 512
EPS = 1e-6

REST_WIDTH = 2 * LRU_WIDTH + 2 * D_MODEL
LANES = 128
SUBLANES = 8

ATTN_TILE = 256
ATTN_TILES_PER_STEP = 4
ATTN_QB = 8
MIX_TILE = 256
SCAN_SEG = MIX_TILE // SUBLANES
SCAN_PITCH = SCAN_SEG + SUBLANES
MOE_TILE = 256
EXPERT_GATHER_SLOTS = 4
EXPERT_BLOCKS_PER_STEP = 8
COMBINE_TILES_PER_STEP = 4
WEIGHT_DMA_PRIORITY = 1
EXPERT_GROUP = 256
MOE_CHUNK = SUBLANES
MOE_BLOCK_CHUNKS = MOE_TILE // MOE_CHUNK
MOE_SLOTS = 768
assert MOE_SLOTS >= TOP_K * MIX_TILE + N_EXPERTS * (MOE_CHUNK - 1) and MOE_SLOTS % MOE_CHUNK == 0
TILE_CHUNKS = MOE_SLOTS // MOE_CHUNK
VMEM_LIMIT = 56 * 1024 * 1024

ROUTE_S0, ROUTE_S1, ROUTE_G0, ROUTE_G1 = range(4)
GROUP_LOGIT_LANE0 = N_EXPERTS


_DONE = object()


def _rmsnorm(x, g):
    return x * lax.rsqrt(jnp.mean(x * x, axis=-1, keepdims=True) + EPS) * g


def _pack_bf16_pairs(v):
    w = v.shape[1] // 2
    bits = lax.bitcast_convert_type(v, jnp.uint32)
    return bits[:, w:] | (bits[:, :w] >> 16)


def _unpack_bf16_pairs(p):
    lo = lax.bitcast_convert_type(p << 16, jnp.float32)
    hi = lax.bitcast_convert_type(p & jnp.uint32(0xFFFF0000), jnp.float32)
    return lo.astype(jnp.bfloat16), hi.astype(jnp.bfloat16)


def _qkv_kernel(x_ref, g_ref, w_ref, *refs):
    tm = ATTN_TILE
    o_refs, slab_refs = refs[:N_GROUPS], refs[N_GROUPS:]
    n_slab = ATTN_WIDTH // LANES

    def tile(t):
        rows = slice(t * tm, (t + 1) * tm)
        slabs = slab_refs[t]
        h = _rmsnorm(x_ref[rows, :], g_ref[...]).astype(jnp.bfloat16)
        yield
        for gi, (o_ref, (_, d)) in enumerate(zip(o_refs, DILATED_GROUPS)):
            cols = slice(gi * ATTN_WIDTH, (gi + 1) * ATTN_WIDTH)
            part = jnp.dot(h, w_ref[:, cols], preferred_element_type=jnp.float32)
            out_rows = slice(t * (tm // d), (t + 1) * (tm // d))
            if d == 1:
                o_ref[0, out_rows, :] = part.astype(jnp.bfloat16)
                yield
                continue
            for c in range(n_slab):
                slabs[c] = part[:, c * LANES:(c + 1) * LANES]
            yield
            for r in range(d):
                o_ref[r, out_rows, :] = jnp.concatenate(
                    [slabs[c, pl.ds(r, tm // d, stride=d), :] for c in range(n_slab)],
                    axis=1).astype(jnp.bfloat16)
            yield

    waiting = [tile(t) for t in range(ATTN_TILES_PER_STEP)]
    live = []
    while waiting or live:
        if waiting:
            live.append(waiting.pop(0))
        for gen in list(live):
            if next(gen, _DONE) is _DONE:
                live.remove(gen)


def _qkv_projection(x2, norm1, w_qkv, B, S):
    tm = ATTN_TILE * ATTN_TILES_PER_STEP
    nt = S // tm
    return pl.pallas_call(
        _qkv_kernel,
        grid=(B, nt),
        in_specs=[
            pl.BlockSpec((tm, D_MODEL), lambda b, j: (b * nt + j, 0)),
            pl.BlockSpec((1, D_MODEL), lambda b, j: (0, 0)),
            pl.BlockSpec((D_MODEL, QKV_WIDTH), lambda b, j: (0, 0)),
        ],
        out_specs=[pl.BlockSpec((None, d, tm // d, ATTN_WIDTH), lambda b, j: (b, 0, j, 0))
                   for _, d in DILATED_GROUPS],
        out_shape=[jax.ShapeDtypeStruct((B, d, S // d, ATTN_WIDTH), jnp.bfloat16)
                   for _, d in DILATED_GROUPS],
        scratch_shapes=[pltpu.VMEM((ATTN_WIDTH // LANES, ATTN_TILE, LANES), jnp.float32)]
        * ATTN_TILES_PER_STEP,
        compiler_params=pltpu.CompilerParams(
            dimension_semantics=("parallel", "parallel"), vmem_limit_bytes=VMEM_LIMIT),
        name="qkv_projection",
    )(x2, norm1, w_qkv)


def _attn_residue(n_qb, q_ref, kp_ref, kc_ref, vp_ref, vc_ref, bias0_ref, bias_ref, o_ref, lse_ref):
    q = q_ref[...] * (HEAD_DIM ** -0.5)
    k = jnp.concatenate([kp_ref[...], kc_ref[...]], axis=0)
    v = jnp.concatenate([vp_ref[...], vc_ref[...]], axis=0)
    pair_w = 2 * HEAD_DIM
    first_head = lax.broadcasted_iota(jnp.int32, (1, pair_w), 1) < HEAD_DIM
    def one_head(x, first):
        bits = pltpu.bitcast(x, jnp.uint32)
        keep = first_head if first else jnp.logical_not(first_head)
        return pltpu.bitcast(jnp.where(keep, bits, jnp.uint32(0)), jnp.bfloat16)

    ones = jnp.ones_like(k[:, :pair_w])
    operands = []
    for pair in range(HEADS_PER_GROUP // 2):
        cols = slice(pair * pair_w, (pair + 1) * pair_w)
        k2, v2 = k[:, cols], v[:, cols]
        k_heads = (one_head(k2, True), one_head(k2, False))
        v_heads = (jnp.concatenate([one_head(v2, True), one_head(ones, True)], axis=1),
                   jnp.concatenate([one_head(v2, False), one_head(ones, False)], axis=1))
        operands.append((cols, k_heads, v_heads))
    done = {}

    def head_pair(b, pair):
        rows = slice(b * ATTN_BLOCK, (b + 1) * ATTN_BLOCK)
        keys = slice(b * ATTN_BLOCK, (b + 2) * ATTN_BLOCK)
        cols, k_heads, v_heads = operands[pair]
        bias_ref_b = bias0_ref if b == 0 else bias_ref
        bias = jnp.concatenate([bias_ref_b[2 * pair], bias_ref_b[2 * pair + 1]], axis=1)
        k_both = jnp.concatenate([k_heads[0][keys], k_heads[1][keys]], axis=0)
        s = lax.dot_general(q[rows, cols], k_both, (((1,), (1,)), ((), ())),
                            preferred_element_type=jnp.float32) + bias
        yield
        n_keys = 2 * ATTN_BLOCK
        s0, s1 = s[:, :n_keys], s[:, n_keys:]
        m0 = jnp.max(s0, axis=-1, keepdims=True)
        m1 = jnp.max(s1, axis=-1, keepdims=True)
        p0 = jnp.exp(s0 - m0).astype(jnp.bfloat16)
        p1 = jnp.exp(s1 - m1).astype(jnp.bfloat16)
        yield
        acc = (jnp.dot(p0, v_heads[0][keys], preferred_element_type=jnp.float32)
               + jnp.dot(p1, v_heads[1][keys], preferred_element_type=jnp.float32))
        yield
        denom = acc[:, pair_w:]
        done[b, pair] = (acc[:, :pair_w] / denom, jnp.where(first_head, m0, m1) + jnp.log(denom))

    problems = [head_pair(b, pair) for b in range(n_qb) for pair in range(HEADS_PER_GROUP // 2)]

    def store_results():
        for b in range(n_qb):
            rows = slice(b * ATTN_BLOCK, (b + 1) * ATTN_BLOCK)
            pairs = [done[b, pair] for pair in range(HEADS_PER_GROUP // 2)]
            o_ref[rows, :] = jnp.concatenate([o for o, _ in pairs], axis=1).astype(o_ref.dtype)
            lse_ref[rows, :] = jnp.concatenate([lse for _, lse in pairs], axis=1)

    return problems, store_results


def _attn_kernel(n_qb, n_res, q_ref, kp_ref, kc_ref, vp_ref, vc_ref, bias0_ref, bias_ref, o_ref, lse_ref):
    waiting, stores = [], []
    for ri in range(n_res):
        problems, store = _attn_residue(n_qb, q_ref.at[ri], kp_ref.at[ri], kc_ref.at[ri], vp_ref.at[ri],
                                        vc_ref.at[ri], bias0_ref, bias_ref, o_ref.at[ri], lse_ref.at[ri])
        waiting += problems
        stores.append(store)
    live = []
    while waiting or live:
        if waiting:
            live.append(waiting.pop(0))
        for gen in list(live):
            if next(gen, _DONE) is _DONE:
                live.remove(gen)
    for store in stores:
        store()


def _attention_group(qkv_g, bias, gi, dilation):
    B, _, L, _ = qkv_g.shape
    n_qb = min(ATTN_QB, L // ATTN_BLOCK)
    rows = n_qb * ATTN_BLOCK
    n_res = max(1, min(dilation, ATTN_QB // n_qb))
    cur = lambda which: pl.BlockSpec((None, n_res, rows, GROUP_WIDTH), lambda b, r, n: (b, r, n, which))
    prev = lambda which: pl.BlockSpec((None, n_res, ATTN_BLOCK, GROUP_WIDTH),
                                      lambda b, r, n: (b, r, jnp.maximum(n * n_qb - 1, 0), which))
    bias_blk = (None, HEADS_PER_GROUP, ATTN_BLOCK, 2 * ATTN_BLOCK)
    return pl.pallas_call(
        functools.partial(_attn_kernel, n_qb, n_res),
        grid=(B, dilation // n_res, L // rows),
        in_specs=[
            cur(0), prev(1), cur(1), prev(2), cur(2),
            pl.BlockSpec(bias_blk, lambda b, r, n: (jnp.minimum(n, 1), 0, 0, 0)),
            pl.BlockSpec(bias_blk, lambda b, r, n: (1, 0, 0, 0)),
        ],
        out_specs=[cur(0), cur(0)],
        out_shape=[
            jax.ShapeDtypeStruct((B, dilation, L, GROUP_WIDTH), jnp.bfloat16),
            jax.ShapeDtypeStruct((B, dilation, L, GROUP_WIDTH), jnp.float32),
        ],
        compiler_params=pltpu.CompilerParams(
            dimension_semantics=("parallel", "parallel", "arbitrary"), vmem_limit_bytes=VMEM_LIMIT),
        name=f"dilated_attention_g{gi}",
    )(qkv_g, qkv_g, qkv_g, qkv_g, qkv_g, bias, bias)


def _t5_causal_bucket(dist):
    max_exact = N_REL_BUCKETS // 2
    d_f = np.maximum(dist, max_exact).astype(np.float32)
    scaled = (np.log(d_f / np.float32(max_exact)) / np.float32(math.log(REL_MAX_DISTANCE / max_exact))
              * np.float32(N_REL_BUCKETS - max_exact))
    large = np.minimum(max_exact + scaled.astype(np.int32), N_REL_BUCKETS - 1)
    return np.where(dist < max_exact, dist, large).astype(np.int32)


def _attention_bias_table(rel_bias_g, window, dilation):
    nw = window // dilation
    qi = np.arange(ATTN_BLOCK)[:, None]
    ki = np.arange(2 * ATTN_BLOCK)[None, :]
    dist = ATTN_BLOCK + qi - ki
    band = (dist >= 0) & (dist <= nw)
    bucket = _t5_causal_bucket(np.maximum(dist, 0) * dilation)
    onehot = (jnp.asarray(bucket)[:, :, None] == jnp.arange(N_REL_BUCKETS)).astype(jnp.float32)
    bias = jnp.einsum('qkb,bh->hqk', onehot, rel_bias_g.astype(jnp.float32),
                      precision=lax.Precision.HIGHEST)
    masks = jnp.asarray(np.stack([band & (ki >= ATTN_BLOCK), band]))
    return jnp.where(masks[:, None], bias[None], NEG_INF)


def _gelu_tanh(x):
    return 0.5 * x * (1.0 + jnp.tanh(math.sqrt(2.0 / math.pi) * (x + 0.044715 * (x * x * x))))


def _softplus(x):
    return jnp.maximum(x, 0.0) + jnp.log(1.0 + jnp.exp(-jnp.abs(x)))


def _natural_order(blk_ref, d, slabs):
    if d == 1:
        return blk_ref[0].astype(jnp.float32)
    per = MIX_TILE // d
    for r in range(d):
        rows = blk_ref[r].astype(jnp.float32)
        for c in range(GROUP_WIDTH // LANES):
            slabs[c, pl.ds(r, per, stride=d), :] = rows[:, c * LANES:(c + 1) * LANES]
    return jnp.concatenate([slabs[c] for c in range(GROUP_WIDTH // LANES)], axis=1)


def _mix_tile(x_ref, o1_ref, o2_ref, o3_ref, l1_ref, l2_ref, l3_ref,
              n1_ref, win_ref, cw_ref, cb_ref, wg_ref, brg_ref, big_ref, lam_ref,
              wpa_ref, wpl_ref, wout_ref, n2_ref, wr_ref,
              x1_ref, xs_ref, route_ref, cnt_ref,
              xbuf, a_sc, b_sc, h_sc, *slabs):
    tm = MIX_TILE
    first_tile = pl.program_id(0) == 0

    @pl.when(first_tile)
    def _():
        xbuf[0:SUBLANES, :] = jnp.zeros((SUBLANES, LRU_WIDTH), jnp.float32)
        h_sc[...] = jnp.zeros_like(h_sc)

    x = x_ref[...]
    h = _rmsnorm(x, n1_ref[...]).astype(jnp.bfloat16)
    def project(first_col):
        return jnp.dot(h, win_ref[:, first_col:first_col + D_MODEL], preferred_element_type=jnp.float32)

    xr = project(0)

    yield
    g_lru = project(LRU_WIDTH)
    xbuf[SUBLANES:SUBLANES + tm, :] = xr
    xc = xr * cw_ref[CONV_WIDTH - 1:CONV_WIDTH, :] + cb_ref[...]
    for j in range(CONV_WIDTH - 1):
        back = CONV_WIDTH - 1 - j
        xc = xc + xbuf[SUBLANES - back:SUBLANES - back + tm, :] * cw_ref[j:j + 1, :]
    xbuf[0:SUBLANES, :] = xbuf[tm:tm + SUBLANES, :]

    xcb = xc.astype(jnp.bfloat16)
    r_parts, i_parts = [], []
    for c in range(N_LRU_CHUNKS):
        ri = jnp.dot(xcb[:, c * LRU_CHUNK:(c + 1) * LRU_CHUNK], wg_ref[c],
                     preferred_element_type=jnp.float32)
        r_parts.append(ri[:, :LRU_CHUNK])
        i_parts.append(ri[:, LRU_CHUNK:])
    yield
    g_a = project(2 * LRU_WIDTH)
    r = jax.nn.sigmoid(jnp.concatenate(r_parts, axis=1) + brg_ref[...])
    ig = jax.nn.sigmoid(jnp.concatenate(i_parts, axis=1) + big_ref[...])
    log_a = (-LRU_C * _softplus(-lam_ref[...])) * r
    a = jnp.exp(log_a)
    gap = 1.0 - a * a
    root = jnp.where(gap > 0.0, gap * lax.rsqrt(gap), 0.0)
    bb = root * (ig * xc)

    yield
    g_b = project(2 * LRU_WIDTH + D_MODEL)
    n_slab = LRU_WIDTH // LANES
    for c in range(n_slab):
        lanes = slice(c * LANES, (c + 1) * LANES)
        for s in range(SUBLANES):
            rows = slice(s * SCAN_SEG, (s + 1) * SCAN_SEG)
            a_sc[c, s * SCAN_PITCH:s * SCAN_PITCH + SCAN_SEG, :] = a[rows, lanes]
            b_sc[c, s * SCAN_PITCH:s * SCAN_PITCH + SCAN_SEG, :] = bb[rows, lanes]

    yield
    local = [jnp.zeros((SUBLANES, LANES), jnp.float32)] * n_slab
    decay = [jnp.ones((SUBLANES, LANES), jnp.float32)] * n_slab
    for j in range(SCAN_SEG):
        for c in range(n_slab):
            step = pl.ds(j, SUBLANES, stride=SCAN_PITCH)
            a_j = a_sc[c, step, :]
            local[c] = a_j * local[c] + b_sc[c, step, :]
            decay[c] = a_j * decay[c]
            b_sc[c, step, :] = local[c]
            a_sc[c, step, :] = decay[c]

    yield
    h_parts = []
    for c in range(n_slab):
        lanes = slice(c * LANES, (c + 1) * LANES)
        entry = h_sc[:, lanes]
        segs = []
        for s in range(SUBLANES):
            rows = slice(s * SCAN_PITCH, s * SCAN_PITCH + SCAN_SEG)
            segs.append(b_sc[c, rows, :] + a_sc[c, rows, :] * entry)
            entry = decay[c][s:s + 1, :] * entry + local[c][s:s + 1, :]
        h_sc[:, lanes] = entry
        h_parts.append(jnp.concatenate(segs, axis=0))
    h_all = jnp.concatenate(h_parts, axis=1)
    yield
    lru = (h_all * _gelu_tanh(g_lru)).astype(jnp.bfloat16)
    p_lru = jnp.dot(lru, wpl_ref[...], preferred_element_type=jnp.float32)

    yield
    dil = [d for _, d in DILATED_GROUPS]
    o1, o2, o3 = (_natural_order(ref, d, slabs[2 * g]) for g, (ref, d) in
                  enumerate(zip((o1_ref, o2_ref, o3_ref), dil)))
    l1, l2, l3 = (_natural_order(ref, d, slabs[2 * g + 1]) for g, (ref, d) in
                  enumerate(zip((l1_ref, l2_ref, l3_ref), dil)))
    lm = jnp.maximum(jnp.maximum(l1, l2), l3)
    e1, e2, e3 = jnp.exp(l1 - lm), jnp.exp(l2 - lm), jnp.exp(l3 - lm)
    attn = (e1 * o1 + e2 * o2 + e3 * o3) / (e1 + e2 + e3)
    p_attn = jnp.dot(attn.astype(jnp.bfloat16), wpa_ref[...], preferred_element_type=jnp.float32)

    yield
    merged = jax.nn.sigmoid(g_a) * p_attn + jax.nn.sigmoid(g_b) * p_lru
    x1 = x + jnp.dot(merged.astype(jnp.bfloat16), wout_ref[...], preferred_element_type=jnp.float32)
    x1_ref[...] = x1
    h2 = _rmsnorm(x1, n2_ref[...])

    yield
    h2_hi = h2.astype(jnp.bfloat16)
    h2_lo = (h2 - h2_hi.astype(jnp.float32)).astype(jnp.bfloat16)
    by_hi = jnp.dot(h2_hi, wr_ref[...], preferred_element_type=jnp.float32)
    by_lo = jnp.dot(h2_lo, wr_ref[...], preferred_element_type=jnp.float32)
    logits = (by_hi[:, :LANES] + (by_hi[:, LANES:] + by_lo[:, :LANES])) + by_lo[:, LANES:]
    lane = lax.broadcasted_iota(jnp.int32, (tm, LANES), 1)
    big = jnp.int32(LANES)
    lowest = jnp.float32(-3.0e38)
    is_g = (lane >= GROUP_LOGIT_LANE0) & (lane < GROUP_LOGIT_LANE0 + N_EXPERT_GROUPS)
    gl = jnp.where(is_g, logits, lowest)
    gmax = jnp.max(gl, axis=-1, keepdims=True)
    gsel = jnp.min(jnp.where(gl == gmax, lane, big), axis=-1, keepdims=True) - GROUP_LOGIT_LANE0
    p_sel = 1.0 / jnp.sum(jnp.where(is_g, jnp.exp(gl - gmax), 0.0), axis=-1, keepdims=True)
    is_e = (lane >= gsel * EXPERTS_PER_GROUP) & (lane < (gsel + 1) * EXPERTS_PER_GROUP)
    el = jnp.where(is_e, logits, lowest)
    v1 = jnp.max(el, axis=-1, keepdims=True)
    i1 = jnp.min(jnp.where(el == v1, lane, big), axis=-1, keepdims=True)
    el2 = jnp.where(lane == i1, lowest, el)
    v2 = jnp.max(el2, axis=-1, keepdims=True)
    i2 = jnp.min(jnp.where(el2 == v2, lane, big), axis=-1, keepdims=True)
    t = jnp.exp(v2 - v1)
    g0 = p_sel / (1.0 + t)
    g1 = p_sel * t / (1.0 + t)

    yield
    oh0 = lane == i1
    oh1 = lane == i2
    both = (oh0 | oh1).astype(jnp.float32)
    tri = (lax.broadcasted_iota(jnp.int32, (tm, tm), 0)
           > lax.broadcasted_iota(jnp.int32, (tm, tm), 1)).astype(jnp.bfloat16)
    before = jnp.dot(tri, both.astype(jnp.bfloat16), preferred_element_type=jnp.float32)
    cnt = jnp.sum(both, axis=0, keepdims=True)
    padded = jnp.floor((cnt + (MOE_CHUNK - 1.0)) * (1.0 / MOE_CHUNK)) * MOE_CHUNK
    upper = (lax.broadcasted_iota(jnp.int32, (LANES, LANES), 0)
             < lax.broadcasted_iota(jnp.int32, (LANES, LANES), 1)).astype(jnp.bfloat16)
    loff = jnp.dot(jnp.broadcast_to(padded, (SUBLANES, LANES)).astype(jnp.bfloat16), upper,
                   preferred_element_type=jnp.float32)[0:1, :]
    base = before + loff
    slot0 = jnp.sum(jnp.where(oh0, base, 0.0), axis=-1, keepdims=True)
    slot1 = jnp.sum(jnp.where(oh1, base, 0.0), axis=-1, keepdims=True)
    cnt_ref[...] = jnp.broadcast_to(cnt, cnt_ref.shape)

    slot_id = lax.broadcasted_iota(jnp.int32, (tm, MOE_SLOTS), 1)
    place = ((slot_id == slot0.astype(jnp.int32)) | (slot_id == slot1.astype(jnp.int32)))
    xs_ref[...] = _pack_bf16_pairs(
        lax.dot_general(place.astype(jnp.bfloat16), h2_hi, (((0,), (0,)), ((), ())),
                        preferred_element_type=jnp.float32))

    route = jnp.zeros((tm, LANES), jnp.float32)
    for lane_id, val in ((ROUTE_S0, slot0), (ROUTE_S1, slot1), (ROUTE_G0, g0), (ROUTE_G1, g1)):
        route = jnp.where(lane == lane_id, val, route)
    route_ref[...] = route


N_MIXER_TILED_INPUTS = 1 + 2 * N_GROUPS
N_MIXER_OUTPUTS = 4


def _mixer_kernel(*refs):
    n_batch = refs[0].shape[0]
    n_weights = len(refs) - N_MIXER_TILED_INPUTS - N_MIXER_OUTPUTS - n_batch * _MIXER_SCRATCH_PER_SEQ
    tiled = refs[:N_MIXER_TILED_INPUTS]
    weights = refs[N_MIXER_TILED_INPUTS:N_MIXER_TILED_INPUTS + n_weights]
    outs = refs[N_MIXER_TILED_INPUTS + n_weights:N_MIXER_TILED_INPUTS + n_weights + N_MIXER_OUTPUTS]
    scratch = refs[N_MIXER_TILED_INPUTS + n_weights + N_MIXER_OUTPUTS:]
    stages = []
    for b in range(n_batch):
        own = scratch[b * _MIXER_SCRATCH_PER_SEQ:(b + 1) * _MIXER_SCRATCH_PER_SEQ]
        stages.append(_mix_tile(*[r.at[b] for r in tiled], *weights, *[r.at[b] for r in outs], *own))
    while stages:
        for gen in list(stages):
            if next(gen, _DONE) is _DONE:
                stages.remove(gen)


def _mixer_scratch(tm):
    return [
        pltpu.VMEM((tm + 2 * SUBLANES, LRU_WIDTH), jnp.float32),
        pltpu.VMEM((LRU_WIDTH // LANES, SUBLANES * SCAN_PITCH, LANES), jnp.float32),
        pltpu.VMEM((LRU_WIDTH // LANES, SUBLANES * SCAN_PITCH, LANES), jnp.float32),
        pltpu.VMEM((1, LRU_WIDTH), jnp.float32),
    ] + [pltpu.VMEM((GROUP_WIDTH // LANES, tm, LANES), jnp.float32)] * (2 * N_GROUPS)


_MIXER_SCRATCH_PER_SEQ = len(_mixer_scratch(MIX_TILE))


def _token_mixer(x, attn_o, attn_lse, weights):
    B, S, _ = x.shape
    tm = MIX_TILE
    nt = S // tm
    rows = lambda j: (0, j, 0)

    def resident(shape):
        return pl.BlockSpec(shape, lambda j: (0,) * len(shape), pipeline_mode=pl.Buffered(1))

    in_specs = [pl.BlockSpec((B, tm, D_MODEL), rows)]
    group_specs = [pl.BlockSpec((B, d, tm // d, GROUP_WIDTH), lambda j: (0, 0, j, 0))
                   for _, d in DILATED_GROUPS]
    in_specs += group_specs + group_specs
    in_specs += [resident(w.shape) for w in weights]
    x1, xs, route, counts = pl.pallas_call(
        _mixer_kernel,
        grid=(nt,),
        in_specs=in_specs,
        out_specs=[
            pl.BlockSpec((B, tm, D_MODEL), rows),
            pl.BlockSpec((B, MOE_SLOTS, D_MODEL // 2), rows),
            pl.BlockSpec((B, tm, LANES), rows),
            pl.BlockSpec((B, None, SUBLANES, LANES), lambda j: (0, j, 0, 0)),
        ],
        out_shape=[
            jax.ShapeDtypeStruct((B, S, D_MODEL), jnp.float32),
            jax.ShapeDtypeStruct((B, nt * MOE_SLOTS, D_MODEL // 2), jnp.uint32),
            jax.ShapeDtypeStruct((B, S, LANES), jnp.float32),
            jax.ShapeDtypeStruct((B, nt, SUBLANES, LANES), jnp.float32),
        ],
        scratch_shapes=_mixer_scratch(tm) * B,
        compiler_params=pltpu.CompilerParams(
            dimension_semantics=("arbitrary",), vmem_limit_bytes=VMEM_LIMIT),
        name="token_mixer",
    )(x, *attn_o, *attn_lse, *weights)
    return (x1.reshape(B * S, D_MODEL), xs.reshape(B * nt * MOE_SLOTS, D_MODEL // 2),
            route.reshape(B * S, LANES), counts.reshape(B * nt, SUBLANES, LANES))


def _moe_tables(cnt_tile):
    ntiles = cnt_tile.shape[0]
    nch = (cnt_tile + MOE_CHUNK - 1) // MOE_CHUNK
    lo_c = jnp.cumsum(nch, axis=1) - nch
    per_expert = jnp.sum(nch, axis=0)
    region = ((per_expert + MOE_BLOCK_CHUNKS - 1) // MOE_BLOCK_CHUNKS) * MOE_BLOCK_CHUNKS
    pend = jnp.cumsum(region)
    glob = (pend - region)[None, :] + jnp.cumsum(nch, axis=0) - nch

    max_rows = TOP_K * ntiles * MIX_TILE + ntiles * N_EXPERTS * (MOE_CHUNK - 1)
    max_chunks = -(-max_rows // MOE_CHUNK) + N_EXPERTS * (MOE_BLOCK_CHUNKS - 1)
    max_blocks = -(-max_chunks // MOE_BLOCK_CHUNKS)
    max_blocks = -(-max_blocks // EXPERT_BLOCKS_PER_STEP) * EXPERT_BLOCKS_PER_STEP
    max_chunks = max_blocks * MOE_BLOCK_CHUNKS

    seg_start = glob.T.reshape(-1)
    seg_src = (jnp.arange(ntiles, dtype=jnp.int32)[:, None] * TILE_CHUNKS + lo_c).T.reshape(-1)
    step = jnp.diff(seg_src - seg_start, prepend=0)
    g = jnp.arange(max_chunks, dtype=jnp.int32)
    src = g + jnp.sum(jnp.where(seg_start[None, :] <= g[:, None], step[None, :], 0), axis=1)
    src = jnp.clip(src, 0, ntiles * TILE_CHUNKS - 1)

    lstep = jnp.diff(glob - lo_c, axis=1, prepend=0)
    lc = jnp.arange(TILE_CHUNKS, dtype=jnp.int32)
    comb = lc[None, :] + jnp.sum(
        jnp.where(lo_c[:, None, :] <= lc[None, :, None], lstep[:, None, :], 0), axis=2)
    comb = jnp.clip(comb, 0, max_chunks - 1).reshape(-1)

    eidx = jnp.arange(N_EXPERTS, dtype=jnp.int32)
    nonempty = region > 0
    later = jnp.where(nonempty[None, :] & (eidx[None, :] > eidx[:, None]), eidx[None, :], N_EXPERTS)
    next_expert = jnp.min(later, axis=1)
    next_expert = jnp.where(next_expert == N_EXPERTS, -1, next_expert)
    wslot = (jnp.cumsum(nonempty.astype(jnp.int32)) - nonempty.astype(jnp.int32)) % 2
    last_expert = jnp.max(jnp.where(nonempty, eidx, 0))
    blk0 = jnp.arange(max_blocks, dtype=jnp.int32) * MOE_BLOCK_CHUNKS
    block_expert = jnp.minimum(jnp.sum((pend[None, :] <= blk0[:, None]).astype(jnp.int32), axis=1),
                               last_expert)
    of_block = lambda per_expert: jnp.sum(
        jnp.where(block_expert[:, None] == eidx[None, :], per_expert[None, :], 0), axis=1)
    n_valid = pend[-1:] // MOE_BLOCK_CHUNKS
    as_i32 = lambda v: v.astype(jnp.int32)
    expert_tables = (as_i32(block_expert), as_i32(of_block(next_expert)), as_i32(of_block(wslot)),
                     as_i32(n_valid), as_i32(src))
    return expert_tables, as_i32(comb)


def _chunk_copy(src_hbm, src_chunk, dst, dst_chunk, sem):
    rows = lambda c: pl.ds(pl.multiple_of(c * MOE_CHUNK, MOE_CHUNK), MOE_CHUNK)
    return pltpu.make_async_copy(src_hbm.at[rows(src_chunk), :], dst.at[rows(dst_chunk), :], sem)


def _expert_kernel(be_ref, nxt_ref, ws_ref, nv_ref, src_ref, xs_hbm, wgu_hbm, wd_hbm, ys_ref, *scratch):
    for k in range(EXPERT_BLOCKS_PER_STEP):
        _expert_block(pl.program_id(0) * EXPERT_BLOCKS_PER_STEP + k,
                      be_ref, nxt_ref, ws_ref, nv_ref, src_ref, xs_hbm, wgu_hbm, wd_hbm,
                      ys_ref.at[pl.ds(k * MOE_TILE, MOE_TILE), :], *scratch)


def _expert_block(i, be_ref, nxt_ref, ws_ref, nv_ref, src_ref, xs_hbm, wgu_hbm, wd_hbm, ys_ref,
                  xbuf, wgu_f32, wd_f32, wgu_bf, wd_bf, sems, wsems):
    valid = i < nv_ref[0]

    def gather(blk, slot):
        return [_chunk_copy(xs_hbm, src_ref[blk * MOE_BLOCK_CHUNKS + c], xbuf.at[slot], c, sems.at[slot])
                for c in range(MOE_BLOCK_CHUNKS)]

    def fetch_weights(e, slot):
        return [pltpu.make_async_copy(wgu_hbm.at[e], wgu_f32.at[slot], wsems.at[slot]),
                pltpu.make_async_copy(wd_hbm.at[e], wd_f32.at[slot], wsems.at[slot])]

    ahead = EXPERT_GATHER_SLOTS - 1

    @pl.when(i == 0)
    def _():
        for cp in fetch_weights(be_ref[0], 0):
            cp.start(priority=WEIGHT_DMA_PRIORITY)
        for first in range(ahead):
            @pl.when(first < nv_ref[0])
            def _():
                for cp in gather(first, first):
                    cp.start()

    @pl.when(i + ahead < nv_ref[0])
    def _():
        for cp in gather(i + ahead, (i + ahead) % EXPERT_GATHER_SLOTS):
            cp.start()

    new_expert = valid & ((i == 0) | (be_ref[i] != be_ref[jnp.maximum(i - 1, 0)]))

    @pl.when(new_expert)
    def _():
        ws = ws_ref[i]
        for cp in fetch_weights(be_ref[i], ws):
            cp.wait()
        wgu_bf[...] = wgu_f32[ws].astype(jnp.bfloat16)
        wd_bf[...] = wd_f32[ws].astype(jnp.bfloat16)

        @pl.when(nxt_ref[i] >= 0)
        def _():
            for cp in fetch_weights(nxt_ref[i], 1 - ws):
                cp.start(priority=WEIGHT_DMA_PRIORITY)

    @pl.when(valid)
    def _():
        slot = i % EXPERT_GATHER_SLOTS
        for cp in gather(i, slot):
            cp.wait()
        x_lo, x_hi = _unpack_bf16_pairs(xbuf[slot])
        half = D_MODEL // 2
        partial = []

        def up_proj(cols):
            return (jnp.dot(x_lo, wgu_bf[:half, cols], preferred_element_type=jnp.float32)
                    + jnp.dot(x_hi, wgu_bf[half:, cols], preferred_element_type=jnp.float32))

        def hidden_group(c):
            cols = slice(c * EXPERT_GROUP, (c + 1) * EXPERT_GROUP)
            g = up_proj(cols)
            u = up_proj(slice(D_EXPERT + cols.start, D_EXPERT + cols.stop))
            yield
            act = (g * jax.nn.sigmoid(g) * u).astype(jnp.bfloat16)
            yield
            partial.append(jnp.dot(act, wd_bf[cols, :], preferred_element_type=jnp.float32))

        waiting = [hidden_group(c) for c in range(D_EXPERT // EXPERT_GROUP)]
        live = []
        while waiting or live:
            if waiting:
                live.append(waiting.pop(0))
            for gen in list(live):
                if next(gen, _DONE) is _DONE:
                    live.remove(gen)
        y = sum(partial[1:], partial[0])
        ys_ref[...] = _pack_bf16_pairs(y.astype(jnp.bfloat16).astype(jnp.float32))

    @pl.when(jnp.logical_not(valid))
    def _():
        ys_ref[...] = jnp.zeros_like(ys_ref)


def _experts(tables, xs, w_gate_up, w_down):
    block_expert, block_next, block_wslot, n_valid, src = tables
    nblk = block_expert.shape[0]
    return pl.pallas_call(
        _expert_kernel,
        grid_spec=pltpu.PrefetchScalarGridSpec(
            num_scalar_prefetch=5,
            grid=(nblk // EXPERT_BLOCKS_PER_STEP,),
            in_specs=[pl.BlockSpec(memory_space=pl.ANY)] * 3,
            out_specs=pl.BlockSpec((EXPERT_BLOCKS_PER_STEP * MOE_TILE, D_MODEL // 2),
                                   lambda i, *_: (i, 0)),
            scratch_shapes=[
                pltpu.VMEM((EXPERT_GATHER_SLOTS, MOE_TILE, D_MODEL // 2), jnp.uint32),
                pltpu.VMEM((2, D_MODEL, 2 * D_EXPERT), jnp.float32),
                pltpu.VMEM((2, D_EXPERT, D_MODEL), jnp.float32),
                pltpu.VMEM((D_MODEL, 2 * D_EXPERT), jnp.bfloat16),
                pltpu.VMEM((D_EXPERT, D_MODEL), jnp.bfloat16),
                pltpu.SemaphoreType.DMA((EXPERT_GATHER_SLOTS,)),
                pltpu.SemaphoreType.DMA((2,)),
            ],
        ),
        out_shape=jax.ShapeDtypeStruct((nblk * MOE_TILE, D_MODEL // 2), jnp.uint32),
        compiler_params=pltpu.CompilerParams(
            dimension_semantics=("arbitrary",), vmem_limit_bytes=VMEM_LIMIT),
        name="moe_experts",
    )(block_expert, block_next, block_wslot, n_valid, src, xs, w_gate_up, w_down)


def _combine_kernel(comb_ref, ys_hbm, x1_ref, route_ref, nf_ref, out_ref, ybuf, sems):
    tc = MIX_TILE
    per_step = COMBINE_TILES_PER_STEP
    i = pl.program_id(0)
    last = pl.num_programs(0) - 1

    def gather(step, slot):
        first = step * per_step * TILE_CHUNKS
        return [_chunk_copy(ys_hbm, comb_ref[first + lc], ybuf.at[slot], lc, sems.at[slot])
                for lc in range(per_step * TILE_CHUNKS)]

    @pl.when(i == 0)
    def _():
        for cp in gather(0, 0):
            cp.start()

    nxt = jnp.minimum(i + 1, last)
    for cp in gather(nxt, (i + 1) % 2):
        cp.start()

    slot = i % 2
    for cp in gather(i, slot):
        cp.wait()

    def tile(t):
        rows = slice(t * tc, (t + 1) * tc)
        y_lo, y_hi = _unpack_bf16_pairs(ybuf[slot, t * MOE_SLOTS:(t + 1) * MOE_SLOTS, :])
        route = route_ref[rows, :]
        slot_id = lax.broadcasted_iota(jnp.int32, (tc, MOE_SLOTS), 1)
        picked = []
        for lane in (ROUTE_S0, ROUTE_S1):
            sel = (slot_id == route[:, lane:lane + 1].astype(jnp.int32)).astype(jnp.bfloat16)
            yield
            picked.append(jnp.concatenate([jnp.dot(sel, y_lo, preferred_element_type=jnp.float32),
                                           jnp.dot(sel, y_hi, preferred_element_type=jnp.float32)], axis=1))
            yield
        g0 = route[:, ROUTE_G0:ROUTE_G0 + 1]
        g1 = route[:, ROUTE_G1:ROUTE_G1 + 1]
        out_ref[rows, :] = _rmsnorm(x1_ref[rows, :] + g0 * picked[0] + g1 * picked[1], nf_ref[...])

    waiting = [tile(t) for t in range(per_step)]
    live = []
    while waiting or live:
        if waiting:
            live.append(waiting.pop(0))
        for gen in list(live):
            if next(gen, _DONE) is _DONE:
                live.remove(gen)

    @pl.when(i == last)
    def _():
        for cp in gather(nxt, (i + 1) % 2):
            cp.wait()


def _combine(comb, ys, x1, route, norm_f):
    T = x1.shape[0]
    tc = MIX_TILE * COMBINE_TILES_PER_STEP
    return pl.pallas_call(
        _combine_kernel,
        grid_spec=pltpu.PrefetchScalarGridSpec(
            num_scalar_prefetch=1,
            grid=(T // tc,),
            in_specs=[
                pl.BlockSpec(memory_space=pl.ANY),
                pl.BlockSpec((tc, D_MODEL), lambda i, c: (i, 0)),
                pl.BlockSpec((tc, LANES), lambda i, c: (i, 0)),
                pl.BlockSpec((1, D_MODEL), lambda i, c: (0, 0)),
            ],
            out_specs=pl.BlockSpec((tc, D_MODEL), lambda i, c: (i, 0)),
            scratch_shapes=[
                pltpu.VMEM((2, COMBINE_TILES_PER_STEP * MOE_SLOTS, D_MODEL // 2), jnp.uint32),
                pltpu.SemaphoreType.DMA((2,)),
            ],
        ),
        out_shape=jax.ShapeDtypeStruct((T, D_MODEL), jnp.float32),
        compiler_params=pltpu.CompilerParams(
            dimension_semantics=("arbitrary",), vmem_limit_bytes=VMEM_LIMIT),
        name="moe_combine",
    )(comb, ys, x1, route, norm_f)


def _block_diag_gates(w_rg, w_ig):
    def bd(w):
        w4 = w.reshape(N_LRU_CHUNKS, LRU_CHUNK_HEADS, LRU_HEAD_DIM, LRU_HEAD_DIM)
        eye = jnp.eye(LRU_CHUNK_HEADS, dtype=w.dtype)
        return jnp.einsum('chij,hk->chikj', w4, eye).reshape(N_LRU_CHUNKS, LRU_CHUNK, LRU_CHUNK)
    return jnp.concatenate([bd(w_rg), bd(w_ig)], axis=-1).astype(jnp.bfloat16)


def kernel(x, rel_bias, norm1, w_in, conv_w, conv_b, w_rg, b_rg, w_ig, b_ig, lru_lambda,
           w_proj_attn, w_proj_lru, w_out, norm2, w_router_group, w_router_expert,
           w_gate_up, w_down, norm_f):
    B, S, D = x.shape
    T = B * S
    assert w_in.shape[0] == 1, "single-layer block"
    layer = 0
    bf16 = jnp.bfloat16
    x2 = x.reshape(T, D)
    row = lambda v: v[layer].reshape(1, -1)
    w_in_bf = w_in[layer].astype(bf16)
    w_qkv = (w_in_bf[:, :QKV_WIDTH].reshape(D, 3, N_GROUPS, GROUP_WIDTH)
             .transpose(0, 2, 1, 3).reshape(D, QKV_WIDTH))
    w_rest = w_in_bf[:, QKV_WIDTH:]

    qkv_groups = _qkv_projection(x2, row(norm1), w_qkv, B, S)
    attn_o, attn_lse = [], []
    for gi, (window, dilation) in enumerate(DILATED_GROUPS):
        hs = slice(gi * HEADS_PER_GROUP, (gi + 1) * HEADS_PER_GROUP)
        bias = _attention_bias_table(rel_bias[:, hs], window, dilation)
        o, lse = _attention_group(qkv_groups[gi], bias, gi, dilation)
        attn_o.append(o)
        attn_lse.append(lse)

    w_router = jnp.concatenate(
        [w_router_expert[layer].astype(jnp.float32), w_router_group[layer].astype(jnp.float32),
         jnp.zeros((D, LANES - N_EXPERTS - N_EXPERT_GROUPS), jnp.float32)], axis=1)
    w_router_hi = w_router.astype(bf16)
    weights = [
        row(norm1), w_rest, conv_w[layer], row(conv_b),
        _block_diag_gates(w_rg[layer], w_ig[layer]), row(b_rg), row(b_ig), row(lru_lambda),
        w_proj_attn[layer].astype(bf16), w_proj_lru[layer].astype(bf16), w_out[layer].astype(bf16),
        row(norm2),
        jnp.concatenate([w_router_hi, (w_router - w_router_hi.astype(jnp.float32)).astype(bf16)], axis=1),
    ]
    x1, xs, route, counts = _token_mixer(x, attn_o, attn_lse, weights)

    expert_tables, comb = _moe_tables(counts[:, 0, :N_EXPERTS].astype(jnp.int32))
    ys = _experts(expert_tables, xs, w_gate_up[layer], w_down[layer])
    out = _combine(comb, ys, x1, route, norm_f.reshape(1, -1))
    return out.reshape(B, S, D)
```

```python
import functools
import math

import jax
import jax.numpy as jnp
import numpy as np
from jax import lax
from jax.experimental import pallas as pl
from jax.experimental.pallas import tpu as pltpu

D_MODEL = 1024
HEAD_DIM = 64
HEADS_PER_GROUP = 4
DILATED_GROUPS = ((128, 1), (512, 4), (2048, 16))
N_GROUPS = len(DILATED_GROUPS)
N_ATTN_HEADS = HEADS_PER_GROUP * N_GROUPS
ATTN_WIDTH = N_ATTN_HEADS * HEAD_DIM
GROUP_WIDTH = HEADS_PER_GROUP * HEAD_DIM
QKV_WIDTH = 3 * ATTN_WIDTH
ATTN_BLOCK = 128
N_REL_BUCKETS = 32
REL_MAX_DISTANCE = 2048
NEG_INF = -1e30

LRU_WIDTH = D_MODEL
LRU_HEADS = 16
LRU_HEAD_DIM = LRU_WIDTH // LRU_HEADS
CONV_WIDTH = 4
LRU_C = 8.0
LRU_CHUNK_HEADS = 4
LRU_CHUNK = LRU_CHUNK_HEADS * LRU_HEAD_DIM
N_LRU_CHUNKS = LRU_WIDTH // LRU_CHUNK

N_EXPERT_GROUPS = 4
EXPERTS_PER_GROUP = 8
N_EXPERTS = N_EXPERT_GROUPS * EXPERTS_PER_GROUP
TOP_K = 2
D_EXPERT = 512
EPS = 1e-6

LANES = 128
SUBLANES = 8

ATTN_TILE = 256
ATTN_TILES_PER_STEP = 4
ATTN_QB = 8
MIX_TILE = 256
SCAN_SEG = MIX_TILE // SUBLANES
SCAN_PITCH = SCAN_SEG + SUBLANES
MOE_TILE = 256
EXPERT_GATHER_SLOTS = 6
EXPERT_BLOCKS_PER_STEP = 8
COMBINE_TILES_PER_STEP = 4
WEIGHT_DMA_PRIORITY = 1
EXPERT_GROUP = 256
MOE_CHUNK = SUBLANES
MOE_BLOCK_CHUNKS = MOE_TILE // MOE_CHUNK
MOE_SLOTS = 768
assert MOE_SLOTS >= TOP_K * MIX_TILE + N_EXPERTS * (MOE_CHUNK - 1) and MOE_SLOTS % MOE_CHUNK == 0
TILE_CHUNKS = MOE_SLOTS // MOE_CHUNK
VMEM_LIMIT = 56 * 1024 * 1024

ROUTE_S0, ROUTE_S1, ROUTE_G0, ROUTE_G1 = range(4)
GROUP_LOGIT_LANE0 = N_EXPERTS


_DONE = object()


def _rmsnorm(x, g):
    return x * lax.rsqrt(jnp.mean(x * x, axis=-1, keepdims=True) + EPS) * g


def _pack_bf16_pairs(v):
    w = v.shape[1] // 2
    bits = lax.bitcast_convert_type(v, jnp.uint32)
    return bits[:, w:] | (bits[:, :w] >> 16)


def _unpack_bf16_pairs(p):
    lo = lax.bitcast_convert_type(p << 16, jnp.float32)
    hi = lax.bitcast_convert_type(p & jnp.uint32(0xFFFF0000), jnp.float32)
    return lo.astype(jnp.bfloat16), hi.astype(jnp.bfloat16)


def _qkv_kernel(x_ref, g_ref, w_ref, *refs):
    tm = ATTN_TILE
    o_refs, slab_refs = refs[:N_GROUPS], refs[N_GROUPS:]
    n_slab = ATTN_WIDTH // LANES

    def tile(t):
        rows = slice(t * tm, (t + 1) * tm)
        slabs = slab_refs[t]
        h = _rmsnorm(x_ref[rows, :], g_ref[...]).astype(jnp.bfloat16)
        yield
        for gi, (o_ref, (_, d)) in enumerate(zip(o_refs, DILATED_GROUPS)):
            cols = slice(gi * ATTN_WIDTH, (gi + 1) * ATTN_WIDTH)
            part = jnp.dot(h, w_ref[:, cols], preferred_element_type=jnp.float32)
            out_rows = slice(t * (tm // d), (t + 1) * (tm // d))
            if d == 1:
                o_ref[0, out_rows, :] = part.astype(jnp.bfloat16)
                yield
                continue
            for c in range(n_slab):
                slabs[c] = part[:, c * LANES:(c + 1) * LANES]
            yield
            for r in range(d):
                o_ref[r, out_rows, :] = jnp.concatenate(
                    [slabs[c, pl.ds(r, tm // d, stride=d), :] for c in range(n_slab)],
                    axis=1).astype(jnp.bfloat16)
            yield

    waiting = [tile(t) for t in range(ATTN_TILES_PER_STEP)]
    live = []
    while waiting or live:
        if waiting:
            live.append(waiting.pop(0))
        for gen in list(live):
            if next(gen, _DONE) is _DONE:
                live.remove(gen)


def _qkv_projection(x2, norm1, w_qkv, B, S):
    tm = ATTN_TILE * ATTN_TILES_PER_STEP
    nt = S // tm
    return pl.pallas_call(
        _qkv_kernel,
        grid=(B, nt),
        in_specs=[
            pl.BlockSpec((tm, D_MODEL), lambda b, j: (b * nt + j, 0)),
            pl.BlockSpec((1, D_MODEL), lambda b, j: (0, 0)),
            pl.BlockSpec((D_MODEL, QKV_WIDTH), lambda b, j: (0, 0)),
        ],
        out_specs=[pl.BlockSpec((None, d, tm // d, ATTN_WIDTH), lambda b, j: (b, 0, j, 0))
                   for _, d in DILATED_GROUPS],
        out_shape=[jax.ShapeDtypeStruct((B, d, S // d, ATTN_WIDTH), jnp.bfloat16)
                   for _, d in DILATED_GROUPS],
        scratch_shapes=[pltpu.VMEM((ATTN_WIDTH // LANES, ATTN_TILE, LANES), jnp.float32)]
        * ATTN_TILES_PER_STEP,
        compiler_params=pltpu.CompilerParams(
            dimension_semantics=("parallel", "parallel"), vmem_limit_bytes=VMEM_LIMIT),
        name="qkv_projection",
    )(x2, norm1, w_qkv)


def _attn_residue(n_qb, q_ref, kp_ref, kc_ref, vp_ref, vc_ref, bias0_ref, bias_ref, o_ref, lse_ref):
    q = q_ref[...] * (HEAD_DIM ** -0.5)
    k = jnp.concatenate([kp_ref[...], kc_ref[...]], axis=0)
    v = jnp.concatenate([vp_ref[...], vc_ref[...]], axis=0)
    pair_w = 2 * HEAD_DIM
    first_head = lax.broadcasted_iota(jnp.int32, (1, pair_w), 1) < HEAD_DIM
    def one_head(x, first):
        bits = pltpu.bitcast(x, jnp.uint32)
        keep = first_head if first else jnp.logical_not(first_head)
        return pltpu.bitcast(jnp.where(keep, bits, jnp.uint32(0)), jnp.bfloat16)

    ones = jnp.ones_like(k[:, :pair_w])
    operands = []
    for pair in range(HEADS_PER_GROUP // 2):
        cols = slice(pair * pair_w, (pair + 1) * pair_w)
        k2, v2 = k[:, cols], v[:, cols]
        k_heads = (one_head(k2, True), one_head(k2, False))
        v_heads = (jnp.concatenate([one_head(v2, True), one_head(ones, True)], axis=1),
                   jnp.concatenate([one_head(v2, False), one_head(ones, False)], axis=1))
        operands.append((cols, k_heads, v_heads))
    done = {}

    def head_pair(b, pair):
        rows = slice(b * ATTN_BLOCK, (b + 1) * ATTN_BLOCK)
        keys = slice(b * ATTN_BLOCK, (b + 2) * ATTN_BLOCK)
        cols, k_heads, v_heads = operands[pair]
        bias_ref_b = bias0_ref if b == 0 else bias_ref
        bias = jnp.concatenate([bias_ref_b[2 * pair], bias_ref_b[2 * pair + 1]], axis=1)
        k_both = jnp.concatenate([k_heads[0][keys], k_heads[1][keys]], axis=0)
        s = lax.dot_general(q[rows, cols], k_both, (((1,), (1,)), ((), ())),
                            preferred_element_type=jnp.float32) + bias
        yield
        n_keys = 2 * ATTN_BLOCK
        s0, s1 = s[:, :n_keys], s[:, n_keys:]
        m0 = jnp.max(s0, axis=-1, keepdims=True)
        m1 = jnp.max(s1, axis=-1, keepdims=True)
        p0 = jnp.exp(s0 - m0).astype(jnp.bfloat16)
        p1 = jnp.exp(s1 - m1).astype(jnp.bfloat16)
        yield
        acc = (jnp.dot(p0, v_heads[0][keys], preferred_element_type=jnp.float32)
               + jnp.dot(p1, v_heads[1][keys], preferred_element_type=jnp.float32))
        yield
        denom = acc[:, pair_w:]
        done[b, pair] = (acc[:, :pair_w] / denom, jnp.where(first_head, m0, m1) + jnp.log(denom))

    problems = [head_pair(b, pair) for b in range(n_qb) for pair in range(HEADS_PER_GROUP // 2)]

    def store_results():
        for b in range(n_qb):
            rows = slice(b * ATTN_BLOCK, (b + 1) * ATTN_BLOCK)
            pairs = [done[b, pair] for pair in range(HEADS_PER_GROUP // 2)]
            o_ref[rows, :] = jnp.concatenate([o for o, _ in pairs], axis=1).astype(o_ref.dtype)
            lse_ref[rows, :] = jnp.concatenate([lse for _, lse in pairs], axis=1)

    return problems, store_results


def _attn_kernel(n_qb, n_res, q_ref, kp_ref, kc_ref, vp_ref, vc_ref, bias0_ref, bias_ref, o_ref, lse_ref):
    waiting, stores = [], []
    for ri in range(n_res):
        problems, store = _attn_residue(n_qb, q_ref.at[ri], kp_ref.at[ri], kc_ref.at[ri], vp_ref.at[ri],
                                        vc_ref.at[ri], bias0_ref, bias_ref, o_ref.at[ri], lse_ref.at[ri])
        waiting += problems
        stores.append(store)
    live = []
    while waiting or live:
        if waiting:
            live.append(waiting.pop(0))
        for gen in list(live):
            if next(gen, _DONE) is _DONE:
                live.remove(gen)
    for store in stores:
        store()


def _attention_group(qkv_g, bias, gi, dilation):
    B, _, L, _ = qkv_g.shape
    n_qb = min(ATTN_QB, L // ATTN_BLOCK)
    rows = n_qb * ATTN_BLOCK
    n_res = max(1, min(dilation, ATTN_QB // n_qb))
    cur = lambda which: pl.BlockSpec((None, n_res, rows, GROUP_WIDTH), lambda b, r, n: (b, r, n, which))
    prev = lambda which: pl.BlockSpec((None, n_res, ATTN_BLOCK, GROUP_WIDTH),
                                      lambda b, r, n: (b, r, jnp.maximum(n * n_qb - 1, 0), which))
    bias_blk = (None, HEADS_PER_GROUP, ATTN_BLOCK, 2 * ATTN_BLOCK)
    return pl.pallas_call(
        functools.partial(_attn_kernel, n_qb, n_res),
        grid=(B, dilation // n_res, L // rows),
        in_specs=[
            cur(0), prev(1), cur(1), prev(2), cur(2),
            pl.BlockSpec(bias_blk, lambda b, r, n: (jnp.minimum(n, 1), 0, 0, 0)),
            pl.BlockSpec(bias_blk, lambda b, r, n: (1, 0, 0, 0)),
        ],
        out_specs=[cur(0), cur(0)],
        out_shape=[
            jax.ShapeDtypeStruct((B, dilation, L, GROUP_WIDTH), jnp.bfloat16),
            jax.ShapeDtypeStruct((B, dilation, L, GROUP_WIDTH), jnp.float32),
        ],
        compiler_params=pltpu.CompilerParams(
            dimension_semantics=("parallel", "parallel", "arbitrary"), vmem_limit_bytes=VMEM_LIMIT),
        name=f"dilated_attention_g{gi}",
    )(qkv_g, qkv_g, qkv_g, qkv_g, qkv_g, bias, bias)


def _t5_causal_bucket(dist):
    max_exact = N_REL_BUCKETS // 2
    d_f = np.maximum(dist, max_exact).astype(np.float32)
    scaled = (np.log(d_f / np.float32(max_exact)) / np.float32(math.log(REL_MAX_DISTANCE / max_exact))
              * np.float32(N_REL_BUCKETS - max_exact))
    large = np.minimum(max_exact + scaled.astype(np.int32), N_REL_BUCKETS - 1)
    return np.where(dist < max_exact, dist, large).astype(np.int32)


def _attention_bias_table(rel_bias_g, window, dilation):
    nw = window // dilation
    qi = np.arange(ATTN_BLOCK)[:, None]
    ki = np.arange(2 * ATTN_BLOCK)[None, :]
    dist = ATTN_BLOCK + qi - ki
    band = (dist >= 0) & (dist <= nw)
    bucket = _t5_causal_bucket(np.maximum(dist, 0) * dilation)
    onehot = (jnp.asarray(bucket)[:, :, None] == jnp.arange(N_REL_BUCKETS)).astype(jnp.float32)
    bias = jnp.einsum('qkb,bh->hqk', onehot, rel_bias_g.astype(jnp.float32),
                      precision=lax.Precision.HIGHEST)
    masks = jnp.asarray(np.stack([band & (ki >= ATTN_BLOCK), band]))
    return jnp.where(masks[:, None], bias[None], NEG_INF)


def _gelu_tanh(x):
    return 0.5 * x * (1.0 + jnp.tanh(math.sqrt(2.0 / math.pi) * (x + 0.044715 * (x * x * x))))


def _softplus(x):
    return jnp.maximum(x, 0.0) + jnp.log(1.0 + jnp.exp(-jnp.abs(x)))


def _natural_order(blk_ref, d, slabs):
    if d == 1:
        return blk_ref[0].astype(jnp.float32)
    per = MIX_TILE // d
    for r in range(d):
        rows = blk_ref[r].astype(jnp.float32)
        for c in range(GROUP_WIDTH // LANES):
            slabs[c, pl.ds(r, per, stride=d), :] = rows[:, c * LANES:(c + 1) * LANES]
    return jnp.concatenate([slabs[c] for c in range(GROUP_WIDTH // LANES)], axis=1)


def _mix_tile(x_ref, o1_ref, o2_ref, o3_ref, l1_ref, l2_ref, l3_ref,
              n1_ref, win_ref, cw_ref, cb_ref, wg_ref, brg_ref, big_ref, lam_ref,
              wpa_ref, wpl_ref, wout_ref, n2_ref, wr_ref,
              x1_ref, xs_ref, route_ref, cnt_ref,
              xbuf, a_sc, b_sc, h_sc, *slabs):
    tm = MIX_TILE
    first_tile = pl.program_id(0) == 0

    @pl.when(first_tile)
    def _():
        xbuf[0:SUBLANES, :] = jnp.zeros((SUBLANES, LRU_WIDTH), jnp.float32)
        h_sc[...] = jnp.zeros_like(h_sc)

    x = x_ref[...]
    h = _rmsnorm(x, n1_ref[...]).astype(jnp.bfloat16)
    def project(first_col):
        return jnp.dot(h, win_ref[:, first_col:first_col + D_MODEL], preferred_element_type=jnp.float32)

    xr = project(0)

    yield
    g_lru = project(LRU_WIDTH)
    xbuf[SUBLANES:SUBLANES + tm, :] = xr
    xc = xr * cw_ref[CONV_WIDTH - 1:CONV_WIDTH, :] + cb_ref[...]
    for j in range(CONV_WIDTH - 1):
        back = CONV_WIDTH - 1 - j
        xc = xc + xbuf[SUBLANES - back:SUBLANES - back + tm, :] * cw_ref[j:j + 1, :]
    xbuf[0:SUBLANES, :] = xbuf[tm:tm + SUBLANES, :]

    xcb = xc.astype(jnp.bfloat16)
    r_parts, i_parts = [], []
    for c in range(N_LRU_CHUNKS):
        ri = jnp.dot(xcb[:, c * LRU_CHUNK:(c + 1) * LRU_CHUNK], wg_ref[c],
                     preferred_element_type=jnp.float32)
        r_parts.append(ri[:, :LRU_CHUNK])
        i_parts.append(ri[:, LRU_CHUNK:])
    yield
    g_a = project(2 * LRU_WIDTH)
    r = jax.nn.sigmoid(jnp.concatenate(r_parts, axis=1) + brg_ref[...])
    ig = jax.nn.sigmoid(jnp.concatenate(i_parts, axis=1) + big_ref[...])
    log_a = (-LRU_C * _softplus(-lam_ref[...])) * r
    a = jnp.exp(log_a)
    gap = 1.0 - a * a
    root = jnp.where(gap > 0.0, gap * lax.rsqrt(gap), 0.0)
    bb = root * (ig * xc)

    yield
    g_b = project(2 * LRU_WIDTH + D_MODEL)
    n_slab = LRU_WIDTH // LANES
    for c in range(n_slab):
        lanes = slice(c * LANES, (c + 1) * LANES)
        for s in range(SUBLANES):
            rows = slice(s * SCAN_SEG, (s + 1) * SCAN_SEG)
            a_sc[c, s * SCAN_PITCH:s * SCAN_PITCH + SCAN_SEG, :] = a[rows, lanes]
            b_sc[c, s * SCAN_PITCH:s * SCAN_PITCH + SCAN_SEG, :] = bb[rows, lanes]

    yield
    local = [jnp.zeros((SUBLANES, LANES), jnp.float32)] * n_slab
    decay = [jnp.ones((SUBLANES, LANES), jnp.float32)] * n_slab
    for j in range(SCAN_SEG):
        for c in range(n_slab):
            step = pl.ds(j, SUBLANES, stride=SCAN_PITCH)
            a_j = a_sc[c, step, :]
            local[c] = a_j * local[c] + b_sc[c, step, :]
            decay[c] = a_j * decay[c]
            b_sc[c, step, :] = local[c]
            a_sc[c, step, :] = decay[c]

    yield
    h_parts = []
    for c in range(n_slab):
        lanes = slice(c * LANES, (c + 1) * LANES)
        entry = h_sc[:, lanes]
        segs = []
        for s in range(SUBLANES):
            rows = slice(s * SCAN_PITCH, s * SCAN_PITCH + SCAN_SEG)
            segs.append(b_sc[c, rows, :] + a_sc[c, rows, :] * entry)
            entry = decay[c][s:s + 1, :] * entry + local[c][s:s + 1, :]
        h_sc[:, lanes] = entry
        h_parts.append(jnp.concatenate(segs, axis=0))
    h_all = jnp.concatenate(h_parts, axis=1)
    yield
    lru = (h_all * _gelu_tanh(g_lru)).astype(jnp.bfloat16)
    p_lru = jnp.dot(lru, wpl_ref[...], preferred_element_type=jnp.float32)

    yield
    dil = [d for _, d in DILATED_GROUPS]
    o1, o2, o3 = (_natural_order(ref, d, slabs[2 * g]) for g, (ref, d) in
                  enumerate(zip((o1_ref, o2_ref, o3_ref), dil)))
    l1, l2, l3 = (_natural_order(ref, d, slabs[2 * g + 1]) for g, (ref, d) in
                  enumerate(zip((l1_ref, l2_ref, l3_ref), dil)))
    lm = jnp.maximum(jnp.maximum(l1, l2), l3)
    e1, e2, e3 = jnp.exp(l1 - lm), jnp.exp(l2 - lm), jnp.exp(l3 - lm)
    attn = (e1 * o1 + e2 * o2 + e3 * o3) / (e1 + e2 + e3)
    p_attn = jnp.dot(attn.astype(jnp.bfloat16), wpa_ref[...], preferred_element_type=jnp.float32)

    yield
    merged = jax.nn.sigmoid(g_a) * p_attn + jax.nn.sigmoid(g_b) * p_lru
    x1 = x + jnp.dot(merged.astype(jnp.bfloat16), wout_ref[...], preferred_element_type=jnp.float32)
    x1_ref[...] = x1
    h2 = _rmsnorm(x1, n2_ref[...])

    yield
    h2_hi = h2.astype(jnp.bfloat16)
    h2_lo = (h2 - h2_hi.astype(jnp.float32)).astype(jnp.bfloat16)
    by_hi = jnp.dot(h2_hi, wr_ref[...], preferred_element_type=jnp.float32)
    by_lo = jnp.dot(h2_lo, wr_ref[...], preferred_element_type=jnp.float32)
    logits = (by_hi[:, :LANES] + (by_hi[:, LANES:] + by_lo[:, :LANES])) + by_lo[:, LANES:]
    lane = lax.broadcasted_iota(jnp.int32, (tm, LANES), 1)
    big = jnp.int32(LANES)
    lowest = jnp.float32(-3.0e38)
    is_g = (lane >= GROUP_LOGIT_LANE0) & (lane < GROUP_LOGIT_LANE0 + N_EXPERT_GROUPS)
    gl = jnp.where(is_g, logits, lowest)
    gmax = jnp.max(gl, axis=-1, keepdims=True)
    gsel = jnp.min(jnp.where(gl == gmax, lane, big), axis=-1, keepdims=True) - GROUP_LOGIT_LANE0
    p_sel = 1.0 / jnp.sum(jnp.where(is_g, jnp.exp(gl - gmax), 0.0), axis=-1, keepdims=True)
    is_e = (lane >= gsel * EXPERTS_PER_GROUP) & (lane < (gsel + 1) * EXPERTS_PER_GROUP)
    el = jnp.where(is_e, logits, lowest)
    v1 = jnp.max(el, axis=-1, keepdims=True)
    i1 = jnp.min(jnp.where(el == v1, lane, big), axis=-1, keepdims=True)
    el2 = jnp.where(lane == i1, lowest, el)
    v2 = jnp.max(el2, axis=-1, keepdims=True)
    i2 = jnp.min(jnp.where(el2 == v2, lane, big), axis=-1, keepdims=True)
    t = jnp.exp(v2 - v1)
    g0 = p_sel / (1.0 + t)
    g1 = p_sel * t / (1.0 + t)

    yield
    oh0 = lane == i1
    oh1 = lane == i2
    both = (oh0 | oh1).astype(jnp.float32)
    tri = (lax.broadcasted_iota(jnp.int32, (tm, tm), 0)
           > lax.broadcasted_iota(jnp.int32, (tm, tm), 1)).astype(jnp.bfloat16)
    before = jnp.dot(tri, both.astype(jnp.bfloat16), preferred_element_type=jnp.float32)
    cnt = jnp.sum(both, axis=0, keepdims=True)
    padded = jnp.floor((cnt + (MOE_CHUNK - 1.0)) * (1.0 / MOE_CHUNK)) * MOE_CHUNK
    upper = (lax.broadcasted_iota(jnp.int32, (LANES, LANES), 0)
             < lax.broadcasted_iota(jnp.int32, (LANES, LANES), 1)).astype(jnp.bfloat16)
    loff = jnp.dot(jnp.broadcast_to(padded, (SUBLANES, LANES)).astype(jnp.bfloat16), upper,
                   preferred_element_type=jnp.float32)[0:1, :]
    base = before + loff
    slot0 = jnp.sum(jnp.where(oh0, base, 0.0), axis=-1, keepdims=True)
    slot1 = jnp.sum(jnp.where(oh1, base, 0.0), axis=-1, keepdims=True)
    cnt_ref[...] = jnp.broadcast_to(cnt, cnt_ref.shape)

    slot_id = lax.broadcasted_iota(jnp.int32, (tm, MOE_SLOTS), 1)
    place = ((slot_id == slot0.astype(jnp.int32)) | (slot_id == slot1.astype(jnp.int32)))
    xs_ref[...] = _pack_bf16_pairs(
        lax.dot_general(place.astype(jnp.bfloat16), h2_hi, (((0,), (0,)), ((), ())),
                        preferred_element_type=jnp.float32))

    route = jnp.zeros((tm, LANES), jnp.float32)
    for lane_id, val in ((ROUTE_S0, slot0), (ROUTE_S1, slot1), (ROUTE_G0, g0), (ROUTE_G1, g1)):
        route = jnp.where(lane == lane_id, val, route)
    route_ref[...] = route


N_MIXER_TILED_INPUTS = 1 + 2 * N_GROUPS
N_MIXER_OUTPUTS = 4


def _mixer_kernel(*refs):
    n_batch = refs[0].shape[0]
    n_weights = len(refs) - N_MIXER_TILED_INPUTS - N_MIXER_OUTPUTS - n_batch * _MIXER_SCRATCH_PER_SEQ
    tiled = refs[:N_MIXER_TILED_INPUTS]
    weights = refs[N_MIXER_TILED_INPUTS:N_MIXER_TILED_INPUTS + n_weights]
    outs = refs[N_MIXER_TILED_INPUTS + n_weights:N_MIXER_TILED_INPUTS + n_weights + N_MIXER_OUTPUTS]
    scratch = refs[N_MIXER_TILED_INPUTS + n_weights + N_MIXER_OUTPUTS:]
    stages = []
    for b in range(n_batch):
        own = scratch[b * _MIXER_SCRATCH_PER_SEQ:(b + 1) * _MIXER_SCRATCH_PER_SEQ]
        stages.append(_mix_tile(*[r.at[b] for r in tiled], *weights, *[r.at[b] for r in outs], *own))
    while stages:
        for gen in list(stages):
            if next(gen, _DONE) is _DONE:
                stages.remove(gen)


def _mixer_scratch(tm):
    return [
        pltpu.VMEM((tm + 2 * SUBLANES, LRU_WIDTH), jnp.float32),
        pltpu.VMEM((LRU_WIDTH // LANES, SUBLANES * SCAN_PITCH, LANES), jnp.float32),
        pltpu.VMEM((LRU_WIDTH // LANES, SUBLANES * SCAN_PITCH, LANES), jnp.float32),
        pltpu.VMEM((1, LRU_WIDTH), jnp.float32),
    ] + [pltpu.VMEM((GROUP_WIDTH // LANES, tm, LANES), jnp.float32)] * (2 * N_GROUPS)


_MIXER_SCRATCH_PER_SEQ = len(_mixer_scratch(MIX_TILE))


def _token_mixer(x, attn_o, attn_lse, weights):
    B, S, _ = x.shape
    tm = MIX_TILE
    nt = S // tm
    rows = lambda j: (0, j, 0)

    def resident(shape):
        return pl.BlockSpec(shape, lambda j: (0,) * len(shape), pipeline_mode=pl.Buffered(1))

    in_specs = [pl.BlockSpec((B, tm, D_MODEL), rows)]
    group_specs = [pl.BlockSpec((B, d, tm // d, GROUP_WIDTH), lambda j: (0, 0, j, 0))
                   for _, d in DILATED_GROUPS]
    in_specs += group_specs + group_specs
    in_specs += [resident(w.shape) for w in weights]
    x1, xs, route, counts = pl.pallas_call(
        _mixer_kernel,
        grid=(nt,),
        in_specs=in_specs,
        out_specs=[
            pl.BlockSpec((B, tm, D_MODEL), rows),
            pl.BlockSpec((B, MOE_SLOTS, D_MODEL // 2), rows),
            pl.BlockSpec((B, tm, LANES), rows),
            pl.BlockSpec((B, None, SUBLANES, LANES), lambda j: (0, j, 0, 0)),
        ],
        out_shape=[
            jax.ShapeDtypeStruct((B, S, D_MODEL), jnp.float32),
            jax.ShapeDtypeStruct((B, nt * MOE_SLOTS, D_MODEL // 2), jnp.uint32),
            jax.ShapeDtypeStruct((B, S, LANES), jnp.float32),
            jax.ShapeDtypeStruct((B, nt, SUBLANES, LANES), jnp.float32),
        ],
        scratch_shapes=_mixer_scratch(tm) * B,
        compiler_params=pltpu.CompilerParams(
            dimension_semantics=("arbitrary",), vmem_limit_bytes=VMEM_LIMIT),
        name="token_mixer",
    )(x, *attn_o, *attn_lse, *weights)
    return (x1.reshape(B * S, D_MODEL), xs.reshape(B * nt * MOE_SLOTS, D_MODEL // 2),
            route.reshape(B * S, LANES), counts.reshape(B * nt, SUBLANES, LANES))


def _moe_tables(cnt_tile):
    ntiles = cnt_tile.shape[0]
    nch = (cnt_tile + MOE_CHUNK - 1) // MOE_CHUNK
    lo_c = jnp.cumsum(nch, axis=1) - nch
    per_expert = jnp.sum(nch, axis=0)
    region = ((per_expert + MOE_BLOCK_CHUNKS - 1) // MOE_BLOCK_CHUNKS) * MOE_BLOCK_CHUNKS
    pend = jnp.cumsum(region)
    glob = (pend - region)[None, :] + jnp.cumsum(nch, axis=0) - nch

    max_rows = TOP_K * ntiles * MIX_TILE + ntiles * N_EXPERTS * (MOE_CHUNK - 1)
    max_chunks = -(-max_rows // MOE_CHUNK) + N_EXPERTS * (MOE_BLOCK_CHUNKS - 1)
    max_blocks = -(-max_chunks // MOE_BLOCK_CHUNKS)
    max_blocks = -(-max_blocks // EXPERT_BLOCKS_PER_STEP) * EXPERT_BLOCKS_PER_STEP
    max_chunks = max_blocks * MOE_BLOCK_CHUNKS

    seg_start = glob.T.reshape(-1)
    seg_src = (jnp.arange(ntiles, dtype=jnp.int32)[:, None] * TILE_CHUNKS + lo_c).T.reshape(-1)
    step = jnp.diff(seg_src - seg_start, prepend=0)
    g = jnp.arange(max_chunks, dtype=jnp.int32)
    src = g + jnp.sum(jnp.where(seg_start[None, :] <= g[:, None], step[None, :], 0), axis=1)
    src = jnp.clip(src, 0, ntiles * TILE_CHUNKS - 1)

    lstep = jnp.diff(glob - lo_c, axis=1, prepend=0)
    lc = jnp.arange(TILE_CHUNKS, dtype=jnp.int32)
    comb = lc[None, :] + jnp.sum(
        jnp.where(lo_c[:, None, :] <= lc[None, :, None], lstep[:, None, :], 0), axis=2)
    comb = jnp.clip(comb, 0, max_chunks - 1).reshape(-1)

    eidx = jnp.arange(N_EXPERTS, dtype=jnp.int32)
    nonempty = region > 0
    later = jnp.where(nonempty[None, :] & (eidx[None, :] > eidx[:, None]), eidx[None, :], N_EXPERTS)
    next_expert = jnp.min(later, axis=1)
    next_expert = jnp.where(next_expert == N_EXPERTS, -1, next_expert)
    wslot = (jnp.cumsum(nonempty.astype(jnp.int32)) - nonempty.astype(jnp.int32)) % 2
    last_expert = jnp.max(jnp.where(nonempty, eidx, 0))
    blk0 = jnp.arange(max_blocks, dtype=jnp.int32) * MOE_BLOCK_CHUNKS
    block_expert = jnp.minimum(jnp.sum((pend[None, :] <= blk0[:, None]).astype(jnp.int32), axis=1),
                               last_expert)
    of_block = lambda per_expert: jnp.sum(
        jnp.where(block_expert[:, None] == eidx[None, :], per_expert[None, :], 0), axis=1)
    n_valid = pend[-1:] // MOE_BLOCK_CHUNKS
    as_i32 = lambda v: v.astype(jnp.int32)
    expert_tables = (as_i32(block_expert), as_i32(of_block(next_expert)), as_i32(of_block(wslot)),
                     as_i32(n_valid), as_i32(src))
    return expert_tables, as_i32(comb)


def _chunk_copy(src_hbm, src_chunk, dst, dst_chunk, sem):
    rows = lambda c: pl.ds(pl.multiple_of(c * MOE_CHUNK, MOE_CHUNK), MOE_CHUNK)
    return pltpu.make_async_copy(src_hbm.at[rows(src_chunk), :], dst.at[rows(dst_chunk), :], sem)


def _expert_kernel(be_ref, nxt_ref, ws_ref, nv_ref, src_ref, xs_hbm, wgu_hbm, wd_hbm, ys_ref, *scratch):
    for k in range(EXPERT_BLOCKS_PER_STEP):
        _expert_block(pl.program_id(0) * EXPERT_BLOCKS_PER_STEP + k,
                      be_ref, nxt_ref, ws_ref, nv_ref, src_ref, xs_hbm, wgu_hbm, wd_hbm,
                      ys_ref.at[pl.ds(k * MOE_TILE, MOE_TILE), :], *scratch)


def _expert_block(i, be_ref, nxt_ref, ws_ref, nv_ref, src_ref, xs_hbm, wgu_hbm, wd_hbm, ys_ref,
                  xbuf, wgu_f32, wd_f32, wgu_bf, wd_bf, sems, wsems):
    valid = i < nv_ref[0]

    def gather(blk, slot):
        return [_chunk_copy(xs_hbm, src_ref[blk * MOE_BLOCK_CHUNKS + c], xbuf.at[slot], c, sems.at[slot])
                for c in range(MOE_BLOCK_CHUNKS)]

    def fetch_weights(e, slot):
        return [pltpu.make_async_copy(wgu_hbm.at[e], wgu_f32.at[slot], wsems.at[slot]),
                pltpu.make_async_copy(wd_hbm.at[e], wd_f32.at[slot], wsems.at[slot])]

    ahead = EXPERT_GATHER_SLOTS - 1

    @pl.when(i == 0)
    def _():
        for cp in fetch_weights(be_ref[0], 0):
            cp.start(priority=WEIGHT_DMA_PRIORITY)
        for first in range(ahead):
            @pl.when(first < nv_ref[0])
            def _():
                for cp in gather(first, first):
                    cp.start()

    @pl.when(i + ahead < nv_ref[0])
    def _():
        for cp in gather(i + ahead, (i + ahead) % EXPERT_GATHER_SLOTS):
            cp.start()

    new_expert = valid & ((i == 0) | (be_ref[i] != be_ref[jnp.maximum(i - 1, 0)]))

    @pl.when(new_expert)
    def _():
        ws = ws_ref[i]
        for cp in fetch_weights(be_ref[i], ws):
            cp.wait()
        wgu_bf[...] = wgu_f32[ws].astype(jnp.bfloat16)
        wd_bf[...] = wd_f32[ws].astype(jnp.bfloat16)

        @pl.when(nxt_ref[i] >= 0)
        def _():
            for cp in fetch_weights(nxt_ref[i], 1 - ws):
                cp.start(priority=WEIGHT_DMA_PRIORITY)

    @pl.when(valid)
    def _():
        slot = i % EXPERT_GATHER_SLOTS
        for cp in gather(i, slot):
            cp.wait()
        x_lo, x_hi = _unpack_bf16_pairs(xbuf[slot])
        half = D_MODEL // 2
        partial = []

        def up_proj(cols):
            return (jnp.dot(x_lo, wgu_bf[:half, cols], preferred_element_type=jnp.float32)
                    + jnp.dot(x_hi, wgu_bf[half:, cols], preferred_element_type=jnp.float32))

        def hidden_group(c):
            cols = slice(c * EXPERT_GROUP, (c + 1) * EXPERT_GROUP)
            g = up_proj(cols)
            u = up_proj(slice(D_EXPERT + cols.start, D_EXPERT + cols.stop))
            yield
            act = (g * jax.nn.sigmoid(g) * u).astype(jnp.bfloat16)
            yield
            partial.append(jnp.dot(act, wd_bf[cols, :], preferred_element_type=jnp.float32))

        waiting = [hidden_group(c) for c in range(D_EXPERT // EXPERT_GROUP)]
        live = []
        while waiting or live:
            if waiting:
                live.append(waiting.pop(0))
            for gen in list(live):
                if next(gen, _DONE) is _DONE:
                    live.remove(gen)
        y = sum(partial[1:], partial[0])
        ys_ref[...] = _pack_bf16_pairs(y.astype(jnp.bfloat16).astype(jnp.float32))

    @pl.when(jnp.logical_not(valid))
    def _():
        ys_ref[...] = jnp.zeros_like(ys_ref)


def _experts(tables, xs, w_gate_up, w_down):
    block_expert, block_next, block_wslot, n_valid, src = tables
    nblk = block_expert.shape[0]
    return pl.pallas_call(
        _expert_kernel,
        grid_spec=pltpu.PrefetchScalarGridSpec(
            num_scalar_prefetch=5,
            grid=(nblk // EXPERT_BLOCKS_PER_STEP,),
            in_specs=[pl.BlockSpec(memory_space=pl.ANY)] * 3,
            out_specs=pl.BlockSpec((EXPERT_BLOCKS_PER_STEP * MOE_TILE, D_MODEL // 2),
                                   lambda i, *_: (i, 0)),
            scratch_shapes=[
                pltpu.VMEM((EXPERT_GATHER_SLOTS, MOE_TILE, D_MODEL // 2), jnp.uint32),
                pltpu.VMEM((2, D_MODEL, 2 * D_EXPERT), jnp.float32),
                pltpu.VMEM((2, D_EXPERT, D_MODEL), jnp.float32),
                pltpu.VMEM((D_MODEL, 2 * D_EXPERT), jnp.bfloat16),
                pltpu.VMEM((D_EXPERT, D_MODEL), jnp.bfloat16),
                pltpu.SemaphoreType.DMA((EXPERT_GATHER_SLOTS,)),
                pltpu.SemaphoreType.DMA((2,)),
            ],
        ),
        out_shape=jax.ShapeDtypeStruct((nblk * MOE_TILE, D_MODEL // 2), jnp.uint32),
        compiler_params=pltpu.CompilerParams(
            dimension_semantics=("arbitrary",), vmem_limit_bytes=VMEM_LIMIT),
        name="moe_experts",
    )(block_expert, block_next, block_wslot, n_valid, src, xs, w_gate_up, w_down)


def _combine_kernel(comb_ref, ys_hbm, x1_ref, route_ref, nf_ref, out_ref, ybuf, sems):
    tc = MIX_TILE
    per_step = COMBINE_TILES_PER_STEP
    i = pl.program_id(0)
    last = pl.num_programs(0) - 1

    def gather(step, slot):
        first = step * per_step * TILE_CHUNKS
        return [_chunk_copy(ys_hbm, comb_ref[first + lc], ybuf.at[slot], lc, sems.at[slot])
                for lc in range(per_step * TILE_CHUNKS)]

    @pl.when(i == 0)
    def _():
        for cp in gather(0, 0):
            cp.start()

    nxt = jnp.minimum(i + 1, last)
    for cp in gather(nxt, (i + 1) % 2):
        cp.start()

    slot = i % 2
    for cp in gather(i, slot):
        cp.wait()

    def tile(t):
        rows = slice(t * tc, (t + 1) * tc)
        y_lo, y_hi = _unpack_bf16_pairs(ybuf[slot, t * MOE_SLOTS:(t + 1) * MOE_SLOTS, :])
        route = route_ref[rows, :]
        slot_id = lax.broadcasted_iota(jnp.int32, (tc, MOE_SLOTS), 1)
        picked = []
        for lane in (ROUTE_S0, ROUTE_S1):
            sel = (slot_id == route[:, lane:lane + 1].astype(jnp.int32)).astype(jnp.bfloat16)
            yield
            picked.append(jnp.concatenate([jnp.dot(sel, y_lo, preferred_element_type=jnp.float32),
                                           jnp.dot(sel, y_hi, preferred_element_type=jnp.float32)], axis=1))
            yield
        g0 = route[:, ROUTE_G0:ROUTE_G0 + 1]
        g1 = route[:, ROUTE_G1:ROUTE_G1 + 1]
        out_ref[rows, :] = _rmsnorm(x1_ref[rows, :] + g0 * picked[0] + g1 * picked[1], nf_ref[...])

    waiting = [tile(t) for t in range(per_step)]
    live = []
    while waiting or live:
        if waiting:
            live.append(waiting.pop(0))
        for gen in list(live):
            if next(gen, _DONE) is _DONE:
                live.remove(gen)

    @pl.when(i == last)
    def _():
        for cp in gather(nxt, (i + 1) % 2):
            cp.wait()


def _combine(comb, ys, x1, route, norm_f):
    T = x1.shape[0]
    tc = MIX_TILE * COMBINE_TILES_PER_STEP
    return pl.pallas_call(
        _combine_kernel,
        grid_spec=pltpu.PrefetchScalarGridSpec(
            num_scalar_prefetch=1,
            grid=(T // tc,),
            in_specs=[
                pl.BlockSpec(memory_space=pl.ANY),
                pl.BlockSpec((tc, D_MODEL), lambda i, c: (i, 0)),
                pl.BlockSpec((tc, LANES), lambda i, c: (i, 0)),
                pl.BlockSpec((1, D_MODEL), lambda i, c: (0, 0)),
            ],
            out_specs=pl.BlockSpec((tc, D_MODEL), lambda i, c: (i, 0)),
            scratch_shapes=[
                pltpu.VMEM((2, COMBINE_TILES_PER_STEP * MOE_SLOTS, D_MODEL // 2), jnp.uint32),
                pltpu.SemaphoreType.DMA((2,)),
            ],
        ),
        out_shape=jax.ShapeDtypeStruct((T, D_MODEL), jnp.float32),
        compiler_params=pltpu.CompilerParams(
            dimension_semantics=("arbitrary",), vmem_limit_bytes=VMEM_LIMIT),
        name="moe_combine",
    )(comb, ys, x1, route, norm_f)


def _block_diag_gates(w_rg, w_ig):
    def bd(w):
        w4 = w.reshape(N_LRU_CHUNKS, LRU_CHUNK_HEADS, LRU_HEAD_DIM, LRU_HEAD_DIM)
        eye = jnp.eye(LRU_CHUNK_HEADS, dtype=w.dtype)
        return jnp.einsum('chij,hk->chikj', w4, eye).reshape(N_LRU_CHUNKS, LRU_CHUNK, LRU_CHUNK)
    return jnp.concatenate([bd(w_rg), bd(w_ig)], axis=-1).astype(jnp.bfloat16)


def kernel(x, rel_bias, norm1, w_in, conv_w, conv_b, w_rg, b_rg, w_ig, b_ig, lru_lambda,
           w_proj_attn, w_proj_lru, w_out, norm2, w_router_group, w_router_expert,
           w_gate_up, w_down, norm_f):
    B, S, D = x.shape
    T = B * S
    assert w_in.shape[0] == 1, "single-layer block"
    layer = 0
    bf16 = jnp.bfloat16
    x2 = x.reshape(T, D)
    row = lambda v: v[layer].reshape(1, -1)
    w_in_bf = w_in[layer].astype(bf16)
    w_qkv = (w_in_bf[:, :QKV_WIDTH].reshape(D, 3, N_GROUPS, GROUP_WIDTH)
             .transpose(0, 2, 1, 3).reshape(D, QKV_WIDTH))
    w_rest = w_in_bf[:, QKV_WIDTH:]

    qkv_groups = _qkv_projection(x2, row(norm1), w_qkv, B, S)
    attn_o, attn_lse = [], []
    for gi, (window, dilation) in enumerate(DILATED_GROUPS):
        hs = slice(gi * HEADS_PER_GROUP, (gi + 1) * HEADS_PER_GROUP)
        bias = _attention_bias_table(rel_bias[:, hs], window, dilation)
        o, lse = _attention_group(qkv_groups[gi], bias, gi, dilation)
        attn_o.append(o)
        attn_lse.append(lse)

    w_router = jnp.concatenate(
        [w_router_expert[layer].astype(jnp.float32), w_router_group[layer].astype(jnp.float32),
         jnp.zeros((D, LANES - N_EXPERTS - N_EXPERT_GROUPS), jnp.float32)], axis=1)
    w_router_hi = w_router.astype(bf16)
    weights = [
        row(norm1), w_rest, conv_w[layer], row(conv_b),
        _block_diag_gates(w_rg[layer], w_ig[layer]), row(b_rg), row(b_ig), row(lru_lambda),
        w_proj_attn[layer].astype(bf16), w_proj_lru[layer].astype(bf16), w_out[layer].astype(bf16),
        row(norm2),
        jnp.concatenate([w_router_hi, (w_router - w_router_hi.astype(jnp.float32)).astype(bf16)], axis=1),
    ]
    x1, xs, route, counts = _token_mixer(x, attn_o, attn_lse, weights)

    expert_tables, comb = _moe_tables(counts[:, 0, :N_EXPERTS].astype(jnp.int32))
    ys = _experts(expert_tables, xs, w_gate_up[layer], w_down[layer])
    out = _combine(comb, ys, x1, route, norm_f.reshape(1, -1))
    return out.reshape(B, S, D)
```

```python
import functools
import math

import jax
import jax.numpy as jnp
import numpy as np
from jax import lax
from jax.experimental import pallas as pl
from jax.experimental.pallas import tpu as pltpu

D_MODEL = 1024
HEAD_DIM = 64
HEADS_PER_GROUP = 4
DILATED_GROUPS = ((128, 1), (512, 4), (2048, 16))
N_GROUPS = len(DILATED_GROUPS)
N_ATTN_HEADS = HEADS_PER_GROUP * N_GROUPS
ATTN_WIDTH = N_ATTN_HEADS * HEAD_DIM
GROUP_WIDTH = HEADS_PER_GROUP * HEAD_DIM
QKV_WIDTH = 3 * ATTN_WIDTH
ATTN_BLOCK = 128
N_REL_BUCKETS = 32
REL_MAX_DISTANCE = 2048
NEG_INF = -1e30

LRU_WIDTH = D_MODEL
LRU_HEADS = 16
LRU_HEAD_DIM = LRU_WIDTH // LRU_HEADS
CONV_WIDTH = 4
LRU_C = 8.0
LRU_CHUNK_HEADS = 4
LRU_CHUNK = LRU_CHUNK_HEADS * LRU_HEAD_DIM
N_LRU_CHUNKS = LRU_WIDTH // LRU_CHUNK

N_EXPERT_GROUPS = 4
EXPERTS_PER_GROUP = 8
N_EXPERTS = N_EXPERT_GROUPS * EXPERTS_PER_GROUP
TOP_K = 2
D_EXPERT = 512
EPS = 1e-6

LANES = 128
SUBLANES = 8

ATTN_TILE = 256
ATTN_TILES_PER_STEP = 4
ATTN_QB = 8
MIX_TILE = 256
SCAN_SEG = MIX_TILE // SUBLANES
SCAN_PITCH = SCAN_SEG + SUBLANES
MOE_TILE = 256
EXPERT_GATHER_SLOTS = 4
EXPERT_BLOCKS_PER_STEP = 8
COMBINE_TILES_PER_STEP = 4
WEIGHT_DMA_PRIORITY = 1
EXPERT_GROUP = 256
MOE_CHUNK = SUBLANES
MOE_BLOCK_CHUNKS = MOE_TILE // MOE_CHUNK
MOE_SLOTS = 768
assert MOE_SLOTS >= TOP_K * MIX_TILE + N_EXPERTS * (MOE_CHUNK - 1) and MOE_SLOTS % MOE_CHUNK == 0
TILE_CHUNKS = MOE_SLOTS // MOE_CHUNK
VMEM_LIMIT = 56 * 1024 * 1024

ROUTE_S0, ROUTE_S1, ROUTE_G0, ROUTE_G1 = range(4)
GROUP_LOGIT_LANE0 = N_EXPERTS


_DONE = object()


def _rmsnorm(x, g):
    return x * lax.rsqrt(jnp.mean(x * x, axis=-1, keepdims=True) + EPS) * g


def _pack_bf16_pairs(v):
    w = v.shape[1] // 2
    bits = lax.bitcast_convert_type(v, jnp.uint32)
    return bits[:, w:] | (bits[:, :w] >> 16)


def _unpack_bf16_pairs(p):
    lo = lax.bitcast_convert_type(p << 16, jnp.float32)
    hi = lax.bitcast_convert_type(p & jnp.uint32(0xFFFF0000), jnp.float32)
    return lo.astype(jnp.bfloat16), hi.astype(jnp.bfloat16)


def _qkv_kernel(x_ref, g_ref, w_ref, *refs):
    tm = ATTN_TILE
    o_refs, slab_refs = refs[:N_GROUPS], refs[N_GROUPS:]
    n_slab = ATTN_WIDTH // LANES

    def tile(t):
        rows = slice(t * tm, (t + 1) * tm)
        slabs = slab_refs[t]
        h = _rmsnorm(x_ref[rows, :], g_ref[...]).astype(jnp.bfloat16)
        yield
        for gi, (o_ref, (_, d)) in enumerate(zip(o_refs, DILATED_GROUPS)):
            cols = slice(gi * ATTN_WIDTH, (gi + 1) * ATTN_WIDTH)
            part = jnp.dot(h, w_ref[:, cols], preferred_element_type=jnp.float32)
            out_rows = slice(t * (tm // d), (t + 1) * (tm // d))
            if d == 1:
                o_ref[0, out_rows, :] = part.astype(jnp.bfloat16)
                yield
                continue
            for c in range(n_slab):
                slabs[c] = part[:, c * LANES:(c + 1) * LANES]
            yield
            for r in range(d):
                o_ref[r, out_rows, :] = jnp.concatenate(
                    [slabs[c, pl.ds(r, tm // d, stride=d), :] for c in range(n_slab)],
                    axis=1).astype(jnp.bfloat16)
            yield

    waiting = [tile(t) for t in range(ATTN_TILES_PER_STEP)]
    live = []
    while waiting or live:
        if waiting:
            live.append(waiting.pop(0))
        for gen in list(live):
            if next(gen, _DONE) is _DONE:
                live.remove(gen)


def _qkv_projection(x2, norm1, w_qkv, B, S):
    tm = ATTN_TILE * ATTN_TILES_PER_STEP
    nt = S // tm
    return pl.pallas_call(
        _qkv_kernel,
        grid=(B, nt),
        in_specs=[
            pl.BlockSpec((tm, D_MODEL), lambda b, j: (b * nt + j, 0)),
            pl.BlockSpec((1, D_MODEL), lambda b, j: (0, 0)),
            pl.BlockSpec((D_MODEL, QKV_WIDTH), lambda b, j: (0, 0)),
        ],
        out_specs=[pl.BlockSpec((None, d, tm // d, ATTN_WIDTH), lambda b, j: (b, 0, j, 0))
                   for _, d in DILATED_GROUPS],
        out_shape=[jax.ShapeDtypeStruct((B, d, S // d, ATTN_WIDTH), jnp.bfloat16)
                   for _, d in DILATED_GROUPS],
        scratch_shapes=[pltpu.VMEM((ATTN_WIDTH // LANES, ATTN_TILE, LANES), jnp.float32)]
        * ATTN_TILES_PER_STEP,
        compiler_params=pltpu.CompilerParams(
            dimension_semantics=("parallel", "parallel"), vmem_limit_bytes=VMEM_LIMIT),
        name="qkv_projection",
    )(x2, norm1, w_qkv)


def _attn_residue(n_qb, q_ref, kp_ref, kc_ref, vp_ref, vc_ref, bias0_ref, bias_ref, o_ref, lse_ref):
    q = q_ref[...] * (HEAD_DIM ** -0.5)
    k = jnp.concatenate([kp_ref[...], kc_ref[...]], axis=0)
    v = jnp.concatenate([vp_ref[...], vc_ref[...]], axis=0)
    pair_w = 2 * HEAD_DIM
    first_head = lax.broadcasted_iota(jnp.int32, (1, pair_w), 1) < HEAD_DIM
    def one_head(x, first):
        bits = pltpu.bitcast(x, jnp.uint32)
        keep = first_head if first else jnp.logical_not(first_head)
        return pltpu.bitcast(jnp.where(keep, bits, jnp.uint32(0)), jnp.bfloat16)

    ones = jnp.ones_like(k[:, :pair_w])
    operands = []
    for pair in range(HEADS_PER_GROUP // 2):
        cols = slice(pair * pair_w, (pair + 1) * pair_w)
        k2, v2 = k[:, cols], v[:, cols]
        k_heads = (one_head(k2, True), one_head(k2, False))
        v_heads = (jnp.concatenate([one_head(v2, True), one_head(ones, True)], axis=1),
                   jnp.concatenate([one_head(v2, False), one_head(ones, False)], axis=1))
        operands.append((cols, k_heads, v_heads))
    done = {}

    def head_pair(b, pair):
        rows = slice(b * ATTN_BLOCK, (b + 1) * ATTN_BLOCK)
        keys = slice(b * ATTN_BLOCK, (b + 2) * ATTN_BLOCK)
        cols, k_heads, v_heads = operands[pair]
        bias_ref_b = bias0_ref if b == 0 else bias_ref
        bias = jnp.concatenate([bias_ref_b[2 * pair], bias_ref_b[2 * pair + 1]], axis=1)
        k_both = jnp.concatenate([k_heads[0][keys], k_heads[1][keys]], axis=0)
        s = lax.dot_general(q[rows, cols], k_both, (((1,), (1,)), ((), ())),
                            preferred_element_type=jnp.float32) + bias
        yield
        n_keys = 2 * ATTN_BLOCK
        s0, s1 = s[:, :n_keys], s[:, n_keys:]
        m0 = jnp.max(s0, axis=-1, keepdims=True)
        m1 = jnp.max(s1, axis=-1, keepdims=True)
        p0 = jnp.exp(s0 - m0).astype(jnp.bfloat16)
        p1 = jnp.exp(s1 - m1).astype(jnp.bfloat16)
        yield
        acc = (jnp.dot(p0, v_heads[0][keys], preferred_element_type=jnp.float32)
               + jnp.dot(p1, v_heads[1][keys], preferred_element_type=jnp.float32))
        yield
        denom = acc[:, pair_w:]
        done[b, pair] = (acc[:, :pair_w] / denom, jnp.where(first_head, m0, m1) + jnp.log(denom))

    problems = [head_pair(b, pair) for b in range(n_qb) for pair in range(HEADS_PER_GROUP // 2)]

    def store_results():
        for b in range(n_qb):
            rows = slice(b * ATTN_BLOCK, (b + 1) * ATTN_BLOCK)
            pairs = [done[b, pair] for pair in range(HEADS_PER_GROUP // 2)]
            o_ref[rows, :] = jnp.concatenate([o for o, _ in pairs], axis=1).astype(o_ref.dtype)
            lse_ref[rows, :] = jnp.concatenate([lse for _, lse in pairs], axis=1)

    return problems, store_results


def _attn_kernel(n_qb, n_res, q_ref, kp_ref, kc_ref, vp_ref, vc_ref, bias0_ref, bias_ref, o_ref, lse_ref):
    waiting, stores = [], []
    for ri in range(n_res):
        problems, store = _attn_residue(n_qb, q_ref.at[ri], kp_ref.at[ri], kc_ref.at[ri], vp_ref.at[ri],
                                        vc_ref.at[ri], bias0_ref, bias_ref, o_ref.at[ri], lse_ref.at[ri])
        waiting += problems
        stores.append(store)
    live = []
    while waiting or live:
        if waiting:
            live.append(waiting.pop(0))
        for gen in list(live):
            if next(gen, _DONE) is _DONE:
                live.remove(gen)
    for store in stores:
        store()


def _attention_group(qkv_g, bias, gi, dilation):
    B, _, L, _ = qkv_g.shape
    n_qb = min(ATTN_QB, L // ATTN_BLOCK)
    rows = n_qb * ATTN_BLOCK
    n_res = max(1, min(dilation, ATTN_QB // n_qb))
    cur = lambda which: pl.BlockSpec((None, n_res, rows, GROUP_WIDTH), lambda b, r, n: (b, r, n, which))
    prev = lambda which: pl.BlockSpec((None, n_res, ATTN_BLOCK, GROUP_WIDTH),
                                      lambda b, r, n: (b, r, jnp.maximum(n * n_qb - 1, 0), which))
    bias_blk = (None, HEADS_PER_GROUP, ATTN_BLOCK, 2 * ATTN_BLOCK)
    return pl.pallas_call(
        functools.partial(_attn_kernel, n_qb, n_res),
        grid=(B, dilation // n_res, L // rows),
        in_specs=[
            cur(0), prev(1), cur(1), prev(2), cur(2),
            pl.BlockSpec(bias_blk, lambda b, r, n: (jnp.minimum(n, 1), 0, 0, 0)),
            pl.BlockSpec(bias_blk, lambda b, r, n: (1, 0, 0, 0)),
        ],
        out_specs=[cur(0), cur(0)],
        out_shape=[
            jax.ShapeDtypeStruct((B, dilation, L, GROUP_WIDTH), jnp.bfloat16),
            jax.ShapeDtypeStruct((B, dilation, L, GROUP_WIDTH), jnp.float32),
        ],
        compiler_params=pltpu.CompilerParams(
            dimension_semantics=("parallel", "parallel", "arbitrary"), vmem_limit_bytes=VMEM_LIMIT),
        name=f"dilated_attention_g{gi}",
    )(qkv_g, qkv_g, qkv_g, qkv_g, qkv_g, bias, bias)


def _t5_causal_bucket(dist):
    max_exact = N_REL_BUCKETS // 2
    d_f = np.maximum(dist, max_exact).astype(np.float32)
    scaled = (np.log(d_f / np.float32(max_exact)) / np.float32(math.log(REL_MAX_DISTANCE / max_exact))
              * np.float32(N_REL_BUCKETS - max_exact))
    large = np.minimum(max_exact + scaled.astype(np.int32), N_REL_BUCKETS - 1)
    return np.where(dist < max_exact, dist, large).astype(np.int32)


def _attention_bias_table(rel_bias_g, window, dilation):
    nw = window // dilation
    qi = np.arange(ATTN_BLOCK)[:, None]
    ki = np.arange(2 * ATTN_BLOCK)[None, :]
    dist = ATTN_BLOCK + qi - ki
    band = (dist >= 0) & (dist <= nw)
    bucket = _t5_causal_bucket(np.maximum(dist, 0) * dilation)
    onehot = (jnp.asarray(bucket)[:, :, None] == jnp.arange(N_REL_BUCKETS)).astype(jnp.float32)
    bias = jnp.einsum('qkb,bh->hqk', onehot, rel_bias_g.astype(jnp.float32),
                      precision=lax.Precision.HIGHEST)
    masks = jnp.asarray(np.stack([band & (ki >= ATTN_BLOCK), band]))
    return jnp.where(masks[:, None], bias[None], NEG_INF)


def _gelu_tanh(x):
    return 0.5 * x * (1.0 + jnp.tanh(math.sqrt(2.0 / math.pi) * (x + 0.044715 * (x * x * x))))


def _softplus(x):
    return jnp.maximum(x, 0.0) + jnp.log(1.0 + jnp.exp(-jnp.abs(x)))


def _natural_order(blk_ref, d, slabs):
    if d == 1:
        return blk_ref[0].astype(jnp.float32)
    per = MIX_TILE // d
    for r in range(d):
        rows = blk_ref[r].astype(jnp.float32)
        for c in range(GROUP_WIDTH // LANES):
            slabs[c, pl.ds(r, per, stride=d), :] = rows[:, c * LANES:(c + 1) * LANES]
    return jnp.concatenate([slabs[c] for c in range(GROUP_WIDTH // LANES)], axis=1)


def _mix_tile(x_ref, o1_ref, o2_ref, o3_ref, l1_ref, l2_ref, l3_ref,
              n1_ref, win_ref, cw_ref, cb_ref, wg_ref, brg_ref, big_ref, lam_ref,
              wpa_ref, wpl_ref, wout_ref, n2_ref, wr_ref,
              x1_ref, xs_ref, route_ref, cnt_ref,
              xbuf, a_sc, b_sc, h_sc, *slabs):
    tm = MIX_TILE
    first_tile = pl.program_id(0) == 0

    @pl.when(first_tile)
    def _():
        xbuf[0:SUBLANES, :] = jnp.zeros((SUBLANES, LRU_WIDTH), jnp.float32)
        h_sc[...] = jnp.zeros_like(h_sc)

    x = x_ref[...]
    h = _rmsnorm(x, n1_ref[...]).astype(jnp.bfloat16)
    def project(first_col):
        return jnp.dot(h, win_ref[:, first_col:first_col + D_MODEL], preferred_element_type=jnp.float32)

    xr = project(0)

    yield
    g_lru = project(LRU_WIDTH)
    xbuf[SUBLANES:SUBLANES + tm, :] = xr
    xc = xr * cw_ref[CONV_WIDTH - 1:CONV_WIDTH, :] + cb_ref[...]
    for j in range(CONV_WIDTH - 1):
        back = CONV_WIDTH - 1 - j
        xc = xc + xbuf[SUBLANES - back:SUBLANES - back + tm, :] * cw_ref[j:j + 1, :]
    xbuf[0:SUBLANES, :] = xbuf[tm:tm + SUBLANES, :]

    xcb = xc.astype(jnp.bfloat16)
    r_parts, i_parts = [], []
    for c in range(N_LRU_CHUNKS):
        ri = jnp.dot(xcb[:, c * LRU_CHUNK:(c + 1) * LRU_CHUNK], wg_ref[c],
                     preferred_element_type=jnp.float32)
        r_parts.append(ri[:, :LRU_CHUNK])
        i_parts.append(ri[:, LRU_CHUNK:])
    yield
    g_a = project(2 * LRU_WIDTH)
    r = jax.nn.sigmoid(jnp.concatenate(r_parts, axis=1) + brg_ref[...])
    ig = jax.nn.sigmoid(jnp.concatenate(i_parts, axis=1) + big_ref[...])
    log_a = (-LRU_C * _softplus(-lam_ref[...])) * r
    a = jnp.exp(log_a)
    gap = 1.0 - a * a
    root = jnp.where(gap > 0.0, gap * lax.rsqrt(gap), 0.0)
    bb = root * (ig * xc)

    yield
    g_b = project(2 * LRU_WIDTH + D_MODEL)
    n_slab = LRU_WIDTH // LANES
    for c in range(n_slab):
        lanes = slice(c * LANES, (c + 1) * LANES)
        for s in range(SUBLANES):
            rows = slice(s * SCAN_SEG, (s + 1) * SCAN_SEG)
            a_sc[c, s * SCAN_PITCH:s * SCAN_PITCH + SCAN_SEG, :] = a[rows, lanes]
            b_sc[c, s * SCAN_PITCH:s * SCAN_PITCH + SCAN_SEG, :] = bb[rows, lanes]

    yield
    local = [jnp.zeros((SUBLANES, LANES), jnp.float32)] * n_slab
    decay = [jnp.ones((SUBLANES, LANES), jnp.float32)] * n_slab
    for j in range(SCAN_SEG):
        for c in range(n_slab):
            step = pl.ds(j, SUBLANES, stride=SCAN_PITCH)
            a_j = a_sc[c, step, :]
            local[c] = a_j * local[c] + b_sc[c, step, :]
            decay[c] = a_j * decay[c]
            b_sc[c, step, :] = local[c]
            a_sc[c, step, :] = decay[c]

    yield
    h_parts = []
    for c in range(n_slab):
        lanes = slice(c * LANES, (c + 1) * LANES)
        entry = h_sc[:, lanes]
        segs = []
        for s in range(SUBLANES):
            rows = slice(s * SCAN_PITCH, s * SCAN_PITCH + SCAN_SEG)
            segs.append(b_sc[c, rows, :] + a_sc[c, rows, :] * entry)
            entry = decay[c][s:s + 1, :] * entry + local[c][s:s + 1, :]
        h_sc[:, lanes] = entry
        h_parts.append(jnp.concatenate(segs, axis=0))
    h_all = jnp.concatenate(h_parts, axis=1)
    yield
    lru = (h_all * _gelu_tanh(g_lru)).astype(jnp.bfloat16)
    p_lru = jnp.dot(lru, wpl_ref[...], preferred_element_type=jnp.float32)

    yield
    dil = [d for _, d in DILATED_GROUPS]
    o1, o2, o3 = (_natural_order(ref, d, slabs[2 * g]) for g, (ref, d) in
                  enumerate(zip((o1_ref, o2_ref, o3_ref), dil)))
    l1, l2, l3 = (_natural_order(ref, d, slabs[2 * g + 1]) for g, (ref, d) in
                  enumerate(zip((l1_ref, l2_ref, l3_ref), dil)))
    lm = jnp.maximum(jnp.maximum(l1, l2), l3)
    e1, e2, e3 = jnp.exp(l1 - lm), jnp.exp(l2 - lm), jnp.exp(l3 - lm)
    attn = (e1 * o1 + e2 * o2 + e3 * o3) / (e1 + e2 + e3)
    p_attn = jnp.dot(attn.astype(jnp.bfloat16), wpa_ref[...], preferred_element_type=jnp.float32)

    yield
    merged = jax.nn.sigmoid(g_a) * p_attn + jax.nn.sigmoid(g_b) * p_lru
    x1 = x + jnp.dot(merged.astype(jnp.bfloat16), wout_ref[...], preferred_element_type=jnp.float32)
    x1_ref[...] = x1
    h2 = _rmsnorm(x1, n2_ref[...])

    yield
    h2_hi = h2.astype(jnp.bfloat16)
    h2_lo = (h2 - h2_hi.astype(jnp.float32)).astype(jnp.bfloat16)
    by_hi = jnp.dot(h2_hi, wr_ref[...], preferred_element_type=jnp.float32)
    by_lo = jnp.dot(h2_lo, wr_ref[...], preferred_element_type=jnp.float32)
    logits = (by_hi[:, :LANES] + (by_hi[:, LANES:] + by_lo[:, :LANES])) + by_lo[:, LANES:]
    lane = lax.broadcasted_iota(jnp.int32, (tm, LANES), 1)
    big = jnp.int32(LANES)
    lowest = jnp.float32(-3.0e38)
    is_g = (lane >= GROUP_LOGIT_LANE0) & (lane < GROUP_LOGIT_LANE0 + N_EXPERT_GROUPS)
    gl = jnp.where(is_g, logits, lowest)
    gmax = jnp.max(gl, axis=-1, keepdims=True)
    gsel = jnp.min(jnp.where(gl == gmax, lane, big), axis=-1, keepdims=True) - GROUP_LOGIT_LANE0
    p_sel = 1.0 / jnp.sum(jnp.where(is_g, jnp.exp(gl - gmax), 0.0), axis=-1, keepdims=True)
    is_e = (lane >= gsel * EXPERTS_PER_GROUP) & (lane < (gsel + 1) * EXPERTS_PER_GROUP)
    el = jnp.where(is_e, logits, lowest)
    v1 = jnp.max(el, axis=-1, keepdims=True)
    i1 = jnp.min(jnp.where(el == v1, lane, big), axis=-1, keepdims=True)
    el2 = jnp.where(lane == i1, lowest, el)
    v2 = jnp.max(el2, axis=-1, keepdims=True)
    i2 = jnp.min(jnp.where(el2 == v2, lane, big), axis=-1, keepdims=True)
    t = jnp.exp(v2 - v1)
    g0 = p_sel / (1.0 + t)
    g1 = p_sel * t / (1.0 + t)

    yield
    oh0 = lane == i1
    oh1 = lane == i2
    both = (oh0 | oh1).astype(jnp.float32)
    tri = (lax.broadcasted_iota(jnp.int32, (tm, tm), 0)
           > lax.broadcasted_iota(jnp.int32, (tm, tm), 1)).astype(jnp.bfloat16)
    before = jnp.dot(tri, both.astype(jnp.bfloat16), preferred_element_type=jnp.float32)
    cnt = jnp.sum(both, axis=0, keepdims=True)
    padded = jnp.floor((cnt + (MOE_CHUNK - 1.0)) * (1.0 / MOE_CHUNK)) * MOE_CHUNK
    upper = (lax.broadcasted_iota(jnp.int32, (LANES, LANES), 0)
             < lax.broadcasted_iota(jnp.int32, (LANES, LANES), 1)).astype(jnp.bfloat16)
    loff = jnp.dot(jnp.broadcast_to(padded, (SUBLANES, LANES)).astype(jnp.bfloat16), upper,
                   preferred_element_type=jnp.float32)[0:1, :]
    base = before + loff
    slot0 = jnp.sum(jnp.where(oh0, base, 0.0), axis=-1, keepdims=True)
    slot1 = jnp.sum(jnp.where(oh1, base, 0.0), axis=-1, keepdims=True)
    cnt_ref[...] = jnp.broadcast_to(cnt, cnt_ref.shape)

    slot_id = lax.broadcasted_iota(jnp.int32, (tm, MOE_SLOTS), 1)
    place = ((slot_id == slot0.astype(jnp.int32)) | (slot_id == slot1.astype(jnp.int32)))
    xs_ref[...] = _pack_bf16_pairs(
        lax.dot_general(place.astype(jnp.bfloat16), h2_hi, (((0,), (0,)), ((), ())),
                        preferred_element_type=jnp.float32))

    route = jnp.zeros((tm, LANES), jnp.float32)
    for lane_id, val in ((ROUTE_S0, slot0), (ROUTE_S1, slot1), (ROUTE_G0, g0), (ROUTE_G1, g1)):
        route = jnp.where(lane == lane_id, val, route)
    route_ref[...] = route


N_MIXER_TILED_INPUTS = 1 + 2 * N_GROUPS
N_MIXER_OUTPUTS = 4


def _mixer_kernel(*refs):
    n_batch = refs[0].shape[0]
    n_weights = len(refs) - N_MIXER_TILED_INPUTS - N_MIXER_OUTPUTS - n_batch * _MIXER_SCRATCH_PER_SEQ
    tiled = refs[:N_MIXER_TILED_INPUTS]
    weights = refs[N_MIXER_TILED_INPUTS:N_MIXER_TILED_INPUTS + n_weights]
    outs = refs[N_MIXER_TILED_INPUTS + n_weights:N_MIXER_TILED_INPUTS + n_weights + N_MIXER_OUTPUTS]
    scratch = refs[N_MIXER_TILED_INPUTS + n_weights + N_MIXER_OUTPUTS:]
    stages = []
    for b in range(n_batch):
        own = scratch[b * _MIXER_SCRATCH_PER_SEQ:(b + 1) * _MIXER_SCRATCH_PER_SEQ]
        stages.append(_mix_tile(*[r.at[b] for r in tiled], *weights, *[r.at[b] for r in outs], *own))
    while stages:
        for gen in list(stages):
            if next(gen, _DONE) is _DONE:
                stages.remove(gen)


def _mixer_scratch(tm):
    return [
        pltpu.VMEM((tm + 2 * SUBLANES, LRU_WIDTH), jnp.float32),
        pltpu.VMEM((LRU_WIDTH // LANES, SUBLANES * SCAN_PITCH, LANES), jnp.float32),
        pltpu.VMEM((LRU_WIDTH // LANES, SUBLANES * SCAN_PITCH, LANES), jnp.float32),
        pltpu.VMEM((1, LRU_WIDTH), jnp.float32),
    ] + [pltpu.VMEM((GROUP_WIDTH // LANES, tm, LANES), jnp.float32)] * (2 * N_GROUPS)


_MIXER_SCRATCH_PER_SEQ = len(_mixer_scratch(MIX_TILE))


def _token_mixer(x, attn_o, attn_lse, weights):
    B, S, _ = x.shape
    tm = MIX_TILE
    nt = S // tm
    rows = lambda j: (0, j, 0)

    def resident(shape):
        return pl.BlockSpec(shape, lambda j: (0,) * len(shape), pipeline_mode=pl.Buffered(1))

    in_specs = [pl.BlockSpec((B, tm, D_MODEL), rows)]
    group_specs = [pl.BlockSpec((B, d, tm // d, GROUP_WIDTH), lambda j: (0, 0, j, 0))
                   for _, d in DILATED_GROUPS]
    in_specs += group_specs + group_specs
    in_specs += [resident(w.shape) for w in weights]
    x1, xs, route, counts = pl.pallas_call(
        _mixer_kernel,
        grid=(nt,),
        in_specs=in_specs,
        out_specs=[
            pl.BlockSpec((B, tm, D_MODEL), rows),
            pl.BlockSpec((B, MOE_SLOTS, D_MODEL // 2), rows),
            pl.BlockSpec((B, tm, LANES), rows),
            pl.BlockSpec((B, None, SUBLANES, LANES), lambda j: (0, j, 0, 0)),
        ],
        out_shape=[
            jax.ShapeDtypeStruct((B, S, D_MODEL), jnp.float32),
            jax.ShapeDtypeStruct((B, nt * MOE_SLOTS, D_MODEL // 2), jnp.uint32),
            jax.ShapeDtypeStruct((B, S, LANES), jnp.float32),
            jax.ShapeDtypeStruct((B, nt, SUBLANES, LANES), jnp.float32),
        ],
        scratch_shapes=_mixer_scratch(tm) * B,
        compiler_params=pltpu.CompilerParams(
            dimension_semantics=("arbitrary",), vmem_limit_bytes=VMEM_LIMIT),
        name="token_mixer",
    )(x, *attn_o, *attn_lse, *weights)
    return (x1.reshape(B * S, D_MODEL), xs.reshape(B * nt * MOE_SLOTS, D_MODEL // 2),
            route.reshape(B * S, LANES), counts.reshape(B * nt, SUBLANES, LANES))


def _moe_tables(cnt_tile):
    ntiles = cnt_tile.shape[0]
    nch = (cnt_tile + MOE_CHUNK - 1) // MOE_CHUNK
    lo_c = jnp.cumsum(nch, axis=1) - nch
    per_expert = jnp.sum(nch, axis=0)
    region = ((per_expert + MOE_BLOCK_CHUNKS - 1) // MOE_BLOCK_CHUNKS) * MOE_BLOCK_CHUNKS
    pend = jnp.cumsum(region)
    glob = (pend - region)[None, :] + jnp.cumsum(nch, axis=0) - nch

    max_rows = TOP_K * ntiles * MIX_TILE + ntiles * N_EXPERTS * (MOE_CHUNK - 1)
    max_chunks = -(-max_rows // MOE_CHUNK) + N_EXPERTS * (MOE_BLOCK_CHUNKS - 1)
    max_blocks = -(-max_chunks // MOE_BLOCK_CHUNKS)
    max_blocks = -(-max_blocks // EXPERT_BLOCKS_PER_STEP) * EXPERT_BLOCKS_PER_STEP
    max_chunks = max_blocks * MOE_BLOCK_CHUNKS

    seg_start = glob.T.reshape(-1)
    seg_src = (jnp.arange(ntiles, dtype=jnp.int32)[:, None] * TILE_CHUNKS + lo_c).T.reshape(-1)
    step = jnp.diff(seg_src - seg_start, prepend=0)
    g = jnp.arange(max_chunks, dtype=jnp.int32)
    src = g + jnp.sum(jnp.where(seg_start[None, :] <= g[:, None], step[None, :], 0), axis=1)
    src = jnp.clip(src, 0, ntiles * TILE_CHUNKS - 1)

    lstep = jnp.diff(glob - lo_c, axis=1, prepend=0)
    lc = jnp.arange(TILE_CHUNKS, dtype=jnp.int32)
    comb = lc[None, :] + jnp.sum(
        jnp.where(lo_c[:, None, :] <= lc[None, :, None], lstep[:, None, :], 0), axis=2)
    comb = jnp.clip(comb, 0, max_chunks - 1).reshape(-1)

    eidx = jnp.arange(N_EXPERTS, dtype=jnp.int32)
    nonempty = region > 0
    later = jnp.where(nonempty[None, :] & (eidx[None, :] > eidx[:, None]), eidx[None, :], N_EXPERTS)
    next_expert = jnp.min(later, axis=1)
    next_expert = jnp.where(next_expert == N_EXPERTS, -1, next_expert)
    wslot = (jnp.cumsum(nonempty.astype(jnp.int32)) - nonempty.astype(jnp.int32)) % 2
    last_expert = jnp.max(jnp.where(nonempty, eidx, 0))
    blk0 = jnp.arange(max_blocks, dtype=jnp.int32) * MOE_BLOCK_CHUNKS
    block_expert = jnp.minimum(jnp.sum((pend[None, :] <= blk0[:, None]).astype(jnp.int32), axis=1),
                               last_expert)
    of_block = lambda per_expert: jnp.sum(
        jnp.where(block_expert[:, None] == eidx[None, :], per_expert[None, :], 0), axis=1)
    n_valid = pend[-1:] // MOE_BLOCK_CHUNKS
    as_i32 = lambda v: v.astype(jnp.int32)
    expert_tables = (as_i32(block_expert), as_i32(of_block(next_expert)), as_i32(of_block(wslot)),
                     as_i32(n_valid), as_i32(src))
    return expert_tables, as_i32(comb)


def _chunk_copy(src_hbm, src_chunk, dst, dst_chunk, sem):
    rows = lambda c: pl.ds(pl.multiple_of(c * MOE_CHUNK, MOE_CHUNK), MOE_CHUNK)
    return pltpu.make_async_copy(src_hbm.at[rows(src_chunk), :], dst.at[rows(dst_chunk), :], sem)


def _expert_kernel(be_ref, nxt_ref, ws_ref, nv_ref, src_ref, xs_hbm, wgu_hbm, wd_hbm, ys_ref, *scratch):
    for k in range(EXPERT_BLOCKS_PER_STEP):
        _expert_block(pl.program_id(0) * EXPERT_BLOCKS_PER_STEP + k,
                      be_ref, nxt_ref, ws_ref, nv_ref, src_ref, xs_hbm, wgu_hbm, wd_hbm,
                      ys_ref.at[pl.ds(k * MOE_TILE, MOE_TILE), :], *scratch)


def _expert_block(i, be_ref, nxt_ref, ws_ref, nv_ref, src_ref, xs_hbm, wgu_hbm, wd_hbm, ys_ref,
                  xbuf, wgu_f32, wd_f32, wgu_bf, wd_bf, sems, wsems):
    valid = i < nv_ref[0]

    def gather(blk, slot):
        return [_chunk_copy(xs_hbm, src_ref[blk * MOE_BLOCK_CHUNKS + c], xbuf.at[slot], c, sems.at[slot])
                for c in range(MOE_BLOCK_CHUNKS)]

    def fetch_weights(e, slot):
        return [pltpu.make_async_copy(wgu_hbm.at[e], wgu_f32.at[slot], wsems.at[slot]),
                pltpu.make_async_copy(wd_hbm.at[e], wd_f32.at[slot], wsems.at[slot])]

    ahead = EXPERT_GATHER_SLOTS - 1

    @pl.when(i == 0)
    def _():
        for cp in fetch_weights(be_ref[0], 0):
            cp.start(priority=WEIGHT_DMA_PRIORITY)
        for first in range(ahead):
            @pl.when(first < nv_ref[0])
            def _():
                for cp in gather(first, first):
                    cp.start()

    @pl.when(i + ahead < nv_ref[0])
    def _():
        for cp in gather(i + ahead, (i + ahead) % EXPERT_GATHER_SLOTS):
            cp.start()

    new_expert = valid & ((i == 0) | (be_ref[i] != be_ref[jnp.maximum(i - 1, 0)]))

    @pl.when(new_expert)
    def _():
        ws = ws_ref[i]
        for cp in fetch_weights(be_ref[i], ws):
            cp.wait()
        wgu_bf[...] = wgu_f32[ws].astype(jnp.bfloat16)
        wd_bf[...] = wd_f32[ws].astype(jnp.bfloat16)

        @pl.when(nxt_ref[i] >= 0)
        def _():
            for cp in fetch_weights(nxt_ref[i], 1 - ws):
                cp.start(priority=WEIGHT_DMA_PRIORITY)

    @pl.when(valid)
    def _():
        slot = i % EXPERT_GATHER_SLOTS
        for cp in gather(i, slot):
            cp.wait()
        x_lo, x_hi = _unpack_bf16_pairs(xbuf[slot])
        half = D_MODEL // 2
        partial = []

        def up_proj(cols):
            return (jnp.dot(x_lo, wgu_bf[:half, cols], preferred_element_type=jnp.float32)
                    + jnp.dot(x_hi, wgu_bf[half:, cols], preferred_element_type=jnp.float32))

        def hidden_group(c):
            cols = slice(c * EXPERT_GROUP, (c + 1) * EXPERT_GROUP)
            g = up_proj(cols)
            u = up_proj(slice(D_EXPERT + cols.start, D_EXPERT + cols.stop))
            yield
            act = (g * jax.nn.sigmoid(g) * u).astype(jnp.bfloat16)
            yield
            partial.append(jnp.dot(act, wd_bf[cols, :], preferred_element_type=jnp.float32))

        waiting = [hidden_group(c) for c in range(D_EXPERT // EXPERT_GROUP)]
        live = []
        while waiting or live:
            if waiting:
                live.append(waiting.pop(0))
            for gen in list(live):
                if next(gen, _DONE) is _DONE:
                    live.remove(gen)
        y = sum(partial[1:], partial[0])
        ys_ref[...] = _pack_bf16_pairs(y.astype(jnp.bfloat16).astype(jnp.float32))

    @pl.when(jnp.logical_not(valid))
    def _():
        ys_ref[...] = jnp.zeros_like(ys_ref)


def _experts(tables, xs, w_gate_up, w_down):
    block_expert, block_next, block_wslot, n_valid, src = tables
    nblk = block_expert.shape[0]
    return pl.pallas_call(
        _expert_kernel,
        grid_spec=pltpu.PrefetchScalarGridSpec(
            num_scalar_prefetch=5,
            grid=(nblk // EXPERT_BLOCKS_PER_STEP,),
            in_specs=[pl.BlockSpec(memory_space=pl.ANY)] * 3,
            out_specs=pl.BlockSpec((EXPERT_BLOCKS_PER_STEP * MOE_TILE, D_MODEL // 2),
                                   lambda i, *_: (i, 0)),
            scratch_shapes=[
                pltpu.VMEM((EXPERT_GATHER_SLOTS, MOE_TILE, D_MODEL // 2), jnp.uint32),
                pltpu.VMEM((2, D_MODEL, 2 * D_EXPERT), jnp.float32),
                pltpu.VMEM((2, D_EXPERT, D_MODEL), jnp.float32),
                pltpu.VMEM((D_MODEL, 2 * D_EXPERT), jnp.bfloat16),
                pltpu.VMEM((D_EXPERT, D_MODEL), jnp.bfloat16),
                pltpu.SemaphoreType.DMA((EXPERT_GATHER_SLOTS,)),
                pltpu.SemaphoreType.DMA((2,)),
            ],
        ),
        out_shape=jax.ShapeDtypeStruct((nblk * MOE_TILE, D_MODEL // 2), jnp.uint32),
        compiler_params=pltpu.CompilerParams(
            dimension_semantics=("arbitrary",), vmem_limit_bytes=VMEM_LIMIT),
        name="moe_experts",
    )(block_expert, block_next, block_wslot, n_valid, src, xs, w_gate_up, w_down)


def _combine_kernel(comb_ref, ys_hbm, x1_ref, route_ref, nf_ref, out_ref, ybuf, sems):
    tc = MIX_TILE
    per_step = COMBINE_TILES_PER_STEP
    i = pl.program_id(0)
    last = pl.num_programs(0) - 1

    def gather(step, slot):
        first = step * per_step * TILE_CHUNKS
        return [_chunk_copy(ys_hbm, comb_ref[first + lc], ybuf.at[slot], lc, sems.at[slot])
                for lc in range(per_step * TILE_CHUNKS)]

    @pl.when(i == 0)
    def _():
        for cp in gather(0, 0):
            cp.start()

    nxt = jnp.minimum(i + 1, last)
    for cp in gather(nxt, (i + 1) % 2):
        cp.start()

    slot = i % 2
    for cp in gather(i, slot):
        cp.wait()

    def tile(t):
        rows = slice(t * tc, (t + 1) * tc)
        y_lo, y_hi = _unpack_bf16_pairs(ybuf[slot, t * MOE_SLOTS:(t + 1) * MOE_SLOTS, :])
        route = route_ref[rows, :]
        slot_id = lax.broadcasted_iota(jnp.int32, (tc, MOE_SLOTS), 1)
        picked = []
        for lane in (ROUTE_S0, ROUTE_S1):
            sel = (slot_id == route[:, lane:lane + 1].astype(jnp.int32)).astype(jnp.bfloat16)
            yield
            picked.append(jnp.concatenate([jnp.dot(sel, y_lo, preferred_element_type=jnp.float32),
                                           jnp.dot(sel, y_hi, preferred_element_type=jnp.float32)], axis=1))
            yield
        g0 = route[:, ROUTE_G0:ROUTE_G0 + 1]
        g1 = route[:, ROUTE_G1:ROUTE_G1 + 1]
        out_ref[rows, :] = _rmsnorm(x1_ref[rows, :] + g0 * picked[0] + g1 * picked[1], nf_ref[...])

    waiting = [tile(t) for t in range(per_step)]
    live = []
    while waiting or live:
        if waiting:
            live.append(waiting.pop(0))
        for gen in list(live):
            if next(gen, _DONE) is _DONE:
                live.remove(gen)

    @pl.when(i == last)
    def _():
        for cp in gather(nxt, (i + 1) % 2):
            cp.wait()


def _combine(comb, ys, x1, route, norm_f):
    T = x1.shape[0]
    tc = MIX_TILE * COMBINE_TILES_PER_STEP
    return pl.pallas_call(
        _combine_kernel,
        grid_spec=pltpu.PrefetchScalarGridSpec(
            num_scalar_prefetch=1,
            grid=(T // tc,),
            in_specs=[
                pl.BlockSpec(memory_space=pl.ANY),
                pl.BlockSpec((tc, D_MODEL), lambda i, c: (i, 0)),
                pl.BlockSpec((tc, LANES), lambda i, c: (i, 0)),
                pl.BlockSpec((1, D_MODEL), lambda i, c: (0, 0)),
            ],
            out_specs=pl.BlockSpec((tc, D_MODEL), lambda i, c: (i, 0)),
            scratch_shapes=[
                pltpu.VMEM((2, COMBINE_TILES_PER_STEP * MOE_SLOTS, D_MODEL // 2), jnp.uint32),
                pltpu.SemaphoreType.DMA((2,)),
            ],
        ),
        out_shape=jax.ShapeDtypeStruct((T, D_MODEL), jnp.float32),
        compiler_params=pltpu.CompilerParams(
            dimension_semantics=("arbitrary",), vmem_limit_bytes=VMEM_LIMIT),
        name="moe_combine",
    )(comb, ys, x1, route, norm_f)


def _block_diag_gates(w_rg, w_ig):
    def bd(w):
        w4 = w.reshape(N_LRU_CHUNKS, LRU_CHUNK_HEADS, LRU_HEAD_DIM, LRU_HEAD_DIM)
        eye = jnp.eye(LRU_CHUNK_HEADS, dtype=w.dtype)
        return jnp.einsum('chij,hk->chikj', w4, eye).reshape(N_LRU_CHUNKS, LRU_CHUNK, LRU_CHUNK)
    return jnp.concatenate([bd(w_rg), bd(w_ig)], axis=-1).astype(jnp.bfloat16)


def kernel(x, rel_bias, norm1, w_in, conv_w, conv_b, w_rg, b_rg, w_ig, b_ig, lru_lambda,
           w_proj_attn, w_proj_lru, w_out, norm2, w_router_group, w_router_expert,
           w_gate_up, w_down, norm_f):
    B, S, D = x.shape
    T = B * S
    assert w_in.shape[0] == 1, "single-layer block"
    layer = 0
    bf16 = jnp.bfloat16
    x2 = x.reshape(T, D)
    row = lambda v: v[layer].reshape(1, -1)
    w_in_bf = w_in[layer].astype(bf16)
    w_qkv = (w_in_bf[:, :QKV_WIDTH].reshape(D, 3, N_GROUPS, GROUP_WIDTH)
             .transpose(0, 2, 1, 3).reshape(D, QKV_WIDTH))
    w_rest = w_in_bf[:, QKV_WIDTH:]

    qkv_groups = _qkv_projection(x2, row(norm1), w_qkv, B, S)
    attn_o, attn_lse = [], []
    for gi, (window, dilation) in enumerate(DILATED_GROUPS):
        hs = slice(gi * HEADS_PER_GROUP, (gi + 1) * HEADS_PER_GROUP)
        bias = _attention_bias_table(rel_bias[:, hs], window, dilation)
        o, lse = _attention_group(qkv_groups[gi], bias, gi, dilation)
        attn_o.append(o)
        attn_lse.append(lse)

    w_router = jnp.concatenate(
        [w_router_expert[layer].astype(jnp.float32), w_router_group[layer].astype(jnp.float32),
         jnp.zeros((D, LANES - N_EXPERTS - N_EXPERT_GROUPS), jnp.float32)], axis=1)
    w_router_hi = w_router.astype(bf16)
    weights = [
        row(norm1), w_rest, conv_w[layer], row(conv_b),
        _block_diag_gates(w_rg[layer], w_ig[layer]), row(b_rg), row(b_ig), row(lru_lambda),
        w_proj_attn[layer].astype(bf16), w_proj_lru[layer].astype(bf16), w_out[layer].astype(bf16),
        row(norm2),
        jnp.concatenate([w_router_hi, (w_router - w_router_hi.astype(jnp.float32)).astype(bf16)], axis=1),
    ]
    x1, xs, route, counts = _token_mixer(x, attn_o, attn_lse, weights)

    expert_tables, comb = _moe_tables(counts[:, 0, :N_EXPERTS].astype(jnp.int32))
    ys = _experts(expert_tables, xs, w_gate_up[layer], w_down[layer])
    out = _combine(comb, ys, x1, route, norm_f.reshape(1, -1))
    return out.reshape(B, S, D)
```

```python
import functools
import math

import jax
import jax.numpy as jnp
import numpy as np
from jax import lax
from jax.experimental import pallas as pl
from jax.experimental.pallas import tpu as pltpu

D_MODEL = 1024
HEAD_DIM = 64
HEADS_PER_GROUP = 4
DILATED_GROUPS = ((128, 1), (512, 4), (2048, 16))
N_GROUPS = len(DILATED_GROUPS)
N_ATTN_HEADS = HEADS_PER_GROUP * N_GROUPS
ATTN_WIDTH = N_ATTN_HEADS * HEAD_DIM
GROUP_WIDTH = HEADS_PER_GROUP * HEAD_DIM
QKV_WIDTH = 3 * ATTN_WIDTH
ATTN_BLOCK = 128
N_REL_BUCKETS = 32
REL_MAX_DISTANCE = 2048
NEG_INF = -1e30

LRU_WIDTH = D_MODEL
LRU_HEADS = 16
LRU_HEAD_DIM = LRU_WIDTH // LRU_HEADS
CONV_WIDTH = 4
LRU_C = 8.0
LRU_CHUNK_HEADS = 4
LRU_CHUNK = LRU_CHUNK_HEADS * LRU_HEAD_DIM
N_LRU_CHUNKS = LRU_WIDTH // LRU_CHUNK

N_EXPERT_GROUPS = 4
EXPERTS_PER_GROUP = 8
N_EXPERTS = N_EXPERT_GROUPS * EXPERTS_PER_GROUP
TOP_K = 2
D_EXPERT = 512
EPS = 1e-6

LANES = 128
SUBLANES = 8

ATTN_TILE = 256
ATTN_TILES_PER_STEP = 4
ATTN_QB = 16
MIX_TILE = 256
SCAN_SEG = MIX_TILE // SUBLANES
SCAN_PITCH = SCAN_SEG + SUBLANES
MOE_TILE = 256
EXPERT_GATHER_SLOTS = 4
EXPERT_BLOCKS_PER_STEP = 8
COMBINE_TILES_PER_STEP = 4
WEIGHT_DMA_PRIORITY = 1
EXPERT_GROUP = 256
MOE_CHUNK = SUBLANES
MOE_BLOCK_CHUNKS = MOE_TILE // MOE_CHUNK
MOE_SLOTS = 768
assert MOE_SLOTS >= TOP_K * MIX_TILE + N_EXPERTS * (MOE_CHUNK - 1) and MOE_SLOTS % MOE_CHUNK == 0
TILE_CHUNKS = MOE_SLOTS // MOE_CHUNK
VMEM_LIMIT = 56 * 1024 * 1024

ROUTE_S0, ROUTE_S1, ROUTE_G0, ROUTE_G1 = range(4)
GROUP_LOGIT_LANE0 = N_EXPERTS


_DONE = object()


def _rmsnorm(x, g):
    return x * lax.rsqrt(jnp.mean(x * x, axis=-1, keepdims=True) + EPS) * g


def _pack_bf16_pairs(v):
    w = v.shape[1] // 2
    bits = lax.bitcast_convert_type(v, jnp.uint32)
    return bits[:, w:] | (bits[:, :w] >> 16)


def _unpack_bf16_pairs(p):
    lo = lax.bitcast_convert_type(p << 16, jnp.float32)
    hi = lax.bitcast_convert_type(p & jnp.uint32(0xFFFF0000), jnp.float32)
    return lo.astype(jnp.bfloat16), hi.astype(jnp.bfloat16)


def _qkv_kernel(x_ref, g_ref, w_ref, *refs):
    tm = ATTN_TILE
    o_refs, slab_refs = refs[:N_GROUPS], refs[N_GROUPS:]
    n_slab = ATTN_WIDTH // LANES

    def tile(t):
        rows = slice(t * tm, (t + 1) * tm)
        slabs = slab_refs[t]
        h = _rmsnorm(x_ref[rows, :], g_ref[...]).astype(jnp.bfloat16)
        yield
        for gi, (o_ref, (_, d)) in enumerate(zip(o_refs, DILATED_GROUPS)):
            cols = slice(gi * ATTN_WIDTH, (gi + 1) * ATTN_WIDTH)
            part = jnp.dot(h, w_ref[:, cols], preferred_element_type=jnp.float32)
            out_rows = slice(t * (tm // d), (t + 1) * (tm // d))
            if d == 1:
                o_ref[0, out_rows, :] = part.astype(jnp.bfloat16)
                yield
                continue
            for c in range(n_slab):
                slabs[c] = part[:, c * LANES:(c + 1) * LANES]
            yield
            for r in range(d):
                o_ref[r, out_rows, :] = jnp.concatenate(
                    [slabs[c, pl.ds(r, tm // d, stride=d), :] for c in range(n_slab)],
                    axis=1).astype(jnp.bfloat16)
            yield

    waiting = [tile(t) for t in range(ATTN_TILES_PER_STEP)]
    live = []
    while waiting or live:
        if waiting:
            live.append(waiting.pop(0))
        for gen in list(live):
            if next(gen, _DONE) is _DONE:
                live.remove(gen)


def _qkv_projection(x2, norm1, w_qkv, B, S):
    tm = ATTN_TILE * ATTN_TILES_PER_STEP
    nt = S // tm
    return pl.pallas_call(
        _qkv_kernel,
        grid=(B, nt),
        in_specs=[
            pl.BlockSpec((tm, D_MODEL), lambda b, j: (b * nt + j, 0)),
            pl.BlockSpec((1, D_MODEL), lambda b, j: (0, 0)),
            pl.BlockSpec((D_MODEL, QKV_WIDTH), lambda b, j: (0, 0)),
        ],
        out_specs=[pl.BlockSpec((None, d, tm // d, ATTN_WIDTH), lambda b, j: (b, 0, j, 0))
                   for _, d in DILATED_GROUPS],
        out_shape=[jax.ShapeDtypeStruct((B, d, S // d, ATTN_WIDTH), jnp.bfloat16)
                   for _, d in DILATED_GROUPS],
        scratch_shapes=[pltpu.VMEM((ATTN_WIDTH // LANES, ATTN_TILE, LANES), jnp.float32)]
        * ATTN_TILES_PER_STEP,
        compiler_params=pltpu.CompilerParams(
            dimension_semantics=("parallel", "parallel"), vmem_limit_bytes=VMEM_LIMIT),
        name="qkv_projection",
    )(x2, norm1, w_qkv)


def _attn_residue(n_qb, q_ref, kp_ref, kc_ref, vp_ref, vc_ref, bias0_ref, bias_ref, o_ref, lse_ref):
    q = q_ref[...] * (HEAD_DIM ** -0.5)
    k = jnp.concatenate([kp_ref[...], kc_ref[...]], axis=0)
    v = jnp.concatenate([vp_ref[...], vc_ref[...]], axis=0)
    pair_w = 2 * HEAD_DIM
    first_head = lax.broadcasted_iota(jnp.int32, (1, pair_w), 1) < HEAD_DIM
    def one_head(x, first):
        bits = pltpu.bitcast(x, jnp.uint32)
        keep = first_head if first else jnp.logical_not(first_head)
        return pltpu.bitcast(jnp.where(keep, bits, jnp.uint32(0)), jnp.bfloat16)

    ones = jnp.ones_like(k[:, :pair_w])
    operands = []
    for pair in range(HEADS_PER_GROUP // 2):
        cols = slice(pair * pair_w, (pair + 1) * pair_w)
        k2, v2 = k[:, cols], v[:, cols]
        k_heads = (one_head(k2, True), one_head(k2, False))
        v_heads = (jnp.concatenate([one_head(v2, True), one_head(ones, True)], axis=1),
                   jnp.concatenate([one_head(v2, False), one_head(ones, False)], axis=1))
        operands.append((cols, k_heads, v_heads))
    done = {}

    def head_pair(b, pair):
        rows = slice(b * ATTN_BLOCK, (b + 1) * ATTN_BLOCK)
        keys = slice(b * ATTN_BLOCK, (b + 2) * ATTN_BLOCK)
        cols, k_heads, v_heads = operands[pair]
        bias_ref_b = bias0_ref if b == 0 else bias_ref
        bias = jnp.concatenate([bias_ref_b[2 * pair], bias_ref_b[2 * pair + 1]], axis=1)
        k_both = jnp.concatenate([k_heads[0][keys], k_heads[1][keys]], axis=0)
        s = lax.dot_general(q[rows, cols], k_both, (((1,), (1,)), ((), ())),
                            preferred_element_type=jnp.float32) + bias
        yield
        n_keys = 2 * ATTN_BLOCK
        s0, s1 = s[:, :n_keys], s[:, n_keys:]
        m0 = jnp.max(s0, axis=-1, keepdims=True)
        m1 = jnp.max(s1, axis=-1, keepdims=True)
        p0 = jnp.exp(s0 - m0).astype(jnp.bfloat16)
        p1 = jnp.exp(s1 - m1).astype(jnp.bfloat16)
        yield
        acc = (jnp.dot(p0, v_heads[0][keys], preferred_element_type=jnp.float32)
               + jnp.dot(p1, v_heads[1][keys], preferred_element_type=jnp.float32))
        yield
        denom = acc[:, pair_w:]
        done[b, pair] = (acc[:, :pair_w] / denom, jnp.where(first_head, m0, m1) + jnp.log(denom))

    problems = [head_pair(b, pair) for b in range(n_qb) for pair in range(HEADS_PER_GROUP // 2)]

    def store_results():
        for b in range(n_qb):
            rows = slice(b * ATTN_BLOCK, (b + 1) * ATTN_BLOCK)
            pairs = [done[b, pair] for pair in range(HEADS_PER_GROUP // 2)]
            o_ref[rows, :] = jnp.concatenate([o for o, _ in pairs], axis=1).astype(o_ref.dtype)
            lse_ref[rows, :] = jnp.concatenate([lse for _, lse in pairs], axis=1)

    return problems, store_results


def _attn_kernel(n_qb, n_res, q_ref, kp_ref, kc_ref, vp_ref, vc_ref, bias0_ref, bias_ref, o_ref, lse_ref):
    waiting, stores = [], []
    for ri in range(n_res):
        problems, store = _attn_residue(n_qb, q_ref.at[ri], kp_ref.at[ri], kc_ref.at[ri], vp_ref.at[ri],
                                        vc_ref.at[ri], bias0_ref, bias_ref, o_ref.at[ri], lse_ref.at[ri])
        waiting += problems
        stores.append(store)
    live = []
    while waiting or live:
        if waiting:
            live.append(waiting.pop(0))
        for gen in list(live):
            if next(gen, _DONE) is _DONE:
                live.remove(gen)
    for store in stores:
        store()


def _attention_group(qkv_g, bias, gi, dilation):
    B, _, L, _ = qkv_g.shape
    n_qb = min(ATTN_QB, L // ATTN_BLOCK)
    rows = n_qb * ATTN_BLOCK
    n_res = max(1, min(dilation, ATTN_QB // n_qb))
    cur = lambda which: pl.BlockSpec((None, n_res, rows, GROUP_WIDTH), lambda b, r, n: (b, r, n, which))
    prev = lambda which: pl.BlockSpec((None, n_res, ATTN_BLOCK, GROUP_WIDTH),
                                      lambda b, r, n: (b, r, jnp.maximum(n * n_qb - 1, 0), which))
    bias_blk = (None, HEADS_PER_GROUP, ATTN_BLOCK, 2 * ATTN_BLOCK)
    return pl.pallas_call(
        functools.partial(_attn_kernel, n_qb, n_res),
        grid=(B, dilation // n_res, L // rows),
        in_specs=[
            cur(0), prev(1), cur(1), prev(2), cur(2),
            pl.BlockSpec(bias_blk, lambda b, r, n: (jnp.minimum(n, 1), 0, 0, 0)),
            pl.BlockSpec(bias_blk, lambda b, r, n: (1, 0, 0, 0)),
        ],
        out_specs=[cur(0), cur(0)],
        out_shape=[
            jax.ShapeDtypeStruct((B, dilation, L, GROUP_WIDTH), jnp.bfloat16),
            jax.ShapeDtypeStruct((B, dilation, L, GROUP_WIDTH), jnp.float32),
        ],
        compiler_params=pltpu.CompilerParams(
            dimension_semantics=("parallel", "parallel", "arbitrary"), vmem_limit_bytes=VMEM_LIMIT),
        name=f"dilated_attention_g{gi}",
    )(qkv_g, qkv_g, qkv_g, qkv_g, qkv_g, bias, bias)


def _t5_causal_bucket(dist):
    max_exact = N_REL_BUCKETS // 2
    d_f = np.maximum(dist, max_exact).astype(np.float32)
    scaled = (np.log(d_f / np.float32(max_exact)) / np.float32(math.log(REL_MAX_DISTANCE / max_exact))
              * np.float32(N_REL_BUCKETS - max_exact))
    large = np.minimum(max_exact + scaled.astype(np.int32), N_REL_BUCKETS - 1)
    return np.where(dist < max_exact, dist, large).astype(np.int32)


def _attention_bias_table(rel_bias_g, window, dilation):
    nw = window // dilation
    qi = np.arange(ATTN_BLOCK)[:, None]
    ki = np.arange(2 * ATTN_BLOCK)[None, :]
    dist = ATTN_BLOCK + qi - ki
    band = (dist >= 0) & (dist <= nw)
    bucket = _t5_causal_bucket(np.maximum(dist, 0) * dilation)
    onehot = (jnp.asarray(bucket)[:, :, None] == jnp.arange(N_REL_BUCKETS)).astype(jnp.float32)
    bias = jnp.einsum('qkb,bh->hqk', onehot, rel_bias_g.astype(jnp.float32),
                      precision=lax.Precision.HIGHEST)
    masks = jnp.asarray(np.stack([band & (ki >= ATTN_BLOCK), band]))
    return jnp.where(masks[:, None], bias[None], NEG_INF)


def _gelu_tanh(x):
    return 0.5 * x * (1.0 + jnp.tanh(math.sqrt(2.0 / math.pi) * (x + 0.044715 * (x * x * x))))


def _softplus(x):
    return jnp.maximum(x, 0.0) + jnp.log(1.0 + jnp.exp(-jnp.abs(x)))


def _natural_order(blk_ref, d, slabs):
    if d == 1:
        return blk_ref[0].astype(jnp.float32)
    per = MIX_TILE // d
    for r in range(d):
        rows = blk_ref[r].astype(jnp.float32)
        for c in range(GROUP_WIDTH // LANES):
            slabs[c, pl.ds(r, per, stride=d), :] = rows[:, c * LANES:(c + 1) * LANES]
    return jnp.concatenate([slabs[c] for c in range(GROUP_WIDTH // LANES)], axis=1)


def _mix_tile(x_ref, o1_ref, o2_ref, o3_ref, l1_ref, l2_ref, l3_ref,
              n1_ref, win_ref, cw_ref, cb_ref, wg_ref, brg_ref, big_ref, lam_ref,
              wpa_ref, wpl_ref, wout_ref, n2_ref, wr_ref,
              x1_ref, xs_ref, route_ref, cnt_ref,
              xbuf, a_sc, b_sc, h_sc, *slabs):
    tm = MIX_TILE
    first_tile = pl.program_id(0) == 0

    @pl.when(first_tile)
    def _():
        xbuf[0:SUBLANES, :] = jnp.zeros((SUBLANES, LRU_WIDTH), jnp.float32)
        h_sc[...] = jnp.zeros_like(h_sc)

    x = x_ref[...]
    h = _rmsnorm(x, n1_ref[...]).astype(jnp.bfloat16)
    def project(first_col):
        return jnp.dot(h, win_ref[:, first_col:first_col + D_MODEL], preferred_element_type=jnp.float32)

    xr = project(0)

    yield
    g_lru = project(LRU_WIDTH)
    xbuf[SUBLANES:SUBLANES + tm, :] = xr
    xc = xr * cw_ref[CONV_WIDTH - 1:CONV_WIDTH, :] + cb_ref[...]
    for j in range(CONV_WIDTH - 1):
        back = CONV_WIDTH - 1 - j
        xc = xc + xbuf[SUBLANES - back:SUBLANES - back + tm, :] * cw_ref[j:j + 1, :]
    xbuf[0:SUBLANES, :] = xbuf[tm:tm + SUBLANES, :]

    xcb = xc.astype(jnp.bfloat16)
    r_parts, i_parts = [], []
    for c in range(N_LRU_CHUNKS):
        ri = jnp.dot(xcb[:, c * LRU_CHUNK:(c + 1) * LRU_CHUNK], wg_ref[c],
                     preferred_element_type=jnp.float32)
        r_parts.append(ri[:, :LRU_CHUNK])
        i_parts.append(ri[:, LRU_CHUNK:])
    yield
    g_a = project(2 * LRU_WIDTH)
    r = jax.nn.sigmoid(jnp.concatenate(r_parts, axis=1) + brg_ref[...])
    ig = jax.nn.sigmoid(jnp.concatenate(i_parts, axis=1) + big_ref[...])
    log_a = (-LRU_C * _softplus(-lam_ref[...])) * r
    a = jnp.exp(log_a)
    gap = 1.0 - a * a
    root = jnp.where(gap > 0.0, gap * lax.rsqrt(gap), 0.0)
    bb = root * (ig * xc)

    yield
    g_b = project(2 * LRU_WIDTH + D_MODEL)
    n_slab = LRU_WIDTH // LANES
    for c in range(n_slab):
        lanes = slice(c * LANES, (c + 1) * LANES)
        for s in range(SUBLANES):
            rows = slice(s * SCAN_SEG, (s + 1) * SCAN_SEG)
            a_sc[c, s * SCAN_PITCH:s * SCAN_PITCH + SCAN_SEG, :] = a[rows, lanes]
            b_sc[c, s * SCAN_PITCH:s * SCAN_PITCH + SCAN_SEG, :] = bb[rows, lanes]

    yield
    local = [jnp.zeros((SUBLANES, LANES), jnp.float32)] * n_slab
    decay = [jnp.ones((SUBLANES, LANES), jnp.float32)] * n_slab
    for j in range(SCAN_SEG):
        for c in range(n_slab):
            step = pl.ds(j, SUBLANES, stride=SCAN_PITCH)
            a_j = a_sc[c, step, :]
            local[c] = a_j * local[c] + b_sc[c, step, :]
            decay[c] = a_j * decay[c]
            b_sc[c, step, :] = local[c]
            a_sc[c, step, :] = decay[c]

    yield
    h_parts = []
    for c in range(n_slab):
        lanes = slice(c * LANES, (c + 1) * LANES)
        entry = h_sc[:, lanes]
        segs = []
        for s in range(SUBLANES):
            rows = slice(s * SCAN_PITCH, s * SCAN_PITCH + SCAN_SEG)
            segs.append(b_sc[c, rows, :] + a_sc[c, rows, :] * entry)
            entry = decay[c][s:s + 1, :] * entry + local[c][s:s + 1, :]
        h_sc[:, lanes] = entry
        h_parts.append(jnp.concatenate(segs, axis=0))
    h_all = jnp.concatenate(h_parts, axis=1)
    yield
    lru = (h_all * _gelu_tanh(g_lru)).astype(jnp.bfloat16)
    p_lru = jnp.dot(lru, wpl_ref[...], preferred_element_type=jnp.float32)

    yield
    dil = [d for _, d in DILATED_GROUPS]
    o1, o2, o3 = (_natural_order(ref, d, slabs[2 * g]) for g, (ref, d) in
                  enumerate(zip((o1_ref, o2_ref, o3_ref), dil)))
    l1, l2, l3 = (_natural_order(ref, d, slabs[2 * g + 1]) for g, (ref, d) in
                  enumerate(zip((l1_ref, l2_ref, l3_ref), dil)))
    lm = jnp.maximum(jnp.maximum(l1, l2), l3)
    e1, e2, e3 = jnp.exp(l1 - lm), jnp.exp(l2 - lm), jnp.exp(l3 - lm)
    attn = (e1 * o1 + e2 * o2 + e3 * o3) / (e1 + e2 + e3)
    p_attn = jnp.dot(attn.astype(jnp.bfloat16), wpa_ref[...], preferred_element_type=jnp.float32)

    yield
    merged = jax.nn.sigmoid(g_a) * p_attn + jax.nn.sigmoid(g_b) * p_lru
    x1 = x + jnp.dot(merged.astype(jnp.bfloat16), wout_ref[...], preferred_element_type=jnp.float32)
    x1_ref[...] = x1
    h2 = _rmsnorm(x1, n2_ref[...])

    yield
    h2_hi = h2.astype(jnp.bfloat16)
    h2_lo = (h2 - h2_hi.astype(jnp.float32)).astype(jnp.bfloat16)
    by_hi = jnp.dot(h2_hi, wr_ref[...], preferred_element_type=jnp.float32)
    by_lo = jnp.dot(h2_lo, wr_ref[...], preferred_element_type=jnp.float32)
    logits = (by_hi[:, :LANES] + (by_hi[:, LANES:] + by_lo[:, :LANES])) + by_lo[:, LANES:]
    lane = lax.broadcasted_iota(jnp.int32, (tm, LANES), 1)
    big = jnp.int32(LANES)
    lowest = jnp.float32(-3.0e38)
    is_g = (lane >= GROUP_LOGIT_LANE0) & (lane < GROUP_LOGIT_LANE0 + N_EXPERT_GROUPS)
    gl = jnp.where(is_g, logits, lowest)
    gmax = jnp.max(gl, axis=-1, keepdims=True)
    gsel = jnp.min(jnp.where(gl == gmax, lane, big), axis=-1, keepdims=True) - GROUP_LOGIT_LANE0
    p_sel = 1.0 / jnp.sum(jnp.where(is_g, jnp.exp(gl - gmax), 0.0), axis=-1, keepdims=True)
    is_e = (lane >= gsel * EXPERTS_PER_GROUP) & (lane < (gsel + 1) * EXPERTS_PER_GROUP)
    el = jnp.where(is_e, logits, lowest)
    v1 = jnp.max(el, axis=-1, keepdims=True)
    i1 = jnp.min(jnp.where(el == v1, lane, big), axis=-1, keepdims=True)
    el2 = jnp.where(lane == i1, lowest, el)
    v2 = jnp.max(el2, axis=-1, keepdims=True)
    i2 = jnp.min(jnp.where(el2 == v2, lane, big), axis=-1, keepdims=True)
    t = jnp.exp(v2 - v1)
    g0 = p_sel / (1.0 + t)
    g1 = p_sel * t / (1.0 + t)

    yield
    oh0 = lane == i1
    oh1 = lane == i2
    both = (oh0 | oh1).astype(jnp.float32)
    tri = (lax.broadcasted_iota(jnp.int32, (tm, tm), 0)
           > lax.broadcasted_iota(jnp.int32, (tm, tm), 1)).astype(jnp.bfloat16)
    before = jnp.dot(tri, both.astype(jnp.bfloat16), preferred_element_type=jnp.float32)
    cnt = jnp.sum(both, axis=0, keepdims=True)
    padded = jnp.floor((cnt + (MOE_CHUNK - 1.0)) * (1.0 / MOE_CHUNK)) * MOE_CHUNK
    upper = (lax.broadcasted_iota(jnp.int32, (LANES, LANES), 0)
             < lax.broadcasted_iota(jnp.int32, (LANES, LANES), 1)).astype(jnp.bfloat16)
    loff = jnp.dot(jnp.broadcast_to(padded, (SUBLANES, LANES)).astype(jnp.bfloat16), upper,
                   preferred_element_type=jnp.float32)[0:1, :]
    base = before + loff
    slot0 = jnp.sum(jnp.where(oh0, base, 0.0), axis=-1, keepdims=True)
    slot1 = jnp.sum(jnp.where(oh1, base, 0.0), axis=-1, keepdims=True)
    cnt_ref[...] = jnp.broadcast_to(cnt, cnt_ref.shape)

    slot_id = lax.broadcasted_iota(jnp.int32, (tm, MOE_SLOTS), 1)
    place = ((slot_id == slot0.astype(jnp.int32)) | (slot_id == slot1.astype(jnp.int32)))
    xs_ref[...] = _pack_bf16_pairs(
        lax.dot_general(place.astype(jnp.bfloat16), h2_hi, (((0,), (0,)), ((), ())),
                        preferred_element_type=jnp.float32))

    route = jnp.zeros((tm, LANES), jnp.float32)
    for lane_id, val in ((ROUTE_S0, slot0), (ROUTE_S1, slot1), (ROUTE_G0, g0), (ROUTE_G1, g1)):
        route = jnp.where(lane == lane_id, val, route)
    route_ref[...] = route


N_MIXER_TILED_INPUTS = 1 + 2 * N_GROUPS
N_MIXER_OUTPUTS = 4


def _mixer_kernel(*refs):
    n_batch = refs[0].shape[0]
    n_weights = len(refs) - N_MIXER_TILED_INPUTS - N_MIXER_OUTPUTS - n_batch * _MIXER_SCRATCH_PER_SEQ
    tiled = refs[:N_MIXER_TILED_INPUTS]
    weights = refs[N_MIXER_TILED_INPUTS:N_MIXER_TILED_INPUTS + n_weights]
    outs = refs[N_MIXER_TILED_INPUTS + n_weights:N_MIXER_TILED_INPUTS + n_weights + N_MIXER_OUTPUTS]
    scratch = refs[N_MIXER_TILED_INPUTS + n_weights + N_MIXER_OUTPUTS:]
    stages = []
    for b in range(n_batch):
        own = scratch[b * _MIXER_SCRATCH_PER_SEQ:(b + 1) * _MIXER_SCRATCH_PER_SEQ]
        stages.append(_mix_tile(*[r.at[b] for r in tiled], *weights, *[r.at[b] for r in outs], *own))
    while stages:
        for gen in list(stages):
            if next(gen, _DONE) is _DONE:
                stages.remove(gen)


def _mixer_scratch(tm):
    return [
        pltpu.VMEM((tm + 2 * SUBLANES, LRU_WIDTH), jnp.float32),
        pltpu.VMEM((LRU_WIDTH // LANES, SUBLANES * SCAN_PITCH, LANES), jnp.float32),
        pltpu.VMEM((LRU_WIDTH // LANES, SUBLANES * SCAN_PITCH, LANES), jnp.float32),
        pltpu.VMEM((1, LRU_WIDTH), jnp.float32),
    ] + [pltpu.VMEM((GROUP_WIDTH // LANES, tm, LANES), jnp.float32)] * (2 * N_GROUPS)


_MIXER_SCRATCH_PER_SEQ = len(_mixer_scratch(MIX_TILE))


def _token_mixer(x, attn_o, attn_lse, weights):
    B, S, _ = x.shape
    tm = MIX_TILE
    nt = S // tm
    rows = lambda j: (0, j, 0)

    def resident(shape):
        return pl.BlockSpec(shape, lambda j: (0,) * len(shape), pipeline_mode=pl.Buffered(1))

    in_specs = [pl.BlockSpec((B, tm, D_MODEL), rows)]
    group_specs = [pl.BlockSpec((B, d, tm // d, GROUP_WIDTH), lambda j: (0, 0, j, 0))
                   for _, d in DILATED_GROUPS]
    in_specs += group_specs + group_specs
    in_specs += [resident(w.shape) for w in weights]
    x1, xs, route, counts = pl.pallas_call(
        _mixer_kernel,
        grid=(nt,),
        in_specs=in_specs,
        out_specs=[
            pl.BlockSpec((B, tm, D_MODEL), rows),
            pl.BlockSpec((B, MOE_SLOTS, D_MODEL // 2), rows),
            pl.BlockSpec((B, tm, LANES), rows),
            pl.BlockSpec((B, None, SUBLANES, LANES), lambda j: (0, j, 0, 0)),
        ],
        out_shape=[
            jax.ShapeDtypeStruct((B, S, D_MODEL), jnp.float32),
            jax.ShapeDtypeStruct((B, nt * MOE_SLOTS, D_MODEL // 2), jnp.uint32),
            jax.ShapeDtypeStruct((B, S, LANES), jnp.float32),
            jax.ShapeDtypeStruct((B, nt, SUBLANES, LANES), jnp.float32),
        ],
        scratch_shapes=_mixer_scratch(tm) * B,
        compiler_params=pltpu.CompilerParams(
            dimension_semantics=("arbitrary",), vmem_limit_bytes=VMEM_LIMIT),
        name="token_mixer",
    )(x, *attn_o, *attn_lse, *weights)
    return (x1.reshape(B * S, D_MODEL), xs.reshape(B * nt * MOE_SLOTS, D_MODEL // 2),
            route.reshape(B * S, LANES), counts.reshape(B * nt, SUBLANES, LANES))


def _moe_tables(cnt_tile):
    ntiles = cnt_tile.shape[0]
    nch = (cnt_tile + MOE_CHUNK - 1) // MOE_CHUNK
    lo_c = jnp.cumsum(nch, axis=1) - nch
    per_expert = jnp.sum(nch, axis=0)
    region = ((per_expert + MOE_BLOCK_CHUNKS - 1) // MOE_BLOCK_CHUNKS) * MOE_BLOCK_CHUNKS
    pend = jnp.cumsum(region)
    glob = (pend - region)[None, :] + jnp.cumsum(nch, axis=0) - nch

    max_rows = TOP_K * ntiles * MIX_TILE + ntiles * N_EXPERTS * (MOE_CHUNK - 1)
    max_chunks = -(-max_rows // MOE_CHUNK) + N_EXPERTS * (MOE_BLOCK_CHUNKS - 1)
    max_blocks = -(-max_chunks // MOE_BLOCK_CHUNKS)
    max_blocks = -(-max_blocks // EXPERT_BLOCKS_PER_STEP) * EXPERT_BLOCKS_PER_STEP
    max_chunks = max_blocks * MOE_BLOCK_CHUNKS

    seg_start = glob.T.reshape(-1)
    seg_src = (jnp.arange(ntiles, dtype=jnp.int32)[:, None] * TILE_CHUNKS + lo_c).T.reshape(-1)
    step = jnp.diff(seg_src - seg_start, prepend=0)
    g = jnp.arange(max_chunks, dtype=jnp.int32)
    src = g + jnp.sum(jnp.where(seg_start[None, :] <= g[:, None], step[None, :], 0), axis=1)
    src = jnp.clip(src, 0, ntiles * TILE_CHUNKS - 1)

    lstep = jnp.diff(glob - lo_c, axis=1, prepend=0)
    lc = jnp.arange(TILE_CHUNKS, dtype=jnp.int32)
    comb = lc[None, :] + jnp.sum(
        jnp.where(lo_c[:, None, :] <= lc[None, :, None], lstep[:, None, :], 0), axis=2)
    comb = jnp.clip(comb, 0, max_chunks - 1).reshape(-1)

    eidx = jnp.arange(N_EXPERTS, dtype=jnp.int32)
    nonempty = region > 0
    later = jnp.where(nonempty[None, :] & (eidx[None, :] > eidx[:, None]), eidx[None, :], N_EXPERTS)
    next_expert = jnp.min(later, axis=1)
    next_expert = jnp.where(next_expert == N_EXPERTS, -1, next_expert)
    wslot = (jnp.cumsum(nonempty.astype(jnp.int32)) - nonempty.astype(jnp.int32)) % 2
    last_expert = jnp.max(jnp.where(nonempty, eidx, 0))
    blk0 = jnp.arange(max_blocks, dtype=jnp.int32) * MOE_BLOCK_CHUNKS
    block_expert = jnp.minimum(jnp.sum((pend[None, :] <= blk0[:, None]).astype(jnp.int32), axis=1),
                               last_expert)
    of_block = lambda per_expert: jnp.sum(
        jnp.where(block_expert[:, None] == eidx[None, :], per_expert[None, :], 0), axis=1)
    n_valid = pend[-1:] // MOE_BLOCK_CHUNKS
    as_i32 = lambda v: v.astype(jnp.int32)
    expert_tables = (as_i32(block_expert), as_i32(of_block(next_expert)), as_i32(of_block(wslot)),
                     as_i32(n_valid), as_i32(src))
    return expert_tables, as_i32(comb)


def _chunk_copy(src_hbm, src_chunk, dst, dst_chunk, sem):
    rows = lambda c: pl.ds(pl.multiple_of(c * MOE_CHUNK, MOE_CHUNK), MOE_CHUNK)
    return pltpu.make_async_copy(src_hbm.at[rows(src_chunk), :], dst.at[rows(dst_chunk), :], sem)


def _expert_kernel(be_ref, nxt_ref, ws_ref, nv_ref, src_ref, xs_hbm, wgu_hbm, wd_hbm, ys_ref, *scratch):
    for k in range(EXPERT_BLOCKS_PER_STEP):
        _expert_block(pl.program_id(0) * EXPERT_BLOCKS_PER_STEP + k,
                      be_ref, nxt_ref, ws_ref, nv_ref, src_ref, xs_hbm, wgu_hbm, wd_hbm,
                      ys_ref.at[pl.ds(k * MOE_TILE, MOE_TILE), :], *scratch)


def _expert_block(i, be_ref, nxt_ref, ws_ref, nv_ref, src_ref, xs_hbm, wgu_hbm, wd_hbm, ys_ref,
                  xbuf, wgu_f32, wd_f32, wgu_bf, wd_bf, sems, wsems):
    valid = i < nv_ref[0]

    def gather(blk, slot):
        return [_chunk_copy(xs_hbm, src_ref[blk * MOE_BLOCK_CHUNKS + c], xbuf.at[slot], c, sems.at[slot])
                for c in range(MOE_BLOCK_CHUNKS)]

    def fetch_weights(e, slot):
        return [pltpu.make_async_copy(wgu_hbm.at[e], wgu_f32.at[slot], wsems.at[slot]),
                pltpu.make_async_copy(wd_hbm.at[e], wd_f32.at[slot], wsems.at[slot])]

    ahead = EXPERT_GATHER_SLOTS - 1

    @pl.when(i == 0)
    def _():
        for cp in fetch_weights(be_ref[0], 0):
            cp.start(priority=WEIGHT_DMA_PRIORITY)
        for first in range(ahead):
            @pl.when(first < nv_ref[0])
            def _():
                for cp in gather(first, first):
                    cp.start()

    @pl.when(i + ahead < nv_ref[0])
    def _():
        for cp in gather(i + ahead, (i + ahead) % EXPERT_GATHER_SLOTS):
            cp.start()

    new_expert = valid & ((i == 0) | (be_ref[i] != be_ref[jnp.maximum(i - 1, 0)]))

    @pl.when(new_expert)
    def _():
        ws = ws_ref[i]
        for cp in fetch_weights(be_ref[i], ws):
            cp.wait()
        wgu_bf[...] = wgu_f32[ws].astype(jnp.bfloat16)
        wd_bf[...] = wd_f32[ws].astype(jnp.bfloat16)

        @pl.when(nxt_ref[i] >= 0)
        def _():
            for cp in fetch_weights(nxt_ref[i], 1 - ws):
                cp.start(priority=WEIGHT_DMA_PRIORITY)

    @pl.when(valid)
    def _():
        slot = i % EXPERT_GATHER_SLOTS
        for cp in gather(i, slot):
            cp.wait()
        x_lo, x_hi = _unpack_bf16_pairs(xbuf[slot])
        half = D_MODEL // 2
        partial = []

        def up_proj(cols):
            return (jnp.dot(x_lo, wgu_bf[:half, cols], preferred_element_type=jnp.float32)
                    + jnp.dot(x_hi, wgu_bf[half:, cols], preferred_element_type=jnp.float32))

        def hidden_group(c):
            cols = slice(c * EXPERT_GROUP, (c + 1) * EXPERT_GROUP)
            g = up_proj(cols)
            u = up_proj(slice(D_EXPERT + cols.start, D_EXPERT + cols.stop))
            yield
            act = (g * jax.nn.sigmoid(g) * u).astype(jnp.bfloat16)
            yield
            partial.append(jnp.dot(act, wd_bf[cols, :], preferred_element_type=jnp.float32))

        waiting = [hidden_group(c) for c in range(D_EXPERT // EXPERT_GROUP)]
        live = []
        while waiting or live:
            if waiting:
                live.append(waiting.pop(0))
            for gen in list(live):
                if next(gen, _DONE) is _DONE:
                    live.remove(gen)
        y = sum(partial[1:], partial[0])
        ys_ref[...] = _pack_bf16_pairs(y.astype(jnp.bfloat16).astype(jnp.float32))

    @pl.when(jnp.logical_not(valid))
    def _():
        ys_ref[...] = jnp.zeros_like(ys_ref)


def _experts(tables, xs, w_gate_up, w_down):
    block_expert, block_next, block_wslot, n_valid, src = tables
    nblk = block_expert.shape[0]
    return pl.pallas_call(
        _expert_kernel,
        grid_spec=pltpu.PrefetchScalarGridSpec(
            num_scalar_prefetch=5,
            grid=(nblk // EXPERT_BLOCKS_PER_STEP,),
            in_specs=[pl.BlockSpec(memory_space=pl.ANY)] * 3,
            out_specs=pl.BlockSpec((EXPERT_BLOCKS_PER_STEP * MOE_TILE, D_MODEL // 2),
                                   lambda i, *_: (i, 0)),
            scratch_shapes=[
                pltpu.VMEM((EXPERT_GATHER_SLOTS, MOE_TILE, D_MODEL // 2), jnp.uint32),
                pltpu.VMEM((2, D_MODEL, 2 * D_EXPERT), jnp.float32),
                pltpu.VMEM((2, D_EXPERT, D_MODEL), jnp.float32),
                pltpu.VMEM((D_MODEL, 2 * D_EXPERT), jnp.bfloat16),
                pltpu.VMEM((D_EXPERT, D_MODEL), jnp.bfloat16),
                pltpu.SemaphoreType.DMA((EXPERT_GATHER_SLOTS,)),
                pltpu.SemaphoreType.DMA((2,)),
            ],
        ),
        out_shape=jax.ShapeDtypeStruct((nblk * MOE_TILE, D_MODEL // 2), jnp.uint32),
        compiler_params=pltpu.CompilerParams(
            dimension_semantics=("arbitrary",), vmem_limit_bytes=VMEM_LIMIT),
        name="moe_experts",
    )(block_expert, block_next, block_wslot, n_valid, src, xs, w_gate_up, w_down)


def _combine_kernel(comb_ref, ys_hbm, x1_ref, route_ref, nf_ref, out_ref, ybuf, sems):
    tc = MIX_TILE
    per_step = COMBINE_TILES_PER_STEP
    i = pl.program_id(0)
    last = pl.num_programs(0) - 1

    def gather(step, slot):
        first = step * per_step * TILE_CHUNKS
        return [_chunk_copy(ys_hbm, comb_ref[first + lc], ybuf.at[slot], lc, sems.at[slot])
                for lc in range(per_step * TILE_CHUNKS)]

    @pl.when(i == 0)
    def _():
        for cp in gather(0, 0):
            cp.start()

    nxt = jnp.minimum(i + 1, last)
    for cp in gather(nxt, (i + 1) % 2):
        cp.start()

    slot = i % 2
    for cp in gather(i, slot):
        cp.wait()

    def tile(t):
        rows = slice(t * tc, (t + 1) * tc)
        y_lo, y_hi = _unpack_bf16_pairs(ybuf[slot, t * MOE_SLOTS:(t + 1) * MOE_SLOTS, :])
        route = route_ref[rows, :]
        slot_id = lax.broadcasted_iota(jnp.int32, (tc, MOE_SLOTS), 1)
        picked = []
        for lane in (ROUTE_S0, ROUTE_S1):
            sel = (slot_id == route[:, lane:lane + 1].astype(jnp.int32)).astype(jnp.bfloat16)
            yield
            picked.append(jnp.concatenate([jnp.dot(sel, y_lo, preferred_element_type=jnp.float32),
                                           jnp.dot(sel, y_hi, preferred_element_type=jnp.float32)], axis=1))
            yield
        g0 = route[:, ROUTE_G0:ROUTE_G0 + 1]
        g1 = route[:, ROUTE_G1:ROUTE_G1 + 1]
        out_ref[rows, :] = _rmsnorm(x1_ref[rows, :] + g0 * picked[0] + g1 * picked[1], nf_ref[...])

    waiting = [tile(t) for t in range(per_step)]
    live = []
    while waiting or live:
        if waiting:
            live.append(waiting.pop(0))
        for gen in list(live):
            if next(gen, _DONE) is _DONE:
                live.remove(gen)

    @pl.when(i == last)
    def _():
        for cp in gather(nxt, (i + 1) % 2):
            cp.wait()


def _combine(comb, ys, x1, route, norm_f):
    T = x1.shape[0]
    tc = MIX_TILE * COMBINE_TILES_PER_STEP
    return pl.pallas_call(
        _combine_kernel,
        grid_spec=pltpu.PrefetchScalarGridSpec(
            num_scalar_prefetch=1,
            grid=(T // tc,),
            in_specs=[
                pl.BlockSpec(memory_space=pl.ANY),
                pl.BlockSpec((tc, D_MODEL), lambda i, c: (i, 0)),
                pl.BlockSpec((tc, LANES), lambda i, c: (i, 0)),
                pl.BlockSpec((1, D_MODEL), lambda i, c: (0, 0)),
            ],
            out_specs=pl.BlockSpec((tc, D_MODEL), lambda i, c: (i, 0)),
            scratch_shapes=[
                pltpu.VMEM((2, COMBINE_TILES_PER_STEP * MOE_SLOTS, D_MODEL // 2), jnp.uint32),
                pltpu.SemaphoreType.DMA((2,)),
            ],
        ),
        out_shape=jax.ShapeDtypeStruct((T, D_MODEL), jnp.float32),
        compiler_params=pltpu.CompilerParams(
            dimension_semantics=("arbitrary",), vmem_limit_bytes=VMEM_LIMIT),
        name="moe_combine",
    )(comb, ys, x1, route, norm_f)


def _block_diag_gates(w_rg, w_ig):
    def bd(w):
        w4 = w.reshape(N_LRU_CHUNKS, LRU_CHUNK_HEADS, LRU_HEAD_DIM, LRU_HEAD_DIM)
        eye = jnp.eye(LRU_CHUNK_HEADS, dtype=w.dtype)
        return jnp.einsum('chij,hk->chikj', w4, eye).reshape(N_LRU_CHUNKS, LRU_CHUNK, LRU_CHUNK)
    return jnp.concatenate([bd(w_rg), bd(w_ig)], axis=-1).astype(jnp.bfloat16)


def kernel(x, rel_bias, norm1, w_in, conv_w, conv_b, w_rg, b_rg, w_ig, b_ig, lru_lambda,
           w_proj_attn, w_proj_lru, w_out, norm2, w_router_group, w_router_expert,
           w_gate_up, w_down, norm_f):
    B, S, D = x.shape
    T = B * S
    assert w_in.shape[0] == 1, "single-layer block"
    layer = 0
    bf16 = jnp.bfloat16
    x2 = x.reshape(T, D)
    row = lambda v: v[layer].reshape(1, -1)
    w_in_bf = w_in[layer].astype(bf16)
    w_qkv = (w_in_bf[:, :QKV_WIDTH].reshape(D, 3, N_GROUPS, GROUP_WIDTH)
             .transpose(0, 2, 1, 3).reshape(D, QKV_WIDTH))
    w_rest = w_in_bf[:, QKV_WIDTH:]

    qkv_groups = _qkv_projection(x2, row(norm1), w_qkv, B, S)
    attn_o, attn_lse = [], []
    for gi, (window, dilation) in enumerate(DILATED_GROUPS):
        hs = slice(gi * HEADS_PER_GROUP, (gi + 1) * HEADS_PER_GROUP)
        bias = _attention_bias_table(rel_bias[:, hs], window, dilation)
        o, lse = _attention_group(qkv_groups[gi], bias, gi, dilation)
        attn_o.append(o)
        attn_lse.append(lse)

    w_router = jnp.concatenate(
        [w_router_expert[layer].astype(jnp.float32), w_router_group[layer].astype(jnp.float32),
         jnp.zeros((D, LANES - N_EXPERTS - N_EXPERT_GROUPS), jnp.float32)], axis=1)
    w_router_hi = w_router.astype(bf16)
    weights = [
        row(norm1), w_rest, conv_w[layer], row(conv_b),
        _block_diag_gates(w_rg[layer], w_ig[layer]), row(b_rg), row(b_ig), row(lru_lambda),
        w_proj_attn[layer].astype(bf16), w_proj_lru[layer].astype(bf16), w_out[layer].astype(bf16),
        row(norm2),
        jnp.concatenate([w_router_hi, (w_router - w_router_hi.astype(jnp.float32)).astype(bf16)], axis=1),
    ]
    x1, xs, route, counts = _token_mixer(x, attn_o, attn_lse, weights)

    expert_tables, comb = _moe_tables(counts[:, 0, :N_EXPERTS].astype(jnp.int32))
    ys = _experts(expert_tables, xs, w_gate_up[layer], w_down[layer])
    out = _combine(comb, ys, x1, route, norm_f.reshape(1, -1))
    return out.reshape(B, S, D)
```

```python
import functools
import math

import jax
import jax.numpy as jnp
import numpy as np
from jax import lax
from jax.experimental import pallas as pl
from jax.experimental.pallas import tpu as pltpu

D_MODEL = 1024
HEAD_DIM = 64
HEADS_PER_GROUP = 4
DILATED_GROUPS = ((128, 1), (512, 4), (2048, 16))
N_GROUPS = len(DILATED_GROUPS)
N_ATTN_HEADS = HEADS_PER_GROUP * N_GROUPS
ATTN_WIDTH = N_ATTN_HEADS * HEAD_DIM
GROUP_WIDTH = HEADS_PER_GROUP * HEAD_DIM
QKV_WIDTH = 3 * ATTN_WIDTH
ATTN_BLOCK = 128
N_REL_BUCKETS = 32
REL_MAX_DISTANCE = 2048
NEG_INF = -1e30

LRU_WIDTH = D_MODEL
LRU_HEADS = 16
LRU_HEAD_DIM = LRU_WIDTH // LRU_HEADS
CONV_WIDTH = 4
LRU_C = 8.0
LRU_CHUNK_HEADS = 4
LRU_CHUNK = LRU_CHUNK_HEADS * LRU_HEAD_DIM
N_LRU_CHUNKS = LRU_WIDTH // LRU_CHUNK

N_EXPERT_GROUPS = 4
EXPERTS_PER_GROUP = 8
N_EXPERTS = N_EXPERT_GROUPS * EXPERTS_PER_GROUP
TOP_K = 2
D_EXPERT = 512
EPS = 1e-6

LANES = 128
SUBLANES = 8

ATTN_TILE = 256
ATTN_TILES_PER_STEP = 4
ATTN_QB = 32
MIX_TILE = 256
SCAN_SEG = MIX_TILE // SUBLANES
SCAN_PITCH = SCAN_SEG + SUBLANES
MOE_TILE = 256
EXPERT_GATHER_SLOTS = 4
EXPERT_BLOCKS_PER_STEP = 8
COMBINE_TILES_PER_STEP = 4
WEIGHT_DMA_PRIORITY = 1
EXPERT_GROUP = 256
MOE_CHUNK = SUBLANES
MOE_BLOCK_CHUNKS = MOE_TILE // MOE_CHUNK
MOE_SLOTS = 768
assert MOE_SLOTS >= TOP_K * MIX_TILE + N_EXPERTS * (MOE_CHUNK - 1) and MOE_SLOTS % MOE_CHUNK == 0
TILE_CHUNKS = MOE_SLOTS // MOE_CHUNK
VMEM_LIMIT = 56 * 1024 * 1024

ROUTE_S0, ROUTE_S1, ROUTE_G0, ROUTE_G1 = range(4)
GROUP_LOGIT_LANE0 = N_EXPERTS


_DONE = object()


def _rmsnorm(x, g):
    return x * lax.rsqrt(jnp.mean(x * x, axis=-1, keepdims=True) + EPS) * g


def _pack_bf16_pairs(v):
    w = v.shape[1] // 2
    bits = lax.bitcast_convert_type(v, jnp.uint32)
    return bits[:, w:] | (bits[:, :w] >> 16)


def _unpack_bf16_pairs(p):
    lo = lax.bitcast_convert_type(p << 16, jnp.float32)
    hi = lax.bitcast_convert_type(p & jnp.uint32(0xFFFF0000), jnp.float32)
    return lo.astype(jnp.bfloat16), hi.astype(jnp.bfloat16)


def _qkv_kernel(x_ref, g_ref, w_ref, *refs):
    tm = ATTN_TILE
    o_refs, slab_refs = refs[:N_GROUPS], refs[N_GROUPS:]
    n_slab = ATTN_WIDTH // LANES

    def tile(t):
        rows = slice(t * tm, (t + 1) * tm)
        slabs = slab_refs[t]
        h = _rmsnorm(x_ref[rows, :], g_ref[...]).astype(jnp.bfloat16)
        yield
        for gi, (o_ref, (_, d)) in enumerate(zip(o_refs, DILATED_GROUPS)):
            cols = slice(gi * ATTN_WIDTH, (gi + 1) * ATTN_WIDTH)
            part = jnp.dot(h, w_ref[:, cols], preferred_element_type=jnp.float32)
            out_rows = slice(t * (tm // d), (t + 1) * (tm // d))
            if d == 1:
                o_ref[0, out_rows, :] = part.astype(jnp.bfloat16)
                yield
                continue
            for c in range(n_slab):
                slabs[c] = part[:, c * LANES:(c + 1) * LANES]
            yield
            for r in range(d):
                o_ref[r, out_rows, :] = jnp.concatenate(
                    [slabs[c, pl.ds(r, tm // d, stride=d), :] for c in range(n_slab)],
                    axis=1).astype(jnp.bfloat16)
            yield

    waiting = [tile(t) for t in range(ATTN_TILES_PER_STEP)]
    live = []
    while waiting or live:
        if waiting:
            live.append(waiting.pop(0))
        for gen in list(live):
            if next(gen, _DONE) is _DONE:
                live.remove(gen)


def _qkv_projection(x2, norm1, w_qkv, B, S):
    tm = ATTN_TILE * ATTN_TILES_PER_STEP
    nt = S // tm
    return pl.pallas_call(
        _qkv_kernel,
        grid=(B, nt),
        in_specs=[
            pl.BlockSpec((tm, D_MODEL), lambda b, j: (b * nt + j, 0)),
            pl.BlockSpec((1, D_MODEL), lambda b, j: (0, 0)),
            pl.BlockSpec((D_MODEL, QKV_WIDTH), lambda b, j: (0, 0)),
        ],
        out_specs=[pl.BlockSpec((None, d, tm // d, ATTN_WIDTH), lambda b, j: (b, 0, j, 0))
                   for _, d in DILATED_GROUPS],
        out_shape=[jax.ShapeDtypeStruct((B, d, S // d, ATTN_WIDTH), jnp.bfloat16)
                   for _, d in DILATED_GROUPS],
        scratch_shapes=[pltpu.VMEM((ATTN_WIDTH // LANES, ATTN_TILE, LANES), jnp.float32)]
        * ATTN_TILES_PER_STEP,
        compiler_params=pltpu.CompilerParams(
            dimension_semantics=("parallel", "parallel"), vmem_limit_bytes=VMEM_LIMIT),
        name="qkv_projection",
    )(x2, norm1, w_qkv)


def _attn_residue(n_qb, q_ref, kp_ref, kc_ref, vp_ref, vc_ref, bias0_ref, bias_ref, o_ref, lse_ref):
    q = q_ref[...] * (HEAD_DIM ** -0.5)
    k = jnp.concatenate([kp_ref[...], kc_ref[...]], axis=0)
    v = jnp.concatenate([vp_ref[...], vc_ref[...]], axis=0)
    pair_w = 2 * HEAD_DIM
    first_head = lax.broadcasted_iota(jnp.int32, (1, pair_w), 1) < HEAD_DIM
    def one_head(x, first):
        bits = pltpu.bitcast(x, jnp.uint32)
        keep = first_head if first else jnp.logical_not(first_head)
        return pltpu.bitcast(jnp.where(keep, bits, jnp.uint32(0)), jnp.bfloat16)

    ones = jnp.ones_like(k[:, :pair_w])
    operands = []
    for pair in range(HEADS_PER_GROUP // 2):
        cols = slice(pair * pair_w, (pair + 1) * pair_w)
        k2, v2 = k[:, cols], v[:, cols]
        k_heads = (one_head(k2, True), one_head(k2, False))
        v_heads = (jnp.concatenate([one_head(v2, True), one_head(ones, True)], axis=1),
                   jnp.concatenate([one_head(v2, False), one_head(ones, False)], axis=1))
        operands.append((cols, k_heads, v_heads))
    done = {}

    def head_pair(b, pair):
        rows = slice(b * ATTN_BLOCK, (b + 1) * ATTN_BLOCK)
        keys = slice(b * ATTN_BLOCK, (b + 2) * ATTN_BLOCK)
        cols, k_heads, v_heads = operands[pair]
        bias_ref_b = bias0_ref if b == 0 else bias_ref
        bias = jnp.concatenate([bias_ref_b[2 * pair], bias_ref_b[2 * pair + 1]], axis=1)
        k_both = jnp.concatenate([k_heads[0][keys], k_heads[1][keys]], axis=0)
        s = lax.dot_general(q[rows, cols], k_both, (((1,), (1,)), ((), ())),
                            preferred_element_type=jnp.float32) + bias
        yield
        n_keys = 2 * ATTN_BLOCK
        s0, s1 = s[:, :n_keys], s[:, n_keys:]
        m0 = jnp.max(s0, axis=-1, keepdims=True)
        m1 = jnp.max(s1, axis=-1, keepdims=True)
        p0 = jnp.exp(s0 - m0).astype(jnp.bfloat16)
        p1 = jnp.exp(s1 - m1).astype(jnp.bfloat16)
        yield
        acc = (jnp.dot(p0, v_heads[0][keys], preferred_element_type=jnp.float32)
               + jnp.dot(p1, v_heads[1][keys], preferred_element_type=jnp.float32))
        yield
        denom = acc[:, pair_w:]
        done[b, pair] = (acc[:, :pair_w] / denom, jnp.where(first_head, m0, m1) + jnp.log(denom))

    problems = [head_pair(b, pair) for b in range(n_qb) for pair in range(HEADS_PER_GROUP // 2)]

    def store_results():
        for b in range(n_qb):
            rows = slice(b * ATTN_BLOCK, (b + 1) * ATTN_BLOCK)
            pairs = [done[b, pair] for pair in range(HEADS_PER_GROUP // 2)]
            o_ref[rows, :] = jnp.concatenate([o for o, _ in pairs], axis=1).astype(o_ref.dtype)
            lse_ref[rows, :] = jnp.concatenate([lse for _, lse in pairs], axis=1)

    return problems, store_results


def _attn_kernel(n_qb, n_res, q_ref, kp_ref, kc_ref, vp_ref, vc_ref, bias0_ref, bias_ref, o_ref, lse_ref):
    waiting, stores = [], []
    for ri in range(n_res):
        problems, store = _attn_residue(n_qb, q_ref.at[ri], kp_ref.at[ri], kc_ref.at[ri], vp_ref.at[ri],
                                        vc_ref.at[ri], bias0_ref, bias_ref, o_ref.at[ri], lse_ref.at[ri])
        waiting += problems
        stores.append(store)
    live = []
    while waiting or live:
        if waiting:
            live.append(waiting.pop(0))
        for gen in list(live):
            if next(gen, _DONE) is _DONE:
                live.remove(gen)
    for store in stores:
        store()


def _attention_group(qkv_g, bias, gi, dilation):
    B, _, L, _ = qkv_g.shape
    n_qb = min(ATTN_QB, L // ATTN_BLOCK)
    rows = n_qb * ATTN_BLOCK
    n_res = max(1, min(dilation, ATTN_QB // n_qb))
    cur = lambda which: pl.BlockSpec((None, n_res, rows, GROUP_WIDTH), lambda b, r, n: (b, r, n, which))
    prev = lambda which: pl.BlockSpec((None, n_res, ATTN_BLOCK, GROUP_WIDTH),
                                      lambda b, r, n: (b, r, jnp.maximum(n * n_qb - 1, 0), which))
    bias_blk = (None, HEADS_PER_GROUP, ATTN_BLOCK, 2 * ATTN_BLOCK)
    return pl.pallas_call(
        functools.partial(_attn_kernel, n_qb, n_res),
        grid=(B, dilation // n_res, L // rows),
        in_specs=[
            cur(0), prev(1), cur(1), prev(2), cur(2),
            pl.BlockSpec(bias_blk, lambda b, r, n: (jnp.minimum(n, 1), 0, 0, 0)),
            pl.BlockSpec(bias_blk, lambda b, r, n: (1, 0, 0, 0)),
        ],
        out_specs=[cur(0), cur(0)],
        out_shape=[
            jax.ShapeDtypeStruct((B, dilation, L, GROUP_WIDTH), jnp.bfloat16),
            jax.ShapeDtypeStruct((B, dilation, L, GROUP_WIDTH), jnp.float32),
        ],
        compiler_params=pltpu.CompilerParams(
            dimension_semantics=("parallel", "parallel", "arbitrary"), vmem_limit_bytes=VMEM_LIMIT),
        name=f"dilated_attention_g{gi}",
    )(qkv_g, qkv_g, qkv_g, qkv_g, qkv_g, bias, bias)


def _t5_causal_bucket(dist):
    max_exact = N_REL_BUCKETS // 2
    d_f = np.maximum(dist, max_exact).astype(np.float32)
    scaled = (np.log(d_f / np.float32(max_exact)) / np.float32(math.log(REL_MAX_DISTANCE / max_exact))
              * np.float32(N_REL_BUCKETS - max_exact))
    large = np.minimum(max_exact + scaled.astype(np.int32), N_REL_BUCKETS - 1)
    return np.where(dist < max_exact, dist, large).astype(np.int32)


def _attention_bias_table(rel_bias_g, window, dilation):
    nw = window // dilation
    qi = np.arange(ATTN_BLOCK)[:, None]
    ki = np.arange(2 * ATTN_BLOCK)[None, :]
    dist = ATTN_BLOCK + qi - ki
    band = (dist >= 0) & (dist <= nw)
    bucket = _t5_causal_bucket(np.maximum(dist, 0) * dilation)
    onehot = (jnp.asarray(bucket)[:, :, None] == jnp.arange(N_REL_BUCKETS)).astype(jnp.float32)
    bias = jnp.einsum('qkb,bh->hqk', onehot, rel_bias_g.astype(jnp.float32),
                      precision=lax.Precision.HIGHEST)
    masks = jnp.asarray(np.stack([band & (ki >= ATTN_BLOCK), band]))
    return jnp.where(masks[:, None], bias[None], NEG_INF)


def _gelu_tanh(x):
    return 0.5 * x * (1.0 + jnp.tanh(math.sqrt(2.0 / math.pi) * (x + 0.044715 * (x * x * x))))


def _softplus(x):
    return jnp.maximum(x, 0.0) + jnp.log(1.0 + jnp.exp(-jnp.abs(x)))


def _natural_order(blk_ref, d, slabs):
    if d == 1:
        return blk_ref[0].astype(jnp.float32)
    per = MIX_TILE // d
    for r in range(d):
        rows = blk_ref[r].astype(jnp.float32)
        for c in range(GROUP_WIDTH // LANES):
            slabs[c, pl.ds(r, per, stride=d), :] = rows[:, c * LANES:(c + 1) * LANES]
    return jnp.concatenate([slabs[c] for c in range(GROUP_WIDTH // LANES)], axis=1)


def _mix_tile(x_ref, o1_ref, o2_ref, o3_ref, l1_ref, l2_ref, l3_ref,
              n1_ref, win_ref, cw_ref, cb_ref, wg_ref, brg_ref, big_ref, lam_ref,
              wpa_ref, wpl_ref, wout_ref, n2_ref, wr_ref,
              x1_ref, xs_ref, route_ref, cnt_ref,
              xbuf, a_sc, b_sc, h_sc, *slabs):
    tm = MIX_TILE
    first_tile = pl.program_id(0) == 0

    @pl.when(first_tile)
    def _():
        xbuf[0:SUBLANES, :] = jnp.zeros((SUBLANES, LRU_WIDTH), jnp.float32)
        h_sc[...] = jnp.zeros_like(h_sc)

    x = x_ref[...]
    h = _rmsnorm(x, n1_ref[...]).astype(jnp.bfloat16)
    def project(first_col):
        return jnp.dot(h, win_ref[:, first_col:first_col + D_MODEL], preferred_element_type=jnp.float32)

    xr = project(0)

    yield
    g_lru = project(LRU_WIDTH)
    xbuf[SUBLANES:SUBLANES + tm, :] = xr
    xc = xr * cw_ref[CONV_WIDTH - 1:CONV_WIDTH, :] + cb_ref[...]
    for j in range(CONV_WIDTH - 1):
        back = CONV_WIDTH - 1 - j
        xc = xc + xbuf[SUBLANES - back:SUBLANES - back + tm, :] * cw_ref[j:j + 1, :]
    xbuf[0:SUBLANES, :] = xbuf[tm:tm + SUBLANES, :]

    xcb = xc.astype(jnp.bfloat16)
    r_parts, i_parts = [], []
    for c in range(N_LRU_CHUNKS):
        ri = jnp.dot(xcb[:, c * LRU_CHUNK:(c + 1) * LRU_CHUNK], wg_ref[c],
                     preferred_element_type=jnp.float32)
        r_parts.append(ri[:, :LRU_CHUNK])
        i_parts.append(ri[:, LRU_CHUNK:])
    yield
    g_a = project(2 * LRU_WIDTH)
    r = jax.nn.sigmoid(jnp.concatenate(r_parts, axis=1) + brg_ref[...])
    ig = jax.nn.sigmoid(jnp.concatenate(i_parts, axis=1) + big_ref[...])
    log_a = (-LRU_C * _softplus(-lam_ref[...])) * r
    a = jnp.exp(log_a)
    gap = 1.0 - a * a
    root = jnp.where(gap > 0.0, gap * lax.rsqrt(gap), 0.0)
    bb = root * (ig * xc)

    yield
    g_b = project(2 * LRU_WIDTH + D_MODEL)
    n_slab = LRU_WIDTH // LANES
    for c in range(n_slab):
        lanes = slice(c * LANES, (c + 1) * LANES)
        for s in range(SUBLANES):
            rows = slice(s * SCAN_SEG, (s + 1) * SCAN_SEG)
            a_sc[c, s * SCAN_PITCH:s * SCAN_PITCH + SCAN_SEG, :] = a[rows, lanes]
            b_sc[c, s * SCAN_PITCH:s * SCAN_PITCH + SCAN_SEG, :] = bb[rows, lanes]

    yield
    local = [jnp.zeros((SUBLANES, LANES), jnp.float32)] * n_slab
    decay = [jnp.ones((SUBLANES, LANES), jnp.float32)] * n_slab
    for j in range(SCAN_SEG):
        for c in range(n_slab):
            step = pl.ds(j, SUBLANES, stride=SCAN_PITCH)
            a_j = a_sc[c, step, :]
            local[c] = a_j * local[c] + b_sc[c, step, :]
            decay[c] = a_j * decay[c]
            b_sc[c, step, :] = local[c]
            a_sc[c, step, :] = decay[c]

    yield
    h_parts = []
    for c in range(n_slab):
        lanes = slice(c * LANES, (c + 1) * LANES)
        entry = h_sc[:, lanes]
        segs = []
        for s in range(SUBLANES):
            rows = slice(s * SCAN_PITCH, s * SCAN_PITCH + SCAN_SEG)
            segs.append(b_sc[c, rows, :] + a_sc[c, rows, :] * entry)
            entry = decay[c][s:s + 1, :] * entry + local[c][s:s + 1, :]
        h_sc[:, lanes] = entry
        h_parts.append(jnp.concatenate(segs, axis=0))
    h_all = jnp.concatenate(h_parts, axis=1)
    yield
    lru = (h_all * _gelu_tanh(g_lru)).astype(jnp.bfloat16)
    p_lru = jnp.dot(lru, wpl_ref[...], preferred_element_type=jnp.float32)

    yield
    dil = [d for _, d in DILATED_GROUPS]
    o1, o2, o3 = (_natural_order(ref, d, slabs[2 * g]) for g, (ref, d) in
                  enumerate(zip((o1_ref, o2_ref, o3_ref), dil)))
    l1, l2, l3 = (_natural_order(ref, d, slabs[2 * g + 1]) for g, (ref, d) in
                  enumerate(zip((l1_ref, l2_ref, l3_ref), dil)))
    lm = jnp.maximum(jnp.maximum(l1, l2), l3)
    e1, e2, e3 = jnp.exp(l1 - lm), jnp.exp(l2 - lm), jnp.exp(l3 - lm)
    attn = (e1 * o1 + e2 * o2 + e3 * o3) / (e1 + e2 + e3)
    p_attn = jnp.dot(attn.astype(jnp.bfloat16), wpa_ref[...], preferred_element_type=jnp.float32)

    yield
    merged = jax.nn.sigmoid(g_a) * p_attn + jax.nn.sigmoid(g_b) * p_lru
    x1 = x + jnp.dot(merged.astype(jnp.bfloat16), wout_ref[...], preferred_element_type=jnp.float32)
    x1_ref[...] = x1
    h2 = _rmsnorm(x1, n2_ref[...])

    yield
    h2_hi = h2.astype(jnp.bfloat16)
    h2_lo = (h2 - h2_hi.astype(jnp.float32)).astype(jnp.bfloat16)
    by_hi = jnp.dot(h2_hi, wr_ref[...], preferred_element_type=jnp.float32)
    by_lo = jnp.dot(h2_lo, wr_ref[...], preferred_element_type=jnp.float32)
    logits = (by_hi[:, :LANES] + (by_hi[:, LANES:] + by_lo[:, :LANES])) + by_lo[:, LANES:]
    lane = lax.broadcasted_iota(jnp.int32, (tm, LANES), 1)
    big = jnp.int32(LANES)
    lowest = jnp.float32(-3.0e38)
    is_g = (lane >= GROUP_LOGIT_LANE0) & (lane < GROUP_LOGIT_LANE0 + N_EXPERT_GROUPS)
    gl = jnp.where(is_g, logits, lowest)
    gmax = jnp.max(gl, axis=-1, keepdims=True)
    gsel = jnp.min(jnp.where(gl == gmax, lane, big), axis=-1, keepdims=True) - GROUP_LOGIT_LANE0
    p_sel = 1.0 / jnp.sum(jnp.where(is_g, jnp.exp(gl - gmax), 0.0), axis=-1, keepdims=True)
    is_e = (lane >= gsel * EXPERTS_PER_GROUP) & (lane < (gsel + 1) * EXPERTS_PER_GROUP)
    el = jnp.where(is_e, logits, lowest)
    v1 = jnp.max(el, axis=-1, keepdims=True)
    i1 = jnp.min(jnp.where(el == v1, lane, big), axis=-1, keepdims=True)
    el2 = jnp.where(lane == i1, lowest, el)
    v2 = jnp.max(el2, axis=-1, keepdims=True)
    i2 = jnp.min(jnp.where(el2 == v2, lane, big), axis=-1, keepdims=True)
    t = jnp.exp(v2 - v1)
    g0 = p_sel / (1.0 + t)
    g1 = p_sel * t / (1.0 + t)

    yield
    oh0 = lane == i1
    oh1 = lane == i2
    both = (oh0 | oh1).astype(jnp.float32)
    tri = (lax.broadcasted_iota(jnp.int32, (tm, tm), 0)
           > lax.broadcasted_iota(jnp.int32, (tm, tm), 1)).astype(jnp.bfloat16)
    before = jnp.dot(tri, both.astype(jnp.bfloat16), preferred_element_type=jnp.float32)
    cnt = jnp.sum(both, axis=0, keepdims=True)
    padded = jnp.floor((cnt + (MOE_CHUNK - 1.0)) * (1.0 / MOE_CHUNK)) * MOE_CHUNK
    upper = (lax.broadcasted_iota(jnp.int32, (LANES, LANES), 0)
             < lax.broadcasted_iota(jnp.int32, (LANES, LANES), 1)).astype(jnp.bfloat16)
    loff = jnp.dot(jnp.broadcast_to(padded, (SUBLANES, LANES)).astype(jnp.bfloat16), upper,
                   preferred_element_type=jnp.float32)[0:1, :]
    base = before + loff
    slot0 = jnp.sum(jnp.where(oh0, base, 0.0), axis=-1, keepdims=True)
    slot1 = jnp.sum(jnp.where(oh1, base, 0.0), axis=-1, keepdims=True)
    cnt_ref[...] = jnp.broadcast_to(cnt, cnt_ref.shape)

    slot_id = lax.broadcasted_iota(jnp.int32, (tm, MOE_SLOTS), 1)
    place = ((slot_id == slot0.astype(jnp.int32)) | (slot_id == slot1.astype(jnp.int32)))
    xs_ref[...] = _pack_bf16_pairs(
        lax.dot_general(place.astype(jnp.bfloat16), h2_hi, (((0,), (0,)), ((), ())),
                        preferred_element_type=jnp.float32))

    route = jnp.zeros((tm, LANES), jnp.float32)
    for lane_id, val in ((ROUTE_S0, slot0), (ROUTE_S1, slot1), (ROUTE_G0, g0), (ROUTE_G1, g1)):
        route = jnp.where(lane == lane_id, val, route)
    route_ref[...] = route


N_MIXER_TILED_INPUTS = 1 + 2 * N_GROUPS
N_MIXER_OUTPUTS = 4


def _mixer_kernel(*refs):
    n_batch = refs[0].shape[0]
    n_weights = len(refs) - N_MIXER_TILED_INPUTS - N_MIXER_OUTPUTS - n_batch * _MIXER_SCRATCH_PER_SEQ
    tiled = refs[:N_MIXER_TILED_INPUTS]
    weights = refs[N_MIXER_TILED_INPUTS:N_MIXER_TILED_INPUTS + n_weights]
    outs = refs[N_MIXER_TILED_INPUTS + n_weights:N_MIXER_TILED_INPUTS + n_weights + N_MIXER_OUTPUTS]
    scratch = refs[N_MIXER_TILED_INPUTS + n_weights + N_MIXER_OUTPUTS:]
    stages = []
    for b in range(n_batch):
        own = scratch[b * _MIXER_SCRATCH_PER_SEQ:(b + 1) * _MIXER_SCRATCH_PER_SEQ]
        stages.append(_mix_tile(*[r.at[b] for r in tiled], *weights, *[r.at[b] for r in outs], *own))
    while stages:
        for gen in list(stages):
            if next(gen, _DONE) is _DONE:
                stages.remove(gen)


def _mixer_scratch(tm):
    return [
        pltpu.VMEM((tm + 2 * SUBLANES, LRU_WIDTH), jnp.float32),
        pltpu.VMEM((LRU_WIDTH // LANES, SUBLANES * SCAN_PITCH, LANES), jnp.float32),
        pltpu.VMEM((LRU_WIDTH // LANES, SUBLANES * SCAN_PITCH, LANES), jnp.float32),
        pltpu.VMEM((1, LRU_WIDTH), jnp.float32),
    ] + [pltpu.VMEM((GROUP_WIDTH // LANES, tm, LANES), jnp.float32)] * (2 * N_GROUPS)


_MIXER_SCRATCH_PER_SEQ = len(_mixer_scratch(MIX_TILE))


def _token_mixer(x, attn_o, attn_lse, weights):
    B, S, _ = x.shape
    tm = MIX_TILE
    nt = S // tm
    rows = lambda j: (0, j, 0)

    def resident(shape):
        return pl.BlockSpec(shape, lambda j: (0,) * len(shape), pipeline_mode=pl.Buffered(1))

    in_specs = [pl.BlockSpec((B, tm, D_MODEL), rows)]
    group_specs = [pl.BlockSpec((B, d, tm // d, GROUP_WIDTH), lambda j: (0, 0, j, 0))
                   for _, d in DILATED_GROUPS]
    in_specs += group_specs + group_specs
    in_specs += [resident(w.shape) for w in weights]
    x1, xs, route, counts = pl.pallas_call(
        _mixer_kernel,
        grid=(nt,),
        in_specs=in_specs,
        out_specs=[
            pl.BlockSpec((B, tm, D_MODEL), rows),
            pl.BlockSpec((B, MOE_SLOTS, D_MODEL // 2), rows),
            pl.BlockSpec((B, tm, LANES), rows),
            pl.BlockSpec((B, None, SUBLANES, LANES), lambda j: (0, j, 0, 0)),
        ],
        out_shape=[
            jax.ShapeDtypeStruct((B, S, D_MODEL), jnp.float32),
            jax.ShapeDtypeStruct((B, nt * MOE_SLOTS, D_MODEL // 2), jnp.uint32),
            jax.ShapeDtypeStruct((B, S, LANES), jnp.float32),
            jax.ShapeDtypeStruct((B, nt, SUBLANES, LANES), jnp.float32),
        ],
        scratch_shapes=_mixer_scratch(tm) * B,
        compiler_params=pltpu.CompilerParams(
            dimension_semantics=("arbitrary",), vmem_limit_bytes=VMEM_LIMIT),
        name="token_mixer",
    )(x, *attn_o, *attn_lse, *weights)
    return (x1.reshape(B * S, D_MODEL), xs.reshape(B * nt * MOE_SLOTS, D_MODEL // 2),
            route.reshape(B * S, LANES), counts.reshape(B * nt, SUBLANES, LANES))


def _moe_tables(cnt_tile):
    ntiles = cnt_tile.shape[0]
    nch = (cnt_tile + MOE_CHUNK - 1) // MOE_CHUNK
    lo_c = jnp.cumsum(nch, axis=1) - nch
    per_expert = jnp.sum(nch, axis=0)
    region = ((per_expert + MOE_BLOCK_CHUNKS - 1) // MOE_BLOCK_CHUNKS) * MOE_BLOCK_CHUNKS
    pend = jnp.cumsum(region)
    glob = (pend - region)[None, :] + jnp.cumsum(nch, axis=0) - nch

    max_rows = TOP_K * ntiles * MIX_TILE + ntiles * N_EXPERTS * (MOE_CHUNK - 1)
    max_chunks = -(-max_rows // MOE_CHUNK) + N_EXPERTS * (MOE_BLOCK_CHUNKS - 1)
    max_blocks = -(-max_chunks // MOE_BLOCK_CHUNKS)
    max_blocks = -(-max_blocks // EXPERT_BLOCKS_PER_STEP) * EXPERT_BLOCKS_PER_STEP
    max_chunks = max_blocks * MOE_BLOCK_CHUNKS

    seg_start = glob.T.reshape(-1)
    seg_src = (jnp.arange(ntiles, dtype=jnp.int32)[:, None] * TILE_CHUNKS + lo_c).T.reshape(-1)
    step = jnp.diff(seg_src - seg_start, prepend=0)
    g = jnp.arange(max_chunks, dtype=jnp.int32)
    src = g + jnp.sum(jnp.where(seg_start[None, :] <= g[:, None], step[None, :], 0), axis=1)
    src = jnp.clip(src, 0, ntiles * TILE_CHUNKS - 1)

    lstep = jnp.diff(glob - lo_c, axis=1, prepend=0)
    lc = jnp.arange(TILE_CHUNKS, dtype=jnp.int32)
    comb = lc[None, :] + jnp.sum(
        jnp.where(lo_c[:, None, :] <= lc[None, :, None], lstep[:, None, :], 0), axis=2)
    comb = jnp.clip(comb, 0, max_chunks - 1).reshape(-1)

    eidx = jnp.arange(N_EXPERTS, dtype=jnp.int32)
    nonempty = region > 0
    later = jnp.where(nonempty[None, :] & (eidx[None, :] > eidx[:, None]), eidx[None, :], N_EXPERTS)
    next_expert = jnp.min(later, axis=1)
    next_expert = jnp.where(next_expert == N_EXPERTS, -1, next_expert)
    wslot = (jnp.cumsum(nonempty.astype(jnp.int32)) - nonempty.astype(jnp.int32)) % 2
    last_expert = jnp.max(jnp.where(nonempty, eidx, 0))
    blk0 = jnp.arange(max_blocks, dtype=jnp.int32) * MOE_BLOCK_CHUNKS
    block_expert = jnp.minimum(jnp.sum((pend[None, :] <= blk0[:, None]).astype(jnp.int32), axis=1),
                               last_expert)
    of_block = lambda per_expert: jnp.sum(
        jnp.where(block_expert[:, None] == eidx[None, :], per_expert[None, :], 0), axis=1)
    n_valid = pend[-1:] // MOE_BLOCK_CHUNKS
    as_i32 = lambda v: v.astype(jnp.int32)
    expert_tables = (as_i32(block_expert), as_i32(of_block(next_expert)), as_i32(of_block(wslot)),
                     as_i32(n_valid), as_i32(src))
    return expert_tables, as_i32(comb)


def _chunk_copy(src_hbm, src_chunk, dst, dst_chunk, sem):
    rows = lambda c: pl.ds(pl.multiple_of(c * MOE_CHUNK, MOE_CHUNK), MOE_CHUNK)
    return pltpu.make_async_copy(src_hbm.at[rows(src_chunk), :], dst.at[rows(dst_chunk), :], sem)


def _expert_kernel(be_ref, nxt_ref, ws_ref, nv_ref, src_ref, xs_hbm, wgu_hbm, wd_hbm, ys_ref, *scratch):
    for k in range(EXPERT_BLOCKS_PER_STEP):
        _expert_block(pl.program_id(0) * EXPERT_BLOCKS_PER_STEP + k,
                      be_ref, nxt_ref, ws_ref, nv_ref, src_ref, xs_hbm, wgu_hbm, wd_hbm,
                      ys_ref.at[pl.ds(k * MOE_TILE, MOE_TILE), :], *scratch)


def _expert_block(i, be_ref, nxt_ref, ws_ref, nv_ref, src_ref, xs_hbm, wgu_hbm, wd_hbm, ys_ref,
                  xbuf, wgu_f32, wd_f32, wgu_bf, wd_bf, sems, wsems):
    valid = i < nv_ref[0]

    def gather(blk, slot):
        return [_chunk_copy(xs_hbm, src_ref[blk * MOE_BLOCK_CHUNKS + c], xbuf.at[slot], c, sems.at[slot])
                for c in range(MOE_BLOCK_CHUNKS)]

    def fetch_weights(e, slot):
        return [pltpu.make_async_copy(wgu_hbm.at[e], wgu_f32.at[slot], wsems.at[slot]),
                pltpu.make_async_copy(wd_hbm.at[e], wd_f32.at[slot], wsems.at[slot])]

    ahead = EXPERT_GATHER_SLOTS - 1

    @pl.when(i == 0)
    def _():
        for cp in fetch_weights(be_ref[0], 0):
            cp.start(priority=WEIGHT_DMA_PRIORITY)
        for first in range(ahead):
            @pl.when(first < nv_ref[0])
            def _():
                for cp in gather(first, first):
                    cp.start()

    @pl.when(i + ahead < nv_ref[0])
    def _():
        for cp in gather(i + ahead, (i + ahead) % EXPERT_GATHER_SLOTS):
            cp.start()

    new_expert = valid & ((i == 0) | (be_ref[i] != be_ref[jnp.maximum(i - 1, 0)]))

    @pl.when(new_expert)
    def _():
        ws = ws_ref[i]
        for cp in fetch_weights(be_ref[i], ws):
            cp.wait()
        wgu_bf[...] = wgu_f32[ws].astype(jnp.bfloat16)
        wd_bf[...] = wd_f32[ws].astype(jnp.bfloat16)

        @pl.when(nxt_ref[i] >= 0)
        def _():
            for cp in fetch_weights(nxt_ref[i], 1 - ws):
                cp.start(priority=WEIGHT_DMA_PRIORITY)

    @pl.when(valid)
    def _():
        slot = i % EXPERT_GATHER_SLOTS
        for cp in gather(i, slot):
            cp.wait()
        x_lo, x_hi = _unpack_bf16_pairs(xbuf[slot])
        half = D_MODEL // 2
        partial = []

        def up_proj(cols):
            return (jnp.dot(x_lo, wgu_bf[:half, cols], preferred_element_type=jnp.float32)
                    + jnp.dot(x_hi, wgu_bf[half:, cols], preferred_element_type=jnp.float32))

        def hidden_group(c):
            cols = slice(c * EXPERT_GROUP, (c + 1) * EXPERT_GROUP)
            g = up_proj(cols)
            u = up_proj(slice(D_EXPERT + cols.start, D_EXPERT + cols.stop))
            yield
            act = (g * jax.nn.sigmoid(g) * u).astype(jnp.bfloat16)
            yield
            partial.append(jnp.dot(act, wd_bf[cols, :], preferred_element_type=jnp.float32))

        waiting = [hidden_group(c) for c in range(D_EXPERT // EXPERT_GROUP)]
        live = []
        while waiting or live:
            if waiting:
                live.append(waiting.pop(0))
            for gen in list(live):
                if next(gen, _DONE) is _DONE:
                    live.remove(gen)
        y = sum(partial[1:], partial[0])
        ys_ref[...] = _pack_bf16_pairs(y.astype(jnp.bfloat16).astype(jnp.float32))

    @pl.when(jnp.logical_not(valid))
    def _():
        ys_ref[...] = jnp.zeros_like(ys_ref)


def _experts(tables, xs, w_gate_up, w_down):
    block_expert, block_next, block_wslot, n_valid, src = tables
    nblk = block_expert.shape[0]
    return pl.pallas_call(
        _expert_kernel,
        grid_spec=pltpu.PrefetchScalarGridSpec(
            num_scalar_prefetch=5,
            grid=(nblk // EXPERT_BLOCKS_PER_STEP,),
            in_specs=[pl.BlockSpec(memory_space=pl.ANY)] * 3,
            out_specs=pl.BlockSpec((EXPERT_BLOCKS_PER_STEP * MOE_TILE, D_MODEL // 2),
                                   lambda i, *_: (i, 0)),
            scratch_shapes=[
                pltpu.VMEM((EXPERT_GATHER_SLOTS, MOE_TILE, D_MODEL // 2), jnp.uint32),
                pltpu.VMEM((2, D_MODEL, 2 * D_EXPERT), jnp.float32),
                pltpu.VMEM((2, D_EXPERT, D_MODEL), jnp.float32),
                pltpu.VMEM((D_MODEL, 2 * D_EXPERT), jnp.bfloat16),
                pltpu.VMEM((D_EXPERT, D_MODEL), jnp.bfloat16),
                pltpu.SemaphoreType.DMA((EXPERT_GATHER_SLOTS,)),
                pltpu.SemaphoreType.DMA((2,)),
            ],
        ),
        out_shape=jax.ShapeDtypeStruct((nblk * MOE_TILE, D_MODEL // 2), jnp.uint32),
        compiler_params=pltpu.CompilerParams(
            dimension_semantics=("arbitrary",), vmem_limit_bytes=VMEM_LIMIT),
        name="moe_experts",
    )(block_expert, block_next, block_wslot, n_valid, src, xs, w_gate_up, w_down)


def _combine_kernel(comb_ref, ys_hbm, x1_ref, route_ref, nf_ref, out_ref, ybuf, sems):
    tc = MIX_TILE
    per_step = COMBINE_TILES_PER_STEP
    i = pl.program_id(0)
    last = pl.num_programs(0) - 1

    def gather(step, slot):
        first = step * per_step * TILE_CHUNKS
        return [_chunk_copy(ys_hbm, comb_ref[first + lc], ybuf.at[slot], lc, sems.at[slot])
                for lc in range(per_step * TILE_CHUNKS)]

    @pl.when(i == 0)
    def _():
        for cp in gather(0, 0):
            cp.start()

    nxt = jnp.minimum(i + 1, last)
    for cp in gather(nxt, (i + 1) % 2):
        cp.start()

    slot = i % 2
    for cp in gather(i, slot):
        cp.wait()

    def tile(t):
        rows = slice(t * tc, (t + 1) * tc)
        y_lo, y_hi = _unpack_bf16_pairs(ybuf[slot, t * MOE_SLOTS:(t + 1) * MOE_SLOTS, :])
        route = route_ref[rows, :]
        slot_id = lax.broadcasted_iota(jnp.int32, (tc, MOE_SLOTS), 1)
        picked = []
        for lane in (ROUTE_S0, ROUTE_S1):
            sel = (slot_id == route[:, lane:lane + 1].astype(jnp.int32)).astype(jnp.bfloat16)
            yield
            picked.append(jnp.concatenate([jnp.dot(sel, y_lo, preferred_element_type=jnp.float32),
                                           jnp.dot(sel, y_hi, preferred_element_type=jnp.float32)], axis=1))
            yield
        g0 = route[:, ROUTE_G0:ROUTE_G0 + 1]
        g1 = route[:, ROUTE_G1:ROUTE_G1 + 1]
        out_ref[rows, :] = _rmsnorm(x1_ref[rows, :] + g0 * picked[0] + g1 * picked[1], nf_ref[...])

    waiting = [tile(t) for t in range(per_step)]
    live = []
    while waiting or live:
        if waiting:
            live.append(waiting.pop(0))
        for gen in list(live):
            if next(gen, _DONE) is _DONE:
                live.remove(gen)

    @pl.when(i == last)
    def _():
        for cp in gather(nxt, (i + 1) % 2):
            cp.wait()


def _combine(comb, ys, x1, route, norm_f):
    T = x1.shape[0]
    tc = MIX_TILE * COMBINE_TILES_PER_STEP
    return pl.pallas_call(
        _combine_kernel,
        grid_spec=pltpu.PrefetchScalarGridSpec(
            num_scalar_prefetch=1,
            grid=(T // tc,),
            in_specs=[
                pl.BlockSpec(memory_space=pl.ANY),
                pl.BlockSpec((tc, D_MODEL), lambda i, c: (i, 0)),
                pl.BlockSpec((tc, LANES), lambda i, c: (i, 0)),
                pl.BlockSpec((1, D_MODEL), lambda i, c: (0, 0)),
            ],
            out_specs=pl.BlockSpec((tc, D_MODEL), lambda i, c: (i, 0)),
            scratch_shapes=[
                pltpu.VMEM((2, COMBINE_TILES_PER_STEP * MOE_SLOTS, D_MODEL // 2), jnp.uint32),
                pltpu.SemaphoreType.DMA((2,)),
            ],
        ),
        out_shape=jax.ShapeDtypeStruct((T, D_MODEL), jnp.float32),
        compiler_params=pltpu.CompilerParams(
            dimension_semantics=("arbitrary",), vmem_limit_bytes=VMEM_LIMIT),
        name="moe_combine",
    )(comb, ys, x1, route, norm_f)


def _block_diag_gates(w_rg, w_ig):
    def bd(w):
        w4 = w.reshape(N_LRU_CHUNKS, LRU_CHUNK_HEADS, LRU_HEAD_DIM, LRU_HEAD_DIM)
        eye = jnp.eye(LRU_CHUNK_HEADS, dtype=w.dtype)
        return jnp.einsum('chij,hk->chikj', w4, eye).reshape(N_LRU_CHUNKS, LRU_CHUNK, LRU_CHUNK)
    return jnp.concatenate([bd(w_rg), bd(w_ig)], axis=-1).astype(jnp.bfloat16)


def kernel(x, rel_bias, norm1, w_in, conv_w, conv_b, w_rg, b_rg, w_ig, b_ig, lru_lambda,
           w_proj_attn, w_proj_lru, w_out, norm2, w_router_group, w_router_expert,
           w_gate_up, w_down, norm_f):
    B, S, D = x.shape
    T = B * S
    assert w_in.shape[0] == 1, "single-layer block"
    layer = 0
    bf16 = jnp.bfloat16
    x2 = x.reshape(T, D)
    row = lambda v: v[layer].reshape(1, -1)
    w_in_bf = w_in[layer].astype(bf16)
    w_qkv = (w_in_bf[:, :QKV_WIDTH].reshape(D, 3, N_GROUPS, GROUP_WIDTH)
             .transpose(0, 2, 1, 3).reshape(D, QKV_WIDTH))
    w_rest = w_in_bf[:, QKV_WIDTH:]

    qkv_groups = _qkv_projection(x2, row(norm1), w_qkv, B, S)
    attn_o, attn_lse = [], []
    for gi, (window, dilation) in enumerate(DILATED_GROUPS):
        hs = slice(gi * HEADS_PER_GROUP, (gi + 1) * HEADS_PER_GROUP)
        bias = _attention_bias_table(rel_bias[:, hs], window, dilation)
        o, lse = _attention_group(qkv_groups[gi], bias, gi, dilation)
        attn_o.append(o)
        attn_lse.append(lse)

    w_router = jnp.concatenate(
        [w_router_expert[layer].astype(jnp.float32), w_router_group[layer].astype(jnp.float32),
         jnp.zeros((D, LANES - N_EXPERTS - N_EXPERT_GROUPS), jnp.float32)], axis=1)
    w_router_hi = w_router.astype(bf16)
    weights = [
        row(norm1), w_rest, conv_w[layer], row(conv_b),
        _block_diag_gates(w_rg[layer], w_ig[layer]), row(b_rg), row(b_ig), row(lru_lambda),
        w_proj_attn[layer].astype(bf16), w_proj_lru[layer].astype(bf16), w_out[layer].astype(bf16),
        row(norm2),
        jnp.concatenate([w_router_hi, (w_router - w_router_hi.astype(jnp.float32)).astype(bf16)], axis=1),
    ]
    x1, xs, route, counts = _token_mixer(x, attn_o, attn_lse, weights)

    expert_tables, comb = _moe_tables(counts[:, 0, :N_EXPERTS].astype(jnp.int32))
    ys = _experts(expert_tables, xs, w_gate_up[layer], w_down[layer])
    out = _combine(comb, ys, x1, route, norm_f.reshape(1, -1))
    return out.reshape(B, S, D)
```
